```python
import math
import jax, jax.numpy as jnp
from jax import lax
import numpy as np

D_MODEL = 1024
BATCH = 8
SEQ = 4096
DEPTH = 2

N_MIXERS = 2
HEAD_DIM_A = 64
DIL_GROUPS = ((128, 1), (512, 4), (2048, 16))
HEADS_PER_GROUP_A = 6
N_HEADS_A = HEADS_PER_GROUP_A * len(DIL_GROUPS)
WIDTH_A = N_HEADS_A * HEAD_DIM_A
N_BUCKETS = 32
T5_MAX_DISTANCE = 1024
N_HEADS_B = 16
QK_NOPE_DIM = 64
QK_ROPE_DIM = 32
V_HEAD_DIM = 64
Q_LORA_RANK = 256
KV_LORA_RANK = 128
WIDTH_B = N_HEADS_B * V_HEAD_DIM
IN_B = Q_LORA_RANK + KV_LORA_RANK + QK_ROPE_DIM + WIDTH_B
ROPE_THETA = 10000.0
Q_BLOCK = 128
DEPTH_A = (DEPTH + N_MIXERS - 1) // N_MIXERS
DEPTH_B = DEPTH // N_MIXERS
DEEPNORM_ALPHA = (2.0 * DEPTH) ** 0.25
DEEPNORM_BETA = (8.0 * DEPTH) ** -0.25
LN_EPS = 1e-5
RMS_EPS = 1e-6
NEG_INF = -1e30

kernel_name = "hybrid_dilated_mla_encoder"


def layernorm(x, g, b):
    xf = x.astype(jnp.float32)
    mu = xf.mean(-1, keepdims=True)
    var = jnp.square(xf - mu).mean(-1, keepdims=True)
    return ((xf - mu) * lax.rsqrt(var + LN_EPS) * g.astype(jnp.float32) + b.astype(jnp.float32)).astype(x.dtype)


def rmsnorm(x, g):
    xf = x.astype(jnp.float32)
    return (xf * lax.rsqrt(jnp.square(xf).mean(-1, keepdims=True) + RMS_EPS) * g.astype(jnp.float32)).astype(x.dtype)


def t5_bucket(rel):
    half = N_BUCKETS // 2
    max_exact = half // 2
    base = jnp.where(rel > 0, half, 0)
    n = jnp.abs(rel)
    nf = jnp.maximum(n, 1).astype(jnp.float32)
    large = max_exact + (jnp.log(nf / max_exact) / math.log(T5_MAX_DISTANCE / max_exact)
                         * (half - max_exact)).astype(jnp.int32)
    large = jnp.minimum(large, half - 1)
    return base + jnp.where(n < max_exact, n, large)


def dilated_window_attention(q, k, v, bias_table, window, dil):
    B, S, H, Dh = q.shape
    R = window // (2 * dil)
    L = S // dil
    nb = -(-L // R)
    Lp = nb * R

    def to_sub(t):
        return t.reshape(B, L, dil, H, Dh).transpose(0, 2, 3, 1, 4)

    qs = jnp.pad(to_sub(q), ((0, 0), (0, 0), (0, 0), (0, Lp - L), (0, 0))).reshape(B, dil, H, nb, R, Dh)

    def key_blocks(t):
        tp = jnp.pad(to_sub(t), ((0, 0), (0, 0), (0, 0), (R, Lp - L + R), (0, 0))).reshape(B, dil, H, nb + 2, R, Dh)
        return jnp.concatenate([tp[:, :, :, :-2], tp[:, :, :, 1:-1], tp[:, :, :, 2:]], axis=4)

    kb = key_blocks(k)
    vb = key_blocks(v)
    a = jnp.arange(R)[:, None]
    bk = jnp.arange(3 * R)[None, :]
    off = bk - R - a
    kpos = jnp.arange(nb)[:, None, None] * R + bk[None] - R
    valid = (jnp.abs(off) <= R)[None] & (kpos >= 0) & (kpos < L)
    bias = bias_table[t5_bucket(off * dil)].transpose(2, 0, 1).astype(jnp.float32)

    s = jnp.einsum('bdhnqe,bdhnke->bdhnqk', qs, kb).astype(jnp.float32) * (Dh ** -0.5) + bias[:, None]
    s = jnp.where(valid, s, NEG_INF)
    m = s.max(-1, keepdims=True)
    p = jnp.exp(s - m)
    denom = p.sum(-1, keepdims=True)
    o = jnp.einsum('bdhnqk,bdhnke->bdhnqe', p, vb.astype(jnp.float32)) / denom
    lse = (m + jnp.log(denom))[..., 0]

    def from_sub(t):
        t = t.reshape(B, dil, H, Lp, *t.shape[5:])[:, :, :, :L]
        t = jnp.moveaxis(t, 3, 1)
        return t.reshape(B, S, H, *t.shape[4:])

    return from_sub(o).astype(q.dtype), from_sub(lse)


def mixer_dilated(u, w_in, w_out, rel_bias):
    B, S, _ = u.shape
    q, k, v, gate = jnp.split(u @ w_in, 4, axis=-1)
    q = q.reshape(B, S, N_HEADS_A, HEAD_DIM_A)
    k = k.reshape(B, S, N_HEADS_A, HEAD_DIM_A)
    v = v.reshape(B, S, N_HEADS_A, HEAD_DIM_A)
    outs, lses = [], []
    for g, (window, dil) in enumerate(DIL_GROUPS):
        hs = slice(g * HEADS_PER_GROUP_A, (g + 1) * HEADS_PER_GROUP_A)
        o, l = dilated_window_attention(q[:, :, hs], k[:, :, hs], v[:, :, hs], rel_bias[:, hs], window, dil)
        outs.append(o)
        lses.append(l)
    wts = jax.nn.softmax(jnp.stack(lses, axis=0), axis=0).astype(u.dtype)
    o = jnp.concatenate([outs[g] * wts[g][..., None] for g in range(len(DIL_GROUPS))], axis=2)
    y = o.reshape(B, S, WIDTH_A) * jax.nn.silu(gate)
    return y @ w_out


def rope(x, cos, sin):
    half = x.shape[-1] // 2
    x1, x2 = x[..., :half], x[..., half:]
    return jnp.concatenate([x1 * cos - x2 * sin, x2 * cos + x1 * sin], axis=-1)


def mixer_mla(u, w_in, q_norm, w_uq, kv_norm, w_ukv, w_out):
    B, S, _ = u.shape
    cq, ckv, k_rope, gate = jnp.split(
        u @ w_in, [Q_LORA_RANK, Q_LORA_RANK + KV_LORA_RANK, Q_LORA_RANK + KV_LORA_RANK + QK_ROPE_DIM], axis=-1)
    q = (rmsnorm(cq, q_norm) @ w_uq).reshape(B, S, N_HEADS_B, QK_NOPE_DIM + QK_ROPE_DIM)
    q_nope, q_rope = q[..., :QK_NOPE_DIM], q[..., QK_NOPE_DIM:]
    kv = (rmsnorm(ckv, kv_norm) @ w_ukv).reshape(B, S, N_HEADS_B, QK_NOPE_DIM + V_HEAD_DIM)
    k_nope, v = kv[..., :QK_NOPE_DIM], kv[..., QK_NOPE_DIM:]

    pos = jnp.arange(S, dtype=jnp.float32)
    inv_freq = ROPE_THETA ** (-jnp.arange(0, QK_ROPE_DIM, 2, dtype=jnp.float32) / QK_ROPE_DIM)
    ang = pos[:, None] * inv_freq[None, :]
    cos, sin = jnp.cos(ang).astype(u.dtype), jnp.sin(ang).astype(u.dtype)
    q_rope = rope(q_rope, cos[:, None], sin[:, None])
    k_rope = rope(k_rope, cos, sin)

    scale = (QK_NOPE_DIM + QK_ROPE_DIM) ** -0.5
    nq = S // Q_BLOCK
    qn_b = q_nope.reshape(B, nq, Q_BLOCK, N_HEADS_B, QK_NOPE_DIM).transpose(1, 0, 2, 3, 4)
    qr_b = q_rope.reshape(B, nq, Q_BLOCK, N_HEADS_B, QK_ROPE_DIM).transpose(1, 0, 2, 3, 4)

    def block(args):
        qn, qr = args
        s = (jnp.einsum('bqhd,bkhd->bhqk', qn, k_nope)
             + jnp.einsum('bqhd,bkd->bhqk', qr, k_rope)).astype(jnp.float32) * scale
        p = jax.nn.softmax(s, axis=-1).astype(v.dtype)
        return jnp.einsum('bhqk,bkhd->bqhd', p, v)

    o = lax.map(block, (qn_b, qr_b))
    o = o.transpose(1, 0, 2, 3, 4).reshape(B, S, WIDTH_B)
    y = o * jax.nn.silu(gate)
    return y @ w_out


def setup_inputs(seed: int = 0) -> dict:
    key = jax.random.key(seed)
    ks = jax.random.split(key, 20)
    nrm = jax.random.normal
    D = D_MODEL
    x = nrm(ks[0], (BATCH, SEQ, D), jnp.float32)
    c = nrm(ks[1], (BATCH, D), jnp.float32)
    rel_bias = 0.2 * nrm(ks[2], (N_BUCKETS, N_HEADS_A), jnp.float32)
    ada_w = 0.5 * D ** -0.5 * nrm(ks[3], (DEPTH, D, 3 * D), jnp.float32)
    ada_b = 0.02 * nrm(ks[4], (DEPTH, 3 * D), jnp.float32)
    ln_g = 1.0 + 0.05 * nrm(ks[5], (DEPTH, D), jnp.float32)
    ln_b = 0.02 * nrm(ks[6], (DEPTH, D), jnp.float32)
    a_w_in = D ** -0.5 * nrm(ks[7], (DEPTH_A, D, 4 * WIDTH_A), jnp.float32)
    a_w_in = a_w_in.at[:, :, 2 * WIDTH_A:3 * WIDTH_A].multiply(DEEPNORM_BETA)
    a_w_out = DEEPNORM_BETA * WIDTH_A ** -0.5 * nrm(ks[8], (DEPTH_A, WIDTH_A, D), jnp.float32)
    b_w_in = D ** -0.5 * nrm(ks[9], (DEPTH_B, D, IN_B), jnp.float32)
    b_q_norm = 1.0 + 0.05 * nrm(ks[10], (DEPTH_B, Q_LORA_RANK), jnp.float32)
    b_w_uq = Q_LORA_RANK ** -0.5 * nrm(ks[11], (DEPTH_B, Q_LORA_RANK, N_HEADS_B * (QK_NOPE_DIM + QK_ROPE_DIM)), jnp.float32)
    b_kv_norm = 1.0 + 0.05 * nrm(ks[12], (DEPTH_B, KV_LORA_RANK), jnp.float32)
    b_w_ukv = KV_LORA_RANK ** -0.5 * nrm(ks[13], (DEPTH_B, KV_LORA_RANK, N_HEADS_B, QK_NOPE_DIM + V_HEAD_DIM), jnp.float32)
    b_w_ukv = b_w_ukv.at[..., QK_NOPE_DIM:].multiply(DEEPNORM_BETA).reshape(
        DEPTH_B, KV_LORA_RANK, N_HEADS_B * (QK_NOPE_DIM + V_HEAD_DIM))
    b_w_out = DEEPNORM_BETA * WIDTH_B ** -0.5 * nrm(ks[14], (DEPTH_B, WIDTH_B, D), jnp.float32)
    return {"x": x, "c": c, "rel_bias": rel_bias, "ada_w": ada_w, "ada_b": ada_b,
            "ln_g": ln_g, "ln_b": ln_b, "a_w_in": a_w_in, "a_w_out": a_w_out,
            "b_w_in": b_w_in, "b_q_norm": b_q_norm, "b_w_uq": b_w_uq, "b_kv_norm": b_kv_norm,
            "b_w_ukv": b_w_ukv, "b_w_out": b_w_out}


def reference(x, c, rel_bias, ada_w, ada_b, ln_g, ln_b, a_w_in, a_w_out,
              b_w_in, b_q_norm, b_w_uq, b_kv_norm, b_w_ukv, b_w_out):
    for i in range(DEPTH):
        mod = jax.nn.silu(c) @ ada_w[i] + ada_b[i]
        shift, scale, gate = jnp.split(mod[:, None, :], 3, axis=-1)
        u = x * (1 + scale) + shift
        j = i // N_MIXERS
        if i % N_MIXERS == 0:
            y = mixer_dilated(u, a_w_in[j], a_w_out[j], rel_bias)
        else:
            y = mixer_mla(u, b_w_in[j], b_q_norm[j], b_w_uq[j], b_kv_norm[j], b_w_ukv[j], b_w_out[j])
        x = layernorm(DEEPNORM_ALPHA * x + gate * y, ln_g[i], ln_b[i])
    return x
```

```python
import functools
import math

import numpy as np
import jax
import jax.numpy as jnp
from jax import lax
from jax.experimental import pallas as pl
from jax.experimental.pallas import tpu as pltpu

D_MODEL = 1024
DEPTH = 2
HEAD_DIM_A = 64
DIL_GROUPS = ((128, 1), (512, 4), (2048, 16))
HEADS_PER_GROUP_A = 6
GROUP_WIDTH_A = HEADS_PER_GROUP_A * HEAD_DIM_A
WIDTH_A = GROUP_WIDTH_A * len(DIL_GROUPS)
N_BUCKETS = 32
T5_MAX_DISTANCE = 1024
N_HEADS_B = 16
QK_NOPE_DIM = 64
QK_ROPE_DIM = 32
V_HEAD_DIM = 64
Q_LORA_RANK = 256
KV_LORA_RANK = 128
WIDTH_B = N_HEADS_B * V_HEAD_DIM
ROPE_THETA = 10000.0
DEEPNORM_ALPHA = (2.0 * DEPTH) ** 0.25
LN_EPS = 1e-5
RMS_EPS = 1e-6
NEG_INF = -1e30

LANES = 128
VMEM_LIMIT_BYTES = 56 * 1024 * 1024

HALF_WINDOW = 64
TQ_A = 128
TK_A = TQ_A + 2 * HALF_WINDOW
TL_PROJ = 512
TQ_B = 256
TK_B = 512

_NT = (((1,), (1,)), ((), ()))


def _silu(x):
    return x * (1.0 / (1.0 + jnp.exp(-x)))


def _params(*sem):
    return pltpu.CompilerParams(dimension_semantics=sem, vmem_limit_bytes=VMEM_LIMIT_BYTES)


def _mod_kernel(c_ref, w_ref, b_ref, o_ref):
    sc = _silu(c_ref[...]).astype(jnp.bfloat16)
    o_ref[...] = jnp.dot(sc, w_ref[...].astype(jnp.bfloat16),
                         preferred_element_type=jnp.float32) + b_ref[...]


def _modulation(c, ada_w, ada_b):
    B, D = c.shape
    nj = 3
    return pl.pallas_call(
        _mod_kernel,
        grid=(DEPTH, nj),
        in_specs=[pl.BlockSpec((B, D), lambda i, j: (0, 0)),
                  pl.BlockSpec((None, D, D), lambda i, j: (i, 0, j)),
                  pl.BlockSpec((None, 1, D), lambda i, j: (i, 0, j))],
        out_specs=pl.BlockSpec((None, B, D), lambda i, j: (i, 0, j)),
        out_shape=jax.ShapeDtypeStruct((DEPTH, B, 3 * D), jnp.float32),
        compiler_params=_params("arbitrary", "arbitrary"),
        name="modulation",
    )(c, ada_w, ada_b.reshape(DEPTH, 1, 3 * D))


def _modulate(x, mod):
    D = x.shape[-1]
    return x * (1.0 + mod[:, D:2 * D]) + mod[:, :D]


def _layernorm(z, g, b):
    mu = jnp.mean(z, axis=-1, keepdims=True)
    zc = z - mu
    var = jnp.mean(zc * zc, axis=-1, keepdims=True)
    return zc * lax.rsqrt(var + LN_EPS) * g + b


def _proj_a_kernel(x_ref, mod_ref, w_ref, *out_refs):
    u = _modulate(x_ref[...], mod_ref[...]).astype(jnp.bfloat16)
    z = jnp.dot(u, w_ref[...], preferred_element_type=jnp.float32)
    col = 0
    for o_ref in out_refs:
        n = o_ref.shape[-1]
        o_ref[...] = z[:, col:col + n].astype(o_ref.dtype)
        col += n


def _proj_a(x, mod, w, dil, with_gate):
    B, S, D = x.shape
    L = S // dil
    tl = min(TL_PROJ, L)
    n_qkv = 3 * GROUP_WIDTH_A
    xv = x.reshape(B, L, dil * D)
    out_shape = [jax.ShapeDtypeStruct((B, dil, L, n_qkv), jnp.bfloat16)]
    out_specs = [pl.BlockSpec((None, None, tl, n_qkv), lambda b, r, i: (b, r, i, 0))]
    if with_gate:
        out_shape.append(jax.ShapeDtypeStruct((B, S, WIDTH_A), jnp.bfloat16))
        out_specs.append(pl.BlockSpec((None, tl, WIDTH_A), lambda b, r, i: (b, i, 0)))
    return pl.pallas_call(
        _proj_a_kernel,
        grid=(B, dil, L // tl),
        in_specs=[pl.BlockSpec((None, tl, D), lambda b, r, i: (b, i, r)),
                  pl.BlockSpec((None, 1, 3 * D), lambda b, r, i: (b, 0, 0)),
                  pl.BlockSpec(w.shape, lambda b, r, i: (0, 0))],
        out_specs=out_specs,
        out_shape=out_shape,
        compiler_params=_params("arbitrary", "arbitrary", "arbitrary"),
        name=f"proj_a_d{dil}",
    )(xv, mod, w)


def _attn_a_kernel(qkv_ref, bias_ref, o_ref, lse_ref):
    L = qkv_ref.shape[0]
    n_tiles = L // TQ_A
    gw = GROUP_WIDTH_A
    lane = lax.broadcasted_iota(jnp.int32, (TQ_A, LANES), 1)

    def tile(t, carry):
        j0 = pl.multiple_of(t * TQ_A, TQ_A)
        ws = pl.multiple_of(jnp.clip(j0 - HALF_WINDOW, 0, L - TK_A), HALF_WINDOW)
        variant = jnp.where(t == 0, 0, jnp.where(t == n_tiles - 1, 2, 1))
        lse_tile = jnp.zeros((TQ_A, LANES), jnp.float32)
        for h in range(HEADS_PER_GROUP_A):
            c0 = h * HEAD_DIM_A
            q = qkv_ref[pl.ds(j0, TQ_A), c0:c0 + HEAD_DIM_A]
            k = qkv_ref[pl.ds(ws, TK_A), gw + c0:gw + c0 + HEAD_DIM_A]
            v = qkv_ref[pl.ds(ws, TK_A), 2 * gw + c0:2 * gw + c0 + HEAD_DIM_A]
            s = lax.dot_general(q, k, _NT, preferred_element_type=jnp.float32)
            s = s + bias_ref[variant, h]
            m = jnp.max(s, axis=-1, keepdims=True)
            p = jnp.exp(s - m)
            denom = jnp.sum(p, axis=-1, keepdims=True)
            o = jnp.dot(p.astype(jnp.bfloat16), v, preferred_element_type=jnp.float32) / denom
            o_ref[pl.ds(j0, TQ_A), c0:c0 + HEAD_DIM_A] = o.astype(o_ref.dtype)
            lse_tile = jnp.where(lane == h, m + jnp.log(denom), lse_tile)
        lse_ref[pl.ds(j0, TQ_A), :] = lse_tile
        return carry

    lax.fori_loop(0, n_tiles, tile, 0)


def _attn_a(qkv, biasm, S):
    B, dil, L, n_qkv = qkv.shape
    gw = GROUP_WIDTH_A
    o, lse = pl.pallas_call(
        _attn_a_kernel,
        grid=(B, dil),
        in_specs=[pl.BlockSpec((None, None, L, n_qkv), lambda b, r: (b, r, 0, 0)),
                  pl.BlockSpec(biasm.shape, lambda b, r: (0, 0, 0, 0))],
        out_specs=[pl.BlockSpec((None, L, gw), lambda b, r: (b, 0, r)),
                   pl.BlockSpec((None, L, LANES), lambda b, r: (b, 0, r))],
        out_shape=[jax.ShapeDtypeStruct((B, L, dil * gw), jnp.bfloat16),
                   jax.ShapeDtypeStruct((B, L, dil * LANES), jnp.float32)],
        compiler_params=_params("arbitrary", "arbitrary"),
        name=f"attn_a_d{dil}",
    )(qkv, biasm)
    return o.reshape(B, S, gw), lse.reshape(B, S, LANES)


def _t5_bucket_np(rel):
    half = N_BUCKETS // 2
    max_exact = half // 2
    base = np.where(rel > 0, half, 0)
    n = np.abs(rel)
    nf = np.maximum(n, 1).astype(np.float32)
    large = max_exact + (np.log(nf / np.float32(max_exact)) / np.float32(math.log(T5_MAX_DISTANCE / max_exact))
                         * np.float32(half - max_exact)).astype(np.int32)
    large = np.minimum(large, half - 1)
    return base + np.where(n < max_exact, n, large)


def _bias_variants(rel_bias_g, dil):
    qq = np.arange(TQ_A)[:, None]
    kk = np.arange(TK_A)[None, :]
    idx = []
    for shift in (0, HALF_WINDOW, 2 * HALF_WINDOW):
        off = kk - shift - qq
        bucket = _t5_bucket_np(off * dil)
        idx.append(np.where(np.abs(off) <= HALF_WINDOW, bucket, N_BUCKETS))
    idx = jnp.asarray(np.stack(idx).astype(np.int32))
    table = jnp.concatenate(
        [rel_bias_g.astype(jnp.float32), jnp.full((1, rel_bias_g.shape[1]), NEG_INF, jnp.float32)], axis=0)
    return jnp.transpose(table[idx], (0, 3, 1, 2))


def _out_a_kernel(x_ref, mod_ref, o0_ref, o1_ref, o2_ref, l0_ref, l1_ref, l2_ref, gate_ref,
                  expand_ref, w_ref, g_ref, b_ref, out_ref):
    D = x_ref.shape[-1]
    gw = GROUP_WIDTH_A
    l0, l1, l2 = l0_ref[...], l1_ref[...], l2_ref[...]
    mx = jnp.maximum(jnp.maximum(l0, l1), l2)
    e0, e1, e2 = jnp.exp(l0 - mx), jnp.exp(l1 - mx), jnp.exp(l2 - mx)
    inv = 1.0 / (e0 + e1 + e2)
    y = jnp.zeros((x_ref.shape[0], D), jnp.float32)
    for g, (e, o_ref) in enumerate(((e0, o0_ref), (e1, o1_ref), (e2, o2_ref))):
        wt = e * inv
        hi = wt.astype(jnp.bfloat16)
        lo = (wt - hi.astype(jnp.float32)).astype(jnp.bfloat16)
        wexp = (jnp.dot(hi, expand_ref[...], preferred_element_type=jnp.float32)
                + jnp.dot(lo, expand_ref[...], preferred_element_type=jnp.float32))
        gate = gate_ref[:, g * gw:(g + 1) * gw].astype(jnp.float32)
        yg = (o_ref[...].astype(jnp.float32) * wexp * _silu(gate)).astype(jnp.bfloat16)
        y = y + jnp.dot(yg, w_ref[g * gw:(g + 1) * gw, :], preferred_element_type=jnp.float32)
    z = DEEPNORM_ALPHA * x_ref[...] + mod_ref[:, 2 * D:] * y
    out_ref[...] = _layernorm(z, g_ref[...], b_ref[...])


def _out_a(x, mod, os_, lses, gate, w_out, ln_g, ln_b):
    B, S, D = x.shape
    tl = TL_PROJ
    gw = GROUP_WIDTH_A
    expand = jnp.asarray(
        (np.arange(LANES)[:, None] == (np.arange(gw)[None, :] // HEAD_DIM_A)).astype(np.float32),
        dtype=jnp.bfloat16)
    row = lambda b, i: (b, i, 0)
    const2 = lambda b, i: (0, 0)
    return pl.pallas_call(
        _out_a_kernel,
        grid=(B, S // tl),
        in_specs=[pl.BlockSpec((None, tl, D), row),
                  pl.BlockSpec((None, 1, 3 * D), lambda b, i: (b, 0, 0)),
                  pl.BlockSpec((None, tl, gw), row), pl.BlockSpec((None, tl, gw), row),
                  pl.BlockSpec((None, tl, gw), row),
                  pl.BlockSpec((None, tl, LANES), row), pl.BlockSpec((None, tl, LANES), row),
                  pl.BlockSpec((None, tl, LANES), row),
                  pl.BlockSpec((None, tl, WIDTH_A), row),
                  pl.BlockSpec(expand.shape, const2),
                  pl.BlockSpec(w_out.shape, const2),
                  pl.BlockSpec((1, D), const2), pl.BlockSpec((1, D), const2)],
        out_specs=pl.BlockSpec((None, tl, D), row),
        out_shape=jax.ShapeDtypeStruct((B, S, D), jnp.float32),
        compiler_params=_params("arbitrary", "arbitrary"),
        name="out_a",
    )(x, mod, *os_, *lses, gate, expand, w_out, ln_g.reshape(1, D), ln_b.reshape(1, D))


def _rmsnorm(x, g):
    return x * lax.rsqrt(jnp.mean(x * x, axis=-1, keepdims=True) + RMS_EPS) * g


def _proj_b_kernel(x_ref, mod_ref, w_in_ref, qn_ref, kvn_ref, wqa_ref, wqb_ref, wk_ref, wv_ref,
                   cq_ref, sq_ref, ck_ref, sk_ref, q_out, k_out, v_out, gate_out):
    u = _modulate(x_ref[...], mod_ref[...]).astype(jnp.bfloat16)
    z = jnp.dot(u, w_in_ref[...], preferred_element_type=jnp.float32)
    c0 = Q_LORA_RANK
    c1 = c0 + KV_LORA_RANK
    cq = z[:, :c0]
    ckv = z[:, c0:c1]
    kr = z[:, c1:c1 + LANES]
    krs = z[:, c1 + LANES:c1 + 2 * LANES]
    gate_out[...] = z[:, c1 + 2 * LANES:].astype(gate_out.dtype)

    cqn = _rmsnorm(cq, qn_ref[...]).astype(jnp.bfloat16)
    qa = jnp.dot(cqn, wqa_ref[...], preferred_element_type=jnp.float32)
    qb = jnp.dot(cqn, wqb_ref[...], preferred_element_type=jnp.float32)
    ckvn = _rmsnorm(ckv, kvn_ref[...]).astype(jnp.bfloat16)
    kn = jnp.dot(ckvn, wk_ref[...], preferred_element_type=jnp.float32)
    v_out[...] = jnp.dot(ckvn, wv_ref[...], preferred_element_type=jnp.float32).astype(v_out.dtype)

    cos_q, sin_q = cq_ref[...], sq_ref[...]
    k_rope = kr * ck_ref[...] + krs * sk_ref[...]
    for h in range(N_HEADS_B):
        blk = slice(h * LANES, (h + 1) * LANES)
        q_out[:, blk] = (qa[:, blk] * cos_q + qb[:, blk] * sin_q).astype(q_out.dtype)
        k_out[:, blk] = (kn[:, blk] + k_rope).astype(k_out.dtype)


def _proj_b(x, mod, w_in, q_norm, kv_norm, wqa, wqb, wk, wv, tables):
    B, S, D = x.shape
    tl = TL_PROJ
    HP = N_HEADS_B * LANES
    row = lambda b, i: (b, i, 0)
    const2 = lambda b, i: (0, 0)
    tab = pl.BlockSpec((tl, LANES), lambda b, i: (i, 0))
    wide = jax.ShapeDtypeStruct((B, S, HP), jnp.bfloat16)
    return pl.pallas_call(
        _proj_b_kernel,
        grid=(B, S // tl),
        in_specs=[pl.BlockSpec((None, tl, D), row),
                  pl.BlockSpec((None, 1, 3 * D), lambda b, i: (b, 0, 0)),
                  pl.BlockSpec(w_in.shape, const2),
                  pl.BlockSpec((1, Q_LORA_RANK), const2), pl.BlockSpec((1, KV_LORA_RANK), const2),
                  pl.BlockSpec(wqa.shape, const2), pl.BlockSpec(wqb.shape, const2),
                  pl.BlockSpec(wk.shape, const2), pl.BlockSpec(wv.shape, const2),
                  tab, tab, tab, tab],
        out_specs=[pl.BlockSpec((None, tl, HP), row), pl.BlockSpec((None, tl, HP), row),
                   pl.BlockSpec((None, tl, HP), row), pl.BlockSpec((None, tl, WIDTH_B), row)],
        out_shape=[wide, wide, wide, jax.ShapeDtypeStruct((B, S, WIDTH_B), jnp.bfloat16)],
        compiler_params=_params("arbitrary", "arbitrary"),
        name="proj_b",
    )(x, mod, w_in, q_norm.reshape(1, -1), kv_norm.reshape(1, -1), wqa, wqb, wk, wv, *tables)


def _attn_b_kernel(q_ref, k_ref, v_ref, o_ref, s_scr):
    tq = q_ref.shape[0]
    S = k_ref.shape[0]
    n_chunks = S // TK_B
    acc = jnp.zeros((tq, LANES), jnp.float32)
    inv = []
    for e in range(2):
        hl = slice(e * LANES, (e + 1) * LANES)
        q = q_ref[:, hl]
        mx = jnp.full((tq, LANES), -jnp.inf, jnp.float32)
        for c in range(n_chunks):
            ks = slice(c * TK_B, (c + 1) * TK_B)
            s = lax.dot_general(q, k_ref[ks, hl], _NT, preferred_element_type=jnp.float32)
            s_scr[:, ks] = s
            for j in range(TK_B // LANES):
                mx = jnp.maximum(mx, s[:, j * LANES:(j + 1) * LANES])
        m = jnp.max(mx, axis=-1, keepdims=True)
        ls = jnp.zeros((tq, LANES), jnp.float32)
        for c in range(n_chunks):
            ks = slice(c * TK_B, (c + 1) * TK_B)
            p = jnp.exp(s_scr[:, ks] - m)
            for j in range(TK_B // LANES):
                ls = ls + p[:, j * LANES:(j + 1) * LANES]
            acc = acc + jnp.dot(p.astype(jnp.bfloat16), v_ref[ks, hl], preferred_element_type=jnp.float32)
        inv.append(1.0 / jnp.sum(ls, axis=-1, keepdims=True))
    lane = lax.broadcasted_iota(jnp.int32, (tq, LANES), 1)
    o_ref[...] = (acc * jnp.where(lane < V_HEAD_DIM, inv[0], inv[1])).astype(o_ref.dtype)


def _attn_b(q2, k2, v2):
    B, S, HP = q2.shape
    n_pairs = N_HEADS_B // 2
    pw = 2 * LANES
    return pl.pallas_call(
        _attn_b_kernel,
        grid=(B, n_pairs, S // TQ_B),
        in_specs=[pl.BlockSpec((None, TQ_B, pw), lambda b, p, i: (b, i, p)),
                  pl.BlockSpec((None, S, pw), lambda b, p, i: (b, 0, p)),
                  pl.BlockSpec((None, S, pw), lambda b, p, i: (b, 0, p))],
        out_specs=pl.BlockSpec((None, TQ_B, LANES), lambda b, p, i: (b, i, p)),
        out_shape=jax.ShapeDtypeStruct((B, S, WIDTH_B), jnp.bfloat16),
        scratch_shapes=[pltpu.VMEM((TQ_B, S), jnp.float32)],
        compiler_params=_params("arbitrary", "arbitrary", "arbitrary"),
        name="attn_b",
    )(q2, k2, v2)


def _out_b_kernel(x_ref, mod_ref, o_ref, gate_ref, w_ref, g_ref, b_ref, out_ref):
    D = x_ref.shape[-1]
    yg = (o_ref[...].astype(jnp.float32) * _silu(gate_ref[...].astype(jnp.float32))).astype(jnp.bfloat16)
    y = jnp.dot(yg, w_ref[...], preferred_element_type=jnp.float32)
    z = DEEPNORM_ALPHA * x_ref[...] + mod_ref[:, 2 * D:] * y
    out_ref[...] = _layernorm(z, g_ref[...], b_ref[...])


def _out_b(x, mod, o, gate, w_out, ln_g, ln_b):
    B, S, D = x.shape
    tl = TL_PROJ
    row = lambda b, i: (b, i, 0)
    const2 = lambda b, i: (0, 0)
    return pl.pallas_call(
        _out_b_kernel,
        grid=(B, S // tl),
        in_specs=[pl.BlockSpec((None, tl, D), row),
                  pl.BlockSpec((None, 1, 3 * D), lambda b, i: (b, 0, 0)),
                  pl.BlockSpec((None, tl, WIDTH_B), row), pl.BlockSpec((None, tl, WIDTH_B), row),
                  pl.BlockSpec(w_out.shape, const2),
                  pl.BlockSpec((1, D), const2), pl.BlockSpec((1, D), const2)],
        out_specs=pl.BlockSpec((None, tl, D), row),
        out_shape=jax.ShapeDtypeStruct((B, S, D), jnp.float32),
        compiler_params=_params("arbitrary", "arbitrary"),
        name="out_b",
    )(x, mod, o, gate, w_out, ln_g.reshape(1, D), ln_b.reshape(1, D))


def _weights_a(a_w_in):
    W = WIDTH_A
    gw = GROUP_WIDTH_A
    q, k, v, gate = (a_w_in[:, i * W:(i + 1) * W] for i in range(4))
    ws = []
    for g in range(len(DIL_GROUPS)):
        cs = slice(g * gw, (g + 1) * gw)
        parts = [q[:, cs] * (HEAD_DIM_A ** -0.5), k[:, cs], v[:, cs]]
        if g == 0:
            parts.append(gate)
        ws.append(jnp.concatenate(parts, axis=1).astype(jnp.bfloat16))
    return ws


def _swap_halves(w):
    half = w.shape[-1] // 2
    return jnp.concatenate([w[..., half:], w[..., :half]], axis=-1)


def _pad_lanes(w, lo):
    n = w.shape[-1]
    return jnp.pad(w, [(0, 0)] * (w.ndim - 1) + [(lo, LANES - lo - n)])


def _weights_b(b_w_in, b_w_uq, b_w_ukv):
    c0 = Q_LORA_RANK
    c1 = c0 + KV_LORA_RANK
    c2 = c1 + QK_ROPE_DIM
    kr = b_w_in[:, c1:c2]
    w_in = jnp.concatenate([b_w_in[:, :c1], _pad_lanes(kr, QK_NOPE_DIM),
                            _pad_lanes(_swap_halves(kr), QK_NOPE_DIM), b_w_in[:, c2:]], axis=1)
    uq = b_w_uq.reshape(Q_LORA_RANK, N_HEADS_B, QK_NOPE_DIM + QK_ROPE_DIM)
    q_rope = uq[..., QK_NOPE_DIM:]
    wqa = _pad_lanes(uq, 0).reshape(Q_LORA_RANK, -1)
    wqb = _pad_lanes(_swap_halves(q_rope), QK_NOPE_DIM).reshape(Q_LORA_RANK, -1)
    ukv = b_w_ukv.reshape(KV_LORA_RANK, N_HEADS_B, QK_NOPE_DIM + V_HEAD_DIM)
    wk = _pad_lanes(ukv[..., :QK_NOPE_DIM], 0).reshape(KV_LORA_RANK, -1)
    vv = ukv[..., QK_NOPE_DIM:]
    odd = (jnp.arange(N_HEADS_B) % 2 == 1)[None, :, None]
    wv = jnp.where(odd, _pad_lanes(vv, LANES - V_HEAD_DIM), _pad_lanes(vv, 0)).reshape(KV_LORA_RANK, -1)
    bf = lambda w: w.astype(jnp.bfloat16)
    return bf(w_in), bf(wqa), bf(wqb), bf(wk), bf(wv)


def _rope_tables(S):
    pos = jnp.arange(S, dtype=jnp.float32)
    inv_freq = ROPE_THETA ** (-jnp.arange(0, QK_ROPE_DIM, 2, dtype=jnp.float32) / QK_ROPE_DIM)
    ang = pos[:, None] * inv_freq[None, :]
    cos, sin = jnp.cos(ang), jnp.sin(ang)
    cosf = _pad_lanes(jnp.concatenate([cos, cos], axis=-1), QK_NOPE_DIM)
    sinf = _pad_lanes(jnp.concatenate([-sin, sin], axis=-1), QK_NOPE_DIM)
    nope = _pad_lanes(jnp.ones((S, QK_NOPE_DIM), jnp.float32), 0)
    scale = (QK_NOPE_DIM + QK_ROPE_DIM) ** -0.5
    return (scale * (nope + cosf), scale * sinf, cosf, sinf)


def kernel(x, c, rel_bias, ada_w, ada_b, ln_g, ln_b, a_w_in, a_w_out,
           b_w_in, b_q_norm, b_w_uq, b_kv_norm, b_w_ukv, b_w_out):
    B, S, D = x.shape
    mods = _modulation(c, ada_w, ada_b)
    mod0 = mods[0].reshape(B, 1, 3 * D)
    mod1 = mods[1].reshape(B, 1, 3 * D)

    os_, lses, gate_a = [], [], None
    for g, ((window, dil), w) in enumerate(zip(DIL_GROUPS, _weights_a(a_w_in[0]))):
        assert window // (2 * dil) == HALF_WINDOW
        outs = _proj_a(x, mod0, w, dil, with_gate=(g == 0))
        if g == 0:
            gate_a = outs[1]
        hs = slice(g * HEADS_PER_GROUP_A, (g + 1) * HEADS_PER_GROUP_A)
        o, lse = _attn_a(outs[0], _bias_variants(rel_bias[:, hs], dil), S)
        os_.append(o)
        lses.append(lse)
    x1 = _out_a(x, mod0, os_, lses, gate_a, a_w_out[0].astype(jnp.bfloat16), ln_g[0], ln_b[0])

    w_in, wqa, wqb, wk, wv = _weights_b(b_w_in[0], b_w_uq[0], b_w_ukv[0])
    q2, k2, v2, gate_b = _proj_b(x1, mod1, w_in, b_q_norm[0], b_kv_norm[0], wqa, wqb, wk, wv,
                                 _rope_tables(S))
    o = _attn_b(q2, k2, v2)
    return _out_b(x1, mod1, o, gate_b, b_w_out[0].astype(jnp.bfloat16), ln_g[1], ln_b[1])
```

```python
import functools
import math

import numpy as np
import jax
import jax.numpy as jnp
from jax import lax
from jax.experimental import pallas as pl
from jax.experimental.pallas import tpu as pltpu

D_MODEL = 1024
DEPTH = 2
HEAD_DIM_A = 64
DIL_GROUPS = ((128, 1), (512, 4), (2048, 16))
HEADS_PER_GROUP_A = 6
GROUP_WIDTH_A = HEADS_PER_GROUP_A * HEAD_DIM_A
WIDTH_A = GROUP_WIDTH_A * len(DIL_GROUPS)
N_BUCKETS = 32
T5_MAX_DISTANCE = 1024
N_HEADS_B = 16
QK_NOPE_DIM = 64
QK_ROPE_DIM = 32
V_HEAD_DIM = 64
Q_LORA_RANK = 256
KV_LORA_RANK = 128
WIDTH_B = N_HEADS_B * V_HEAD_DIM
ROPE_THETA = 10000.0
DEEPNORM_ALPHA = (2.0 * DEPTH) ** 0.25
LN_EPS = 1e-5
RMS_EPS = 1e-6
NEG_INF = -1e30

LANES = 128
VMEM_LIMIT_BYTES = 56 * 1024 * 1024

HALF_WINDOW = 64
TQ_A = 128
TK_A = TQ_A + 2 * HALF_WINDOW
TL_PROJ = 512
TQ_B = 256
TK_B = 512

_NT = (((1,), (1,)), ((), ()))


def _silu(x):
    return x * (1.0 / (1.0 + jnp.exp(-x)))


def _params(*sem):
    return pltpu.CompilerParams(dimension_semantics=sem, vmem_limit_bytes=VMEM_LIMIT_BYTES)


def _mod_kernel(c_ref, w_ref, b_ref, o_ref):
    sc = _silu(c_ref[...]).astype(jnp.bfloat16)
    o_ref[...] = jnp.dot(sc, w_ref[...].astype(jnp.bfloat16),
                         preferred_element_type=jnp.float32) + b_ref[...]


def _modulation(c, ada_w, ada_b):
    B, D = c.shape
    nj = 3
    return pl.pallas_call(
        _mod_kernel,
        grid=(DEPTH, nj),
        in_specs=[pl.BlockSpec((B, D), lambda i, j: (0, 0)),
                  pl.BlockSpec((None, D, D), lambda i, j: (i, 0, j)),
                  pl.BlockSpec((None, 1, D), lambda i, j: (i, 0, j))],
        out_specs=pl.BlockSpec((None, B, D), lambda i, j: (i, 0, j)),
        out_shape=jax.ShapeDtypeStruct((DEPTH, B, 3 * D), jnp.float32),
        compiler_params=_params("arbitrary", "arbitrary"),
        name="modulation",
    )(c, ada_w, ada_b.reshape(DEPTH, 1, 3 * D))


def _modulate(x, mod):
    D = x.shape[-1]
    return x * (1.0 + mod[:, D:2 * D]) + mod[:, :D]


def _layernorm(z, g, b):
    mu = jnp.mean(z, axis=-1, keepdims=True)
    zc = z - mu
    var = jnp.mean(zc * zc, axis=-1, keepdims=True)
    return zc * lax.rsqrt(var + LN_EPS) * g + b


def _class_major_perm(tl, dil, inverse=False):
    p = np.arange(tl)
    src = (p % (tl // dil)) * dil + p // (tl // dil)
    mat = (src[:, None] == np.arange(tl)[None, :]).astype(np.float32)
    return jnp.asarray(mat.T if inverse else mat, dtype=jnp.bfloat16)


def _proj_a_kernel(dil, with_gate, x_ref, mod_ref, w_ref, *refs):
    u = _modulate(x_ref[...], mod_ref[...]).astype(jnp.bfloat16)
    if dil > 1:
        perm_ref, refs = refs[0], refs[1:]
        u = jnp.dot(perm_ref[...], u, preferred_element_type=jnp.float32).astype(jnp.bfloat16)
    z = jnp.dot(u, w_ref[...], preferred_element_type=jnp.float32)
    qkv_ref = refs[0]
    rows, n_qkv = qkv_ref.shape[-2:]
    for r in range(dil):
        qkv_ref[r] = z[r * rows:(r + 1) * rows, :n_qkv].astype(qkv_ref.dtype)
    if with_gate:
        refs[1][...] = z[:, n_qkv:].astype(refs[1].dtype)


def _proj_a(x, mod, w, dil, with_gate):
    B, S, D = x.shape
    L = S // dil
    tl = TL_PROJ if dil == 1 else TL_PROJ // 2
    n_qkv = 3 * GROUP_WIDTH_A
    row = lambda b, i: (b, i, 0)
    const2 = lambda b, i: (0, 0)
    in_specs = [pl.BlockSpec((None, tl, D), row),
                pl.BlockSpec((None, 1, 3 * D), lambda b, i: (b, 0, 0)),
                pl.BlockSpec(w.shape, const2)]
    args = [x, mod, w]
    if dil > 1:
        in_specs.append(pl.BlockSpec((tl, tl), const2))
        args.append(_class_major_perm(tl, dil))
    out_shape = [jax.ShapeDtypeStruct((B, dil, L, n_qkv), jnp.bfloat16)]
    out_specs = [pl.BlockSpec((None, dil, tl // dil, n_qkv), lambda b, i: (b, 0, i, 0))]
    if with_gate:
        out_shape.append(jax.ShapeDtypeStruct((B, S, WIDTH_A), jnp.bfloat16))
        out_specs.append(pl.BlockSpec((None, tl, WIDTH_A), row))
    return pl.pallas_call(
        functools.partial(_proj_a_kernel, dil, with_gate),
        grid=(B, S // tl),
        in_specs=in_specs,
        out_specs=out_specs,
        out_shape=out_shape,
        compiler_params=_params("arbitrary", "arbitrary"),
        name=f"proj_a_d{dil}",
    )(*args)


def _attn_a_kernel(qkv_ref, bias_ref, o_ref, lse_ref):
    L = qkv_ref.shape[0]
    n_tiles = L // TQ_A
    gw = GROUP_WIDTH_A
    lane = lax.broadcasted_iota(jnp.int32, (TQ_A, LANES), 1)

    def tile(t, carry):
        j0 = pl.multiple_of(t * TQ_A, TQ_A)
        ws = pl.multiple_of(jnp.clip(j0 - HALF_WINDOW, 0, L - TK_A), HALF_WINDOW)
        variant = jnp.where(t == 0, 0, jnp.where(t == n_tiles - 1, 2, 1))
        lse_tile = jnp.zeros((TQ_A, LANES), jnp.float32)
        for h in range(HEADS_PER_GROUP_A):
            c0 = h * HEAD_DIM_A
            q = qkv_ref[pl.ds(j0, TQ_A), c0:c0 + HEAD_DIM_A]
            k = qkv_ref[pl.ds(ws, TK_A), gw + c0:gw + c0 + HEAD_DIM_A]
            v = qkv_ref[pl.ds(ws, TK_A), 2 * gw + c0:2 * gw + c0 + HEAD_DIM_A]
            s = lax.dot_general(q, k, _NT, preferred_element_type=jnp.float32)
            s = s + bias_ref[variant, h]
            m = jnp.max(s, axis=-1, keepdims=True)
            p = jnp.exp(s - m)
            denom = jnp.sum(p, axis=-1, keepdims=True)
            o = jnp.dot(p.astype(jnp.bfloat16), v, preferred_element_type=jnp.float32) / denom
            o_ref[pl.ds(j0, TQ_A), c0:c0 + HEAD_DIM_A] = o.astype(o_ref.dtype)
            lse_tile = jnp.where(lane == h, m + jnp.log(denom), lse_tile)
        lse_ref[pl.ds(j0, TQ_A), :] = lse_tile
        return carry

    lax.fori_loop(0, n_tiles, tile, 0)


def _attn_a(qkv, biasm):
    B, dil, L, n_qkv = qkv.shape
    gw = GROUP_WIDTH_A
    cls = lambda b, r: (b, r, 0, 0)
    return pl.pallas_call(
        _attn_a_kernel,
        grid=(B, dil),
        in_specs=[pl.BlockSpec((None, None, L, n_qkv), cls),
                  pl.BlockSpec(biasm.shape, lambda b, r: (0, 0, 0, 0))],
        out_specs=[pl.BlockSpec((None, None, L, gw), cls),
                   pl.BlockSpec((None, None, L, LANES), cls)],
        out_shape=[jax.ShapeDtypeStruct((B, dil, L, gw), jnp.bfloat16),
                   jax.ShapeDtypeStruct((B, dil, L, LANES), jnp.float32)],
        compiler_params=_params("arbitrary", "arbitrary"),
        name=f"attn_a_d{dil}",
    )(qkv, biasm)


def _t5_bucket_np(rel):
    half = N_BUCKETS // 2
    max_exact = half // 2
    base = np.where(rel > 0, half, 0)
    n = np.abs(rel)
    nf = np.maximum(n, 1).astype(np.float32)
    large = max_exact + (np.log(nf / np.float32(max_exact)) / np.float32(math.log(T5_MAX_DISTANCE / max_exact))
                         * np.float32(half - max_exact)).astype(np.int32)
    large = np.minimum(large, half - 1)
    return base + np.where(n < max_exact, n, large)


def _bias_variants(rel_bias_g, dil):
    H = rel_bias_g.shape[1]
    M = 2 * TK_A
    off = np.arange(M) - (TK_A - 1)
    idx = np.where(np.abs(off) <= HALF_WINDOW, _t5_bucket_np(off * dil), N_BUCKETS).astype(np.int32)
    table = jnp.concatenate([rel_bias_g.astype(jnp.float32), jnp.full((1, H), NEG_INF, jnp.float32)], axis=0)
    w = table[jnp.asarray(idx)].T
    rot = jnp.tile(w, (1, TQ_A + 1))[:, :TQ_A * (M + 1)].reshape(H, TQ_A, M + 1)[:, ::-1, :]
    variants = []
    for shift in (0, HALF_WINDOW, 2 * HALF_WINDOW):
        j0 = TK_A - TQ_A - shift
        variants.append(rot[:, :, j0:j0 + TK_A])
    return jnp.stack(variants)


def _split3_bf16(v):
    hi = v.astype(jnp.bfloat16)
    r1 = v - hi.astype(jnp.float32)
    mid = r1.astype(jnp.bfloat16)
    lo = (r1 - mid.astype(jnp.float32)).astype(jnp.bfloat16)
    return hi, mid, lo


def _out_a_kernel(x_ref, mod_ref, o0_ref, o1_ref, o2_ref, l0_ref, l1_ref, l2_ref, gate_ref,
                  unperm1_ref, unperm2_ref, expand_ref, w_ref, g_ref, b_ref, out_ref):
    tl, D = x_ref.shape
    gw = GROUP_WIDTH_A

    def natural_order(o_ref, l_ref, unperm_ref):
        o = o_ref[...].reshape(tl, gw)
        lse = l_ref[...].reshape(tl, LANES)
        if unperm_ref is None:
            return o.astype(jnp.float32), lse
        pt = unperm_ref[...]
        o = jnp.dot(pt, o, preferred_element_type=jnp.float32)
        lse = sum(jnp.dot(pt, part, preferred_element_type=jnp.float32) for part in _split3_bf16(lse))
        return o, lse

    o0, l0 = natural_order(o0_ref, l0_ref, None)
    o1, l1 = natural_order(o1_ref, l1_ref, unperm1_ref)
    o2, l2 = natural_order(o2_ref, l2_ref, unperm2_ref)
    mx = jnp.maximum(jnp.maximum(l0, l1), l2)
    e0, e1, e2 = jnp.exp(l0 - mx), jnp.exp(l1 - mx), jnp.exp(l2 - mx)
    inv = 1.0 / (e0 + e1 + e2)
    y = jnp.zeros((tl, D), jnp.float32)
    for g, (e, o) in enumerate(((e0, o0), (e1, o1), (e2, o2))):
        wt = e * inv
        hi = wt.astype(jnp.bfloat16)
        lo = (wt - hi.astype(jnp.float32)).astype(jnp.bfloat16)
        wexp = (jnp.dot(hi, expand_ref[...], preferred_element_type=jnp.float32)
                + jnp.dot(lo, expand_ref[...], preferred_element_type=jnp.float32))
        gate = gate_ref[:, g * gw:(g + 1) * gw].astype(jnp.float32)
        yg = (o * wexp * _silu(gate)).astype(jnp.bfloat16)
        y = y + jnp.dot(yg, w_ref[g * gw:(g + 1) * gw, :], preferred_element_type=jnp.float32)
    z = DEEPNORM_ALPHA * x_ref[...] + mod_ref[:, 2 * D:] * y
    out_ref[...] = _layernorm(z, g_ref[...], b_ref[...])


def _out_a(x, mod, os_, lses, gate, w_out, ln_g, ln_b):
    B, S, D = x.shape
    tl = TL_PROJ
    gw = GROUP_WIDTH_A
    expand = jnp.asarray(
        (np.arange(LANES)[:, None] == (np.arange(gw)[None, :] // HEAD_DIM_A)).astype(np.float32),
        dtype=jnp.bfloat16)
    unperms = [_class_major_perm(tl, dil, inverse=True) for _, dil in DIL_GROUPS[1:]]
    row = lambda b, i: (b, i, 0)
    cls = lambda b, i: (b, 0, i, 0)
    const2 = lambda b, i: (0, 0)
    o_specs = [pl.BlockSpec((None, dil, tl // dil, gw), cls) for _, dil in DIL_GROUPS]
    l_specs = [pl.BlockSpec((None, dil, tl // dil, LANES), cls) for _, dil in DIL_GROUPS]
    return pl.pallas_call(
        _out_a_kernel,
        grid=(B, S // tl),
        in_specs=[pl.BlockSpec((None, tl, D), row),
                  pl.BlockSpec((None, 1, 3 * D), lambda b, i: (b, 0, 0)),
                  *o_specs, *l_specs,
                  pl.BlockSpec((None, tl, WIDTH_A), row),
                  pl.BlockSpec((tl, tl), const2), pl.BlockSpec((tl, tl), const2),
                  pl.BlockSpec(expand.shape, const2),
                  pl.BlockSpec(w_out.shape, const2),
                  pl.BlockSpec((1, D), const2), pl.BlockSpec((1, D), const2)],
        out_specs=pl.BlockSpec((None, tl, D), row),
        out_shape=jax.ShapeDtypeStruct((B, S, D), jnp.float32),
        compiler_params=_params("arbitrary", "arbitrary"),
        name="out_a",
    )(x, mod, *os_, *lses, gate, *unperms, expand, w_out, ln_g.reshape(1, D), ln_b.reshape(1, D))


def _rmsnorm(x, g):
    return x * lax.rsqrt(jnp.mean(x * x, axis=-1, keepdims=True) + RMS_EPS) * g


def _proj_b_kernel(x_ref, mod_ref, w_in_ref, qn_ref, kvn_ref, wqa_ref, wqb_ref, wk_ref, wv_ref,
                   cq_ref, sq_ref, ck_ref, sk_ref, q_out, k_out, v_out, gate_out):
    u = _modulate(x_ref[...], mod_ref[...]).astype(jnp.bfloat16)
    z = jnp.dot(u, w_in_ref[...], preferred_element_type=jnp.float32)
    c0 = Q_LORA_RANK
    c1 = c0 + KV_LORA_RANK
    cq = z[:, :c0]
    ckv = z[:, c0:c1]
    kr = z[:, c1:c1 + LANES]
    krs = z[:, c1 + LANES:c1 + 2 * LANES]
    gate_out[...] = z[:, c1 + 2 * LANES:].astype(gate_out.dtype)

    cqn = _rmsnorm(cq, qn_ref[...]).astype(jnp.bfloat16)
    qa = jnp.dot(cqn, wqa_ref[...], preferred_element_type=jnp.float32)
    qb = jnp.dot(cqn, wqb_ref[...], preferred_element_type=jnp.float32)
    ckvn = _rmsnorm(ckv, kvn_ref[...]).astype(jnp.bfloat16)
    kn = jnp.dot(ckvn, wk_ref[...], preferred_element_type=jnp.float32)
    v_out[...] = jnp.dot(ckvn, wv_ref[...], preferred_element_type=jnp.float32).astype(v_out.dtype)

    cos_q, sin_q = cq_ref[...], sq_ref[...]
    k_rope = kr * ck_ref[...] + krs * sk_ref[...]
    for h in range(N_HEADS_B):
        blk = slice(h * LANES, (h + 1) * LANES)
        q_out[:, blk] = (qa[:, blk] * cos_q + qb[:, blk] * sin_q).astype(q_out.dtype)
        k_out[:, blk] = (kn[:, blk] + k_rope).astype(k_out.dtype)


def _proj_b(x, mod, w_in, q_norm, kv_norm, wqa, wqb, wk, wv, tables):
    B, S, D = x.shape
    tl = TL_PROJ
    HP = N_HEADS_B * LANES
    row = lambda b, i: (b, i, 0)
    const2 = lambda b, i: (0, 0)
    tab = pl.BlockSpec((tl, LANES), lambda b, i: (i, 0))
    wide = jax.ShapeDtypeStruct((B, S, HP), jnp.bfloat16)
    return pl.pallas_call(
        _proj_b_kernel,
        grid=(B, S // tl),
        in_specs=[pl.BlockSpec((None, tl, D), row),
                  pl.BlockSpec((None, 1, 3 * D), lambda b, i: (b, 0, 0)),
                  pl.BlockSpec(w_in.shape, const2),
                  pl.BlockSpec((1, Q_LORA_RANK), const2), pl.BlockSpec((1, KV_LORA_RANK), const2),
                  pl.BlockSpec(wqa.shape, const2), pl.BlockSpec(wqb.shape, const2),
                  pl.BlockSpec(wk.shape, const2), pl.BlockSpec(wv.shape, const2),
                  tab, tab, tab, tab],
        out_specs=[pl.BlockSpec((None, tl, HP), row), pl.BlockSpec((None, tl, HP), row),
                   pl.BlockSpec((None, tl, HP), row), pl.BlockSpec((None, tl, WIDTH_B), row)],
        out_shape=[wide, wide, wide, jax.ShapeDtypeStruct((B, S, WIDTH_B), jnp.bfloat16)],
        compiler_params=_params("arbitrary", "arbitrary"),
        name="proj_b",
    )(x, mod, w_in, q_norm.reshape(1, -1), kv_norm.reshape(1, -1), wqa, wqb, wk, wv, *tables)


def _attn_b_kernel(q_ref, k_ref, v_ref, o_ref, s_scr):
    tq = q_ref.shape[0]
    S = k_ref.shape[0]
    n_chunks = S // TK_B
    acc = jnp.zeros((tq, LANES), jnp.float32)
    inv = []
    for e in range(2):
        hl = slice(e * LANES, (e + 1) * LANES)
        q = q_ref[:, hl]
        mx = jnp.full((tq, LANES), -jnp.inf, jnp.float32)
        for c in range(n_chunks):
            ks = slice(c * TK_B, (c + 1) * TK_B)
            s = lax.dot_general(q, k_ref[ks, hl], _NT, preferred_element_type=jnp.float32)
            s_scr[:, ks] = s
            for j in range(TK_B // LANES):
                mx = jnp.maximum(mx, s[:, j * LANES:(j + 1) * LANES])
        m = jnp.max(mx, axis=-1, keepdims=True)
        ls = jnp.zeros((tq, LANES), jnp.float32)
        for c in range(n_chunks):
            ks = slice(c * TK_B, (c + 1) * TK_B)
            p = jnp.exp(s_scr[:, ks] - m)
            for j in range(TK_B // LANES):
                ls = ls + p[:, j * LANES:(j + 1) * LANES]
            acc = acc + jnp.dot(p.astype(jnp.bfloat16), v_ref[ks, hl], preferred_element_type=jnp.float32)
        inv.append(1.0 / jnp.sum(ls, axis=-1, keepdims=True))
    lane = lax.broadcasted_iota(jnp.int32, (tq, LANES), 1)
    o_ref[...] = (acc * jnp.where(lane < V_HEAD_DIM, inv[0], inv[1])).astype(o_ref.dtype)


def _attn_b(q2, k2, v2):
    B, S, HP = q2.shape
    n_pairs = N_HEADS_B // 2
    pw = 2 * LANES
    return pl.pallas_call(
        _attn_b_kernel,
        grid=(B, n_pairs, S // TQ_B),
        in_specs=[pl.BlockSpec((None, TQ_B, pw), lambda b, p, i: (b, i, p)),
                  pl.BlockSpec((None, S, pw), lambda b, p, i: (b, 0, p)),
                  pl.BlockSpec((None, S, pw), lambda b, p, i: (b, 0, p))],
        out_specs=pl.BlockSpec((None, TQ_B, LANES), lambda b, p, i: (b, i, p)),
        out_shape=jax.ShapeDtypeStruct((B, S, WIDTH_B), jnp.bfloat16),
        scratch_shapes=[pltpu.VMEM((TQ_B, S), jnp.float32)],
        compiler_params=_params("arbitrary", "arbitrary", "arbitrary"),
        name="attn_b",
    )(q2, k2, v2)


def _out_b_kernel(x_ref, mod_ref, o_ref, gate_ref, w_ref, g_ref, b_ref, out_ref):
    D = x_ref.shape[-1]
    yg = (o_ref[...].astype(jnp.float32) * _silu(gate_ref[...].astype(jnp.float32))).astype(jnp.bfloat16)
    y = jnp.dot(yg, w_ref[...], preferred_element_type=jnp.float32)
    z = DEEPNORM_ALPHA * x_ref[...] + mod_ref[:, 2 * D:] * y
    out_ref[...] = _layernorm(z, g_ref[...], b_ref[...])


def _out_b(x, mod, o, gate, w_out, ln_g, ln_b):
    B, S, D = x.shape
    tl = TL_PROJ
    row = lambda b, i: (b, i, 0)
    const2 = lambda b, i: (0, 0)
    return pl.pallas_call(
        _out_b_kernel,
        grid=(B, S // tl),
        in_specs=[pl.BlockSpec((None, tl, D), row),
                  pl.BlockSpec((None, 1, 3 * D), lambda b, i: (b, 0, 0)),
                  pl.BlockSpec((None, tl, WIDTH_B), row), pl.BlockSpec((None, tl, WIDTH_B), row),
                  pl.BlockSpec(w_out.shape, const2),
                  pl.BlockSpec((1, D), const2), pl.BlockSpec((1, D), const2)],
        out_specs=pl.BlockSpec((None, tl, D), row),
        out_shape=jax.ShapeDtypeStruct((B, S, D), jnp.float32),
        compiler_params=_params("arbitrary", "arbitrary"),
        name="out_b",
    )(x, mod, o, gate, w_out, ln_g.reshape(1, D), ln_b.reshape(1, D))


def _weights_a(a_w_in):
    W = WIDTH_A
    gw = GROUP_WIDTH_A
    q, k, v, gate = (a_w_in[:, i * W:(i + 1) * W] for i in range(4))
    ws = []
    for g in range(len(DIL_GROUPS)):
        cs = slice(g * gw, (g + 1) * gw)
        parts = [q[:, cs] * (HEAD_DIM_A ** -0.5), k[:, cs], v[:, cs]]
        if g == 0:
            parts.append(gate)
        ws.append(jnp.concatenate(parts, axis=1).astype(jnp.bfloat16))
    return ws


def _swap_halves(w):
    half = w.shape[-1] // 2
    return jnp.concatenate([w[..., half:], w[..., :half]], axis=-1)


def _pad_lanes(w, lo):
    n = w.shape[-1]
    return jnp.pad(w, [(0, 0)] * (w.ndim - 1) + [(lo, LANES - lo - n)])


def _weights_b(b_w_in, b_w_uq, b_w_ukv):
    c0 = Q_LORA_RANK
    c1 = c0 + KV_LORA_RANK
    c2 = c1 + QK_ROPE_DIM
    kr = b_w_in[:, c1:c2]
    w_in = jnp.concatenate([b_w_in[:, :c1], _pad_lanes(kr, QK_NOPE_DIM),
                            _pad_lanes(_swap_halves(kr), QK_NOPE_DIM), b_w_in[:, c2:]], axis=1)
    uq = b_w_uq.reshape(Q_LORA_RANK, N_HEADS_B, QK_NOPE_DIM + QK_ROPE_DIM)
    q_rope = uq[..., QK_NOPE_DIM:]
    wqa = _pad_lanes(uq, 0).reshape(Q_LORA_RANK, -1)
    wqb = _pad_lanes(_swap_halves(q_rope), QK_NOPE_DIM).reshape(Q_LORA_RANK, -1)
    ukv = b_w_ukv.reshape(KV_LORA_RANK, N_HEADS_B, QK_NOPE_DIM + V_HEAD_DIM)
    wk = _pad_lanes(ukv[..., :QK_NOPE_DIM], 0).reshape(KV_LORA_RANK, -1)
    vv = ukv[..., QK_NOPE_DIM:]
    odd = (jnp.arange(N_HEADS_B) % 2 == 1)[None, :, None]
    wv = jnp.where(odd, _pad_lanes(vv, LANES - V_HEAD_DIM), _pad_lanes(vv, 0)).reshape(KV_LORA_RANK, -1)
    bf = lambda w: w.astype(jnp.bfloat16)
    return bf(w_in), bf(wqa), bf(wqb), bf(wk), bf(wv)


def _rope_tables(S):
    pos = jnp.arange(S, dtype=jnp.float32)
    inv_freq = ROPE_THETA ** (-jnp.arange(0, QK_ROPE_DIM, 2, dtype=jnp.float32) / QK_ROPE_DIM)
    ang = pos[:, None] * inv_freq[None, :]
    cos, sin = jnp.cos(ang), jnp.sin(ang)
    cosf = _pad_lanes(jnp.concatenate([cos, cos], axis=-1), QK_NOPE_DIM)
    sinf = _pad_lanes(jnp.concatenate([-sin, sin], axis=-1), QK_NOPE_DIM)
    nope = _pad_lanes(jnp.ones((S, QK_NOPE_DIM), jnp.float32), 0)
    scale = (QK_NOPE_DIM + QK_ROPE_DIM) ** -0.5
    return (scale * (nope + cosf), scale * sinf, cosf, sinf)


def kernel(x, c, rel_bias, ada_w, ada_b, ln_g, ln_b, a_w_in, a_w_out,
           b_w_in, b_q_norm, b_w_uq, b_kv_norm, b_w_ukv, b_w_out):
    B, S, D = x.shape
    mods = _modulation(c, ada_w, ada_b)
    mod0 = mods[0].reshape(B, 1, 3 * D)
    mod1 = mods[1].reshape(B, 1, 3 * D)

    os_, lses, gate_a = [], [], None
    for g, ((window, dil), w) in enumerate(zip(DIL_GROUPS, _weights_a(a_w_in[0]))):
        assert window // (2 * dil) == HALF_WINDOW
        outs = _proj_a(x, mod0, w, dil, with_gate=(g == 0))
        if g == 0:
            gate_a = outs[1]
        hs = slice(g * HEADS_PER_GROUP_A, (g + 1) * HEADS_PER_GROUP_A)
        o, lse = _attn_a(outs[0], _bias_variants(rel_bias[:, hs], dil))
        os_.append(o)
        lses.append(lse)
    x1 = _out_a(x, mod0, os_, lses, gate_a, a_w_out[0].astype(jnp.bfloat16), ln_g[0], ln_b[0])

    w_in, wqa, wqb, wk, wv = _weights_b(b_w_in[0], b_w_uq[0], b_w_ukv[0])
    q2, k2, v2, gate_b = _proj_b(x1, mod1, w_in, b_q_norm[0], b_kv_norm[0], wqa, wqb, wk, wv,
                                 _rope_tables(S))
    o = _attn_b(q2, k2, v2)
    return _out_b(x1, mod1, o, gate_b, b_w_out[0].astype(jnp.bfloat16), ln_g[1], ln_b[1])
```

```python
import functools
import math

import numpy as np
import jax
import jax.numpy as jnp
from jax import lax
from jax.experimental import pallas as pl
from jax.experimental.pallas import tpu as pltpu

D_MODEL = 1024
DEPTH = 2
HEAD_DIM_A = 64
DIL_GROUPS = ((128, 1), (512, 4), (2048, 16))
HEADS_PER_GROUP_A = 6
GROUP_WIDTH_A = HEADS_PER_GROUP_A * HEAD_DIM_A
WIDTH_A = GROUP_WIDTH_A * len(DIL_GROUPS)
N_BUCKETS = 32
T5_MAX_DISTANCE = 1024
N_HEADS_B = 16
QK_NOPE_DIM = 64
QK_ROPE_DIM = 32
V_HEAD_DIM = 64
Q_LORA_RANK = 256
KV_LORA_RANK = 128
WIDTH_B = N_HEADS_B * V_HEAD_DIM
ROPE_THETA = 10000.0
DEEPNORM_ALPHA = (2.0 * DEPTH) ** 0.25
LN_EPS = 1e-5
RMS_EPS = 1e-6
NEG_INF = -1e30

LANES = 128
VMEM_LIMIT_BYTES = 56 * 1024 * 1024

HALF_WINDOW = 64
TQ_A = 128
TK_A = TQ_A + 2 * HALF_WINDOW
TL_PROJ = 512
TQ_B = 512
TK_B = 512

_NT = (((1,), (1,)), ((), ()))


def _silu(x):
    return x * (1.0 / (1.0 + jnp.exp(-x)))


def _params(*sem):
    return pltpu.CompilerParams(dimension_semantics=sem, vmem_limit_bytes=VMEM_LIMIT_BYTES)


def _mod_kernel(c_ref, w_ref, b_ref, o_ref):
    sc = _silu(c_ref[...]).astype(jnp.bfloat16)
    o_ref[...] = jnp.dot(sc, w_ref[...].astype(jnp.bfloat16),
                         preferred_element_type=jnp.float32) + b_ref[...]


def _modulation(c, ada_w, ada_b):
    B, D = c.shape
    nj = 3
    return pl.pallas_call(
        _mod_kernel,
        grid=(DEPTH, nj),
        in_specs=[pl.BlockSpec((B, D), lambda i, j: (0, 0)),
                  pl.BlockSpec((None, D, D), lambda i, j: (i, 0, j)),
                  pl.BlockSpec((None, 1, D), lambda i, j: (i, 0, j))],
        out_specs=pl.BlockSpec((None, B, D), lambda i, j: (i, 0, j)),
        out_shape=jax.ShapeDtypeStruct((DEPTH, B, 3 * D), jnp.float32),
        compiler_params=_params("arbitrary", "arbitrary"),
        name="modulation",
    )(c, ada_w, ada_b.reshape(DEPTH, 1, 3 * D))


def _modulate(x, mod):
    D = x.shape[-1]
    return x * (1.0 + mod[:, D:2 * D]) + mod[:, :D]


def _layernorm(z, g, b):
    mu = jnp.mean(z, axis=-1, keepdims=True)
    zc = z - mu
    var = jnp.mean(zc * zc, axis=-1, keepdims=True)
    return zc * lax.rsqrt(var + LN_EPS) * g + b


def _class_major_perm(tl, dil, inverse=False):
    p = np.arange(tl)
    src = (p % (tl // dil)) * dil + p // (tl // dil)
    mat = (src[:, None] == np.arange(tl)[None, :]).astype(np.float32)
    return jnp.asarray(mat.T if inverse else mat, dtype=jnp.bfloat16)


def _proj_a_kernel(dil, with_gate, x_ref, mod_ref, w_ref, *refs):
    u = _modulate(x_ref[...], mod_ref[...]).astype(jnp.bfloat16)
    if dil > 1:
        perm_ref, refs = refs[0], refs[1:]
        u = jnp.dot(perm_ref[...], u, preferred_element_type=jnp.float32).astype(jnp.bfloat16)
    z = jnp.dot(u, w_ref[...], preferred_element_type=jnp.float32)
    qkv_ref = refs[0]
    rows, n_qkv = qkv_ref.shape[-2:]
    for r in range(dil):
        qkv_ref[r] = z[r * rows:(r + 1) * rows, :n_qkv].astype(qkv_ref.dtype)
    if with_gate:
        refs[1][...] = z[:, n_qkv:].astype(refs[1].dtype)


def _proj_a(x, mod, w, dil, with_gate):
    B, S, D = x.shape
    L = S // dil
    tl = TL_PROJ if dil == 1 else TL_PROJ // 2
    n_qkv = 3 * GROUP_WIDTH_A
    row = lambda b, i: (b, i, 0)
    const2 = lambda b, i: (0, 0)
    in_specs = [pl.BlockSpec((None, tl, D), row),
                pl.BlockSpec((None, 1, 3 * D), lambda b, i: (b, 0, 0)),
                pl.BlockSpec(w.shape, const2)]
    args = [x, mod, w]
    if dil > 1:
        in_specs.append(pl.BlockSpec((tl, tl), const2))
        args.append(_class_major_perm(tl, dil))
    out_shape = [jax.ShapeDtypeStruct((B, dil, L, n_qkv), jnp.bfloat16)]
    out_specs = [pl.BlockSpec((None, dil, tl // dil, n_qkv), lambda b, i: (b, 0, i, 0))]
    if with_gate:
        out_shape.append(jax.ShapeDtypeStruct((B, S, WIDTH_A), jnp.bfloat16))
        out_specs.append(pl.BlockSpec((None, tl, WIDTH_A), row))
    return pl.pallas_call(
        functools.partial(_proj_a_kernel, dil, with_gate),
        grid=(B, S // tl),
        in_specs=in_specs,
        out_specs=out_specs,
        out_shape=out_shape,
        compiler_params=_params("arbitrary", "arbitrary"),
        name=f"proj_a_d{dil}",
    )(*args)


def _attn_a_kernel(qkv_ref, bias_ref, o_ref, lse_ref):
    L = qkv_ref.shape[0]
    n_tiles = L // TQ_A
    gw = GROUP_WIDTH_A
    lane = lax.broadcasted_iota(jnp.int32, (TQ_A, LANES), 1)

    def tile(t, carry):
        j0 = pl.multiple_of(t * TQ_A, TQ_A)
        ws = pl.multiple_of(jnp.clip(j0 - HALF_WINDOW, 0, L - TK_A), HALF_WINDOW)
        variant = jnp.where(t == 0, 0, jnp.where(t == n_tiles - 1, 2, 1))
        lse_tile = jnp.zeros((TQ_A, LANES), jnp.float32)
        for h in range(HEADS_PER_GROUP_A):
            c0 = h * HEAD_DIM_A
            q = qkv_ref[pl.ds(j0, TQ_A), c0:c0 + HEAD_DIM_A]
            k = qkv_ref[pl.ds(ws, TK_A), gw + c0:gw + c0 + HEAD_DIM_A]
            v = qkv_ref[pl.ds(ws, TK_A), 2 * gw + c0:2 * gw + c0 + HEAD_DIM_A]
            s = lax.dot_general(q, k, _NT, preferred_element_type=jnp.float32)
            s = s + bias_ref[variant, h]
            m = jnp.max(s, axis=-1, keepdims=True)
            p = jnp.exp(s - m)
            denom = jnp.sum(p, axis=-1, keepdims=True)
            o = jnp.dot(p.astype(jnp.bfloat16), v, preferred_element_type=jnp.float32) / denom
            o_ref[pl.ds(j0, TQ_A), c0:c0 + HEAD_DIM_A] = o.astype(o_ref.dtype)
            lse_tile = jnp.where(lane == h, m + jnp.log(denom), lse_tile)
        lse_ref[pl.ds(j0, TQ_A), :] = lse_tile
        return carry

    lax.fori_loop(0, n_tiles, tile, 0)


def _attn_a(qkv, biasm):
    B, dil, L, n_qkv = qkv.shape
    gw = GROUP_WIDTH_A
    cls = lambda b, r: (b, r, 0, 0)
    return pl.pallas_call(
        _attn_a_kernel,
        grid=(B, dil),
        in_specs=[pl.BlockSpec((None, None, L, n_qkv), cls),
                  pl.BlockSpec(biasm.shape, lambda b, r: (0, 0, 0, 0))],
        out_specs=[pl.BlockSpec((None, None, L, gw), cls),
                   pl.BlockSpec((None, None, L, LANES), cls)],
        out_shape=[jax.ShapeDtypeStruct((B, dil, L, gw), jnp.bfloat16),
                   jax.ShapeDtypeStruct((B, dil, L, LANES), jnp.float32)],
        compiler_params=_params("arbitrary", "arbitrary"),
        name=f"attn_a_d{dil}",
    )(qkv, biasm)


def _t5_bucket_np(rel):
    half = N_BUCKETS // 2
    max_exact = half // 2
    base = np.where(rel > 0, half, 0)
    n = np.abs(rel)
    nf = np.maximum(n, 1).astype(np.float32)
    large = max_exact + (np.log(nf / np.float32(max_exact)) / np.float32(math.log(T5_MAX_DISTANCE / max_exact))
                         * np.float32(half - max_exact)).astype(np.int32)
    large = np.minimum(large, half - 1)
    return base + np.where(n < max_exact, n, large)


def _bias_variants(rel_bias_g, dil):
    H = rel_bias_g.shape[1]
    M = 2 * TK_A
    off = np.arange(M) - (TK_A - 1)
    idx = np.where(np.abs(off) <= HALF_WINDOW, _t5_bucket_np(off * dil), N_BUCKETS).astype(np.int32)
    table = jnp.concatenate([rel_bias_g.astype(jnp.float32), jnp.full((1, H), NEG_INF, jnp.float32)], axis=0)
    w = table[jnp.asarray(idx)].T
    rot = jnp.tile(w, (1, TQ_A + 1))[:, :TQ_A * (M + 1)].reshape(H, TQ_A, M + 1)[:, ::-1, :]
    variants = []
    for shift in (0, HALF_WINDOW, 2 * HALF_WINDOW):
        j0 = TK_A - TQ_A - shift
        variants.append(rot[:, :, j0:j0 + TK_A])
    return jnp.stack(variants)


def _split3_bf16(v):
    hi = v.astype(jnp.bfloat16)
    r1 = v - hi.astype(jnp.float32)
    mid = r1.astype(jnp.bfloat16)
    lo = (r1 - mid.astype(jnp.float32)).astype(jnp.bfloat16)
    return hi, mid, lo


def _out_a_kernel(x_ref, mod_ref, o0_ref, o1_ref, o2_ref, l0_ref, l1_ref, l2_ref, gate_ref,
                  unperm1_ref, unperm2_ref, expand_ref, w_ref, g_ref, b_ref, out_ref):
    tl, D = x_ref.shape
    gw = GROUP_WIDTH_A

    def natural_order(o_ref, l_ref, unperm_ref):
        o = o_ref[...].reshape(tl, gw)
        lse = l_ref[...].reshape(tl, LANES)
        if unperm_ref is None:
            return o.astype(jnp.float32), lse
        pt = unperm_ref[...]
        o = jnp.dot(pt, o, preferred_element_type=jnp.float32)
        lse = sum(jnp.dot(pt, part, preferred_element_type=jnp.float32) for part in _split3_bf16(lse))
        return o, lse

    o0, l0 = natural_order(o0_ref, l0_ref, None)
    o1, l1 = natural_order(o1_ref, l1_ref, unperm1_ref)
    o2, l2 = natural_order(o2_ref, l2_ref, unperm2_ref)
    mx = jnp.maximum(jnp.maximum(l0, l1), l2)
    e0, e1, e2 = jnp.exp(l0 - mx), jnp.exp(l1 - mx), jnp.exp(l2 - mx)
    inv = 1.0 / (e0 + e1 + e2)
    y = jnp.zeros((tl, D), jnp.float32)
    for g, (e, o) in enumerate(((e0, o0), (e1, o1), (e2, o2))):
        wt = e * inv
        hi = wt.astype(jnp.bfloat16)
        lo = (wt - hi.astype(jnp.float32)).astype(jnp.bfloat16)
        wexp = (jnp.dot(hi, expand_ref[...], preferred_element_type=jnp.float32)
                + jnp.dot(lo, expand_ref[...], preferred_element_type=jnp.float32))
        gate = gate_ref[:, g * gw:(g + 1) * gw].astype(jnp.float32)
        yg = (o * wexp * _silu(gate)).astype(jnp.bfloat16)
        y = y + jnp.dot(yg, w_ref[g * gw:(g + 1) * gw, :], preferred_element_type=jnp.float32)
    z = DEEPNORM_ALPHA * x_ref[...] + mod_ref[:, 2 * D:] * y
    out_ref[...] = _layernorm(z, g_ref[...], b_ref[...])


def _out_a(x, mod, os_, lses, gate, w_out, ln_g, ln_b):
    B, S, D = x.shape
    tl = TL_PROJ
    gw = GROUP_WIDTH_A
    expand = jnp.asarray(
        (np.arange(LANES)[:, None] == (np.arange(gw)[None, :] // HEAD_DIM_A)).astype(np.float32),
        dtype=jnp.bfloat16)
    unperms = [_class_major_perm(tl, dil, inverse=True) for _, dil in DIL_GROUPS[1:]]
    row = lambda b, i: (b, i, 0)
    cls = lambda b, i: (b, 0, i, 0)
    const2 = lambda b, i: (0, 0)
    o_specs = [pl.BlockSpec((None, dil, tl // dil, gw), cls) for _, dil in DIL_GROUPS]
    l_specs = [pl.BlockSpec((None, dil, tl // dil, LANES), cls) for _, dil in DIL_GROUPS]
    return pl.pallas_call(
        _out_a_kernel,
        grid=(B, S // tl),
        in_specs=[pl.BlockSpec((None, tl, D), row),
                  pl.BlockSpec((None, 1, 3 * D), lambda b, i: (b, 0, 0)),
                  *o_specs, *l_specs,
                  pl.BlockSpec((None, tl, WIDTH_A), row),
                  pl.BlockSpec((tl, tl), const2), pl.BlockSpec((tl, tl), const2),
                  pl.BlockSpec(expand.shape, const2),
                  pl.BlockSpec(w_out.shape, const2),
                  pl.BlockSpec((1, D), const2), pl.BlockSpec((1, D), const2)],
        out_specs=pl.BlockSpec((None, tl, D), row),
        out_shape=jax.ShapeDtypeStruct((B, S, D), jnp.float32),
        compiler_params=_params("arbitrary", "arbitrary"),
        name="out_a",
    )(x, mod, *os_, *lses, gate, *unperms, expand, w_out, ln_g.reshape(1, D), ln_b.reshape(1, D))


def _rmsnorm(x, g):
    return x * lax.rsqrt(jnp.mean(x * x, axis=-1, keepdims=True) + RMS_EPS) * g


def _proj_b_kernel(x_ref, mod_ref, w_in_ref, qn_ref, kvn_ref, wqa_ref, wqb_ref, wk_ref, wv_ref,
                   vone_ref, cq_ref, sq_ref, ck_ref, sk_ref, q_out, k_out, v_out, gate_out):
    u = _modulate(x_ref[...], mod_ref[...]).astype(jnp.bfloat16)
    z = jnp.dot(u, w_in_ref[...], preferred_element_type=jnp.float32)
    c0 = Q_LORA_RANK
    c1 = c0 + KV_LORA_RANK
    cq = z[:, :c0]
    ckv = z[:, c0:c1]
    kr = z[:, c1:c1 + LANES]
    krs = z[:, c1 + LANES:c1 + 2 * LANES]
    gate_out[...] = z[:, c1 + 2 * LANES:].astype(gate_out.dtype)

    cqn = _rmsnorm(cq, qn_ref[...]).astype(jnp.bfloat16)
    qa = jnp.dot(cqn, wqa_ref[...], preferred_element_type=jnp.float32)
    qb = jnp.dot(cqn, wqb_ref[...], preferred_element_type=jnp.float32)
    ckvn = _rmsnorm(ckv, kvn_ref[...]).astype(jnp.bfloat16)
    kn = jnp.dot(ckvn, wk_ref[...], preferred_element_type=jnp.float32)
    v_out[...] = (jnp.dot(ckvn, wv_ref[...], preferred_element_type=jnp.float32)
                  + vone_ref[...]).astype(v_out.dtype)

    cos_q, sin_q = cq_ref[...], sq_ref[...]
    k_rope = kr * ck_ref[...] + krs * sk_ref[...]
    for h in range(N_HEADS_B):
        blk = slice(h * LANES, (h + 1) * LANES)
        q_out[:, blk] = (qa[:, blk] * cos_q + qb[:, blk] * sin_q).astype(q_out.dtype)
        k_out[:, blk] = (kn[:, blk] + k_rope).astype(k_out.dtype)


def _proj_b(x, mod, w_in, q_norm, kv_norm, wqa, wqb, wk, wv, tables):
    B, S, D = x.shape
    tl = TL_PROJ
    HP = N_HEADS_B * LANES
    row = lambda b, i: (b, i, 0)
    const2 = lambda b, i: (0, 0)
    tab = pl.BlockSpec((tl, LANES), lambda b, i: (i, 0))
    wide = jax.ShapeDtypeStruct((B, S, HP), jnp.bfloat16)
    return pl.pallas_call(
        _proj_b_kernel,
        grid=(B, S // tl),
        in_specs=[pl.BlockSpec((None, tl, D), row),
                  pl.BlockSpec((None, 1, 3 * D), lambda b, i: (b, 0, 0)),
                  pl.BlockSpec(w_in.shape, const2),
                  pl.BlockSpec((1, Q_LORA_RANK), const2), pl.BlockSpec((1, KV_LORA_RANK), const2),
                  pl.BlockSpec(wqa.shape, const2), pl.BlockSpec(wqb.shape, const2),
                  pl.BlockSpec(wk.shape, const2), pl.BlockSpec(wv.shape, const2),
                  pl.BlockSpec((1, HP), const2),
                  tab, tab, tab, tab],
        out_specs=[pl.BlockSpec((None, tl, HP), row), pl.BlockSpec((None, tl, HP), row),
                   pl.BlockSpec((None, tl, HP), row), pl.BlockSpec((None, tl, WIDTH_B), row)],
        out_shape=[wide, wide, wide, jax.ShapeDtypeStruct((B, S, WIDTH_B), jnp.bfloat16)],
        compiler_params=_params("arbitrary", "arbitrary"),
        name="proj_b",
    )(x, mod, w_in, q_norm.reshape(1, -1), kv_norm.reshape(1, -1), wqa, wqb, wk, wv,
      _v_ones_row(), *tables)


def _attn_b_kernel(q_ref, k_ref, v_ref, o_ref, s_even, s_odd):
    tq = q_ref.shape[0]
    S = k_ref.shape[0]
    n_chunks = S // TK_B
    s_scr = (s_even, s_odd)

    def head_lanes(e):
        return slice(e * LANES, (e + 1) * LANES)

    def scores(e, c, mx):
        ks = slice(c * TK_B, (c + 1) * TK_B)
        s = lax.dot_general(q_ref[:, head_lanes(e)], k_ref[ks, head_lanes(e)], _NT,
                            preferred_element_type=jnp.float32)
        s_scr[e][:, ks] = s
        for j in range(TK_B // LANES):
            mx = jnp.maximum(mx, s[:, j * LANES:(j + 1) * LANES])
        return mx

    def weighted_values(e, c, m, acc):
        ks = slice(c * TK_B, (c + 1) * TK_B)
        p = jnp.exp2(s_scr[e][:, ks] - m).astype(jnp.bfloat16)
        return acc + jnp.dot(p, v_ref[ks, head_lanes(e)], preferred_element_type=jnp.float32)

    neg = jnp.full((tq, LANES), -jnp.inf, jnp.float32)
    zero = jnp.zeros((tq, LANES), jnp.float32)
    mx_e = neg
    for c in range(n_chunks):
        mx_e = scores(0, c, mx_e)
    m_e = jnp.max(mx_e, axis=-1, keepdims=True)
    mx_o, acc_e = neg, zero
    for c in range(n_chunks):
        mx_o = scores(1, c, mx_o)
        acc_e = weighted_values(0, c, m_e, acc_e)
    m_o = jnp.max(mx_o, axis=-1, keepdims=True)
    acc_o = zero
    for c in range(n_chunks):
        acc_o = weighted_values(1, c, m_o, acc_o)
    inv_e = 1.0 / acc_e[:, V_HEAD_DIM:V_HEAD_DIM + 1]
    inv_o = 1.0 / acc_o[:, 0:1]
    lane = lax.broadcasted_iota(jnp.int32, (tq, LANES), 1)
    o_ref[...] = jnp.where(lane < V_HEAD_DIM, acc_e * inv_e, acc_o * inv_o).astype(o_ref.dtype)


def _attn_b(q2, k2, v2):
    B, S, HP = q2.shape
    n_pairs = N_HEADS_B // 2
    pw = 2 * LANES
    return pl.pallas_call(
        _attn_b_kernel,
        grid=(B, n_pairs, S // TQ_B),
        in_specs=[pl.BlockSpec((None, TQ_B, pw), lambda b, p, i: (b, i, p)),
                  pl.BlockSpec((None, S, pw), lambda b, p, i: (b, 0, p)),
                  pl.BlockSpec((None, S, pw), lambda b, p, i: (b, 0, p))],
        out_specs=pl.BlockSpec((None, TQ_B, LANES), lambda b, p, i: (b, i, p)),
        out_shape=jax.ShapeDtypeStruct((B, S, WIDTH_B), jnp.bfloat16),
        scratch_shapes=[pltpu.VMEM((TQ_B, S), jnp.float32), pltpu.VMEM((TQ_B, S), jnp.float32)],
        compiler_params=_params("arbitrary", "arbitrary", "arbitrary"),
        name="attn_b",
    )(q2, k2, v2)


def _out_b_kernel(x_ref, mod_ref, o_ref, gate_ref, w_ref, g_ref, b_ref, out_ref):
    D = x_ref.shape[-1]
    yg = (o_ref[...].astype(jnp.float32) * _silu(gate_ref[...].astype(jnp.float32))).astype(jnp.bfloat16)
    y = jnp.dot(yg, w_ref[...], preferred_element_type=jnp.float32)
    z = DEEPNORM_ALPHA * x_ref[...] + mod_ref[:, 2 * D:] * y
    out_ref[...] = _layernorm(z, g_ref[...], b_ref[...])


def _out_b(x, mod, o, gate, w_out, ln_g, ln_b):
    B, S, D = x.shape
    tl = TL_PROJ
    row = lambda b, i: (b, i, 0)
    const2 = lambda b, i: (0, 0)
    return pl.pallas_call(
        _out_b_kernel,
        grid=(B, S // tl),
        in_specs=[pl.BlockSpec((None, tl, D), row),
                  pl.BlockSpec((None, 1, 3 * D), lambda b, i: (b, 0, 0)),
                  pl.BlockSpec((None, tl, WIDTH_B), row), pl.BlockSpec((None, tl, WIDTH_B), row),
                  pl.BlockSpec(w_out.shape, const2),
                  pl.BlockSpec((1, D), const2), pl.BlockSpec((1, D), const2)],
        out_specs=pl.BlockSpec((None, tl, D), row),
        out_shape=jax.ShapeDtypeStruct((B, S, D), jnp.float32),
        compiler_params=_params("arbitrary", "arbitrary"),
        name="out_b",
    )(x, mod, o, gate, w_out, ln_g.reshape(1, D), ln_b.reshape(1, D))


def _weights_a(a_w_in):
    W = WIDTH_A
    gw = GROUP_WIDTH_A
    q, k, v, gate = (a_w_in[:, i * W:(i + 1) * W] for i in range(4))
    ws = []
    for g in range(len(DIL_GROUPS)):
        cs = slice(g * gw, (g + 1) * gw)
        parts = [q[:, cs] * (HEAD_DIM_A ** -0.5), k[:, cs], v[:, cs]]
        if g == 0:
            parts.append(gate)
        ws.append(jnp.concatenate(parts, axis=1).astype(jnp.bfloat16))
    return ws


def _swap_halves(w):
    half = w.shape[-1] // 2
    return jnp.concatenate([w[..., half:], w[..., :half]], axis=-1)


def _pad_lanes(w, lo):
    n = w.shape[-1]
    return jnp.pad(w, [(0, 0)] * (w.ndim - 1) + [(lo, LANES - lo - n)])


def _weights_b(b_w_in, b_w_uq, b_w_ukv):
    c0 = Q_LORA_RANK
    c1 = c0 + KV_LORA_RANK
    c2 = c1 + QK_ROPE_DIM
    kr = b_w_in[:, c1:c2]
    w_in = jnp.concatenate([b_w_in[:, :c1], _pad_lanes(kr, QK_NOPE_DIM),
                            _pad_lanes(_swap_halves(kr), QK_NOPE_DIM), b_w_in[:, c2:]], axis=1)
    uq = b_w_uq.reshape(Q_LORA_RANK, N_HEADS_B, QK_NOPE_DIM + QK_ROPE_DIM)
    q_rope = uq[..., QK_NOPE_DIM:]
    wqa = _pad_lanes(uq, 0).reshape(Q_LORA_RANK, -1)
    wqb = _pad_lanes(_swap_halves(q_rope), QK_NOPE_DIM).reshape(Q_LORA_RANK, -1)
    ukv = b_w_ukv.reshape(KV_LORA_RANK, N_HEADS_B, QK_NOPE_DIM + V_HEAD_DIM)
    wk = _pad_lanes(ukv[..., :QK_NOPE_DIM], 0).reshape(KV_LORA_RANK, -1)
    vv = ukv[..., QK_NOPE_DIM:]
    odd = (jnp.arange(N_HEADS_B) % 2 == 1)[None, :, None]
    wv = jnp.where(odd, _pad_lanes(vv, LANES - V_HEAD_DIM), _pad_lanes(vv, 0)).reshape(KV_LORA_RANK, -1)
    bf = lambda w: w.astype(jnp.bfloat16)
    return bf(w_in), bf(wqa), bf(wqb), bf(wk), bf(wv)


def _rope_tables(S):
    pos = jnp.arange(S, dtype=jnp.float32)
    inv_freq = ROPE_THETA ** (-jnp.arange(0, QK_ROPE_DIM, 2, dtype=jnp.float32) / QK_ROPE_DIM)
    ang = pos[:, None] * inv_freq[None, :]
    cos, sin = jnp.cos(ang), jnp.sin(ang)
    cosf = _pad_lanes(jnp.concatenate([cos, cos], axis=-1), QK_NOPE_DIM)
    sinf = _pad_lanes(jnp.concatenate([-sin, sin], axis=-1), QK_NOPE_DIM)
    nope = _pad_lanes(jnp.ones((S, QK_NOPE_DIM), jnp.float32), 0)
    scale = (QK_NOPE_DIM + QK_ROPE_DIM) ** -0.5
    scale = scale * math.log2(math.e)
    return (scale * (nope + cosf), scale * sinf, cosf, sinf)


def _v_ones_row():
    row = np.zeros((N_HEADS_B, LANES), np.float32)
    row[0::2, V_HEAD_DIM] = 1.0
    row[1::2, 0] = 1.0
    return jnp.asarray(row.reshape(1, -1))


def kernel(x, c, rel_bias, ada_w, ada_b, ln_g, ln_b, a_w_in, a_w_out,
           b_w_in, b_q_norm, b_w_uq, b_kv_norm, b_w_ukv, b_w_out):
    B, S, D = x.shape
    mods = _modulation(c, ada_w, ada_b)
    mod0 = mods[0].reshape(B, 1, 3 * D)
    mod1 = mods[1].reshape(B, 1, 3 * D)

    os_, lses, gate_a = [], [], None
    for g, ((window, dil), w) in enumerate(zip(DIL_GROUPS, _weights_a(a_w_in[0]))):
        assert window // (2 * dil) == HALF_WINDOW
        outs = _proj_a(x, mod0, w, dil, with_gate=(g == 0))
        if g == 0:
            gate_a = outs[1]
        hs = slice(g * HEADS_PER_GROUP_A, (g + 1) * HEADS_PER_GROUP_A)
        o, lse = _attn_a(outs[0], _bias_variants(rel_bias[:, hs], dil))
        os_.append(o)
        lses.append(lse)
    x1 = _out_a(x, mod0, os_, lses, gate_a, a_w_out[0].astype(jnp.bfloat16), ln_g[0], ln_b[0])

    w_in, wqa, wqb, wk, wv = _weights_b(b_w_in[0], b_w_uq[0], b_w_ukv[0])
    q2, k2, v2, gate_b = _proj_b(x1, mod1, w_in, b_q_norm[0], b_kv_norm[0], wqa, wqb, wk, wv,
                                 _rope_tables(S))
    o = _attn_b(q2, k2, v2)
    return _out_b(x1, mod1, o, gate_b, b_w_out[0].astype(jnp.bfloat16), ln_g[1], ln_b[1])
```

```python
import functools
import math

import numpy as np
import jax
import jax.numpy as jnp
from jax import lax
from jax.experimental import pallas as pl
from jax.experimental.pallas import tpu as pltpu

D_MODEL = 1024
DEPTH = 2
HEAD_DIM_A = 64
DIL_GROUPS = ((128, 1), (512, 4), (2048, 16))
HEADS_PER_GROUP_A = 6
GROUP_WIDTH_A = HEADS_PER_GROUP_A * HEAD_DIM_A
WIDTH_A = GROUP_WIDTH_A * len(DIL_GROUPS)
N_BUCKETS = 32
T5_MAX_DISTANCE = 1024
N_HEADS_B = 16
QK_NOPE_DIM = 64
QK_ROPE_DIM = 32
V_HEAD_DIM = 64
Q_LORA_RANK = 256
KV_LORA_RANK = 128
WIDTH_B = N_HEADS_B * V_HEAD_DIM
ROPE_THETA = 10000.0
DEEPNORM_ALPHA = (2.0 * DEPTH) ** 0.25
LN_EPS = 1e-5
RMS_EPS = 1e-6
NEG_INF = -1e30

LANES = 128
VMEM_LIMIT_BYTES = 56 * 1024 * 1024

HALF_WINDOW = 64
TQ_A = 128
TK_A = TQ_A + 2 * HALF_WINDOW
TILES_PER_STEP_A = 2
TL_PROJ = 512
TQ_B = 512
TK_B = 512

_NT = (((1,), (1,)), ((), ()))


def _silu(x):
    return x * (1.0 / (1.0 + jnp.exp(-x)))


def _params(*sem):
    return pltpu.CompilerParams(dimension_semantics=sem, vmem_limit_bytes=VMEM_LIMIT_BYTES)


def _mod_kernel(c_ref, w_ref, b_ref, o_ref):
    sc = _silu(c_ref[...]).astype(jnp.bfloat16)
    o_ref[...] = jnp.dot(sc, w_ref[...].astype(jnp.bfloat16),
                         preferred_element_type=jnp.float32) + b_ref[...]


def _modulation(c, ada_w, ada_b):
    B, D = c.shape
    nj = 3
    return pl.pallas_call(
        _mod_kernel,
        grid=(DEPTH, nj),
        in_specs=[pl.BlockSpec((B, D), lambda i, j: (0, 0)),
                  pl.BlockSpec((None, D, D), lambda i, j: (i, 0, j)),
                  pl.BlockSpec((None, 1, D), lambda i, j: (i, 0, j))],
        out_specs=pl.BlockSpec((None, B, D), lambda i, j: (i, 0, j)),
        out_shape=jax.ShapeDtypeStruct((DEPTH, B, 3 * D), jnp.float32),
        compiler_params=_params("arbitrary", "arbitrary"),
        name="modulation",
    )(c, ada_w, ada_b.reshape(DEPTH, 1, 3 * D))


def _modulate(x, mod):
    D = x.shape[-1]
    return x * (1.0 + mod[:, D:2 * D]) + mod[:, :D]


def _layernorm(z, g, b):
    mu = jnp.mean(z, axis=-1, keepdims=True)
    zc = z - mu
    var = jnp.mean(zc * zc, axis=-1, keepdims=True)
    return zc * lax.rsqrt(var + LN_EPS) * g + b


def _class_major_perm(tl, dil, inverse=False):
    p = np.arange(tl)
    src = (p % (tl // dil)) * dil + p // (tl // dil)
    mat = (src[:, None] == np.arange(tl)[None, :]).astype(np.float32)
    return jnp.asarray(mat.T if inverse else mat, dtype=jnp.bfloat16)


def _proj_a_kernel(dil, with_gate, x_ref, mod_ref, w_ref, *refs):
    u = _modulate(x_ref[...], mod_ref[...]).astype(jnp.bfloat16)
    if dil > 1:
        perm_ref, refs = refs[0], refs[1:]
        u = jnp.dot(perm_ref[...], u, preferred_element_type=jnp.float32).astype(jnp.bfloat16)
    z = jnp.dot(u, w_ref[...], preferred_element_type=jnp.float32)
    qkv_ref = refs[0]
    rows, n_qkv = qkv_ref.shape[-2:]
    for r in range(dil):
        qkv_ref[r] = z[r * rows:(r + 1) * rows, :n_qkv].astype(qkv_ref.dtype)
    if with_gate:
        refs[1][...] = z[:, n_qkv:].astype(refs[1].dtype)


def _proj_a(x, mod, w, dil, with_gate):
    B, S, D = x.shape
    L = S // dil
    tl = TL_PROJ if dil == 1 else TL_PROJ // 2
    n_qkv = 3 * GROUP_WIDTH_A
    row = lambda b, i: (b, i, 0)
    const2 = lambda b, i: (0, 0)
    in_specs = [pl.BlockSpec((None, tl, D), row),
                pl.BlockSpec((None, 1, 3 * D), lambda b, i: (b, 0, 0)),
                pl.BlockSpec(w.shape, const2)]
    args = [x, mod, w]
    if dil > 1:
        in_specs.append(pl.BlockSpec((tl, tl), const2))
        args.append(_class_major_perm(tl, dil))
    out_shape = [jax.ShapeDtypeStruct((B, dil, L, n_qkv), jnp.bfloat16)]
    out_specs = [pl.BlockSpec((None, dil, tl // dil, n_qkv), lambda b, i: (b, 0, i, 0))]
    if with_gate:
        out_shape.append(jax.ShapeDtypeStruct((B, S, WIDTH_A), jnp.bfloat16))
        out_specs.append(pl.BlockSpec((None, tl, WIDTH_A), row))
    return pl.pallas_call(
        functools.partial(_proj_a_kernel, dil, with_gate),
        grid=(B, S // tl),
        in_specs=in_specs,
        out_specs=out_specs,
        out_shape=out_shape,
        compiler_params=_params("arbitrary", "arbitrary"),
        name=f"proj_a_d{dil}",
    )(*args)


def _attn_a_kernel(qkv_ref, bias_ref, o_ref, lse_ref, s_scr, p_scr):
    L = qkv_ref.shape[0]
    n_tiles = L // TQ_A
    gw = GROUP_WIDTH_A
    lane = lax.broadcasted_iota(jnp.int32, (TQ_A, LANES), 1)
    low_q = lane < HEAD_DIM_A
    low_k = lax.broadcasted_iota(jnp.int32, (TK_A, LANES), 1) < HEAD_DIM_A
    ones = jnp.ones((TK_A, LANES), jnp.bfloat16)

    heads = range(HEADS_PER_GROUP_A)

    def head_operand(ref, rows, section, h):
        c0 = section * gw + (h // 2) * LANES
        both = ref[rows, c0:c0 + LANES]
        mine = low_k if h % 2 == 0 else jnp.logical_not(low_k)
        return jnp.where(mine, both, jnp.zeros_like(both))

    def tile_group(tg, carry):
        geo = []
        for u in range(TILES_PER_STEP_A):
            t = tg * TILES_PER_STEP_A + u
            j0 = pl.multiple_of(t * TQ_A, TQ_A)
            ws = pl.multiple_of(jnp.clip(j0 - HALF_WINDOW, 0, L - TK_A), HALF_WINDOW)
            variant = jnp.where(t == 0, 0, jnp.where(t == n_tiles - 1, 2, 1))
            geo.append((pl.ds(j0, TQ_A), pl.ds(ws, TK_A), variant))
        for u, (rq, rk, variant) in enumerate(geo):
            for h in heads:
                c0 = (h // 2) * LANES
                s = lax.dot_general(qkv_ref[rq, c0:c0 + LANES], head_operand(qkv_ref, rk, 1, h), _NT,
                                    preferred_element_type=jnp.float32)
                s_scr[u, h] = s + bias_ref[variant, h]
        m = [[jnp.max(s_scr[u, h], axis=-1, keepdims=True) for h in heads] for u in range(len(geo))]
        for u in range(len(geo)):
            for h in heads:
                p_scr[u, h] = jnp.exp(s_scr[u, h] - m[u][h]).astype(jnp.bfloat16)
        for u, (rq, rk, variant) in enumerate(geo):
            lse_tile = jnp.zeros((TQ_A, LANES), jnp.float32)
            for pair in range(HEADS_PER_GROUP_A // 2):
                acc = jnp.zeros((TQ_A, LANES), jnp.float32)
                inv = []
                for h in (2 * pair, 2 * pair + 1):
                    p = p_scr[u, h]
                    denom = jnp.dot(p, ones, preferred_element_type=jnp.float32)
                    acc = acc + jnp.dot(p, head_operand(qkv_ref, rk, 2, h), preferred_element_type=jnp.float32)
                    inv.append(1.0 / denom)
                    lse_tile = jnp.where(lane == h, m[u][h] + jnp.log(denom), lse_tile)
                o = acc * jnp.where(low_q, inv[0], inv[1])
                o_ref[rq, pair * LANES:(pair + 1) * LANES] = o.astype(o_ref.dtype)
            lse_ref[rq, :] = lse_tile
        return carry

    lax.fori_loop(0, n_tiles // TILES_PER_STEP_A, tile_group, 0)


def _attn_a(qkv, biasm):
    B, dil, L, n_qkv = qkv.shape
    gw = GROUP_WIDTH_A
    cls = lambda b, r: (b, r, 0, 0)
    return pl.pallas_call(
        _attn_a_kernel,
        grid=(B, dil),
        in_specs=[pl.BlockSpec((None, None, L, n_qkv), cls),
                  pl.BlockSpec(biasm.shape, lambda b, r: (0, 0, 0, 0))],
        out_specs=[pl.BlockSpec((None, None, L, gw), cls),
                   pl.BlockSpec((None, None, L, LANES), cls)],
        out_shape=[jax.ShapeDtypeStruct((B, dil, L, gw), jnp.bfloat16),
                   jax.ShapeDtypeStruct((B, dil, L, LANES), jnp.float32)],
        scratch_shapes=[
            pltpu.VMEM((TILES_PER_STEP_A, HEADS_PER_GROUP_A, TQ_A, TK_A), jnp.float32),
            pltpu.VMEM((TILES_PER_STEP_A, HEADS_PER_GROUP_A, TQ_A, TK_A), jnp.bfloat16)],
        compiler_params=_params("arbitrary", "arbitrary"),
        name=f"attn_a_d{dil}",
    )(qkv, biasm)


def _t5_bucket_np(rel):
    half = N_BUCKETS // 2
    max_exact = half // 2
    base = np.where(rel > 0, half, 0)
    n = np.abs(rel)
    nf = np.maximum(n, 1).astype(np.float32)
    large = max_exact + (np.log(nf / np.float32(max_exact)) / np.float32(math.log(T5_MAX_DISTANCE / max_exact))
                         * np.float32(half - max_exact)).astype(np.int32)
    large = np.minimum(large, half - 1)
    return base + np.where(n < max_exact, n, large)


def _bias_variants(rel_bias_g, dil):
    H = rel_bias_g.shape[1]
    M = 2 * TK_A
    off = np.arange(M) - (TK_A - 1)
    idx = np.where(np.abs(off) <= HALF_WINDOW, _t5_bucket_np(off * dil), N_BUCKETS).astype(np.int32)
    table = jnp.concatenate([rel_bias_g.astype(jnp.float32), jnp.full((1, H), NEG_INF, jnp.float32)], axis=0)
    w = table[jnp.asarray(idx)].T
    rot = jnp.tile(w, (1, TQ_A + 1))[:, :TQ_A * (M + 1)].reshape(H, TQ_A, M + 1)[:, ::-1, :]
    variants = []
    for shift in (0, HALF_WINDOW, 2 * HALF_WINDOW):
        j0 = TK_A - TQ_A - shift
        variants.append(rot[:, :, j0:j0 + TK_A])
    return jnp.stack(variants)


def _split3_bf16(v):
    hi = v.astype(jnp.bfloat16)
    r1 = v - hi.astype(jnp.float32)
    mid = r1.astype(jnp.bfloat16)
    lo = (r1 - mid.astype(jnp.float32)).astype(jnp.bfloat16)
    return hi, mid, lo


def _out_a_kernel(x_ref, mod_ref, o0_ref, o1_ref, o2_ref, l0_ref, l1_ref, l2_ref, gate_ref,
                  unperm1_ref, unperm2_ref, expand_ref, w_ref, g_ref, b_ref, out_ref):
    tl, D = x_ref.shape
    gw = GROUP_WIDTH_A

    def natural_order(o_ref, l_ref, unperm_ref):
        o = o_ref[...].reshape(tl, gw)
        lse = l_ref[...].reshape(tl, LANES)
        if unperm_ref is None:
            return o.astype(jnp.float32), lse
        pt = unperm_ref[...]
        o = jnp.dot(pt, o, preferred_element_type=jnp.float32)
        lse = sum(jnp.dot(pt, part, preferred_element_type=jnp.float32) for part in _split3_bf16(lse))
        return o, lse

    o0, l0 = natural_order(o0_ref, l0_ref, None)
    o1, l1 = natural_order(o1_ref, l1_ref, unperm1_ref)
    o2, l2 = natural_order(o2_ref, l2_ref, unperm2_ref)
    mx = jnp.maximum(jnp.maximum(l0, l1), l2)
    e0, e1, e2 = jnp.exp(l0 - mx), jnp.exp(l1 - mx), jnp.exp(l2 - mx)
    inv = 1.0 / (e0 + e1 + e2)
    y = jnp.zeros((tl, D), jnp.float32)
    for g, (e, o) in enumerate(((e0, o0), (e1, o1), (e2, o2))):
        wt = e * inv
        hi = wt.astype(jnp.bfloat16)
        lo = (wt - hi.astype(jnp.float32)).astype(jnp.bfloat16)
        wexp = (jnp.dot(hi, expand_ref[...], preferred_element_type=jnp.float32)
                + jnp.dot(lo, expand_ref[...], preferred_element_type=jnp.float32))
        gate = gate_ref[:, g * gw:(g + 1) * gw].astype(jnp.float32)
        yg = (o * wexp * _silu(gate)).astype(jnp.bfloat16)
        y = y + jnp.dot(yg, w_ref[g * gw:(g + 1) * gw, :], preferred_element_type=jnp.float32)
    z = DEEPNORM_ALPHA * x_ref[...] + mod_ref[:, 2 * D:] * y
    out_ref[...] = _layernorm(z, g_ref[...], b_ref[...])


def _out_a(x, mod, os_, lses, gate, w_out, ln_g, ln_b):
    B, S, D = x.shape
    tl = TL_PROJ
    gw = GROUP_WIDTH_A
    expand = jnp.asarray(
        (np.arange(LANES)[:, None] == (np.arange(gw)[None, :] // HEAD_DIM_A)).astype(np.float32),
        dtype=jnp.bfloat16)
    unperms = [_class_major_perm(tl, dil, inverse=True) for _, dil in DIL_GROUPS[1:]]
    row = lambda b, i: (b, i, 0)
    cls = lambda b, i: (b, 0, i, 0)
    const2 = lambda b, i: (0, 0)
    o_specs = [pl.BlockSpec((None, dil, tl // dil, gw), cls) for _, dil in DIL_GROUPS]
    l_specs = [pl.BlockSpec((None, dil, tl // dil, LANES), cls) for _, dil in DIL_GROUPS]
    return pl.pallas_call(
        _out_a_kernel,
        grid=(B, S // tl),
        in_specs=[pl.BlockSpec((None, tl, D), row),
                  pl.BlockSpec((None, 1, 3 * D), lambda b, i: (b, 0, 0)),
                  *o_specs, *l_specs,
                  pl.BlockSpec((None, tl, WIDTH_A), row),
                  pl.BlockSpec((tl, tl), const2), pl.BlockSpec((tl, tl), const2),
                  pl.BlockSpec(expand.shape, const2),
                  pl.BlockSpec(w_out.shape, const2),
                  pl.BlockSpec((1, D), const2), pl.BlockSpec((1, D), const2)],
        out_specs=pl.BlockSpec((None, tl, D), row),
        out_shape=jax.ShapeDtypeStruct((B, S, D), jnp.float32),
        compiler_params=_params("arbitrary", "arbitrary"),
        name="out_a",
    )(x, mod, *os_, *lses, gate, *unperms, expand, w_out, ln_g.reshape(1, D), ln_b.reshape(1, D))


def _rmsnorm(x, g):
    return x * lax.rsqrt(jnp.mean(x * x, axis=-1, keepdims=True) + RMS_EPS) * g


def _proj_b_kernel(x_ref, mod_ref, w_in_ref, qn_ref, kvn_ref, wqa_ref, wqb_ref, wk_ref, wv_ref,
                   vone_ref, cq_ref, sq_ref, ck_ref, sk_ref, q_out, k_out, v_out, gate_out):
    u = _modulate(x_ref[...], mod_ref[...]).astype(jnp.bfloat16)
    z = jnp.dot(u, w_in_ref[...], preferred_element_type=jnp.float32)
    c0 = Q_LORA_RANK
    c1 = c0 + KV_LORA_RANK
    cq = z[:, :c0]
    ckv = z[:, c0:c1]
    kr = z[:, c1:c1 + LANES]
    krs = z[:, c1 + LANES:c1 + 2 * LANES]
    gate_out[...] = z[:, c1 + 2 * LANES:].astype(gate_out.dtype)

    cqn = _rmsnorm(cq, qn_ref[...]).astype(jnp.bfloat16)
    qa = jnp.dot(cqn, wqa_ref[...], preferred_element_type=jnp.float32)
    qb = jnp.dot(cqn, wqb_ref[...], preferred_element_type=jnp.float32)
    ckvn = _rmsnorm(ckv, kvn_ref[...]).astype(jnp.bfloat16)
    kn = jnp.dot(ckvn, wk_ref[...], preferred_element_type=jnp.float32)
    v_out[...] = (jnp.dot(ckvn, wv_ref[...], preferred_element_type=jnp.float32)
                  + vone_ref[...]).astype(v_out.dtype)

    cos_q, sin_q = cq_ref[...], sq_ref[...]
    k_rope = kr * ck_ref[...] + krs * sk_ref[...]
    for h in range(N_HEADS_B):
        blk = slice(h * LANES, (h + 1) * LANES)
        q_out[:, blk] = (qa[:, blk] * cos_q + qb[:, blk] * sin_q).astype(q_out.dtype)
        k_out[:, blk] = (kn[:, blk] + k_rope).astype(k_out.dtype)


def _proj_b(x, mod, w_in, q_norm, kv_norm, wqa, wqb, wk, wv, tables):
    B, S, D = x.shape
    tl = TL_PROJ
    HP = N_HEADS_B * LANES
    row = lambda b, i: (b, i, 0)
    const2 = lambda b, i: (0, 0)
    tab = pl.BlockSpec((tl, LANES), lambda b, i: (i, 0))
    wide = jax.ShapeDtypeStruct((B, S, HP), jnp.bfloat16)
    return pl.pallas_call(
        _proj_b_kernel,
        grid=(B, S // tl),
        in_specs=[pl.BlockSpec((None, tl, D), row),
                  pl.BlockSpec((None, 1, 3 * D), lambda b, i: (b, 0, 0)),
                  pl.BlockSpec(w_in.shape, const2),
                  pl.BlockSpec((1, Q_LORA_RANK), const2), pl.BlockSpec((1, KV_LORA_RANK), const2),
                  pl.BlockSpec(wqa.shape, const2), pl.BlockSpec(wqb.shape, const2),
                  pl.BlockSpec(wk.shape, const2), pl.BlockSpec(wv.shape, const2),
                  pl.BlockSpec((1, HP), const2),
                  tab, tab, tab, tab],
        out_specs=[pl.BlockSpec((None, tl, HP), row), pl.BlockSpec((None, tl, HP), row),
                   pl.BlockSpec((None, tl, HP), row), pl.BlockSpec((None, tl, WIDTH_B), row)],
        out_shape=[wide, wide, wide, jax.ShapeDtypeStruct((B, S, WIDTH_B), jnp.bfloat16)],
        compiler_params=_params("arbitrary", "arbitrary"),
        name="proj_b",
    )(x, mod, w_in, q_norm.reshape(1, -1), kv_norm.reshape(1, -1), wqa, wqb, wk, wv,
      _v_ones_row(), *tables)


def _attn_b_kernel(q_ref, k_ref, v_ref, o_ref, s_even, s_odd):
    tq = q_ref.shape[0]
    S = k_ref.shape[0]
    n_chunks = S // TK_B
    s_scr = (s_even, s_odd)

    def head_lanes(e):
        return slice(e * LANES, (e + 1) * LANES)

    def scores(e, c, mx):
        ks = slice(c * TK_B, (c + 1) * TK_B)
        s = lax.dot_general(q_ref[:, head_lanes(e)], k_ref[ks, head_lanes(e)], _NT,
                            preferred_element_type=jnp.float32)
        s_scr[e][:, ks] = s
        for j in range(TK_B // LANES):
            mx = jnp.maximum(mx, s[:, j * LANES:(j + 1) * LANES])
        return mx

    def weighted_values(e, c, m, acc):
        ks = slice(c * TK_B, (c + 1) * TK_B)
        p = jnp.exp2(s_scr[e][:, ks] - m).astype(jnp.bfloat16)
        return acc + jnp.dot(p, v_ref[ks, head_lanes(e)], preferred_element_type=jnp.float32)

    neg = jnp.full((tq, LANES), -jnp.inf, jnp.float32)
    zero = jnp.zeros((tq, LANES), jnp.float32)
    mx_e = neg
    for c in range(n_chunks):
        mx_e = scores(0, c, mx_e)
    m_e = jnp.max(mx_e, axis=-1, keepdims=True)
    mx_o, acc_e = neg, zero
    for c in range(n_chunks):
        mx_o = scores(1, c, mx_o)
        acc_e = weighted_values(0, c, m_e, acc_e)
    m_o = jnp.max(mx_o, axis=-1, keepdims=True)
    acc_o = zero
    for c in range(n_chunks):
        acc_o = weighted_values(1, c, m_o, acc_o)
    inv_e = 1.0 / acc_e[:, V_HEAD_DIM:V_HEAD_DIM + 1]
    inv_o = 1.0 / acc_o[:, 0:1]
    lane = lax.broadcasted_iota(jnp.int32, (tq, LANES), 1)
    o_ref[...] = jnp.where(lane < V_HEAD_DIM, acc_e * inv_e, acc_o * inv_o).astype(o_ref.dtype)


def _attn_b(q2, k2, v2):
    B, S, HP = q2.shape
    n_pairs = N_HEADS_B // 2
    pw = 2 * LANES
    return pl.pallas_call(
        _attn_b_kernel,
        grid=(B, n_pairs, S // TQ_B),
        in_specs=[pl.BlockSpec((None, TQ_B, pw), lambda b, p, i: (b, i, p)),
                  pl.BlockSpec((None, S, pw), lambda b, p, i: (b, 0, p)),
                  pl.BlockSpec((None, S, pw), lambda b, p, i: (b, 0, p))],
        out_specs=pl.BlockSpec((None, TQ_B, LANES), lambda b, p, i: (b, i, p)),
        out_shape=jax.ShapeDtypeStruct((B, S, WIDTH_B), jnp.bfloat16),
        scratch_shapes=[pltpu.VMEM((TQ_B, S), jnp.float32), pltpu.VMEM((TQ_B, S), jnp.float32)],
        compiler_params=_params("arbitrary", "arbitrary", "arbitrary"),
        name="attn_b",
    )(q2, k2, v2)


def _out_b_kernel(x_ref, mod_ref, o_ref, gate_ref, w_ref, g_ref, b_ref, out_ref):
    D = x_ref.shape[-1]
    yg = (o_ref[...].astype(jnp.float32) * _silu(gate_ref[...].astype(jnp.float32))).astype(jnp.bfloat16)
    y = jnp.dot(yg, w_ref[...], preferred_element_type=jnp.float32)
    z = DEEPNORM_ALPHA * x_ref[...] + mod_ref[:, 2 * D:] * y
    out_ref[...] = _layernorm(z, g_ref[...], b_ref[...])


def _out_b(x, mod, o, gate, w_out, ln_g, ln_b):
    B, S, D = x.shape
    tl = TL_PROJ
    row = lambda b, i: (b, i, 0)
    const2 = lambda b, i: (0, 0)
    return pl.pallas_call(
        _out_b_kernel,
        grid=(B, S // tl),
        in_specs=[pl.BlockSpec((None, tl, D), row),
                  pl.BlockSpec((None, 1, 3 * D), lambda b, i: (b, 0, 0)),
                  pl.BlockSpec((None, tl, WIDTH_B), row), pl.BlockSpec((None, tl, WIDTH_B), row),
                  pl.BlockSpec(w_out.shape, const2),
                  pl.BlockSpec((1, D), const2), pl.BlockSpec((1, D), const2)],
        out_specs=pl.BlockSpec((None, tl, D), row),
        out_shape=jax.ShapeDtypeStruct((B, S, D), jnp.float32),
        compiler_params=_params("arbitrary", "arbitrary"),
        name="out_b",
    )(x, mod, o, gate, w_out, ln_g.reshape(1, D), ln_b.reshape(1, D))


def _weights_a(a_w_in):
    W = WIDTH_A
    gw = GROUP_WIDTH_A
    q, k, v, gate = (a_w_in[:, i * W:(i + 1) * W] for i in range(4))
    ws = []
    for g in range(len(DIL_GROUPS)):
        cs = slice(g * gw, (g + 1) * gw)
        parts = [q[:, cs] * (HEAD_DIM_A ** -0.5), k[:, cs], v[:, cs]]
        if g == 0:
            parts.append(gate)
        ws.append(jnp.concatenate(parts, axis=1).astype(jnp.bfloat16))
    return ws


def _swap_halves(w):
    half = w.shape[-1] // 2
    return jnp.concatenate([w[..., half:], w[..., :half]], axis=-1)


def _pad_lanes(w, lo):
    n = w.shape[-1]
    return jnp.pad(w, [(0, 0)] * (w.ndim - 1) + [(lo, LANES - lo - n)])


def _weights_b(b_w_in, b_w_uq, b_w_ukv):
    c0 = Q_LORA_RANK
    c1 = c0 + KV_LORA_RANK
    c2 = c1 + QK_ROPE_DIM
    kr = b_w_in[:, c1:c2]
    w_in = jnp.concatenate([b_w_in[:, :c1], _pad_lanes(kr, QK_NOPE_DIM),
                            _pad_lanes(_swap_halves(kr), QK_NOPE_DIM), b_w_in[:, c2:]], axis=1)
    uq = b_w_uq.reshape(Q_LORA_RANK, N_HEADS_B, QK_NOPE_DIM + QK_ROPE_DIM)
    q_rope = uq[..., QK_NOPE_DIM:]
    wqa = _pad_lanes(uq, 0).reshape(Q_LORA_RANK, -1)
    wqb = _pad_lanes(_swap_halves(q_rope), QK_NOPE_DIM).reshape(Q_LORA_RANK, -1)
    ukv = b_w_ukv.reshape(KV_LORA_RANK, N_HEADS_B, QK_NOPE_DIM + V_HEAD_DIM)
    wk = _pad_lanes(ukv[..., :QK_NOPE_DIM], 0).reshape(KV_LORA_RANK, -1)
    vv = ukv[..., QK_NOPE_DIM:]
    odd = (jnp.arange(N_HEADS_B) % 2 == 1)[None, :, None]
    wv = jnp.where(odd, _pad_lanes(vv, LANES - V_HEAD_DIM), _pad_lanes(vv, 0)).reshape(KV_LORA_RANK, -1)
    bf = lambda w: w.astype(jnp.bfloat16)
    return bf(w_in), bf(wqa), bf(wqb), bf(wk), bf(wv)


def _rope_tables(S):
    pos = jnp.arange(S, dtype=jnp.float32)
    inv_freq = ROPE_THETA ** (-jnp.arange(0, QK_ROPE_DIM, 2, dtype=jnp.float32) / QK_ROPE_DIM)
    ang = pos[:, None] * inv_freq[None, :]
    cos, sin = jnp.cos(ang), jnp.sin(ang)
    cosf = _pad_lanes(jnp.concatenate([cos, cos], axis=-1), QK_NOPE_DIM)
    sinf = _pad_lanes(jnp.concatenate([-sin, sin], axis=-1), QK_NOPE_DIM)
    nope = _pad_lanes(jnp.ones((S, QK_NOPE_DIM), jnp.float32), 0)
    scale = (QK_NOPE_DIM + QK_ROPE_DIM) ** -0.5
    scale = scale * math.log2(math.e)
    return (scale * (nope + cosf), scale * sinf, cosf, sinf)


def _v_ones_row():
    row = np.zeros((N_HEADS_B, LANES), np.float32)
    row[0::2, V_HEAD_DIM] = 1.0
    row[1::2, 0] = 1.0
    return jnp.asarray(row.reshape(1, -1))


def kernel(x, c, rel_bias, ada_w, ada_b, ln_g, ln_b, a_w_in, a_w_out,
           b_w_in, b_q_norm, b_w_uq, b_kv_norm, b_w_ukv, b_w_out):
    B, S, D = x.shape
    mods = _modulation(c, ada_w, ada_b)
    mod0 = mods[0].reshape(B, 1, 3 * D)
    mod1 = mods[1].reshape(B, 1, 3 * D)

    os_, lses, gate_a = [], [], None
    for g, ((window, dil), w) in enumerate(zip(DIL_GROUPS, _weights_a(a_w_in[0]))):
        assert window // (2 * dil) == HALF_WINDOW
        outs = _proj_a(x, mod0, w, dil, with_gate=(g == 0))
        if g == 0:
            gate_a = outs[1]
        hs = slice(g * HEADS_PER_GROUP_A, (g + 1) * HEADS_PER_GROUP_A)
        o, lse = _attn_a(outs[0], _bias_variants(rel_bias[:, hs], dil))
        os_.append(o)
        lses.append(lse)
    x1 = _out_a(x, mod0, os_, lses, gate_a, a_w_out[0].astype(jnp.bfloat16), ln_g[0], ln_b[0])

    w_in, wqa, wqb, wk, wv = _weights_b(b_w_in[0], b_w_uq[0], b_w_ukv[0])
    q2, k2, v2, gate_b = _proj_b(x1, mod1, w_in, b_q_norm[0], b_kv_norm[0], wqa, wqb, wk, wv,
                                 _rope_tables(S))
    o = _attn_b(q2, k2, v2)
    return _out_b(x1, mod1, o, gate_b, b_w_out[0].astype(jnp.bfloat16), ln_g[1], ln_b[1])
```

```python
import functools
import math

import numpy as np
import jax
import jax.numpy as jnp
from jax import lax
from jax.experimental import pallas as pl
from jax.experimental.pallas import tpu as pltpu

D_MODEL = 1024
DEPTH = 2
HEAD_DIM_A = 64
DIL_GROUPS = ((128, 1), (512, 4), (2048, 16))
HEADS_PER_GROUP_A = 6
GROUP_WIDTH_A = HEADS_PER_GROUP_A * HEAD_DIM_A
WIDTH_A = GROUP_WIDTH_A * len(DIL_GROUPS)
N_BUCKETS = 32
T5_MAX_DISTANCE = 1024
N_HEADS_B = 16
QK_NOPE_DIM = 64
QK_ROPE_DIM = 32
V_HEAD_DIM = 64
Q_LORA_RANK = 256
KV_LORA_RANK = 128
WIDTH_B = N_HEADS_B * V_HEAD_DIM
ROPE_THETA = 10000.0
DEEPNORM_ALPHA = (2.0 * DEPTH) ** 0.25
LN_EPS = 1e-5
RMS_EPS = 1e-6
NEG_INF = -1e30

LANES = 128
VMEM_LIMIT_BYTES = 56 * 1024 * 1024

HALF_WINDOW = 64
TQ_A = 128
TK_A = TQ_A + 2 * HALF_WINDOW
TILES_PER_STEP_A = 2
TL_PROJ = 512
TQ_B = 512
TK_B = 512

_NT = (((1,), (1,)), ((), ()))


def _silu(x):
    return x * (1.0 / (1.0 + jnp.exp(-x)))


def _params(*sem):
    return pltpu.CompilerParams(dimension_semantics=sem, vmem_limit_bytes=VMEM_LIMIT_BYTES)


def _mod_kernel(c_ref, w_ref, b_ref, o_ref):
    sc = _silu(c_ref[...]).astype(jnp.bfloat16)
    o_ref[...] = jnp.dot(sc, w_ref[...].astype(jnp.bfloat16),
                         preferred_element_type=jnp.float32) + b_ref[...]


def _modulation(c, ada_w, ada_b):
    B, D = c.shape
    nj = 3
    return pl.pallas_call(
        _mod_kernel,
        grid=(DEPTH, nj),
        in_specs=[pl.BlockSpec((B, D), lambda i, j: (0, 0)),
                  pl.BlockSpec((None, D, D), lambda i, j: (i, 0, j)),
                  pl.BlockSpec((None, 1, D), lambda i, j: (i, 0, j))],
        out_specs=pl.BlockSpec((None, B, D), lambda i, j: (i, 0, j)),
        out_shape=jax.ShapeDtypeStruct((DEPTH, B, 3 * D), jnp.float32),
        compiler_params=_params("arbitrary", "arbitrary"),
        name="modulation",
    )(c, ada_w, ada_b.reshape(DEPTH, 1, 3 * D))


def _modulate(x, mod):
    D = x.shape[-1]
    return x * (1.0 + mod[:, D:2 * D]) + mod[:, :D]


def _layernorm(z, g, b):
    mu = jnp.mean(z, axis=-1, keepdims=True)
    zc = z - mu
    var = jnp.mean(zc * zc, axis=-1, keepdims=True)
    return zc * lax.rsqrt(var + LN_EPS) * g + b


def _class_major_perm(tl, dil, inverse=False):
    p = np.arange(tl)
    src = (p % (tl // dil)) * dil + p // (tl // dil)
    mat = (src[:, None] == np.arange(tl)[None, :]).astype(np.float32)
    return jnp.asarray(mat.T if inverse else mat, dtype=jnp.bfloat16)


def _proj_a_kernel(x_ref, mod_ref, w0_ref, w1_ref, w2_ref, perm1_ref, perm2_ref,
                   qkv0_ref, qkv1_ref, qkv2_ref, gate_ref):
    u = _modulate(x_ref[...], mod_ref[...]).astype(jnp.bfloat16)
    n_qkv = qkv0_ref.shape[-1]
    z = jnp.dot(u, w0_ref[...], preferred_element_type=jnp.float32)
    qkv0_ref[0] = z[:, :n_qkv].astype(qkv0_ref.dtype)
    gate_ref[...] = z[:, n_qkv:].astype(gate_ref.dtype)
    for perm_ref, w_ref, qkv_ref in ((perm1_ref, w1_ref, qkv1_ref), (perm2_ref, w2_ref, qkv2_ref)):
        up = jnp.dot(perm_ref[...], u, preferred_element_type=jnp.float32).astype(jnp.bfloat16)
        z = jnp.dot(up, w_ref[...], preferred_element_type=jnp.float32)
        dil, rows = qkv_ref.shape[:2]
        for r in range(dil):
            qkv_ref[r] = z[r * rows:(r + 1) * rows].astype(qkv_ref.dtype)


def _proj_a(x, mod, ws):
    B, S, D = x.shape
    tl = TL_PROJ
    n_qkv = 3 * GROUP_WIDTH_A
    dils = [dil for _, dil in DIL_GROUPS]
    assert dils[0] == 1
    row = lambda b, i: (b, i, 0)
    cls = lambda b, i: (b, 0, i, 0)
    const2 = lambda b, i: (0, 0)
    perms = [_class_major_perm(tl, dil) for dil in dils[1:]]
    return pl.pallas_call(
        _proj_a_kernel,
        grid=(B, S // tl),
        in_specs=[pl.BlockSpec((None, tl, D), row),
                  pl.BlockSpec((None, 1, 3 * D), lambda b, i: (b, 0, 0)),
                  *[pl.BlockSpec(w.shape, const2) for w in ws],
                  *[pl.BlockSpec((tl, tl), const2) for _ in perms]],
        out_specs=[*[pl.BlockSpec((None, dil, tl // dil, n_qkv), cls) for dil in dils],
                   pl.BlockSpec((None, tl, WIDTH_A), row)],
        out_shape=[*[jax.ShapeDtypeStruct((B, dil, S // dil, n_qkv), jnp.bfloat16) for dil in dils],
                   jax.ShapeDtypeStruct((B, S, WIDTH_A), jnp.bfloat16)],
        compiler_params=_params("arbitrary", "arbitrary"),
        name="proj_a",
    )(x, mod, *ws, *perms)


def _attn_a_kernel(qkv_ref, bias_ref, o_ref, lse_ref, s_scr, p_scr):
    L = qkv_ref.shape[0]
    n_tiles = L // TQ_A
    gw = GROUP_WIDTH_A
    lane = lax.broadcasted_iota(jnp.int32, (TQ_A, LANES), 1)
    low_q = lane < HEAD_DIM_A
    low_k = lax.broadcasted_iota(jnp.int32, (TK_A, LANES), 1) < HEAD_DIM_A
    ones = jnp.ones((TK_A, LANES), jnp.bfloat16)

    heads = range(HEADS_PER_GROUP_A)

    def head_operand(ref, rows, section, h):
        c0 = section * gw + (h // 2) * LANES
        both = ref[rows, c0:c0 + LANES]
        mine = low_k if h % 2 == 0 else jnp.logical_not(low_k)
        return jnp.where(mine, both, jnp.zeros_like(both))

    def tile_group(tg, carry):
        geo = []
        for u in range(TILES_PER_STEP_A):
            t = tg * TILES_PER_STEP_A + u
            j0 = pl.multiple_of(t * TQ_A, TQ_A)
            ws = pl.multiple_of(jnp.clip(j0 - HALF_WINDOW, 0, L - TK_A), HALF_WINDOW)
            variant = jnp.where(t == 0, 0, jnp.where(t == n_tiles - 1, 2, 1))
            geo.append((pl.ds(j0, TQ_A), pl.ds(ws, TK_A), variant))
        for u, (rq, rk, variant) in enumerate(geo):
            for h in heads:
                c0 = (h // 2) * LANES
                s = lax.dot_general(qkv_ref[rq, c0:c0 + LANES], head_operand(qkv_ref, rk, 1, h), _NT,
                                    preferred_element_type=jnp.float32)
                s_scr[u, h] = s + bias_ref[variant, h]
        m = [[jnp.max(s_scr[u, h], axis=-1, keepdims=True) for h in heads] for u in range(len(geo))]
        for u in range(len(geo)):
            for h in heads:
                p_scr[u, h] = jnp.exp(s_scr[u, h] - m[u][h]).astype(jnp.bfloat16)
        for u, (rq, rk, variant) in enumerate(geo):
            lse_tile = jnp.zeros((TQ_A, LANES), jnp.float32)
            for pair in range(HEADS_PER_GROUP_A // 2):
                acc = jnp.zeros((TQ_A, LANES), jnp.float32)
                inv = []
                for h in (2 * pair, 2 * pair + 1):
                    p = p_scr[u, h]
                    denom = jnp.dot(p, ones, preferred_element_type=jnp.float32)
                    acc = acc + jnp.dot(p, head_operand(qkv_ref, rk, 2, h), preferred_element_type=jnp.float32)
                    inv.append(1.0 / denom)
                    lse_tile = jnp.where(lane == h, m[u][h] + jnp.log(denom), lse_tile)
                o = acc * jnp.where(low_q, inv[0], inv[1])
                o_ref[rq, pair * LANES:(pair + 1) * LANES] = o.astype(o_ref.dtype)
            lse_ref[rq, :] = lse_tile
        return carry

    lax.fori_loop(0, n_tiles // TILES_PER_STEP_A, tile_group, 0)


def _attn_a(qkv, biasm):
    B, dil, L, n_qkv = qkv.shape
    gw = GROUP_WIDTH_A
    cls = lambda b, r: (b, r, 0, 0)
    return pl.pallas_call(
        _attn_a_kernel,
        grid=(B, dil),
        in_specs=[pl.BlockSpec((None, None, L, n_qkv), cls),
                  pl.BlockSpec(biasm.shape, lambda b, r: (0, 0, 0, 0))],
        out_specs=[pl.BlockSpec((None, None, L, gw), cls),
                   pl.BlockSpec((None, None, L, LANES), cls)],
        out_shape=[jax.ShapeDtypeStruct((B, dil, L, gw), jnp.bfloat16),
                   jax.ShapeDtypeStruct((B, dil, L, LANES), jnp.float32)],
        scratch_shapes=[
            pltpu.VMEM((TILES_PER_STEP_A, HEADS_PER_GROUP_A, TQ_A, TK_A), jnp.float32),
            pltpu.VMEM((TILES_PER_STEP_A, HEADS_PER_GROUP_A, TQ_A, TK_A), jnp.bfloat16)],
        compiler_params=_params("arbitrary", "arbitrary"),
        name=f"attn_a_d{dil}",
    )(qkv, biasm)


def _t5_bucket_np(rel):
    half = N_BUCKETS // 2
    max_exact = half // 2
    base = np.where(rel > 0, half, 0)
    n = np.abs(rel)
    nf = np.maximum(n, 1).astype(np.float32)
    large = max_exact + (np.log(nf / np.float32(max_exact)) / np.float32(math.log(T5_MAX_DISTANCE / max_exact))
                         * np.float32(half - max_exact)).astype(np.int32)
    large = np.minimum(large, half - 1)
    return base + np.where(n < max_exact, n, large)


def _bias_variants(rel_bias_g, dil):
    H = rel_bias_g.shape[1]
    M = 2 * TK_A
    off = np.arange(M) - (TK_A - 1)
    idx = np.where(np.abs(off) <= HALF_WINDOW, _t5_bucket_np(off * dil), N_BUCKETS).astype(np.int32)
    table = jnp.concatenate([rel_bias_g.astype(jnp.float32), jnp.full((1, H), NEG_INF, jnp.float32)], axis=0)
    w = table[jnp.asarray(idx)].T
    rot = jnp.tile(w, (1, TQ_A + 1))[:, :TQ_A * (M + 1)].reshape(H, TQ_A, M + 1)[:, ::-1, :]
    variants = []
    for shift in (0, HALF_WINDOW, 2 * HALF_WINDOW):
        j0 = TK_A - TQ_A - shift
        variants.append(rot[:, :, j0:j0 + TK_A])
    return jnp.stack(variants)


def _split3_bf16(v):
    hi = v.astype(jnp.bfloat16)
    r1 = v - hi.astype(jnp.float32)
    mid = r1.astype(jnp.bfloat16)
    lo = (r1 - mid.astype(jnp.float32)).astype(jnp.bfloat16)
    return hi, mid, lo


def _out_a_kernel(x_ref, mod_ref, o0_ref, o1_ref, o2_ref, l0_ref, l1_ref, l2_ref, gate_ref,
                  unperm1_ref, unperm2_ref, expand_ref, w_ref, g_ref, b_ref, out_ref):
    tl, D = x_ref.shape
    gw = GROUP_WIDTH_A

    def natural_order(o_ref, l_ref, unperm_ref):
        o = o_ref[...].reshape(tl, gw)
        lse = l_ref[...].reshape(tl, LANES)
        if unperm_ref is None:
            return o.astype(jnp.float32), lse
        pt = unperm_ref[...]
        o = jnp.dot(pt, o, preferred_element_type=jnp.float32)
        lse = sum(jnp.dot(pt, part, preferred_element_type=jnp.float32) for part in _split3_bf16(lse))
        return o, lse

    o0, l0 = natural_order(o0_ref, l0_ref, None)
    o1, l1 = natural_order(o1_ref, l1_ref, unperm1_ref)
    o2, l2 = natural_order(o2_ref, l2_ref, unperm2_ref)
    mx = jnp.maximum(jnp.maximum(l0, l1), l2)
    e0, e1, e2 = jnp.exp(l0 - mx), jnp.exp(l1 - mx), jnp.exp(l2 - mx)
    inv = 1.0 / (e0 + e1 + e2)
    y = jnp.zeros((tl, D), jnp.float32)
    for g, (e, o) in enumerate(((e0, o0), (e1, o1), (e2, o2))):
        wt = e * inv
        hi = wt.astype(jnp.bfloat16)
        lo = (wt - hi.astype(jnp.float32)).astype(jnp.bfloat16)
        wexp = (jnp.dot(hi, expand_ref[...], preferred_element_type=jnp.float32)
                + jnp.dot(lo, expand_ref[...], preferred_element_type=jnp.float32))
        gate = gate_ref[:, g * gw:(g + 1) * gw].astype(jnp.float32)
        yg = (o * wexp * _silu(gate)).astype(jnp.bfloat16)
        y = y + jnp.dot(yg, w_ref[g * gw:(g + 1) * gw, :], preferred_element_type=jnp.float32)
    z = DEEPNORM_ALPHA * x_ref[...] + mod_ref[:, 2 * D:] * y
    out_ref[...] = _layernorm(z, g_ref[...], b_ref[...])


def _out_a(x, mod, os_, lses, gate, w_out, ln_g, ln_b):
    B, S, D = x.shape
    tl = TL_PROJ
    gw = GROUP_WIDTH_A
    expand = jnp.asarray(
        (np.arange(LANES)[:, None] == (np.arange(gw)[None, :] // HEAD_DIM_A)).astype(np.float32),
        dtype=jnp.bfloat16)
    unperms = [_class_major_perm(tl, dil, inverse=True) for _, dil in DIL_GROUPS[1:]]
    row = lambda b, i: (b, i, 0)
    cls = lambda b, i: (b, 0, i, 0)
    const2 = lambda b, i: (0, 0)
    o_specs = [pl.BlockSpec((None, dil, tl // dil, gw), cls) for _, dil in DIL_GROUPS]
    l_specs = [pl.BlockSpec((None, dil, tl // dil, LANES), cls) for _, dil in DIL_GROUPS]
    return pl.pallas_call(
        _out_a_kernel,
        grid=(B, S // tl),
        in_specs=[pl.BlockSpec((None, tl, D), row),
                  pl.BlockSpec((None, 1, 3 * D), lambda b, i: (b, 0, 0)),
                  *o_specs, *l_specs,
                  pl.BlockSpec((None, tl, WIDTH_A), row),
                  pl.BlockSpec((tl, tl), const2), pl.BlockSpec((tl, tl), const2),
                  pl.BlockSpec(expand.shape, const2),
                  pl.BlockSpec(w_out.shape, const2),
                  pl.BlockSpec((1, D), const2), pl.BlockSpec((1, D), const2)],
        out_specs=pl.BlockSpec((None, tl, D), row),
        out_shape=jax.ShapeDtypeStruct((B, S, D), jnp.float32),
        compiler_params=_params("arbitrary", "arbitrary"),
        name="out_a",
    )(x, mod, *os_, *lses, gate, *unperms, expand, w_out, ln_g.reshape(1, D), ln_b.reshape(1, D))


def _rmsnorm(x, g):
    return x * lax.rsqrt(jnp.mean(x * x, axis=-1, keepdims=True) + RMS_EPS) * g


def _proj_b_kernel(x_ref, mod_ref, w_in_ref, qn_ref, kvn_ref, wqa_ref, wqb_ref, wk_ref, wv_ref,
                   vone_ref, cq_ref, sq_ref, ck_ref, sk_ref, q_out, k_out, v_out, gate_out):
    u = _modulate(x_ref[...], mod_ref[...]).astype(jnp.bfloat16)
    z = jnp.dot(u, w_in_ref[...], preferred_element_type=jnp.float32)
    c0 = Q_LORA_RANK
    c1 = c0 + KV_LORA_RANK
    cq = z[:, :c0]
    ckv = z[:, c0:c1]
    kr = z[:, c1:c1 + LANES]
    krs = z[:, c1 + LANES:c1 + 2 * LANES]
    gate_out[...] = z[:, c1 + 2 * LANES:].astype(gate_out.dtype)

    cqn = _rmsnorm(cq, qn_ref[...]).astype(jnp.bfloat16)
    qa = jnp.dot(cqn, wqa_ref[...], preferred_element_type=jnp.float32)
    qb = jnp.dot(cqn, wqb_ref[...], preferred_element_type=jnp.float32)
    ckvn = _rmsnorm(ckv, kvn_ref[...]).astype(jnp.bfloat16)
    kn = jnp.dot(ckvn, wk_ref[...], preferred_element_type=jnp.float32)
    v_out[...] = (jnp.dot(ckvn, wv_ref[...], preferred_element_type=jnp.float32)
                  + vone_ref[...]).astype(v_out.dtype)

    cos_q, sin_q = cq_ref[...], sq_ref[...]
    k_rope = kr * ck_ref[...] + krs * sk_ref[...]
    for h in range(N_HEADS_B):
        blk = slice(h * LANES, (h + 1) * LANES)
        q_out[:, blk] = (qa[:, blk] * cos_q + qb[:, blk] * sin_q).astype(q_out.dtype)
        k_out[:, blk] = (kn[:, blk] + k_rope).astype(k_out.dtype)


def _proj_b(x, mod, w_in, q_norm, kv_norm, wqa, wqb, wk, wv, tables):
    B, S, D = x.shape
    tl = TL_PROJ
    HP = N_HEADS_B * LANES
    row = lambda b, i: (b, i, 0)
    const2 = lambda b, i: (0, 0)
    tab = pl.BlockSpec((tl, LANES), lambda b, i: (i, 0))
    wide = jax.ShapeDtypeStruct((B, S, HP), jnp.bfloat16)
    return pl.pallas_call(
        _proj_b_kernel,
        grid=(B, S // tl),
        in_specs=[pl.BlockSpec((None, tl, D), row),
                  pl.BlockSpec((None, 1, 3 * D), lambda b, i: (b, 0, 0)),
                  pl.BlockSpec(w_in.shape, const2),
                  pl.BlockSpec((1, Q_LORA_RANK), const2), pl.BlockSpec((1, KV_LORA_RANK), const2),
                  pl.BlockSpec(wqa.shape, const2), pl.BlockSpec(wqb.shape, const2),
                  pl.BlockSpec(wk.shape, const2), pl.BlockSpec(wv.shape, const2),
                  pl.BlockSpec((1, HP), const2),
                  tab, tab, tab, tab],
        out_specs=[pl.BlockSpec((None, tl, HP), row), pl.BlockSpec((None, tl, HP), row),
                   pl.BlockSpec((None, tl, HP), row), pl.BlockSpec((None, tl, WIDTH_B), row)],
        out_shape=[wide, wide, wide, jax.ShapeDtypeStruct((B, S, WIDTH_B), jnp.bfloat16)],
        compiler_params=_params("arbitrary", "arbitrary"),
        name="proj_b",
    )(x, mod, w_in, q_norm.reshape(1, -1), kv_norm.reshape(1, -1), wqa, wqb, wk, wv,
      _v_ones_row(), *tables)


def _attn_b_kernel(q_ref, k_ref, v_ref, o_ref, s_even, s_odd):
    tq = q_ref.shape[0]
    S = k_ref.shape[0]
    n_chunks = S // TK_B
    s_scr = (s_even, s_odd)

    def head_lanes(e):
        return slice(e * LANES, (e + 1) * LANES)

    def scores(e, c, mx):
        ks = slice(c * TK_B, (c + 1) * TK_B)
        s = lax.dot_general(q_ref[:, head_lanes(e)], k_ref[ks, head_lanes(e)], _NT,
                            preferred_element_type=jnp.float32)
        s_scr[e][:, ks] = s
        for j in range(TK_B // LANES):
            mx = jnp.maximum(mx, s[:, j * LANES:(j + 1) * LANES])
        return mx

    def weighted_values(e, c, m, acc):
        ks = slice(c * TK_B, (c + 1) * TK_B)
        p = jnp.exp2(s_scr[e][:, ks] - m).astype(jnp.bfloat16)
        return acc + jnp.dot(p, v_ref[ks, head_lanes(e)], preferred_element_type=jnp.float32)

    neg = jnp.full((tq, LANES), -jnp.inf, jnp.float32)
    zero = jnp.zeros((tq, LANES), jnp.float32)
    mx_e = neg
    for c in range(n_chunks):
        mx_e = scores(0, c, mx_e)
    m_e = jnp.max(mx_e, axis=-1, keepdims=True)
    mx_o, acc_e = neg, zero
    for c in range(n_chunks):
        mx_o = scores(1, c, mx_o)
        acc_e = weighted_values(0, c, m_e, acc_e)
    m_o = jnp.max(mx_o, axis=-1, keepdims=True)
    acc_o = zero
    for c in range(n_chunks):
        acc_o = weighted_values(1, c, m_o, acc_o)
    inv_e = 1.0 / acc_e[:, V_HEAD_DIM:V_HEAD_DIM + 1]
    inv_o = 1.0 / acc_o[:, 0:1]
    lane = lax.broadcasted_iota(jnp.int32, (tq, LANES), 1)
    o_ref[...] = jnp.where(lane < V_HEAD_DIM, acc_e * inv_e, acc_o * inv_o).astype(o_ref.dtype)


def _attn_b(q2, k2, v2):
    B, S, HP = q2.shape
    n_pairs = N_HEADS_B // 2
    pw = 2 * LANES
    return pl.pallas_call(
        _attn_b_kernel,
        grid=(B, n_pairs, S // TQ_B),
        in_specs=[pl.BlockSpec((None, TQ_B, pw), lambda b, p, i: (b, i, p)),
                  pl.BlockSpec((None, S, pw), lambda b, p, i: (b, 0, p)),
                  pl.BlockSpec((None, S, pw), lambda b, p, i: (b, 0, p))],
        out_specs=pl.BlockSpec((None, TQ_B, LANES), lambda b, p, i: (b, i, p)),
        out_shape=jax.ShapeDtypeStruct((B, S, WIDTH_B), jnp.bfloat16),
        scratch_shapes=[pltpu.VMEM((TQ_B, S), jnp.float32), pltpu.VMEM((TQ_B, S), jnp.float32)],
        compiler_params=_params("arbitrary", "arbitrary", "arbitrary"),
        name="attn_b",
    )(q2, k2, v2)


def _out_b_kernel(x_ref, mod_ref, o_ref, gate_ref, w_ref, g_ref, b_ref, out_ref):
    D = x_ref.shape[-1]
    yg = (o_ref[...].astype(jnp.float32) * _silu(gate_ref[...].astype(jnp.float32))).astype(jnp.bfloat16)
    y = jnp.dot(yg, w_ref[...], preferred_element_type=jnp.float32)
    z = DEEPNORM_ALPHA * x_ref[...] + mod_ref[:, 2 * D:] * y
    out_ref[...] = _layernorm(z, g_ref[...], b_ref[...])


def _out_b(x, mod, o, gate, w_out, ln_g, ln_b):
    B, S, D = x.shape
    tl = TL_PROJ
    row = lambda b, i: (b, i, 0)
    const2 = lambda b, i: (0, 0)
    return pl.pallas_call(
        _out_b_kernel,
        grid=(B, S // tl),
        in_specs=[pl.BlockSpec((None, tl, D), row),
                  pl.BlockSpec((None, 1, 3 * D), lambda b, i: (b, 0, 0)),
                  pl.BlockSpec((None, tl, WIDTH_B), row), pl.BlockSpec((None, tl, WIDTH_B), row),
                  pl.BlockSpec(w_out.shape, const2),
                  pl.BlockSpec((1, D), const2), pl.BlockSpec((1, D), const2)],
        out_specs=pl.BlockSpec((None, tl, D), row),
        out_shape=jax.ShapeDtypeStruct((B, S, D), jnp.float32),
        compiler_params=_params("arbitrary", "arbitrary"),
        name="out_b",
    )(x, mod, o, gate, w_out, ln_g.reshape(1, D), ln_b.reshape(1, D))


def _weights_a(a_w_in):
    W = WIDTH_A
    gw = GROUP_WIDTH_A
    q, k, v, gate = (a_w_in[:, i * W:(i + 1) * W] for i in range(4))
    ws = []
    for g in range(len(DIL_GROUPS)):
        cs = slice(g * gw, (g + 1) * gw)
        parts = [q[:, cs] * (HEAD_DIM_A ** -0.5), k[:, cs], v[:, cs]]
        if g == 0:
            parts.append(gate)
        ws.append(jnp.concatenate(parts, axis=1).astype(jnp.bfloat16))
    return ws


def _swap_halves(w):
    half = w.shape[-1] // 2
    return jnp.concatenate([w[..., half:], w[..., :half]], axis=-1)


def _pad_lanes(w, lo):
    n = w.shape[-1]
    return jnp.pad(w, [(0, 0)] * (w.ndim - 1) + [(lo, LANES - lo - n)])


def _weights_b(b_w_in, b_w_uq, b_w_ukv):
    c0 = Q_LORA_RANK
    c1 = c0 + KV_LORA_RANK
    c2 = c1 + QK_ROPE_DIM
    kr = b_w_in[:, c1:c2]
    w_in = jnp.concatenate([b_w_in[:, :c1], _pad_lanes(kr, QK_NOPE_DIM),
                            _pad_lanes(_swap_halves(kr), QK_NOPE_DIM), b_w_in[:, c2:]], axis=1)
    uq = b_w_uq.reshape(Q_LORA_RANK, N_HEADS_B, QK_NOPE_DIM + QK_ROPE_DIM)
    q_rope = uq[..., QK_NOPE_DIM:]
    wqa = _pad_lanes(uq, 0).reshape(Q_LORA_RANK, -1)
    wqb = _pad_lanes(_swap_halves(q_rope), QK_NOPE_DIM).reshape(Q_LORA_RANK, -1)
    ukv = b_w_ukv.reshape(KV_LORA_RANK, N_HEADS_B, QK_NOPE_DIM + V_HEAD_DIM)
    wk = _pad_lanes(ukv[..., :QK_NOPE_DIM], 0).reshape(KV_LORA_RANK, -1)
    vv = ukv[..., QK_NOPE_DIM:]
    odd = (jnp.arange(N_HEADS_B) % 2 == 1)[None, :, None]
    wv = jnp.where(odd, _pad_lanes(vv, LANES - V_HEAD_DIM), _pad_lanes(vv, 0)).reshape(KV_LORA_RANK, -1)
    bf = lambda w: w.astype(jnp.bfloat16)
    return bf(w_in), bf(wqa), bf(wqb), bf(wk), bf(wv)


def _rope_tables(S):
    pos = jnp.arange(S, dtype=jnp.float32)
    inv_freq = ROPE_THETA ** (-jnp.arange(0, QK_ROPE_DIM, 2, dtype=jnp.float32) / QK_ROPE_DIM)
    ang = pos[:, None] * inv_freq[None, :]
    cos, sin = jnp.cos(ang), jnp.sin(ang)
    cosf = _pad_lanes(jnp.concatenate([cos, cos], axis=-1), QK_NOPE_DIM)
    sinf = _pad_lanes(jnp.concatenate([-sin, sin], axis=-1), QK_NOPE_DIM)
    nope = _pad_lanes(jnp.ones((S, QK_NOPE_DIM), jnp.float32), 0)
    scale = (QK_NOPE_DIM + QK_ROPE_DIM) ** -0.5
    scale = scale * math.log2(math.e)
    return (scale * (nope + cosf), scale * sinf, cosf, sinf)


def _v_ones_row():
    row = np.zeros((N_HEADS_B, LANES), np.float32)
    row[0::2, V_HEAD_DIM] = 1.0
    row[1::2, 0] = 1.0
    return jnp.asarray(row.reshape(1, -1))


def kernel(x, c, rel_bias, ada_w, ada_b, ln_g, ln_b, a_w_in, a_w_out,
           b_w_in, b_q_norm, b_w_uq, b_kv_norm, b_w_ukv, b_w_out):
    B, S, D = x.shape
    mods = _modulation(c, ada_w, ada_b)
    mod0 = mods[0].reshape(B, 1, 3 * D)
    mod1 = mods[1].reshape(B, 1, 3 * D)

    for window, dil in DIL_GROUPS:
        assert window // (2 * dil) == HALF_WINDOW
    *qkvs, gate_a = _proj_a(x, mod0, _weights_a(a_w_in[0]))
    os_, lses = [], []
    for g, ((_, dil), qkv) in enumerate(zip(DIL_GROUPS, qkvs)):
        hs = slice(g * HEADS_PER_GROUP_A, (g + 1) * HEADS_PER_GROUP_A)
        o, lse = _attn_a(qkv, _bias_variants(rel_bias[:, hs], dil))
        os_.append(o)
        lses.append(lse)
    x1 = _out_a(x, mod0, os_, lses, gate_a, a_w_out[0].astype(jnp.bfloat16), ln_g[0], ln_b[0])

    w_in, wqa, wqb, wk, wv = _weights_b(b_w_in[0], b_w_uq[0], b_w_ukv[0])
    q2, k2, v2, gate_b = _proj_b(x1, mod1, w_in, b_q_norm[0], b_kv_norm[0], wqa, wqb, wk, wv,
                                 _rope_tables(S))
    o = _attn_b(q2, k2, v2)
    return _out_b(x1, mod1, o, gate_b, b_w_out[0].astype(jnp.bfloat16), ln_g[1], ln_b[1])
```

```python
import math

import numpy as np
import jax
import jax.numpy as jnp
from jax import lax
from jax.experimental import pallas as pl
from jax.experimental.pallas import tpu as pltpu

D_MODEL = 1024
DEPTH = 2
HEAD_DIM_A = 64
DIL_GROUPS = ((128, 1), (512, 4), (2048, 16))
HEADS_PER_GROUP_A = 6
GROUP_WIDTH_A = HEADS_PER_GROUP_A * HEAD_DIM_A
WIDTH_A = GROUP_WIDTH_A * len(DIL_GROUPS)
N_BUCKETS = 32
T5_MAX_DISTANCE = 1024
N_HEADS_B = 16
QK_NOPE_DIM = 64
QK_ROPE_DIM = 32
V_HEAD_DIM = 64
Q_LORA_RANK = 256
KV_LORA_RANK = 128
WIDTH_B = N_HEADS_B * V_HEAD_DIM
ROPE_THETA = 10000.0
DEEPNORM_ALPHA = (2.0 * DEPTH) ** 0.25
LN_EPS = 1e-5
RMS_EPS = 1e-6
NEG_INF = -1e30
LOG2E = math.log2(math.e)
LN2 = math.log(2.0)

LANES = 128
VMEM_LIMIT_BYTES = 56 * 1024 * 1024

HALF_WINDOW = 64
TQ_A = 128
TK_A = TQ_A + 2 * HALF_WINDOW
LSE_COPY_STRIDE = 8
LSE_COPIES = 3
TILES_PER_STEP_A = 2
TL_PROJ = 512
PERM_ROWS = 256
TQ_B = 512
TK_B = 512

_NT = (((1,), (1,)), ((), ()))


def _silu(x):
    return x * (1.0 / (1.0 + jnp.exp(-x)))


def _params(*sem):
    return pltpu.CompilerParams(dimension_semantics=sem, vmem_limit_bytes=VMEM_LIMIT_BYTES)


def _mod_kernel(c_ref, w_ref, b_ref, o_ref):
    sc = _silu(c_ref[...]).astype(jnp.bfloat16)
    o_ref[...] = jnp.dot(sc, w_ref[...].astype(jnp.bfloat16),
                         preferred_element_type=jnp.float32) + b_ref[...]


def _modulation(c, ada_w, ada_b):
    B, D = c.shape
    nj = 3
    return pl.pallas_call(
        _mod_kernel,
        grid=(DEPTH, nj),
        in_specs=[pl.BlockSpec((B, D), lambda i, j: (0, 0)),
                  pl.BlockSpec((None, D, D), lambda i, j: (i, 0, j)),
                  pl.BlockSpec((None, 1, D), lambda i, j: (i, 0, j))],
        out_specs=pl.BlockSpec((None, B, D), lambda i, j: (i, 0, j)),
        out_shape=jax.ShapeDtypeStruct((DEPTH, B, 3 * D), jnp.float32),
        compiler_params=_params("arbitrary", "arbitrary"),
        name="modulation",
    )(c, ada_w, ada_b.reshape(DEPTH, 1, 3 * D))


def _modulate(x, mod):
    D = x.shape[-1]
    return x * (1.0 + mod[:, D:2 * D]) + mod[:, :D]


def _layernorm(z, g, b):
    mu = jnp.mean(z, axis=-1, keepdims=True)
    zc = z - mu
    var = jnp.mean(zc * zc, axis=-1, keepdims=True)
    return zc * lax.rsqrt(var + LN_EPS) * g + b


def _class_major_perm(tl, dil, inverse=False):
    p = np.arange(tl)
    src = (p % (tl // dil)) * dil + p // (tl // dil)
    mat = (src[:, None] == np.arange(tl)[None, :]).astype(np.float32)
    return jnp.asarray(mat.T if inverse else mat, dtype=jnp.bfloat16)


def _proj_a_kernel(x_ref, mod_ref, w0_ref, w1_ref, w2_ref, perm1_ref, perm2_ref,
                   qkv0_ref, qkv1_ref, qkv2_ref, gate_ref):
    tl = x_ref.shape[0]
    n_sub = tl // PERM_ROWS
    u = _modulate(x_ref[...], mod_ref[...]).astype(jnp.bfloat16)
    n_qkv = qkv0_ref.shape[-1]
    z = jnp.dot(u, w0_ref[...], preferred_element_type=jnp.float32)
    qkv0_ref[0] = z[:, :n_qkv].astype(qkv0_ref.dtype)
    gate_ref[...] = z[:, n_qkv:].astype(gate_ref.dtype)
    for perm_ref, w_ref, qkv_ref in ((perm1_ref, w1_ref, qkv1_ref), (perm2_ref, w2_ref, qkv2_ref)):
        pm = perm_ref[...]
        up = jnp.concatenate(
            [jnp.dot(pm, u[sub * PERM_ROWS:(sub + 1) * PERM_ROWS], preferred_element_type=jnp.float32)
             for sub in range(n_sub)], axis=0).astype(jnp.bfloat16)
        z = jnp.dot(up, w_ref[...], preferred_element_type=jnp.float32)
        dil, rows = qkv_ref.shape[:2]
        rs = rows // n_sub
        for sub in range(n_sub):
            for r in range(dil):
                r0 = sub * PERM_ROWS + r * rs
                qkv_ref[r, sub * rs:(sub + 1) * rs] = z[r0:r0 + rs].astype(qkv_ref.dtype)


def _proj_a(x, mod, ws):
    B, S, D = x.shape
    tl = TL_PROJ
    n_qkv = 3 * GROUP_WIDTH_A
    dils = [dil for _, dil in DIL_GROUPS]
    assert dils[0] == 1
    row = lambda b, i: (b, i, 0)
    cls = lambda b, i: (b, 0, i, 0)
    const2 = lambda b, i: (0, 0)
    perms = [_class_major_perm(PERM_ROWS, dil) for dil in dils[1:]]
    return pl.pallas_call(
        _proj_a_kernel,
        grid=(B, S // tl),
        in_specs=[pl.BlockSpec((None, tl, D), row),
                  pl.BlockSpec((None, 1, 3 * D), lambda b, i: (b, 0, 0)),
                  *[pl.BlockSpec(w.shape, const2) for w in ws],
                  *[pl.BlockSpec(p.shape, const2) for p in perms]],
        out_specs=[*[pl.BlockSpec((None, dil, tl // dil, n_qkv), cls) for dil in dils],
                   pl.BlockSpec((None, tl, WIDTH_A), row)],
        out_shape=[*[jax.ShapeDtypeStruct((B, dil, S // dil, n_qkv), jnp.bfloat16) for dil in dils],
                   jax.ShapeDtypeStruct((B, S, WIDTH_A), jnp.bfloat16)],
        compiler_params=_params("arbitrary", "arbitrary"),
        name="proj_a",
    )(x, mod, *ws, *perms)


def _attn_a_kernel(qkv_ref, bias_ref, o_ref, lse_ref, s_scr, p_scr):
    L = qkv_ref.shape[0]
    n_tiles = L // TQ_A
    gw = GROUP_WIDTH_A
    lane = lax.broadcasted_iota(jnp.int32, (TQ_A, LANES), 1)
    low_q = lane < HEAD_DIM_A
    lse_lane = jnp.where(lane < LSE_COPIES * LSE_COPY_STRIDE, lane % LSE_COPY_STRIDE, -1)
    low_k = lax.broadcasted_iota(jnp.int32, (TK_A, LANES), 1) < HEAD_DIM_A
    ones = jnp.ones((TK_A, LANES), jnp.bfloat16)

    heads = range(HEADS_PER_GROUP_A)

    def head_operand(ref, rows, section, h):
        c0 = section * gw + (h // 2) * LANES
        both = ref[rows, c0:c0 + LANES]
        mine = low_k if h % 2 == 0 else jnp.logical_not(low_k)
        return jnp.where(mine, both, jnp.zeros_like(both))

    def tile_group(tg, carry):
        geo = []
        for u in range(TILES_PER_STEP_A):
            t = tg * TILES_PER_STEP_A + u
            j0 = pl.multiple_of(t * TQ_A, TQ_A)
            ws = pl.multiple_of(jnp.clip(j0 - HALF_WINDOW, 0, L - TK_A), HALF_WINDOW)
            variant = jnp.where(t == 0, 0, jnp.where(t == n_tiles - 1, 2, 1))
            geo.append((pl.ds(j0, TQ_A), pl.ds(ws, TK_A), variant))
        for u, (rq, rk, variant) in enumerate(geo):
            for h in heads:
                c0 = (h // 2) * LANES
                s = lax.dot_general(qkv_ref[rq, c0:c0 + LANES], head_operand(qkv_ref, rk, 1, h), _NT,
                                    preferred_element_type=jnp.float32)
                s_scr[u, h] = s + bias_ref[variant, h]
        m = [[jnp.max(s_scr[u, h], axis=-1, keepdims=True) for h in heads] for u in range(len(geo))]
        for u in range(len(geo)):
            for h in heads:
                p_scr[u, h] = jnp.exp2(s_scr[u, h] - m[u][h]).astype(jnp.bfloat16)
        for u, (rq, rk, variant) in enumerate(geo):
            lse_tile = jnp.zeros((TQ_A, LANES), jnp.float32)
            for pair in range(HEADS_PER_GROUP_A // 2):
                acc = jnp.zeros((TQ_A, LANES), jnp.float32)
                inv = []
                for h in (2 * pair, 2 * pair + 1):
                    p = p_scr[u, h]
                    denom = jnp.dot(p, ones, preferred_element_type=jnp.float32)
                    acc = acc + jnp.dot(p, head_operand(qkv_ref, rk, 2, h), preferred_element_type=jnp.float32)
                    inv.append(1.0 / denom)
                    lse = (m[u][h] + jnp.log2(denom)) * LN2
                    lse_tile = jnp.where(lse_lane == h, lse, lse_tile)
                o = acc * jnp.where(low_q, inv[0], inv[1])
                o_ref[rq, pair * LANES:(pair + 1) * LANES] = o.astype(o_ref.dtype)
            lse_ref[rq, :] = lse_tile
        return carry

    lax.fori_loop(0, n_tiles // TILES_PER_STEP_A, tile_group, 0)


def _attn_a(qkv, biasm):
    B, dil, L, n_qkv = qkv.shape
    gw = GROUP_WIDTH_A
    cls = lambda b, r: (b, r, 0, 0)
    return pl.pallas_call(
        _attn_a_kernel,
        grid=(B, dil),
        in_specs=[pl.BlockSpec((None, None, L, n_qkv), cls),
                  pl.BlockSpec(biasm.shape, lambda b, r: (0, 0, 0, 0))],
        out_specs=[pl.BlockSpec((None, None, L, gw), cls),
                   pl.BlockSpec((None, None, L, LANES), cls)],
        out_shape=[jax.ShapeDtypeStruct((B, dil, L, gw), jnp.bfloat16),
                   jax.ShapeDtypeStruct((B, dil, L, LANES), jnp.float32)],
        scratch_shapes=[
            pltpu.VMEM((TILES_PER_STEP_A, HEADS_PER_GROUP_A, TQ_A, TK_A), jnp.float32),
            pltpu.VMEM((TILES_PER_STEP_A, HEADS_PER_GROUP_A, TQ_A, TK_A), jnp.bfloat16)],
        compiler_params=_params("arbitrary", "arbitrary"),
        name=f"attn_a_d{dil}",
    )(qkv, biasm)


def _t5_bucket_np(rel):
    half = N_BUCKETS // 2
    max_exact = half // 2
    base = np.where(rel > 0, half, 0)
    n = np.abs(rel)
    nf = np.maximum(n, 1).astype(np.float32)
    large = max_exact + (np.log(nf / np.float32(max_exact)) / np.float32(math.log(T5_MAX_DISTANCE / max_exact))
                         * np.float32(half - max_exact)).astype(np.int32)
    large = np.minimum(large, half - 1)
    return base + np.where(n < max_exact, n, large)


def _bias_variants(rel_bias_g, dil):
    H = rel_bias_g.shape[1]
    M = 2 * TK_A
    off = np.arange(M) - (TK_A - 1)
    idx = np.where(np.abs(off) <= HALF_WINDOW, _t5_bucket_np(off * dil), N_BUCKETS).astype(np.int32)
    table = jnp.concatenate([rel_bias_g.astype(jnp.float32) * LOG2E, jnp.full((1, H), NEG_INF, jnp.float32)], axis=0)
    w = table[jnp.asarray(idx)].T
    rot = jnp.tile(w, (1, TQ_A + 1))[:, :TQ_A * (M + 1)].reshape(H, TQ_A, M + 1)[:, ::-1, :]
    variants = []
    for shift in (0, HALF_WINDOW, 2 * HALF_WINDOW):
        j0 = TK_A - TQ_A - shift
        variants.append(rot[:, :, j0:j0 + TK_A])
    return jnp.stack(variants)


def _split3_bf16(v):
    hi = v.astype(jnp.bfloat16)
    r1 = v - hi.astype(jnp.float32)
    mid = r1.astype(jnp.bfloat16)
    lo = (r1 - mid.astype(jnp.float32)).astype(jnp.bfloat16)
    return hi, mid, lo


def _out_a_kernel(x_ref, mod_ref, o0_ref, o1_ref, o2_ref, l0_ref, l1_ref, l2_ref, gate_ref,
                  unperm1_ref, unperm2_ref, expand_ref, w_ref, g_ref, b_ref, out_ref):
    tl, D = x_ref.shape
    stride = LSE_COPY_STRIDE
    lane_p = lax.broadcasted_iota(jnp.int32, (PERM_ROWS, LANES), 1)
    lane = lax.broadcasted_iota(jnp.int32, (tl, LANES), 1)

    def class_major_rows(ref, sub):
        dil, rows = ref.shape[:2]
        rs = rows * PERM_ROWS // tl
        return jnp.concatenate([ref[r, sub * rs:(sub + 1) * rs] for r in range(dil)], axis=0)

    def natural_order(o_ref, l_ref, unperm_ref):
        if unperm_ref is None:
            return o_ref[0].astype(jnp.float32), l_ref[0]
        pt = unperm_ref[...]
        o_parts, l_parts = [], []
        for sub in range(tl // PERM_ROWS):
            o_parts.append(jnp.dot(pt, class_major_rows(o_ref, sub), preferred_element_type=jnp.float32))
            hi, mid, lo = _split3_bf16(class_major_rows(l_ref, sub))
            packed = jnp.where(lane_p < stride, hi, jnp.where(lane_p < 2 * stride, mid, lo))
            res = jnp.dot(pt, packed, preferred_element_type=jnp.float32)
            l_parts.append(res + pltpu.roll(res, LANES - stride, 1) + pltpu.roll(res, LANES - 2 * stride, 1))
        return jnp.concatenate(o_parts, axis=0), jnp.concatenate(l_parts, axis=0)

    o0, l0 = natural_order(o0_ref, l0_ref, None)
    o1, l1 = natural_order(o1_ref, l1_ref, unperm1_ref)
    o2, l2 = natural_order(o2_ref, l2_ref, unperm2_ref)
    mx = jnp.maximum(jnp.maximum(l0, l1), l2)
    e0, e1, e2 = jnp.exp(l0 - mx), jnp.exp(l1 - mx), jnp.exp(l2 - mx)
    inv = 1.0 / (e0 + e1 + e2)
    wt = jnp.where(lane < stride, e0 * inv,
                   jnp.where(lane < 2 * stride, pltpu.roll(e1 * inv, stride, 1), pltpu.roll(e2 * inv, 2 * stride, 1)))
    hi = wt.astype(jnp.bfloat16)
    lo = (wt - hi.astype(jnp.float32)).astype(jnp.bfloat16)
    wexp = (jnp.dot(hi, expand_ref[...], preferred_element_type=jnp.float32)
            + jnp.dot(lo, expand_ref[...], preferred_element_type=jnp.float32))
    o = jnp.concatenate([o0, o1, o2], axis=1)
    yg = (o * wexp * _silu(gate_ref[...].astype(jnp.float32))).astype(jnp.bfloat16)
    y = jnp.dot(yg, w_ref[...], preferred_element_type=jnp.float32)
    z = DEEPNORM_ALPHA * x_ref[...] + mod_ref[:, 2 * D:] * y
    out_ref[...] = _layernorm(z, g_ref[...], b_ref[...])


def _out_a(x, mod, os_, lses, gate, w_out, ln_g, ln_b):
    B, S, D = x.shape
    tl = TL_PROJ
    gw = GROUP_WIDTH_A
    col = np.arange(WIDTH_A)
    src_lane = (col // gw) * LSE_COPY_STRIDE + (col % gw) // HEAD_DIM_A
    expand = jnp.asarray((np.arange(LANES)[:, None] == src_lane[None, :]).astype(np.float32), dtype=jnp.bfloat16)
    unperms = [_class_major_perm(PERM_ROWS, dil, inverse=True) for _, dil in DIL_GROUPS[1:]]
    row = lambda b, i: (b, i, 0)
    cls = lambda b, i: (b, 0, i, 0)
    const2 = lambda b, i: (0, 0)
    o_specs = [pl.BlockSpec((None, dil, tl // dil, gw), cls) for _, dil in DIL_GROUPS]
    l_specs = [pl.BlockSpec((None, dil, tl // dil, LANES), cls) for _, dil in DIL_GROUPS]
    return pl.pallas_call(
        _out_a_kernel,
        grid=(B, S // tl),
        in_specs=[pl.BlockSpec((None, tl, D), row),
                  pl.BlockSpec((None, 1, 3 * D), lambda b, i: (b, 0, 0)),
                  *o_specs, *l_specs,
                  pl.BlockSpec((None, tl, WIDTH_A), row),
                  *[pl.BlockSpec(p.shape, const2) for p in unperms],
                  pl.BlockSpec(expand.shape, const2),
                  pl.BlockSpec(w_out.shape, const2),
                  pl.BlockSpec((1, D), const2), pl.BlockSpec((1, D), const2)],
        out_specs=pl.BlockSpec((None, tl, D), row),
        out_shape=jax.ShapeDtypeStruct((B, S, D), jnp.float32),
        compiler_params=_params("arbitrary", "arbitrary"),
        name="out_a",
    )(x, mod, *os_, *lses, gate, *unperms, expand, w_out, ln_g.reshape(1, D), ln_b.reshape(1, D))


def _rmsnorm(x, g):
    return x * lax.rsqrt(jnp.mean(x * x, axis=-1, keepdims=True) + RMS_EPS) * g


def _proj_b_kernel(x_ref, mod_ref, w_in_ref, qn_ref, kvn_ref, wqa_ref, wk_ref, wv_ref,
                   vone_ref, cq_ref, sq_ref, ck_ref, sk_ref, q_out, k_out, v_out, gate_out):
    u = _modulate(x_ref[...], mod_ref[...]).astype(jnp.bfloat16)
    z = jnp.dot(u, w_in_ref[...], preferred_element_type=jnp.float32)
    c0 = Q_LORA_RANK
    c1 = c0 + KV_LORA_RANK
    cq = z[:, :c0]
    ckv = z[:, c0:c1]
    kr = z[:, c1:c1 + LANES]
    gate_out[...] = z[:, c1 + LANES:].astype(gate_out.dtype)
    to_rope_lanes = LANES - QK_ROPE_DIM

    cqn = _rmsnorm(cq, qn_ref[...]).astype(jnp.bfloat16)
    qa = jnp.dot(cqn, wqa_ref[...], preferred_element_type=jnp.float32)
    ckvn = _rmsnorm(ckv, kvn_ref[...]).astype(jnp.bfloat16)
    kn = jnp.dot(ckvn, wk_ref[...], preferred_element_type=jnp.float32)
    v_out[...] = (jnp.dot(ckvn, wv_ref[...], preferred_element_type=jnp.float32)
                  + vone_ref[...]).astype(v_out.dtype)

    cos_q, sin_q = cq_ref[...], sq_ref[...]
    k_rope = kr * ck_ref[...] + pltpu.roll(kr, to_rope_lanes, 1) * sk_ref[...]
    for h in range(N_HEADS_B):
        blk = slice(h * LANES, (h + 1) * LANES)
        qh = qa[:, blk]
        q_out[:, blk] = (qh * cos_q + pltpu.roll(qh, to_rope_lanes, 1) * sin_q).astype(q_out.dtype)
        k_out[:, blk] = (kn[:, blk] + k_rope).astype(k_out.dtype)


def _proj_b(x, mod, w_in, q_norm, kv_norm, wqa, wk, wv, tables):
    B, S, D = x.shape
    tl = TL_PROJ
    HP = N_HEADS_B * LANES
    row = lambda b, i: (b, i, 0)
    const2 = lambda b, i: (0, 0)
    tab = pl.BlockSpec((tl, LANES), lambda b, i: (i, 0))
    wide = jax.ShapeDtypeStruct((B, S, HP), jnp.bfloat16)
    return pl.pallas_call(
        _proj_b_kernel,
        grid=(B, S // tl),
        in_specs=[pl.BlockSpec((None, tl, D), row),
                  pl.BlockSpec((None, 1, 3 * D), lambda b, i: (b, 0, 0)),
                  pl.BlockSpec(w_in.shape, const2),
                  pl.BlockSpec((1, Q_LORA_RANK), const2), pl.BlockSpec((1, KV_LORA_RANK), const2),
                  pl.BlockSpec(wqa.shape, const2),
                  pl.BlockSpec(wk.shape, const2), pl.BlockSpec(wv.shape, const2),
                  pl.BlockSpec((1, HP), const2),
                  tab, tab, tab, tab],
        out_specs=[pl.BlockSpec((None, tl, HP), row), pl.BlockSpec((None, tl, HP), row),
                   pl.BlockSpec((None, tl, HP), row), pl.BlockSpec((None, tl, WIDTH_B), row)],
        out_shape=[wide, wide, wide, jax.ShapeDtypeStruct((B, S, WIDTH_B), jnp.bfloat16)],
        compiler_params=_params("arbitrary", "arbitrary"),
        name="proj_b",
    )(x, mod, w_in, q_norm.reshape(1, -1), kv_norm.reshape(1, -1), wqa, wk, wv,
      _v_ones_row(), *tables)


def _attn_b_kernel(q_ref, k_ref, v_ref, o_ref, s_even, s_odd):
    tq = q_ref.shape[0]
    S = k_ref.shape[0]
    n_chunks = S // TK_B
    s_scr = (s_even, s_odd)

    def head_lanes(e):
        return slice(e * LANES, (e + 1) * LANES)

    def scores(e, c, mx):
        ks = slice(c * TK_B, (c + 1) * TK_B)
        s = lax.dot_general(q_ref[:, head_lanes(e)], k_ref[ks, head_lanes(e)], _NT,
                            preferred_element_type=jnp.float32)
        s_scr[e][:, ks] = s
        for j in range(TK_B // LANES):
            mx = jnp.maximum(mx, s[:, j * LANES:(j + 1) * LANES])
        return mx

    def weighted_values(e, c, m, acc):
        ks = slice(c * TK_B, (c + 1) * TK_B)
        p = jnp.exp2(s_scr[e][:, ks] - m).astype(jnp.bfloat16)
        return acc + jnp.dot(p, v_ref[ks, head_lanes(e)], preferred_element_type=jnp.float32)

    neg = jnp.full((tq, LANES), -jnp.inf, jnp.float32)
    zero = jnp.zeros((tq, LANES), jnp.float32)
    mx_e = neg
    for c in range(n_chunks):
        mx_e = scores(0, c, mx_e)
    m_e = jnp.max(mx_e, axis=-1, keepdims=True)
    mx_o, acc_e = neg, zero
    for c in range(n_chunks):
        mx_o = scores(1, c, mx_o)
        acc_e = weighted_values(0, c, m_e, acc_e)
    m_o = jnp.max(mx_o, axis=-1, keepdims=True)
    acc_o = zero
    for c in range(n_chunks):
        acc_o = weighted_values(1, c, m_o, acc_o)
    inv_e = 1.0 / acc_e[:, V_HEAD_DIM:V_HEAD_DIM + 1]
    inv_o = 1.0 / acc_o[:, 0:1]
    lane = lax.broadcasted_iota(jnp.int32, (tq, LANES), 1)
    o_ref[...] = jnp.where(lane < V_HEAD_DIM, acc_e * inv_e, acc_o * inv_o).astype(o_ref.dtype)


def _attn_b(q2, k2, v2):
    B, S, HP = q2.shape
    n_pairs = N_HEADS_B // 2
    pw = 2 * LANES
    return pl.pallas_call(
        _attn_b_kernel,
        grid=(B, n_pairs, S // TQ_B),
        in_specs=[pl.BlockSpec((None, TQ_B, pw), lambda b, p, i: (b, i, p)),
                  pl.BlockSpec((None, S, pw), lambda b, p, i: (b, 0, p)),
                  pl.BlockSpec((None, S, pw), lambda b, p, i: (b, 0, p))],
        out_specs=pl.BlockSpec((None, TQ_B, LANES), lambda b, p, i: (b, i, p)),
        out_shape=jax.ShapeDtypeStruct((B, S, WIDTH_B), jnp.bfloat16),
        scratch_shapes=[pltpu.VMEM((TQ_B, S), jnp.float32), pltpu.VMEM((TQ_B, S), jnp.float32)],
        compiler_params=_params("arbitrary", "arbitrary", "arbitrary"),
        name="attn_b",
    )(q2, k2, v2)


def _out_b_kernel(x_ref, mod_ref, o_ref, gate_ref, w_ref, g_ref, b_ref, out_ref):
    D = x_ref.shape[-1]
    yg = (o_ref[...].astype(jnp.float32) * _silu(gate_ref[...].astype(jnp.float32))).astype(jnp.bfloat16)
    y = jnp.dot(yg, w_ref[...], preferred_element_type=jnp.float32)
    z = DEEPNORM_ALPHA * x_ref[...] + mod_ref[:, 2 * D:] * y
    out_ref[...] = _layernorm(z, g_ref[...], b_ref[...])


def _out_b(x, mod, o, gate, w_out, ln_g, ln_b):
    B, S, D = x.shape
    tl = TL_PROJ
    row = lambda b, i: (b, i, 0)
    const2 = lambda b, i: (0, 0)
    return pl.pallas_call(
        _out_b_kernel,
        grid=(B, S // tl),
        in_specs=[pl.BlockSpec((None, tl, D), row),
                  pl.BlockSpec((None, 1, 3 * D), lambda b, i: (b, 0, 0)),
                  pl.BlockSpec((None, tl, WIDTH_B), row), pl.BlockSpec((None, tl, WIDTH_B), row),
                  pl.BlockSpec(w_out.shape, const2),
                  pl.BlockSpec((1, D), const2), pl.BlockSpec((1, D), const2)],
        out_specs=pl.BlockSpec((None, tl, D), row),
        out_shape=jax.ShapeDtypeStruct((B, S, D), jnp.float32),
        compiler_params=_params("arbitrary", "arbitrary"),
        name="out_b",
    )(x, mod, o, gate, w_out, ln_g.reshape(1, D), ln_b.reshape(1, D))


def _weights_a(a_w_in):
    W = WIDTH_A
    gw = GROUP_WIDTH_A
    q, k, v, gate = (a_w_in[:, i * W:(i + 1) * W] for i in range(4))
    ws = []
    for g in range(len(DIL_GROUPS)):
        cs = slice(g * gw, (g + 1) * gw)
        parts = [q[:, cs] * (HEAD_DIM_A ** -0.5 * LOG2E), k[:, cs], v[:, cs]]
        if g == 0:
            parts.append(gate)
        ws.append(jnp.concatenate(parts, axis=1).astype(jnp.bfloat16))
    return ws


def _swap_halves(w):
    half = w.shape[-1] // 2
    return jnp.concatenate([w[..., half:], w[..., :half]], axis=-1)


def _pad_lanes(w, lo):
    n = w.shape[-1]
    return jnp.pad(w, [(0, 0)] * (w.ndim - 1) + [(lo, LANES - lo - n)])


def _weights_b(b_w_in, b_w_uq, b_w_ukv):
    c0 = Q_LORA_RANK
    c1 = c0 + KV_LORA_RANK
    c2 = c1 + QK_ROPE_DIM
    kr = b_w_in[:, c1:c2]
    w_in = jnp.concatenate([b_w_in[:, :c1], _pad_lanes(jnp.concatenate([kr, _swap_halves(kr)], axis=1), QK_NOPE_DIM),
                            b_w_in[:, c2:]], axis=1)
    uq = b_w_uq.reshape(Q_LORA_RANK, N_HEADS_B, QK_NOPE_DIM + QK_ROPE_DIM)
    wqa = jnp.concatenate([uq, _swap_halves(uq[..., QK_NOPE_DIM:])], axis=-1).reshape(Q_LORA_RANK, -1)
    ukv = b_w_ukv.reshape(KV_LORA_RANK, N_HEADS_B, QK_NOPE_DIM + V_HEAD_DIM)
    wk = _pad_lanes(ukv[..., :QK_NOPE_DIM], 0).reshape(KV_LORA_RANK, -1)
    vv = ukv[..., QK_NOPE_DIM:]
    odd = (jnp.arange(N_HEADS_B) % 2 == 1)[None, :, None]
    wv = jnp.where(odd, _pad_lanes(vv, LANES - V_HEAD_DIM), _pad_lanes(vv, 0)).reshape(KV_LORA_RANK, -1)
    bf = lambda w: w.astype(jnp.bfloat16)
    return bf(w_in), bf(wqa), bf(wk), bf(wv)


def _rope_tables(S):
    pos = jnp.arange(S, dtype=jnp.float32)
    inv_freq = ROPE_THETA ** (-jnp.arange(0, QK_ROPE_DIM, 2, dtype=jnp.float32) / QK_ROPE_DIM)
    ang = pos[:, None] * inv_freq[None, :]
    cos, sin = jnp.cos(ang), jnp.sin(ang)
    cosf = _pad_lanes(jnp.concatenate([cos, cos], axis=-1), QK_NOPE_DIM)
    sinf = _pad_lanes(jnp.concatenate([-sin, sin], axis=-1), QK_NOPE_DIM)
    nope = _pad_lanes(jnp.ones((S, QK_NOPE_DIM), jnp.float32), 0)
    scale = (QK_NOPE_DIM + QK_ROPE_DIM) ** -0.5 * LOG2E
    return (scale * (nope + cosf), scale * sinf, cosf, sinf)


def _v_ones_row():
    row = np.zeros((N_HEADS_B, LANES), np.float32)
    row[0::2, V_HEAD_DIM] = 1.0
    row[1::2, 0] = 1.0
    return jnp.asarray(row.reshape(1, -1))


def kernel(x, c, rel_bias, ada_w, ada_b, ln_g, ln_b, a_w_in, a_w_out,
           b_w_in, b_q_norm, b_w_uq, b_kv_norm, b_w_ukv, b_w_out):
    B, S, D = x.shape
    mods = _modulation(c, ada_w, ada_b)
    mod0 = mods[0].reshape(B, 1, 3 * D)
    mod1 = mods[1].reshape(B, 1, 3 * D)

    for window, dil in DIL_GROUPS:
        assert window // (2 * dil) == HALF_WINDOW
    *qkvs, gate_a = _proj_a(x, mod0, _weights_a(a_w_in[0]))
    os_, lses = [], []
    for g, ((_, dil), qkv) in enumerate(zip(DIL_GROUPS, qkvs)):
        hs = slice(g * HEADS_PER_GROUP_A, (g + 1) * HEADS_PER_GROUP_A)
        o, lse = _attn_a(qkv, _bias_variants(rel_bias[:, hs], dil))
        os_.append(o)
        lses.append(lse)
    x1 = _out_a(x, mod0, os_, lses, gate_a, a_w_out[0].astype(jnp.bfloat16), ln_g[0], ln_b[0])

    w_in, wqa, wk, wv = _weights_b(b_w_in[0], b_w_uq[0], b_w_ukv[0])
    q2, k2, v2, gate_b = _proj_b(x1, mod1, w_in, b_q_norm[0], b_kv_norm[0], wqa, wk, wv,
                                 _rope_tables(S))
    o = _attn_b(q2, k2, v2)
    return _out_b(x1, mod1, o, gate_b, b_w_out[0].astype(jnp.bfloat16), ln_g[1], ln_b[1])
```

```python
import math

import numpy as np
import jax
import jax.numpy as jnp
from jax import lax
from jax.experimental import pallas as pl
from jax.experimental.pallas import tpu as pltpu

D_MODEL = 1024
DEPTH = 2
HEAD_DIM_A = 64
DIL_GROUPS = ((128, 1), (512, 4), (2048, 16))
HEADS_PER_GROUP_A = 6
GROUP_WIDTH_A = HEADS_PER_GROUP_A * HEAD_DIM_A
WIDTH_A = GROUP_WIDTH_A * len(DIL_GROUPS)
N_BUCKETS = 32
T5_MAX_DISTANCE = 1024
N_HEADS_B = 16
QK_NOPE_DIM = 64
QK_ROPE_DIM = 32
V_HEAD_DIM = 64
Q_LORA_RANK = 256
KV_LORA_RANK = 128
WIDTH_B = N_HEADS_B * V_HEAD_DIM
ROPE_THETA = 10000.0
DEEPNORM_ALPHA = (2.0 * DEPTH) ** 0.25
LN_EPS = 1e-5
RMS_EPS = 1e-6
NEG_INF = -1e30
LOG2E = math.log2(math.e)
LN2 = math.log(2.0)

LANES = 128
VMEM_LIMIT_BYTES = 56 * 1024 * 1024

HALF_WINDOW = 64
TQ_A = 128
TK_A = TQ_A + 2 * HALF_WINDOW
LSE_COPY_STRIDE = 8
LSE_COPIES = 3
TILES_PER_STEP_A = 2
TL_PROJ = 512
PERM_ROWS = 256
TQ_B = 512
TK_B = 512
HEADS_PER_STEP_B = 4

_NT = (((1,), (1,)), ((), ()))


def _silu(x):
    return x * (1.0 / (1.0 + jnp.exp(-x)))


def _params(*sem):
    return pltpu.CompilerParams(dimension_semantics=sem, vmem_limit_bytes=VMEM_LIMIT_BYTES)


def _mod_kernel(c_ref, w_ref, b_ref, o_ref):
    sc = _silu(c_ref[...]).astype(jnp.bfloat16)
    o_ref[...] = jnp.dot(sc, w_ref[...].astype(jnp.bfloat16),
                         preferred_element_type=jnp.float32) + b_ref[...]


def _modulation(c, ada_w, ada_b):
    B, D = c.shape
    nj = 3
    return pl.pallas_call(
        _mod_kernel,
        grid=(DEPTH, nj),
        in_specs=[pl.BlockSpec((B, D), lambda i, j: (0, 0)),
                  pl.BlockSpec((None, D, D), lambda i, j: (i, 0, j)),
                  pl.BlockSpec((None, 1, D), lambda i, j: (i, 0, j))],
        out_specs=pl.BlockSpec((None, B, D), lambda i, j: (i, 0, j)),
        out_shape=jax.ShapeDtypeStruct((DEPTH, B, 3 * D), jnp.float32),
        compiler_params=_params("arbitrary", "arbitrary"),
        name="modulation",
    )(c, ada_w, ada_b.reshape(DEPTH, 1, 3 * D))


def _modulate(x, mod):
    D = x.shape[-1]
    return x * (1.0 + mod[:, D:2 * D]) + mod[:, :D]


def _layernorm(z, g, b):
    mu = jnp.mean(z, axis=-1, keepdims=True)
    zc = z - mu
    var = jnp.mean(zc * zc, axis=-1, keepdims=True)
    return zc * lax.rsqrt(var + LN_EPS) * g + b


def _class_major_perm(tl, dil, inverse=False):
    p = np.arange(tl)
    src = (p % (tl // dil)) * dil + p // (tl // dil)
    mat = (src[:, None] == np.arange(tl)[None, :]).astype(np.float32)
    return jnp.asarray(mat.T if inverse else mat, dtype=jnp.bfloat16)


def _proj_a_kernel(x_ref, mod_ref, w0_ref, w1_ref, w2_ref, perm1_ref, perm2_ref,
                   qkv0_ref, qkv1_ref, qkv2_ref, gate_ref):
    tl = x_ref.shape[0]
    n_sub = tl // PERM_ROWS
    u = _modulate(x_ref[...], mod_ref[...]).astype(jnp.bfloat16)
    n_qkv = qkv0_ref.shape[-1]
    z = jnp.dot(u, w0_ref[...], preferred_element_type=jnp.float32)
    qkv0_ref[0] = z[:, :n_qkv].astype(qkv0_ref.dtype)
    gate_ref[...] = z[:, n_qkv:].astype(gate_ref.dtype)
    for perm_ref, w_ref, qkv_ref in ((perm1_ref, w1_ref, qkv1_ref), (perm2_ref, w2_ref, qkv2_ref)):
        pm = perm_ref[...]
        up = jnp.concatenate(
            [jnp.dot(pm, u[sub * PERM_ROWS:(sub + 1) * PERM_ROWS], preferred_element_type=jnp.float32)
             for sub in range(n_sub)], axis=0).astype(jnp.bfloat16)
        z = jnp.dot(up, w_ref[...], preferred_element_type=jnp.float32)
        dil, rows = qkv_ref.shape[:2]
        rs = rows // n_sub
        for sub in range(n_sub):
            for r in range(dil):
                r0 = sub * PERM_ROWS + r * rs
                qkv_ref[r, sub * rs:(sub + 1) * rs] = z[r0:r0 + rs].astype(qkv_ref.dtype)


def _proj_a(x, mod, ws):
    B, S, D = x.shape
    tl = TL_PROJ
    n_qkv = 3 * GROUP_WIDTH_A
    dils = [dil for _, dil in DIL_GROUPS]
    assert dils[0] == 1
    row = lambda b, i: (b, i, 0)
    cls = lambda b, i: (b, 0, i, 0)
    const2 = lambda b, i: (0, 0)
    perms = [_class_major_perm(PERM_ROWS, dil) for dil in dils[1:]]
    return pl.pallas_call(
        _proj_a_kernel,
        grid=(B, S // tl),
        in_specs=[pl.BlockSpec((None, tl, D), row),
                  pl.BlockSpec((None, 1, 3 * D), lambda b, i: (b, 0, 0)),
                  *[pl.BlockSpec(w.shape, const2) for w in ws],
                  *[pl.BlockSpec(p.shape, const2) for p in perms]],
        out_specs=[*[pl.BlockSpec((None, dil, tl // dil, n_qkv), cls) for dil in dils],
                   pl.BlockSpec((None, tl, WIDTH_A), row)],
        out_shape=[*[jax.ShapeDtypeStruct((B, dil, S // dil, n_qkv), jnp.bfloat16) for dil in dils],
                   jax.ShapeDtypeStruct((B, S, WIDTH_A), jnp.bfloat16)],
        compiler_params=_params("arbitrary", "arbitrary"),
        name="proj_a",
    )(x, mod, *ws, *perms)


def _attn_a_kernel(qkv_ref, bias_ref, o_ref, lse_ref, s_scr, p_scr):
    L = qkv_ref.shape[0]
    n_tiles = L // TQ_A
    gw = GROUP_WIDTH_A
    lane = lax.broadcasted_iota(jnp.int32, (TQ_A, LANES), 1)
    low_q = lane < HEAD_DIM_A
    lse_lane = jnp.where(lane < LSE_COPIES * LSE_COPY_STRIDE, lane % LSE_COPY_STRIDE, -1)
    low_k = lax.broadcasted_iota(jnp.int32, (TK_A, LANES), 1) < HEAD_DIM_A

    heads = range(HEADS_PER_GROUP_A)

    def pair_block(rows, section, h):
        c0 = section * gw + (h // 2) * LANES
        return qkv_ref[rows, c0:c0 + LANES]

    def other_lanes(block, low_mask, h, fill):
        mine = low_mask if h % 2 == 0 else jnp.logical_not(low_mask)
        return jnp.where(mine, block, jnp.full_like(block, fill))

    def tile_group(tg, carry):
        geo = []
        for u in range(TILES_PER_STEP_A):
            t = tg * TILES_PER_STEP_A + u
            j0 = pl.multiple_of(t * TQ_A, TQ_A)
            ws = pl.multiple_of(jnp.clip(j0 - HALF_WINDOW, 0, L - TK_A), HALF_WINDOW)
            variant = jnp.where(t == 0, 0, jnp.where(t == n_tiles - 1, 2, 1))
            geo.append((pl.ds(j0, TQ_A), pl.ds(ws, TK_A), variant))
        for u, (rq, rk, variant) in enumerate(geo):
            for h in heads:
                q = other_lanes(pair_block(rq, 0, h), low_q, h, 0)
                s = lax.dot_general(q, pair_block(rk, 1, h), _NT, preferred_element_type=jnp.float32)
                s_scr[u, h] = s + bias_ref[variant, h]
        m = [[jnp.max(s_scr[u, h], axis=-1, keepdims=True) for h in heads] for u in range(len(geo))]
        for u in range(len(geo)):
            for h in heads:
                p_scr[u, h] = jnp.exp2(s_scr[u, h] - m[u][h]).astype(jnp.bfloat16)
        for u, (rq, rk, variant) in enumerate(geo):
            lse_tile = jnp.zeros((TQ_A, LANES), jnp.float32)
            for pair in range(HEADS_PER_GROUP_A // 2):
                he, ho = 2 * pair, 2 * pair + 1
                acc_e = jnp.dot(p_scr[u, he], other_lanes(pair_block(rk, 2, he), low_k, he, 1),
                                preferred_element_type=jnp.float32)
                acc_o = jnp.dot(p_scr[u, ho], other_lanes(pair_block(rk, 2, ho), low_k, ho, 1),
                                preferred_element_type=jnp.float32)
                numer = jnp.where(low_q, acc_e, acc_o)
                den_other = jnp.where(low_q, acc_o, acc_e)
                den = pltpu.roll(den_other, HEAD_DIM_A, 1)
                o_ref[rq, pair * LANES:(pair + 1) * LANES] = (numer / den).astype(o_ref.dtype)
                is_odd = lse_lane == ho
                lse = (jnp.where(is_odd, m[u][ho], m[u][he])
                       + jnp.log2(jnp.where(is_odd, den_other, den))) * LN2
                lse_tile = jnp.where(jnp.logical_or(is_odd, lse_lane == he), lse, lse_tile)
            lse_ref[rq, :] = lse_tile
        return carry

    lax.fori_loop(0, n_tiles // TILES_PER_STEP_A, tile_group, 0)


def _attn_a(qkv, biasm):
    B, dil, L, n_qkv = qkv.shape
    gw = GROUP_WIDTH_A
    cls = lambda b, r: (b, r, 0, 0)
    return pl.pallas_call(
        _attn_a_kernel,
        grid=(B, dil),
        in_specs=[pl.BlockSpec((None, None, L, n_qkv), cls),
                  pl.BlockSpec(biasm.shape, lambda b, r: (0, 0, 0, 0))],
        out_specs=[pl.BlockSpec((None, None, L, gw), cls),
                   pl.BlockSpec((None, None, L, LANES), cls)],
        out_shape=[jax.ShapeDtypeStruct((B, dil, L, gw), jnp.bfloat16),
                   jax.ShapeDtypeStruct((B, dil, L, LANES), jnp.float32)],
        scratch_shapes=[
            pltpu.VMEM((TILES_PER_STEP_A, HEADS_PER_GROUP_A, TQ_A, TK_A), jnp.float32),
            pltpu.VMEM((TILES_PER_STEP_A, HEADS_PER_GROUP_A, TQ_A, TK_A), jnp.bfloat16)],
        compiler_params=_params("arbitrary", "arbitrary"),
        name=f"attn_a_d{dil}",
    )(qkv, biasm)


def _t5_bucket_np(rel):
    half = N_BUCKETS // 2
    max_exact = half // 2
    base = np.where(rel > 0, half, 0)
    n = np.abs(rel)
    nf = np.maximum(n, 1).astype(np.float32)
    large = max_exact + (np.log(nf / np.float32(max_exact)) / np.float32(math.log(T5_MAX_DISTANCE / max_exact))
                         * np.float32(half - max_exact)).astype(np.int32)
    large = np.minimum(large, half - 1)
    return base + np.where(n < max_exact, n, large)


def _bias_variants(rel_bias_g, dil):
    H = rel_bias_g.shape[1]
    M = 2 * TK_A
    off = np.arange(M) - (TK_A - 1)
    idx = np.where(np.abs(off) <= HALF_WINDOW, _t5_bucket_np(off * dil), N_BUCKETS).astype(np.int32)
    table = jnp.concatenate([rel_bias_g.astype(jnp.float32) * LOG2E, jnp.full((1, H), NEG_INF, jnp.float32)], axis=0)
    w = table[jnp.asarray(idx)].T
    rot = jnp.tile(w, (1, TQ_A + 1))[:, :TQ_A * (M + 1)].reshape(H, TQ_A, M + 1)[:, ::-1, :]
    variants = []
    for shift in (0, HALF_WINDOW, 2 * HALF_WINDOW):
        j0 = TK_A - TQ_A - shift
        variants.append(rot[:, :, j0:j0 + TK_A])
    return jnp.stack(variants)


def _split3_bf16(v):
    hi = v.astype(jnp.bfloat16)
    r1 = v - hi.astype(jnp.float32)
    mid = r1.astype(jnp.bfloat16)
    lo = (r1 - mid.astype(jnp.float32)).astype(jnp.bfloat16)
    return hi, mid, lo


def _out_a_kernel(x_ref, mod_ref, o0_ref, o1_ref, o2_ref, l0_ref, l1_ref, l2_ref, gate_ref,
                  unperm1_ref, unperm2_ref, expand_ref, w_ref, g_ref, b_ref, out_ref):
    tl, D = x_ref.shape
    stride = LSE_COPY_STRIDE
    lane_p = lax.broadcasted_iota(jnp.int32, (PERM_ROWS, LANES), 1)
    lane = lax.broadcasted_iota(jnp.int32, (tl, LANES), 1)

    def class_major_rows(ref, sub):
        dil, rows = ref.shape[:2]
        rs = rows * PERM_ROWS // tl
        return jnp.concatenate([ref[r, sub * rs:(sub + 1) * rs] for r in range(dil)], axis=0)

    def natural_order(o_ref, l_ref, unperm_ref):
        if unperm_ref is None:
            return o_ref[0].astype(jnp.float32), l_ref[0]
        pt = unperm_ref[...]
        o_parts, l_parts = [], []
        for sub in range(tl // PERM_ROWS):
            o_parts.append(jnp.dot(pt, class_major_rows(o_ref, sub), preferred_element_type=jnp.float32))
            hi, mid, lo = _split3_bf16(class_major_rows(l_ref, sub))
            packed = jnp.where(lane_p < stride, hi, jnp.where(lane_p < 2 * stride, mid, lo))
            res = jnp.dot(pt, packed, preferred_element_type=jnp.float32)
            l_parts.append(res + pltpu.roll(res, LANES - stride, 1) + pltpu.roll(res, LANES - 2 * stride, 1))
        return jnp.concatenate(o_parts, axis=0), jnp.concatenate(l_parts, axis=0)

    o0, l0 = natural_order(o0_ref, l0_ref, None)
    o1, l1 = natural_order(o1_ref, l1_ref, unperm1_ref)
    o2, l2 = natural_order(o2_ref, l2_ref, unperm2_ref)
    mx = jnp.maximum(jnp.maximum(l0, l1), l2)
    e0, e1, e2 = jnp.exp(l0 - mx), jnp.exp(l1 - mx), jnp.exp(l2 - mx)
    inv = 1.0 / (e0 + e1 + e2)
    wt = jnp.where(lane < stride, e0 * inv,
                   jnp.where(lane < 2 * stride, pltpu.roll(e1 * inv, stride, 1), pltpu.roll(e2 * inv, 2 * stride, 1)))
    hi = wt.astype(jnp.bfloat16)
    lo = (wt - hi.astype(jnp.float32)).astype(jnp.bfloat16)
    wexp = (jnp.dot(hi, expand_ref[...], preferred_element_type=jnp.float32)
            + jnp.dot(lo, expand_ref[...], preferred_element_type=jnp.float32))
    o = jnp.concatenate([o0, o1, o2], axis=1)
    yg = (o * wexp * _silu(gate_ref[...].astype(jnp.float32))).astype(jnp.bfloat16)
    y = jnp.dot(yg, w_ref[...], preferred_element_type=jnp.float32)
    z = DEEPNORM_ALPHA * x_ref[...] + mod_ref[:, 2 * D:] * y
    out_ref[...] = _layernorm(z, g_ref[...], b_ref[...])


def _out_a(x, mod, os_, lses, gate, w_out, ln_g, ln_b):
    B, S, D = x.shape
    tl = TL_PROJ
    gw = GROUP_WIDTH_A
    col = np.arange(WIDTH_A)
    src_lane = (col // gw) * LSE_COPY_STRIDE + (col % gw) // HEAD_DIM_A
    expand = jnp.asarray((np.arange(LANES)[:, None] == src_lane[None, :]).astype(np.float32), dtype=jnp.bfloat16)
    unperms = [_class_major_perm(PERM_ROWS, dil, inverse=True) for _, dil in DIL_GROUPS[1:]]
    row = lambda b, i: (b, i, 0)
    cls = lambda b, i: (b, 0, i, 0)
    const2 = lambda b, i: (0, 0)
    o_specs = [pl.BlockSpec((None, dil, tl // dil, gw), cls) for _, dil in DIL_GROUPS]
    l_specs = [pl.BlockSpec((None, dil, tl // dil, LANES), cls) for _, dil in DIL_GROUPS]
    return pl.pallas_call(
        _out_a_kernel,
        grid=(B, S // tl),
        in_specs=[pl.BlockSpec((None, tl, D), row),
                  pl.BlockSpec((None, 1, 3 * D), lambda b, i: (b, 0, 0)),
                  *o_specs, *l_specs,
                  pl.BlockSpec((None, tl, WIDTH_A), row),
                  *[pl.BlockSpec(p.shape, const2) for p in unperms],
                  pl.BlockSpec(expand.shape, const2),
                  pl.BlockSpec(w_out.shape, const2),
                  pl.BlockSpec((1, D), const2), pl.BlockSpec((1, D), const2)],
        out_specs=pl.BlockSpec((None, tl, D), row),
        out_shape=jax.ShapeDtypeStruct((B, S, D), jnp.float32),
        compiler_params=_params("arbitrary", "arbitrary"),
        name="out_a",
    )(x, mod, *os_, *lses, gate, *unperms, expand, w_out, ln_g.reshape(1, D), ln_b.reshape(1, D))


def _rmsnorm(x, g):
    return x * lax.rsqrt(jnp.mean(x * x, axis=-1, keepdims=True) + RMS_EPS) * g


def _proj_b_kernel(x_ref, mod_ref, w_in_ref, qn_ref, kvn_ref, wqa_ref, wk_ref, wv_ref,
                   vone_ref, cq_ref, sq_ref, ck_ref, sk_ref, q_out, k_out, v_out, gate_out):
    u = _modulate(x_ref[...], mod_ref[...]).astype(jnp.bfloat16)
    z = jnp.dot(u, w_in_ref[...], preferred_element_type=jnp.float32)
    c0 = Q_LORA_RANK
    c1 = c0 + KV_LORA_RANK
    cq = z[:, :c0]
    ckv = z[:, c0:c1]
    kr = z[:, c1:c1 + LANES]
    gate_out[...] = z[:, c1 + LANES:].astype(gate_out.dtype)
    to_rope_lanes = LANES - QK_ROPE_DIM

    cqn = _rmsnorm(cq, qn_ref[...]).astype(jnp.bfloat16)
    qa = jnp.dot(cqn, wqa_ref[...], preferred_element_type=jnp.float32)
    ckvn = _rmsnorm(ckv, kvn_ref[...]).astype(jnp.bfloat16)
    kn = jnp.dot(ckvn, wk_ref[...], preferred_element_type=jnp.float32)
    v_out[...] = (jnp.dot(ckvn, wv_ref[...], preferred_element_type=jnp.float32)
                  + vone_ref[...]).astype(v_out.dtype)

    cos_q, sin_q = cq_ref[...], sq_ref[...]
    k_rope = kr * ck_ref[...] + pltpu.roll(kr, to_rope_lanes, 1) * sk_ref[...]
    for h in range(N_HEADS_B):
        blk = slice(h * LANES, (h + 1) * LANES)
        qh = qa[:, blk]
        q_out[:, blk] = (qh * cos_q + pltpu.roll(qh, to_rope_lanes, 1) * sin_q).astype(q_out.dtype)
        k_out[:, blk] = (kn[:, blk] + k_rope).astype(k_out.dtype)


def _proj_b(x, mod, w_in, q_norm, kv_norm, wqa, wk, wv, tables):
    B, S, D = x.shape
    tl = TL_PROJ
    HP = N_HEADS_B * LANES
    row = lambda b, i: (b, i, 0)
    const2 = lambda b, i: (0, 0)
    tab = pl.BlockSpec((tl, LANES), lambda b, i: (i, 0))
    wide = jax.ShapeDtypeStruct((B, S, HP), jnp.bfloat16)
    return pl.pallas_call(
        _proj_b_kernel,
        grid=(B, S // tl),
        in_specs=[pl.BlockSpec((None, tl, D), row),
                  pl.BlockSpec((None, 1, 3 * D), lambda b, i: (b, 0, 0)),
                  pl.BlockSpec(w_in.shape, const2),
                  pl.BlockSpec((1, Q_LORA_RANK), const2), pl.BlockSpec((1, KV_LORA_RANK), const2),
                  pl.BlockSpec(wqa.shape, const2),
                  pl.BlockSpec(wk.shape, const2), pl.BlockSpec(wv.shape, const2),
                  pl.BlockSpec((1, HP), const2),
                  tab, tab, tab, tab],
        out_specs=[pl.BlockSpec((None, tl, HP), row), pl.BlockSpec((None, tl, HP), row),
                   pl.BlockSpec((None, tl, HP), row), pl.BlockSpec((None, tl, WIDTH_B), row)],
        out_shape=[wide, wide, wide, jax.ShapeDtypeStruct((B, S, WIDTH_B), jnp.bfloat16)],
        compiler_params=_params("arbitrary", "arbitrary"),
        name="proj_b",
    )(x, mod, w_in, q_norm.reshape(1, -1), kv_norm.reshape(1, -1), wqa, wk, wv,
      _v_ones_row(), *tables)


def _attn_b_kernel(q_ref, k_ref, v_ref, o_ref, s_even, s_odd):
    tq = q_ref.shape[0]
    S = k_ref.shape[0]
    n_chunks = S // TK_B
    s_bufs = (s_even, s_odd)

    def head_lanes(e):
        return slice(e * LANES, (e + 1) * LANES)

    def scores(e, c, mx):
        ks = slice(c * TK_B, (c + 1) * TK_B)
        s = lax.dot_general(q_ref[:, head_lanes(e)], k_ref[ks, head_lanes(e)], _NT,
                            preferred_element_type=jnp.float32)
        s_bufs[e % 2][:, ks] = s
        for j in range(TK_B // LANES):
            mx = jnp.maximum(mx, s[:, j * LANES:(j + 1) * LANES])
        return mx

    def weighted_values(e, c, m, acc):
        ks = slice(c * TK_B, (c + 1) * TK_B)
        p = jnp.exp2(s_bufs[e % 2][:, ks] - m).astype(jnp.bfloat16)
        return acc + jnp.dot(p, v_ref[ks, head_lanes(e)], preferred_element_type=jnp.float32)

    neg = jnp.full((tq, LANES), -jnp.inf, jnp.float32)
    zero = jnp.zeros((tq, LANES), jnp.float32)
    n_heads = q_ref.shape[1] // LANES
    mx = neg
    for c in range(n_chunks):
        mx = scores(0, c, mx)
    acc = []
    for e in range(n_heads):
        m = jnp.max(mx, axis=-1, keepdims=True)
        mx, a = neg, zero
        for c in range(n_chunks):
            if e + 1 < n_heads:
                mx = scores(e + 1, c, mx)
            a = weighted_values(e, c, m, a)
        acc.append(a)
    lane = lax.broadcasted_iota(jnp.int32, (tq, LANES), 1)
    for pair in range(n_heads // 2):
        acc_e, acc_o = acc[2 * pair], acc[2 * pair + 1]
        inv_e = 1.0 / acc_e[:, V_HEAD_DIM:V_HEAD_DIM + 1]
        inv_o = 1.0 / acc_o[:, 0:1]
        o_ref[:, pair * LANES:(pair + 1) * LANES] = jnp.where(
            lane < V_HEAD_DIM, acc_e * inv_e, acc_o * inv_o).astype(o_ref.dtype)


def _attn_b(q2, k2, v2):
    B, S, HP = q2.shape
    hw = HEADS_PER_STEP_B * LANES
    ow = HEADS_PER_STEP_B * V_HEAD_DIM
    return pl.pallas_call(
        _attn_b_kernel,
        grid=(B, N_HEADS_B // HEADS_PER_STEP_B, S // TQ_B),
        in_specs=[pl.BlockSpec((None, TQ_B, hw), lambda b, p, i: (b, i, p)),
                  pl.BlockSpec((None, S, hw), lambda b, p, i: (b, 0, p)),
                  pl.BlockSpec((None, S, hw), lambda b, p, i: (b, 0, p))],
        out_specs=pl.BlockSpec((None, TQ_B, ow), lambda b, p, i: (b, i, p)),
        out_shape=jax.ShapeDtypeStruct((B, S, WIDTH_B), jnp.bfloat16),
        scratch_shapes=[pltpu.VMEM((TQ_B, S), jnp.float32), pltpu.VMEM((TQ_B, S), jnp.float32)],
        compiler_params=_params("arbitrary", "arbitrary", "arbitrary"),
        name="attn_b",
    )(q2, k2, v2)


def _out_b_kernel(x_ref, mod_ref, o_ref, gate_ref, w_ref, g_ref, b_ref, out_ref):
    D = x_ref.shape[-1]
    yg = (o_ref[...].astype(jnp.float32) * _silu(gate_ref[...].astype(jnp.float32))).astype(jnp.bfloat16)
    y = jnp.dot(yg, w_ref[...], preferred_element_type=jnp.float32)
    z = DEEPNORM_ALPHA * x_ref[...] + mod_ref[:, 2 * D:] * y
    out_ref[...] = _layernorm(z, g_ref[...], b_ref[...])


def _out_b(x, mod, o, gate, w_out, ln_g, ln_b):
    B, S, D = x.shape
    tl = TL_PROJ
    row = lambda b, i: (b, i, 0)
    const2 = lambda b, i: (0, 0)
    return pl.pallas_call(
        _out_b_kernel,
        grid=(B, S // tl),
        in_specs=[pl.BlockSpec((None, tl, D), row),
                  pl.BlockSpec((None, 1, 3 * D), lambda b, i: (b, 0, 0)),
                  pl.BlockSpec((None, tl, WIDTH_B), row), pl.BlockSpec((None, tl, WIDTH_B), row),
                  pl.BlockSpec(w_out.shape, const2),
                  pl.BlockSpec((1, D), const2), pl.BlockSpec((1, D), const2)],
        out_specs=pl.BlockSpec((None, tl, D), row),
        out_shape=jax.ShapeDtypeStruct((B, S, D), jnp.float32),
        compiler_params=_params("arbitrary", "arbitrary"),
        name="out_b",
    )(x, mod, o, gate, w_out, ln_g.reshape(1, D), ln_b.reshape(1, D))


def _weights_a(a_w_in):
    W = WIDTH_A
    gw = GROUP_WIDTH_A
    q, k, v, gate = (a_w_in[:, i * W:(i + 1) * W] for i in range(4))
    ws = []
    for g in range(len(DIL_GROUPS)):
        cs = slice(g * gw, (g + 1) * gw)
        parts = [q[:, cs] * (HEAD_DIM_A ** -0.5 * LOG2E), k[:, cs], v[:, cs]]
        if g == 0:
            parts.append(gate)
        ws.append(jnp.concatenate(parts, axis=1).astype(jnp.bfloat16))
    return ws


def _swap_halves(w):
    half = w.shape[-1] // 2
    return jnp.concatenate([w[..., half:], w[..., :half]], axis=-1)


def _pad_lanes(w, lo):
    n = w.shape[-1]
    return jnp.pad(w, [(0, 0)] * (w.ndim - 1) + [(lo, LANES - lo - n)])


def _weights_b(b_w_in, b_w_uq, b_w_ukv):
    c0 = Q_LORA_RANK
    c1 = c0 + KV_LORA_RANK
    c2 = c1 + QK_ROPE_DIM
    kr = b_w_in[:, c1:c2]
    w_in = jnp.concatenate([b_w_in[:, :c1], _pad_lanes(jnp.concatenate([kr, _swap_halves(kr)], axis=1), QK_NOPE_DIM),
                            b_w_in[:, c2:]], axis=1)
    uq = b_w_uq.reshape(Q_LORA_RANK, N_HEADS_B, QK_NOPE_DIM + QK_ROPE_DIM)
    wqa = jnp.concatenate([uq, _swap_halves(uq[..., QK_NOPE_DIM:])], axis=-1).reshape(Q_LORA_RANK, -1)
    ukv = b_w_ukv.reshape(KV_LORA_RANK, N_HEADS_B, QK_NOPE_DIM + V_HEAD_DIM)
    wk = _pad_lanes(ukv[..., :QK_NOPE_DIM], 0).reshape(KV_LORA_RANK, -1)
    vv = ukv[..., QK_NOPE_DIM:]
    odd = (jnp.arange(N_HEADS_B) % 2 == 1)[None, :, None]
    wv = jnp.where(odd, _pad_lanes(vv, LANES - V_HEAD_DIM), _pad_lanes(vv, 0)).reshape(KV_LORA_RANK, -1)
    bf = lambda w: w.astype(jnp.bfloat16)
    return bf(w_in), bf(wqa), bf(wk), bf(wv)


def _rope_tables(S):
    pos = jnp.arange(S, dtype=jnp.float32)
    inv_freq = ROPE_THETA ** (-jnp.arange(0, QK_ROPE_DIM, 2, dtype=jnp.float32) / QK_ROPE_DIM)
    ang = pos[:, None] * inv_freq[None, :]
    cos, sin = jnp.cos(ang), jnp.sin(ang)
    cosf = _pad_lanes(jnp.concatenate([cos, cos], axis=-1), QK_NOPE_DIM)
    sinf = _pad_lanes(jnp.concatenate([-sin, sin], axis=-1), QK_NOPE_DIM)
    nope = _pad_lanes(jnp.ones((S, QK_NOPE_DIM), jnp.float32), 0)
    scale = (QK_NOPE_DIM + QK_ROPE_DIM) ** -0.5 * LOG2E
    return (scale * (nope + cosf), scale * sinf, cosf, sinf)


def _v_ones_row():
    row = np.zeros((N_HEADS_B, LANES), np.float32)
    row[0::2, V_HEAD_DIM] = 1.0
    row[1::2, 0] = 1.0
    return jnp.asarray(row.reshape(1, -1))


def kernel(x, c, rel_bias, ada_w, ada_b, ln_g, ln_b, a_w_in, a_w_out,
           b_w_in, b_q_norm, b_w_uq, b_kv_norm, b_w_ukv, b_w_out):
    B, S, D = x.shape
    mods = _modulation(c, ada_w, ada_b)
    mod0 = mods[0].reshape(B, 1, 3 * D)
    mod1 = mods[1].reshape(B, 1, 3 * D)

    for window, dil in DIL_GROUPS:
        assert window // (2 * dil) == HALF_WINDOW
    *qkvs, gate_a = _proj_a(x, mod0, _weights_a(a_w_in[0]))
    os_, lses = [], []
    for g, ((_, dil), qkv) in enumerate(zip(DIL_GROUPS, qkvs)):
        hs = slice(g * HEADS_PER_GROUP_A, (g + 1) * HEADS_PER_GROUP_A)
        o, lse = _attn_a(qkv, _bias_variants(rel_bias[:, hs], dil))
        os_.append(o)
        lses.append(lse)
    x1 = _out_a(x, mod0, os_, lses, gate_a, a_w_out[0].astype(jnp.bfloat16), ln_g[0], ln_b[0])

    w_in, wqa, wk, wv = _weights_b(b_w_in[0], b_w_uq[0], b_w_ukv[0])
    q2, k2, v2, gate_b = _proj_b(x1, mod1, w_in, b_q_norm[0], b_kv_norm[0], wqa, wk, wv,
                                 _rope_tables(S))
    o = _attn_b(q2, k2, v2)
    return _out_b(x1, mod1, o, gate_b, b_w_out[0].astype(jnp.bfloat16), ln_g[1], ln_b[1])
```

```python
import math

import numpy as np
import jax
import jax.numpy as jnp
from jax import lax
from jax.experimental import pallas as pl
from jax.experimental.pallas import tpu as pltpu

D_MODEL = 1024
DEPTH = 2
HEAD_DIM_A = 64
DIL_GROUPS = ((128, 1), (512, 4), (2048, 16))
HEADS_PER_GROUP_A = 6
GROUP_WIDTH_A = HEADS_PER_GROUP_A * HEAD_DIM_A
WIDTH_A = GROUP_WIDTH_A * len(DIL_GROUPS)
N_BUCKETS = 32
T5_MAX_DISTANCE = 1024
N_HEADS_B = 16
QK_NOPE_DIM = 64
QK_ROPE_DIM = 32
V_HEAD_DIM = 64
Q_LORA_RANK = 256
KV_LORA_RANK = 128
WIDTH_B = N_HEADS_B * V_HEAD_DIM
ROPE_THETA = 10000.0
DEEPNORM_ALPHA = (2.0 * DEPTH) ** 0.25
LN_EPS = 1e-5
RMS_EPS = 1e-6
NEG_INF = -1e30
LOG2E = math.log2(math.e)
LN2 = math.log(2.0)

LANES = 128
VMEM_LIMIT_BYTES = 56 * 1024 * 1024

HALF_WINDOW = 64
TQ_A = 128
TK_A = TQ_A + 2 * HALF_WINDOW
LSE_COPY_STRIDE = 8
LSE_COPIES = 3
MAX_TILES_PER_STEP_A = 4
TL_PROJ = 512
PERM_ROWS = 256
TQ_B = 512
TK_B = 512
HEADS_PER_STEP_B = 4

_NT = (((1,), (1,)), ((), ()))


def _silu(x):
    return x * (1.0 / (1.0 + jnp.exp(-x)))


def _params(*sem):
    return pltpu.CompilerParams(dimension_semantics=sem, vmem_limit_bytes=VMEM_LIMIT_BYTES)


def _mod_kernel(c_ref, w_ref, b_ref, o_ref):
    sc = _silu(c_ref[...]).astype(jnp.bfloat16)
    o_ref[...] = jnp.dot(sc, w_ref[...].astype(jnp.bfloat16),
                         preferred_element_type=jnp.float32) + b_ref[...]


def _modulation(c, ada_w, ada_b):
    B, D = c.shape
    nj = 3
    return pl.pallas_call(
        _mod_kernel,
        grid=(DEPTH, nj),
        in_specs=[pl.BlockSpec((B, D), lambda i, j: (0, 0)),
                  pl.BlockSpec((None, D, D), lambda i, j: (i, 0, j)),
                  pl.BlockSpec((None, 1, D), lambda i, j: (i, 0, j))],
        out_specs=pl.BlockSpec((None, B, D), lambda i, j: (i, 0, j)),
        out_shape=jax.ShapeDtypeStruct((DEPTH, B, 3 * D), jnp.float32),
        compiler_params=_params("arbitrary", "arbitrary"),
        name="modulation",
    )(c, ada_w, ada_b.reshape(DEPTH, 1, 3 * D))


def _modulate(x, mod):
    D = x.shape[-1]
    return x * (1.0 + mod[:, D:2 * D]) + mod[:, :D]


def _layernorm(z, g, b):
    mu = jnp.mean(z, axis=-1, keepdims=True)
    zc = z - mu
    var = jnp.mean(zc * zc, axis=-1, keepdims=True)
    return zc * lax.rsqrt(var + LN_EPS) * g + b


def _class_major_perm(tl, dil, inverse=False):
    p = np.arange(tl)
    src = (p % (tl // dil)) * dil + p // (tl // dil)
    mat = (src[:, None] == np.arange(tl)[None, :]).astype(np.float32)
    return jnp.asarray(mat.T if inverse else mat, dtype=jnp.bfloat16)


def _proj_a_kernel(x_ref, mod_ref, w0_ref, w1_ref, w2_ref, perm1_ref, perm2_ref,
                   qkv0_ref, qkv1_ref, qkv2_ref, gate_ref):
    tl = x_ref.shape[0]
    n_sub = tl // PERM_ROWS
    u = _modulate(x_ref[...], mod_ref[...]).astype(jnp.bfloat16)
    n_qkv = qkv0_ref.shape[-1]
    z = jnp.dot(u, w0_ref[...], preferred_element_type=jnp.float32)
    qkv0_ref[0] = z[:, :n_qkv].astype(qkv0_ref.dtype)
    gate_ref[...] = z[:, n_qkv:].astype(gate_ref.dtype)
    for perm_ref, w_ref, qkv_ref in ((perm1_ref, w1_ref, qkv1_ref), (perm2_ref, w2_ref, qkv2_ref)):
        pm = perm_ref[...]
        up = jnp.concatenate(
            [jnp.dot(pm, u[sub * PERM_ROWS:(sub + 1) * PERM_ROWS], preferred_element_type=jnp.float32)
             for sub in range(n_sub)], axis=0).astype(jnp.bfloat16)
        z = jnp.dot(up, w_ref[...], preferred_element_type=jnp.float32)
        dil, rows = qkv_ref.shape[:2]
        rs = rows // n_sub
        for sub in range(n_sub):
            for r in range(dil):
                r0 = sub * PERM_ROWS + r * rs
                qkv_ref[r, sub * rs:(sub + 1) * rs] = z[r0:r0 + rs].astype(qkv_ref.dtype)


def _proj_a(x, mod, ws):
    B, S, D = x.shape
    tl = TL_PROJ
    n_qkv = 3 * GROUP_WIDTH_A
    dils = [dil for _, dil in DIL_GROUPS]
    assert dils[0] == 1
    row = lambda b, i: (b, i, 0)
    cls = lambda b, i: (b, 0, i, 0)
    const2 = lambda b, i: (0, 0)
    perms = [_class_major_perm(PERM_ROWS, dil) for dil in dils[1:]]
    return pl.pallas_call(
        _proj_a_kernel,
        grid=(B, S // tl),
        in_specs=[pl.BlockSpec((None, tl, D), row),
                  pl.BlockSpec((None, 1, 3 * D), lambda b, i: (b, 0, 0)),
                  *[pl.BlockSpec(w.shape, const2) for w in ws],
                  *[pl.BlockSpec(p.shape, const2) for p in perms]],
        out_specs=[*[pl.BlockSpec((None, dil, tl // dil, n_qkv), cls) for dil in dils],
                   pl.BlockSpec((None, tl, WIDTH_A), row)],
        out_shape=[*[jax.ShapeDtypeStruct((B, dil, S // dil, n_qkv), jnp.bfloat16) for dil in dils],
                   jax.ShapeDtypeStruct((B, S, WIDTH_A), jnp.bfloat16)],
        compiler_params=_params("arbitrary", "arbitrary"),
        name="proj_a",
    )(x, mod, *ws, *perms)


def _attn_a_kernel(qkv_ref, bias_ref, o_ref, lse_ref, s_scr, p_scr):
    L = qkv_ref.shape[0]
    n_tiles = L // TQ_A
    gw = GROUP_WIDTH_A
    lane = lax.broadcasted_iota(jnp.int32, (TQ_A, LANES), 1)
    low_q = lane < HEAD_DIM_A
    lse_lane = jnp.where(lane < LSE_COPIES * LSE_COPY_STRIDE, lane % LSE_COPY_STRIDE, -1)
    low_k = lax.broadcasted_iota(jnp.int32, (TK_A, LANES), 1) < HEAD_DIM_A

    heads = range(HEADS_PER_GROUP_A)

    def pair_block(rows, section, h):
        c0 = section * gw + (h // 2) * LANES
        return qkv_ref[rows, c0:c0 + LANES]

    def other_lanes(block, low_mask, h, fill):
        mine = low_mask if h % 2 == 0 else jnp.logical_not(low_mask)
        return jnp.where(mine, block, jnp.full_like(block, fill))

    tiles_per_step = s_scr.shape[0]

    def tile_group(tg, carry):
        geo = []
        for u in range(tiles_per_step):
            t = tg * tiles_per_step + u
            j0 = pl.multiple_of(t * TQ_A, TQ_A)
            ws = pl.multiple_of(jnp.clip(j0 - HALF_WINDOW, 0, L - TK_A), HALF_WINDOW)
            variant = jnp.where(t == 0, 0, jnp.where(t == n_tiles - 1, 2, 1))
            geo.append((pl.ds(j0, TQ_A), pl.ds(ws, TK_A), variant))
        for u, (rq, rk, variant) in enumerate(geo):
            for h in heads:
                q = other_lanes(pair_block(rq, 0, h), low_q, h, 0)
                s = lax.dot_general(q, pair_block(rk, 1, h), _NT, preferred_element_type=jnp.float32)
                s_scr[u, h] = s + bias_ref[variant, h]
        m = [[jnp.max(s_scr[u, h], axis=-1, keepdims=True) for h in heads] for u in range(len(geo))]
        for u in range(len(geo)):
            for h in heads:
                p_scr[u, h] = jnp.exp2(s_scr[u, h] - m[u][h]).astype(jnp.bfloat16)
        for u, (rq, rk, variant) in enumerate(geo):
            lse_tile = jnp.zeros((TQ_A, LANES), jnp.float32)
            for pair in range(HEADS_PER_GROUP_A // 2):
                he, ho = 2 * pair, 2 * pair + 1
                acc_e = jnp.dot(p_scr[u, he], other_lanes(pair_block(rk, 2, he), low_k, he, 1),
                                preferred_element_type=jnp.float32)
                acc_o = jnp.dot(p_scr[u, ho], other_lanes(pair_block(rk, 2, ho), low_k, ho, 1),
                                preferred_element_type=jnp.float32)
                numer = jnp.where(low_q, acc_e, acc_o)
                den_other = jnp.where(low_q, acc_o, acc_e)
                den = pltpu.roll(den_other, HEAD_DIM_A, 1)
                o_ref[rq, pair * LANES:(pair + 1) * LANES] = (numer / den).astype(o_ref.dtype)
                is_odd = lse_lane == ho
                lse = (jnp.where(is_odd, m[u][ho], m[u][he])
                       + jnp.log2(jnp.where(is_odd, den_other, den))) * LN2
                lse_tile = jnp.where(jnp.logical_or(is_odd, lse_lane == he), lse, lse_tile)
            lse_ref[rq, :] = lse_tile
        return carry

    lax.fori_loop(0, n_tiles // tiles_per_step, tile_group, 0)


def _attn_a(qkv, biasm):
    B, dil, L, n_qkv = qkv.shape
    gw = GROUP_WIDTH_A
    tiles_per_step = math.gcd(MAX_TILES_PER_STEP_A, L // TQ_A)
    cls = lambda b, r: (b, r, 0, 0)
    return pl.pallas_call(
        _attn_a_kernel,
        grid=(B, dil),
        in_specs=[pl.BlockSpec((None, None, L, n_qkv), cls),
                  pl.BlockSpec(biasm.shape, lambda b, r: (0, 0, 0, 0))],
        out_specs=[pl.BlockSpec((None, None, L, gw), cls),
                   pl.BlockSpec((None, None, L, LANES), cls)],
        out_shape=[jax.ShapeDtypeStruct((B, dil, L, gw), jnp.bfloat16),
                   jax.ShapeDtypeStruct((B, dil, L, LANES), jnp.float32)],
        scratch_shapes=[
            pltpu.VMEM((tiles_per_step, HEADS_PER_GROUP_A, TQ_A, TK_A), jnp.float32),
            pltpu.VMEM((tiles_per_step, HEADS_PER_GROUP_A, TQ_A, TK_A), jnp.bfloat16)],
        compiler_params=_params("arbitrary", "arbitrary"),
        name=f"attn_a_d{dil}",
    )(qkv, biasm)


def _t5_bucket_np(rel):
    half = N_BUCKETS // 2
    max_exact = half // 2
    base = np.where(rel > 0, half, 0)
    n = np.abs(rel)
    nf = np.maximum(n, 1).astype(np.float32)
    large = max_exact + (np.log(nf / np.float32(max_exact)) / np.float32(math.log(T5_MAX_DISTANCE / max_exact))
                         * np.float32(half - max_exact)).astype(np.int32)
    large = np.minimum(large, half - 1)
    return base + np.where(n < max_exact, n, large)


def _bias_variants(rel_bias_g, dil):
    H = rel_bias_g.shape[1]
    M = 2 * TK_A
    off = np.arange(M) - (TK_A - 1)
    idx = np.where(np.abs(off) <= HALF_WINDOW, _t5_bucket_np(off * dil), N_BUCKETS).astype(np.int32)
    table = jnp.concatenate([rel_bias_g.astype(jnp.float32) * LOG2E, jnp.full((1, H), NEG_INF, jnp.float32)], axis=0)
    w = table[jnp.asarray(idx)].T
    rot = jnp.tile(w, (1, TQ_A + 1))[:, :TQ_A * (M + 1)].reshape(H, TQ_A, M + 1)[:, ::-1, :]
    variants = []
    for shift in (0, HALF_WINDOW, 2 * HALF_WINDOW):
        j0 = TK_A - TQ_A - shift
        variants.append(rot[:, :, j0:j0 + TK_A])
    return jnp.stack(variants)


def _split3_bf16(v):
    hi = v.astype(jnp.bfloat16)
    r1 = v - hi.astype(jnp.float32)
    mid = r1.astype(jnp.bfloat16)
    lo = (r1 - mid.astype(jnp.float32)).astype(jnp.bfloat16)
    return hi, mid, lo


N_MERGE_A_REFS = 15


def _merge_a(x_ref, mod_ref, o0_ref, o1_ref, o2_ref, l0_ref, l1_ref, l2_ref, gate_ref,
             unperm1_ref, unperm2_ref, expand_ref, w_ref, g_ref, b_ref):
    tl, D = x_ref.shape
    stride = LSE_COPY_STRIDE
    lane_p = lax.broadcasted_iota(jnp.int32, (PERM_ROWS, LANES), 1)
    lane = lax.broadcasted_iota(jnp.int32, (tl, LANES), 1)

    def class_major_rows(ref, sub):
        dil, rows = ref.shape[:2]
        rs = rows * PERM_ROWS // tl
        return jnp.concatenate([ref[r, sub * rs:(sub + 1) * rs] for r in range(dil)], axis=0)

    def natural_order(o_ref, l_ref, unperm_ref):
        if unperm_ref is None:
            return o_ref[0].astype(jnp.float32), l_ref[0]
        pt = unperm_ref[...]
        o_parts, l_parts = [], []
        for sub in range(tl // PERM_ROWS):
            o_parts.append(jnp.dot(pt, class_major_rows(o_ref, sub), preferred_element_type=jnp.float32))
            hi, mid, lo = _split3_bf16(class_major_rows(l_ref, sub))
            packed = jnp.where(lane_p < stride, hi, jnp.where(lane_p < 2 * stride, mid, lo))
            res = jnp.dot(pt, packed, preferred_element_type=jnp.float32)
            l_parts.append(res + pltpu.roll(res, LANES - stride, 1) + pltpu.roll(res, LANES - 2 * stride, 1))
        return jnp.concatenate(o_parts, axis=0), jnp.concatenate(l_parts, axis=0)

    o0, l0 = natural_order(o0_ref, l0_ref, None)
    o1, l1 = natural_order(o1_ref, l1_ref, unperm1_ref)
    o2, l2 = natural_order(o2_ref, l2_ref, unperm2_ref)
    mx = jnp.maximum(jnp.maximum(l0, l1), l2)
    e0, e1, e2 = jnp.exp(l0 - mx), jnp.exp(l1 - mx), jnp.exp(l2 - mx)
    inv = 1.0 / (e0 + e1 + e2)
    wt = jnp.where(lane < stride, e0 * inv,
                   jnp.where(lane < 2 * stride, pltpu.roll(e1 * inv, stride, 1), pltpu.roll(e2 * inv, 2 * stride, 1)))
    hi = wt.astype(jnp.bfloat16)
    lo = (wt - hi.astype(jnp.float32)).astype(jnp.bfloat16)
    wexp = (jnp.dot(hi, expand_ref[...], preferred_element_type=jnp.float32)
            + jnp.dot(lo, expand_ref[...], preferred_element_type=jnp.float32))
    o = jnp.concatenate([o0, o1, o2], axis=1)
    yg = (o * wexp * _silu(gate_ref[...].astype(jnp.float32))).astype(jnp.bfloat16)
    y = jnp.dot(yg, w_ref[...], preferred_element_type=jnp.float32)
    z = DEEPNORM_ALPHA * x_ref[...] + mod_ref[:, 2 * D:] * y
    return _layernorm(z, g_ref[...], b_ref[...])


def _rmsnorm(x, g):
    return x * lax.rsqrt(jnp.mean(x * x, axis=-1, keepdims=True) + RMS_EPS) * g


def _project_b(x, mod_ref, w_in_ref, qn_ref, kvn_ref, wqa_ref, wk_ref, wv_ref,
               vone_ref, cq_ref, sq_ref, ck_ref, sk_ref, q_out, k_out, v_out, gate_out):
    u = _modulate(x, mod_ref[...]).astype(jnp.bfloat16)
    z = jnp.dot(u, w_in_ref[...], preferred_element_type=jnp.float32)
    c0 = Q_LORA_RANK
    c1 = c0 + KV_LORA_RANK
    cq = z[:, :c0]
    ckv = z[:, c0:c1]
    kr = z[:, c1:c1 + LANES]
    gate_out[...] = z[:, c1 + LANES:].astype(gate_out.dtype)
    to_rope_lanes = LANES - QK_ROPE_DIM

    cqn = _rmsnorm(cq, qn_ref[...]).astype(jnp.bfloat16)
    qa = jnp.dot(cqn, wqa_ref[...], preferred_element_type=jnp.float32)
    ckvn = _rmsnorm(ckv, kvn_ref[...]).astype(jnp.bfloat16)
    kn = jnp.dot(ckvn, wk_ref[...], preferred_element_type=jnp.float32)
    v_out[...] = (jnp.dot(ckvn, wv_ref[...], preferred_element_type=jnp.float32)
                  + vone_ref[...]).astype(v_out.dtype)

    cos_q, sin_q = cq_ref[...], sq_ref[...]
    k_rope = kr * ck_ref[...] + pltpu.roll(kr, to_rope_lanes, 1) * sk_ref[...]
    for h in range(N_HEADS_B):
        blk = slice(h * LANES, (h + 1) * LANES)
        qh = qa[:, blk]
        q_out[:, blk] = (qh * cos_q + pltpu.roll(qh, to_rope_lanes, 1) * sin_q).astype(q_out.dtype)
        k_out[:, blk] = (kn[:, blk] + k_rope).astype(k_out.dtype)


def _mid_kernel(*refs):
    merge_refs, rest = refs[:N_MERGE_A_REFS], refs[N_MERGE_A_REFS:]
    x1_out, project_refs = rest[-5], rest[:-5] + rest[-4:]
    x1 = _merge_a(*merge_refs)
    x1_out[...] = x1
    _project_b(x1, *project_refs)


def _mid(x, mod0, os_, lses, gate_a, w_out_a, ln_g, ln_b,
         mod1, w_in, q_norm, kv_norm, wqa, wk, wv, tables):
    B, S, D = x.shape
    tl = TL_PROJ
    gw = GROUP_WIDTH_A
    HP = N_HEADS_B * LANES
    col = np.arange(WIDTH_A)
    src_lane = (col // gw) * LSE_COPY_STRIDE + (col % gw) // HEAD_DIM_A
    expand = jnp.asarray((np.arange(LANES)[:, None] == src_lane[None, :]).astype(np.float32), dtype=jnp.bfloat16)
    unperms = [_class_major_perm(PERM_ROWS, dil, inverse=True) for _, dil in DIL_GROUPS[1:]]
    row = lambda b, i: (b, i, 0)
    cls = lambda b, i: (b, 0, i, 0)
    const2 = lambda b, i: (0, 0)
    mod_spec = pl.BlockSpec((None, 1, 3 * D), lambda b, i: (b, 0, 0))
    tab = pl.BlockSpec((tl, LANES), lambda b, i: (i, 0))
    o_specs = [pl.BlockSpec((None, dil, tl // dil, gw), cls) for _, dil in DIL_GROUPS]
    l_specs = [pl.BlockSpec((None, dil, tl // dil, LANES), cls) for _, dil in DIL_GROUPS]
    merge_args = [x, mod0, *os_, *lses, gate_a, *unperms, expand, w_out_a, ln_g.reshape(1, D), ln_b.reshape(1, D)]
    merge_specs = [pl.BlockSpec((None, tl, D), row), mod_spec, *o_specs, *l_specs,
                   pl.BlockSpec((None, tl, WIDTH_A), row),
                   *[pl.BlockSpec(p.shape, const2) for p in unperms],
                   pl.BlockSpec(expand.shape, const2), pl.BlockSpec(w_out_a.shape, const2),
                   pl.BlockSpec((1, D), const2), pl.BlockSpec((1, D), const2)]
    assert len(merge_args) == N_MERGE_A_REFS
    project_args = [mod1, w_in, q_norm.reshape(1, -1), kv_norm.reshape(1, -1), wqa, wk, wv, _v_ones_row(), *tables]
    project_specs = [mod_spec, pl.BlockSpec(w_in.shape, const2),
                     pl.BlockSpec((1, Q_LORA_RANK), const2), pl.BlockSpec((1, KV_LORA_RANK), const2),
                     pl.BlockSpec(wqa.shape, const2), pl.BlockSpec(wk.shape, const2),
                     pl.BlockSpec(wv.shape, const2), pl.BlockSpec((1, HP), const2), tab, tab, tab, tab]
    wide = jax.ShapeDtypeStruct((B, S, HP), jnp.bfloat16)
    return pl.pallas_call(
        _mid_kernel,
        grid=(B, S // tl),
        in_specs=merge_specs + project_specs,
        out_specs=[pl.BlockSpec((None, tl, D), row),
                   pl.BlockSpec((None, tl, HP), row), pl.BlockSpec((None, tl, HP), row),
                   pl.BlockSpec((None, tl, HP), row), pl.BlockSpec((None, tl, WIDTH_B), row)],
        out_shape=[jax.ShapeDtypeStruct((B, S, D), jnp.float32), wide, wide, wide,
                   jax.ShapeDtypeStruct((B, S, WIDTH_B), jnp.bfloat16)],
        compiler_params=_params("arbitrary", "arbitrary"),
        name="mid",
    )(*merge_args, *project_args)


def _attn_b_kernel(q_ref, k_ref, v_ref, o_ref, s_even, s_odd):
    tq = q_ref.shape[0]
    S = k_ref.shape[0]
    n_chunks = S // TK_B
    s_bufs = (s_even, s_odd)

    def head_lanes(e):
        return slice(e * LANES, (e + 1) * LANES)

    def scores(e, c, mx):
        ks = slice(c * TK_B, (c + 1) * TK_B)
        s = lax.dot_general(q_ref[:, head_lanes(e)], k_ref[ks, head_lanes(e)], _NT,
                            preferred_element_type=jnp.float32)
        s_bufs[e % 2][:, ks] = s
        for j in range(TK_B // LANES):
            mx = jnp.maximum(mx, s[:, j * LANES:(j + 1) * LANES])
        return mx

    def weighted_values(e, c, m, acc):
        ks = slice(c * TK_B, (c + 1) * TK_B)
        p = jnp.exp2(s_bufs[e % 2][:, ks] - m).astype(jnp.bfloat16)
        return acc + jnp.dot(p, v_ref[ks, head_lanes(e)], preferred_element_type=jnp.float32)

    neg = jnp.full((tq, LANES), -jnp.inf, jnp.float32)
    zero = jnp.zeros((tq, LANES), jnp.float32)
    n_heads = q_ref.shape[1] // LANES
    mx = neg
    for c in range(n_chunks):
        mx = scores(0, c, mx)
    acc = []
    for e in range(n_heads):
        m = jnp.max(mx, axis=-1, keepdims=True)
        mx, a = neg, zero
        for c in range(n_chunks):
            if e + 1 < n_heads:
                mx = scores(e + 1, c, mx)
            a = weighted_values(e, c, m, a)
        acc.append(a)
    lane = lax.broadcasted_iota(jnp.int32, (tq, LANES), 1)
    for pair in range(n_heads // 2):
        acc_e, acc_o = acc[2 * pair], acc[2 * pair + 1]
        inv_e = 1.0 / acc_e[:, V_HEAD_DIM:V_HEAD_DIM + 1]
        inv_o = 1.0 / acc_o[:, 0:1]
        o_ref[:, pair * LANES:(pair + 1) * LANES] = jnp.where(
            lane < V_HEAD_DIM, acc_e * inv_e, acc_o * inv_o).astype(o_ref.dtype)


def _attn_b(q2, k2, v2):
    B, S, HP = q2.shape
    hw = HEADS_PER_STEP_B * LANES
    ow = HEADS_PER_STEP_B * V_HEAD_DIM
    return pl.pallas_call(
        _attn_b_kernel,
        grid=(B, N_HEADS_B // HEADS_PER_STEP_B, S // TQ_B),
        in_specs=[pl.BlockSpec((None, TQ_B, hw), lambda b, p, i: (b, i, p)),
                  pl.BlockSpec((None, S, hw), lambda b, p, i: (b, 0, p)),
                  pl.BlockSpec((None, S, hw), lambda b, p, i: (b, 0, p))],
        out_specs=pl.BlockSpec((None, TQ_B, ow), lambda b, p, i: (b, i, p)),
        out_shape=jax.ShapeDtypeStruct((B, S, WIDTH_B), jnp.bfloat16),
        scratch_shapes=[pltpu.VMEM((TQ_B, S), jnp.float32), pltpu.VMEM((TQ_B, S), jnp.float32)],
        compiler_params=_params("arbitrary", "arbitrary", "arbitrary"),
        name="attn_b",
    )(q2, k2, v2)


def _out_b_kernel(x_ref, mod_ref, o_ref, gate_ref, w_ref, g_ref, b_ref, out_ref):
    D = x_ref.shape[-1]
    yg = (o_ref[...].astype(jnp.float32) * _silu(gate_ref[...].astype(jnp.float32))).astype(jnp.bfloat16)
    y = jnp.dot(yg, w_ref[...], preferred_element_type=jnp.float32)
    z = DEEPNORM_ALPHA * x_ref[...] + mod_ref[:, 2 * D:] * y
    out_ref[...] = _layernorm(z, g_ref[...], b_ref[...])


def _out_b(x, mod, o, gate, w_out, ln_g, ln_b):
    B, S, D = x.shape
    tl = TL_PROJ
    row = lambda b, i: (b, i, 0)
    const2 = lambda b, i: (0, 0)
    return pl.pallas_call(
        _out_b_kernel,
        grid=(B, S // tl),
        in_specs=[pl.BlockSpec((None, tl, D), row),
                  pl.BlockSpec((None, 1, 3 * D), lambda b, i: (b, 0, 0)),
                  pl.BlockSpec((None, tl, WIDTH_B), row), pl.BlockSpec((None, tl, WIDTH_B), row),
                  pl.BlockSpec(w_out.shape, const2),
                  pl.BlockSpec((1, D), const2), pl.BlockSpec((1, D), const2)],
        out_specs=pl.BlockSpec((None, tl, D), row),
        out_shape=jax.ShapeDtypeStruct((B, S, D), jnp.float32),
        compiler_params=_params("arbitrary", "arbitrary"),
        name="out_b",
    )(x, mod, o, gate, w_out, ln_g.reshape(1, D), ln_b.reshape(1, D))


def _weights_a(a_w_in):
    W = WIDTH_A
    gw = GROUP_WIDTH_A
    q, k, v, gate = (a_w_in[:, i * W:(i + 1) * W] for i in range(4))
    ws = []
    for g in range(len(DIL_GROUPS)):
        cs = slice(g * gw, (g + 1) * gw)
        parts = [q[:, cs] * (HEAD_DIM_A ** -0.5 * LOG2E), k[:, cs], v[:, cs]]
        if g == 0:
            parts.append(gate)
        ws.append(jnp.concatenate(parts, axis=1).astype(jnp.bfloat16))
    return ws


def _swap_halves(w):
    half = w.shape[-1] // 2
    return jnp.concatenate([w[..., half:], w[..., :half]], axis=-1)


def _pad_lanes(w, lo):
    n = w.shape[-1]
    return jnp.pad(w, [(0, 0)] * (w.ndim - 1) + [(lo, LANES - lo - n)])


def _weights_b(b_w_in, b_w_uq, b_w_ukv):
    c0 = Q_LORA_RANK
    c1 = c0 + KV_LORA_RANK
    c2 = c1 + QK_ROPE_DIM
    kr = b_w_in[:, c1:c2]
    w_in = jnp.concatenate([b_w_in[:, :c1], _pad_lanes(jnp.concatenate([kr, _swap_halves(kr)], axis=1), QK_NOPE_DIM),
                            b_w_in[:, c2:]], axis=1)
    uq = b_w_uq.reshape(Q_LORA_RANK, N_HEADS_B, QK_NOPE_DIM + QK_ROPE_DIM)
    wqa = jnp.concatenate([uq, _swap_halves(uq[..., QK_NOPE_DIM:])], axis=-1).reshape(Q_LORA_RANK, -1)
    ukv = b_w_ukv.reshape(KV_LORA_RANK, N_HEADS_B, QK_NOPE_DIM + V_HEAD_DIM)
    wk = _pad_lanes(ukv[..., :QK_NOPE_DIM], 0).reshape(KV_LORA_RANK, -1)
    vv = ukv[..., QK_NOPE_DIM:]
    odd = (jnp.arange(N_HEADS_B) % 2 == 1)[None, :, None]
    wv = jnp.where(odd, _pad_lanes(vv, LANES - V_HEAD_DIM), _pad_lanes(vv, 0)).reshape(KV_LORA_RANK, -1)
    bf = lambda w: w.astype(jnp.bfloat16)
    return bf(w_in), bf(wqa), bf(wk), bf(wv)


def _rope_tables(S):
    pos = jnp.arange(S, dtype=jnp.float32)
    inv_freq = ROPE_THETA ** (-jnp.arange(0, QK_ROPE_DIM, 2, dtype=jnp.float32) / QK_ROPE_DIM)
    ang = pos[:, None] * inv_freq[None, :]
    cos, sin = jnp.cos(ang), jnp.sin(ang)
    cosf = _pad_lanes(jnp.concatenate([cos, cos], axis=-1), QK_NOPE_DIM)
    sinf = _pad_lanes(jnp.concatenate([-sin, sin], axis=-1), QK_NOPE_DIM)
    nope = _pad_lanes(jnp.ones((S, QK_NOPE_DIM), jnp.float32), 0)
    scale = (QK_NOPE_DIM + QK_ROPE_DIM) ** -0.5 * LOG2E
    return (scale * (nope + cosf), scale * sinf, cosf, sinf)


def _v_ones_row():
    row = np.zeros((N_HEADS_B, LANES), np.float32)
    row[0::2, V_HEAD_DIM] = 1.0
    row[1::2, 0] = 1.0
    return jnp.asarray(row.reshape(1, -1))


def kernel(x, c, rel_bias, ada_w, ada_b, ln_g, ln_b, a_w_in, a_w_out,
           b_w_in, b_q_norm, b_w_uq, b_kv_norm, b_w_ukv, b_w_out):
    B, S, D = x.shape
    mods = _modulation(c, ada_w, ada_b)
    mod0 = mods[0].reshape(B, 1, 3 * D)
    mod1 = mods[1].reshape(B, 1, 3 * D)

    for window, dil in DIL_GROUPS:
        assert window // (2 * dil) == HALF_WINDOW
    *qkvs, gate_a = _proj_a(x, mod0, _weights_a(a_w_in[0]))
    os_, lses = [], []
    for g, ((_, dil), qkv) in enumerate(zip(DIL_GROUPS, qkvs)):
        hs = slice(g * HEADS_PER_GROUP_A, (g + 1) * HEADS_PER_GROUP_A)
        o, lse = _attn_a(qkv, _bias_variants(rel_bias[:, hs], dil))
        os_.append(o)
        lses.append(lse)

    w_in, wqa, wk, wv = _weights_b(b_w_in[0], b_w_uq[0], b_w_ukv[0])
    x1, q2, k2, v2, gate_b = _mid(x, mod0, os_, lses, gate_a, a_w_out[0].astype(jnp.bfloat16), ln_g[0], ln_b[0],
                                  mod1, w_in, b_q_norm[0], b_kv_norm[0], wqa, wk, wv, _rope_tables(S))
    o = _attn_b(q2, k2, v2)
    return _out_b(x1, mod1, o, gate_b, b_w_out[0].astype(jnp.bfloat16), ln_g[1], ln_b[1])
```

```python
import math

import numpy as np
import jax
import jax.numpy as jnp
from jax import lax
from jax.experimental import pallas as pl
from jax.experimental.pallas import tpu as pltpu

D_MODEL = 1024
DEPTH = 2
HEAD_DIM_A = 64
DIL_GROUPS = ((128, 1), (512, 4), (2048, 16))
HEADS_PER_GROUP_A = 6
GROUP_WIDTH_A = HEADS_PER_GROUP_A * HEAD_DIM_A
WIDTH_A = GROUP_WIDTH_A * len(DIL_GROUPS)
N_BUCKETS = 32
T5_MAX_DISTANCE = 1024
N_HEADS_B = 16
QK_NOPE_DIM = 64
QK_ROPE_DIM = 32
V_HEAD_DIM = 64
Q_LORA_RANK = 256
KV_LORA_RANK = 128
WIDTH_B = N_HEADS_B * V_HEAD_DIM
ROPE_THETA = 10000.0
DEEPNORM_ALPHA = (2.0 * DEPTH) ** 0.25
LN_EPS = 1e-5
RMS_EPS = 1e-6
NEG_INF = -1e30
LOG2E = math.log2(math.e)
LN2 = math.log(2.0)

LANES = 128
VMEM_LIMIT_BYTES = 56 * 1024 * 1024

HALF_WINDOW = 64
TQ_A = 128
TK_A = TQ_A + 2 * HALF_WINDOW
LSE_COPY_STRIDE = 8
LSE_COPIES = 3
MAX_TILES_PER_STEP_A = 4
TL_PROJ = 512
PERM_ROWS = 256
TQ_B = 512
TK_B = 512
HEADS_PER_STEP_B = 4

_NT = (((1,), (1,)), ((), ()))


def _silu(x):
    return x * (1.0 / (1.0 + jnp.exp(-x)))


def _params(*sem):
    return pltpu.CompilerParams(dimension_semantics=sem, vmem_limit_bytes=VMEM_LIMIT_BYTES)


def _mod_kernel(c_ref, w_ref, b_ref, o_ref):
    sc = _silu(c_ref[...]).astype(jnp.bfloat16)
    o_ref[...] = jnp.dot(sc, w_ref[...].astype(jnp.bfloat16),
                         preferred_element_type=jnp.float32) + b_ref[...]


def _modulation(c, ada_w, ada_b):
    B, D = c.shape
    nj = 3
    return pl.pallas_call(
        _mod_kernel,
        grid=(DEPTH, nj),
        in_specs=[pl.BlockSpec((B, D), lambda i, j: (0, 0)),
                  pl.BlockSpec((None, D, D), lambda i, j: (i, 0, j)),
                  pl.BlockSpec((None, 1, D), lambda i, j: (i, 0, j))],
        out_specs=pl.BlockSpec((None, B, D), lambda i, j: (i, 0, j)),
        out_shape=jax.ShapeDtypeStruct((DEPTH, B, 3 * D), jnp.float32),
        compiler_params=_params("arbitrary", "arbitrary"),
        name="modulation",
    )(c, ada_w, ada_b.reshape(DEPTH, 1, 3 * D))


def _modulate(x, mod):
    D = x.shape[-1]
    return x * (1.0 + mod[:, D:2 * D]) + mod[:, :D]


def _layernorm(z, g, b):
    mu = jnp.mean(z, axis=-1, keepdims=True)
    zc = z - mu
    var = jnp.mean(zc * zc, axis=-1, keepdims=True)
    return zc * lax.rsqrt(var + LN_EPS) * g + b


def _class_major_perm(tl, dil, inverse=False):
    p = np.arange(tl)
    src = (p % (tl // dil)) * dil + p // (tl // dil)
    mat = (src[:, None] == np.arange(tl)[None, :]).astype(np.float32)
    return jnp.asarray(mat.T if inverse else mat, dtype=jnp.bfloat16)


def _proj_a_kernel(x_ref, mod_ref, w0_ref, w1_ref, w2_ref, perm1_ref, perm2_ref,
                   qkv0_ref, qkv1_ref, qkv2_ref, gate_ref):
    tl = x_ref.shape[0]
    n_sub = tl // PERM_ROWS
    u = _modulate(x_ref[...], mod_ref[...]).astype(jnp.bfloat16)
    n_qkv = qkv0_ref.shape[-1]
    z = jnp.dot(u, w0_ref[...], preferred_element_type=jnp.float32)
    qkv0_ref[0] = z[:, :n_qkv].astype(qkv0_ref.dtype)
    gate_ref[...] = z[:, n_qkv:].astype(gate_ref.dtype)
    for perm_ref, w_ref, qkv_ref in ((perm1_ref, w1_ref, qkv1_ref), (perm2_ref, w2_ref, qkv2_ref)):
        pm = perm_ref[...]
        up = jnp.concatenate(
            [jnp.dot(pm, u[sub * PERM_ROWS:(sub + 1) * PERM_ROWS], preferred_element_type=jnp.float32)
             for sub in range(n_sub)], axis=0).astype(jnp.bfloat16)
        z = jnp.dot(up, w_ref[...], preferred_element_type=jnp.float32)
        dil, rows = qkv_ref.shape[:2]
        rs = rows // n_sub
        for sub in range(n_sub):
            for r in range(dil):
                r0 = sub * PERM_ROWS + r * rs
                qkv_ref[r, sub * rs:(sub + 1) * rs] = z[r0:r0 + rs].astype(qkv_ref.dtype)


def _proj_a(x, mod, ws):
    B, S, D = x.shape
    tl = TL_PROJ
    n_qkv = 3 * GROUP_WIDTH_A
    dils = [dil for _, dil in DIL_GROUPS]
    assert dils[0] == 1
    row = lambda b, i: (b, i, 0)
    cls = lambda b, i: (b, 0, i, 0)
    const2 = lambda b, i: (0, 0)
    perms = [_class_major_perm(PERM_ROWS, dil) for dil in dils[1:]]
    return pl.pallas_call(
        _proj_a_kernel,
        grid=(B, S // tl),
        in_specs=[pl.BlockSpec((None, tl, D), row),
                  pl.BlockSpec((None, 1, 3 * D), lambda b, i: (b, 0, 0)),
                  *[pl.BlockSpec(w.shape, const2) for w in ws],
                  *[pl.BlockSpec(p.shape, const2) for p in perms]],
        out_specs=[*[pl.BlockSpec((None, dil, tl // dil, n_qkv), cls) for dil in dils],
                   pl.BlockSpec((None, tl, WIDTH_A), row)],
        out_shape=[*[jax.ShapeDtypeStruct((B, dil, S // dil, n_qkv), jnp.bfloat16) for dil in dils],
                   jax.ShapeDtypeStruct((B, S, WIDTH_A), jnp.bfloat16)],
        compiler_params=_params("arbitrary", "arbitrary"),
        name="proj_a",
    )(x, mod, *ws, *perms)


def _attn_a_kernel(qkv_ref, bias_ref, o_ref, lse_ref, s_scr, p_scr):
    L = qkv_ref.shape[0]
    n_tiles = L // TQ_A
    gw = GROUP_WIDTH_A
    lane = lax.broadcasted_iota(jnp.int32, (TQ_A, LANES), 1)
    low_q = lane < HEAD_DIM_A
    lse_lane = jnp.where(lane < LSE_COPIES * LSE_COPY_STRIDE, lane % LSE_COPY_STRIDE, -1)
    low_k = lax.broadcasted_iota(jnp.int32, (TK_A, LANES), 1) < HEAD_DIM_A

    heads = range(HEADS_PER_GROUP_A)

    def pair_block(rows, section, h):
        c0 = section * gw + (h // 2) * LANES
        return qkv_ref[rows, c0:c0 + LANES]

    def other_lanes(block, low_mask, h, fill):
        mine = low_mask if h % 2 == 0 else jnp.logical_not(low_mask)
        return jnp.where(mine, block, jnp.full_like(block, fill))

    tiles_per_step = s_scr.shape[0]

    def tile_group(tg, carry):
        geo = []
        for u in range(tiles_per_step):
            t = tg * tiles_per_step + u
            j0 = pl.multiple_of(t * TQ_A, TQ_A)
            ws = pl.multiple_of(jnp.clip(j0 - HALF_WINDOW, 0, L - TK_A), HALF_WINDOW)
            variant = jnp.where(t == 0, 0, jnp.where(t == n_tiles - 1, 2, 1))
            geo.append((pl.ds(j0, TQ_A), pl.ds(ws, TK_A), variant))
        for u, (rq, rk, variant) in enumerate(geo):
            for h in heads:
                q = other_lanes(pair_block(rq, 0, h), low_q, h, 0)
                s = lax.dot_general(q, pair_block(rk, 1, h), _NT, preferred_element_type=jnp.float32)
                s_scr[u, h] = s + bias_ref[variant, h]
        m = [[jnp.max(s_scr[u, h], axis=-1, keepdims=True) for h in heads] for u in range(len(geo))]
        for u in range(len(geo)):
            for h in heads:
                p_scr[u, h] = jnp.exp2(s_scr[u, h] - m[u][h]).astype(jnp.bfloat16)
        for u, (rq, rk, variant) in enumerate(geo):
            lse_tile = jnp.zeros((TQ_A, LANES), jnp.float32)
            for pair in range(HEADS_PER_GROUP_A // 2):
                he, ho = 2 * pair, 2 * pair + 1
                acc_e = jnp.dot(p_scr[u, he], other_lanes(pair_block(rk, 2, he), low_k, he, 1),
                                preferred_element_type=jnp.float32)
                acc_o = jnp.dot(p_scr[u, ho], other_lanes(pair_block(rk, 2, ho), low_k, ho, 1),
                                preferred_element_type=jnp.float32)
                numer = jnp.where(low_q, acc_e, acc_o)
                den_other = jnp.where(low_q, acc_o, acc_e)
                den = pltpu.roll(den_other, HEAD_DIM_A, 1)
                o_ref[rq, pair * LANES:(pair + 1) * LANES] = (numer / den).astype(o_ref.dtype)
                is_odd = lse_lane == ho
                lse = (jnp.where(is_odd, m[u][ho], m[u][he])
                       + jnp.log2(jnp.where(is_odd, den_other, den))) * LN2
                lse_tile = jnp.where(jnp.logical_or(is_odd, lse_lane == he), lse, lse_tile)
            lse_ref[rq, :] = lse_tile
        return carry

    lax.fori_loop(0, n_tiles // tiles_per_step, tile_group, 0)


def _attn_a(qkv, biasm):
    B, dil, L, n_qkv = qkv.shape
    gw = GROUP_WIDTH_A
    tiles_per_step = math.gcd(MAX_TILES_PER_STEP_A, L // TQ_A)
    cls = lambda b, r: (b, r, 0, 0)
    return pl.pallas_call(
        _attn_a_kernel,
        grid=(B, dil),
        in_specs=[pl.BlockSpec((None, None, L, n_qkv), cls),
                  pl.BlockSpec(biasm.shape, lambda b, r: (0, 0, 0, 0))],
        out_specs=[pl.BlockSpec((None, None, L, gw), cls),
                   pl.BlockSpec((None, None, L, LANES), cls)],
        out_shape=[jax.ShapeDtypeStruct((B, dil, L, gw), jnp.bfloat16),
                   jax.ShapeDtypeStruct((B, dil, L, LANES), jnp.float32)],
        scratch_shapes=[
            pltpu.VMEM((tiles_per_step, HEADS_PER_GROUP_A, TQ_A, TK_A), jnp.float32),
            pltpu.VMEM((tiles_per_step, HEADS_PER_GROUP_A, TQ_A, TK_A), jnp.bfloat16)],
        compiler_params=_params("arbitrary", "arbitrary"),
        name=f"attn_a_d{dil}",
    )(qkv, biasm)


def _t5_bucket_np(rel):
    half = N_BUCKETS // 2
    max_exact = half // 2
    base = np.where(rel > 0, half, 0)
    n = np.abs(rel)
    nf = np.maximum(n, 1).astype(np.float32)
    large = max_exact + (np.log(nf / np.float32(max_exact)) / np.float32(math.log(T5_MAX_DISTANCE / max_exact))
                         * np.float32(half - max_exact)).astype(np.int32)
    large = np.minimum(large, half - 1)
    return base + np.where(n < max_exact, n, large)


def _bias_variants(rel_bias_g, dil):
    H = rel_bias_g.shape[1]
    M = 2 * TK_A
    off = np.arange(M) - (TK_A - 1)
    idx = np.where(np.abs(off) <= HALF_WINDOW, _t5_bucket_np(off * dil), N_BUCKETS).astype(np.int32)
    table = jnp.concatenate([rel_bias_g.astype(jnp.float32) * LOG2E, jnp.full((1, H), NEG_INF, jnp.float32)], axis=0)
    w = table[jnp.asarray(idx)].T
    rot = jnp.tile(w, (1, TQ_A + 1))[:, :TQ_A * (M + 1)].reshape(H, TQ_A, M + 1)[:, ::-1, :]
    variants = []
    for shift in (0, HALF_WINDOW, 2 * HALF_WINDOW):
        j0 = TK_A - TQ_A - shift
        variants.append(rot[:, :, j0:j0 + TK_A])
    return jnp.stack(variants)


def _split3_bf16(v):
    hi = v.astype(jnp.bfloat16)
    r1 = v - hi.astype(jnp.float32)
    mid = r1.astype(jnp.bfloat16)
    lo = (r1 - mid.astype(jnp.float32)).astype(jnp.bfloat16)
    return hi, mid, lo


N_MERGE_A_REFS = 15


def _merge_a(x_ref, mod_ref, o0_ref, o1_ref, o2_ref, l0_ref, l1_ref, l2_ref, gate_ref,
             unperm1_ref, unperm2_ref, expand_ref, w_ref, g_ref, b_ref):
    tl, D = x_ref.shape
    stride = LSE_COPY_STRIDE
    lane_p = lax.broadcasted_iota(jnp.int32, (PERM_ROWS, LANES), 1)
    lane = lax.broadcasted_iota(jnp.int32, (tl, LANES), 1)

    def class_major_rows(ref, sub):
        dil, rows = ref.shape[:2]
        rs = rows * PERM_ROWS // tl
        return jnp.concatenate([ref[r, sub * rs:(sub + 1) * rs] for r in range(dil)], axis=0)

    def natural_order(o_ref, l_ref, unperm_ref):
        if unperm_ref is None:
            return o_ref[0].astype(jnp.float32), l_ref[0]
        pt = unperm_ref[...]
        o_parts, l_parts = [], []
        for sub in range(tl // PERM_ROWS):
            o_parts.append(jnp.dot(pt, class_major_rows(o_ref, sub), preferred_element_type=jnp.float32))
            hi, mid, lo = _split3_bf16(class_major_rows(l_ref, sub))
            packed = jnp.where(lane_p < stride, hi, jnp.where(lane_p < 2 * stride, mid, lo))
            res = jnp.dot(pt, packed, preferred_element_type=jnp.float32)
            l_parts.append(res + pltpu.roll(res, LANES - stride, 1) + pltpu.roll(res, LANES - 2 * stride, 1))
        return jnp.concatenate(o_parts, axis=0), jnp.concatenate(l_parts, axis=0)

    o0, l0 = natural_order(o0_ref, l0_ref, None)
    o1, l1 = natural_order(o1_ref, l1_ref, unperm1_ref)
    o2, l2 = natural_order(o2_ref, l2_ref, unperm2_ref)
    mx = jnp.maximum(jnp.maximum(l0, l1), l2)
    e0, e1, e2 = jnp.exp(l0 - mx), jnp.exp(l1 - mx), jnp.exp(l2 - mx)
    inv = 1.0 / (e0 + e1 + e2)
    wt = jnp.where(lane < stride, e0 * inv,
                   jnp.where(lane < 2 * stride, pltpu.roll(e1 * inv, stride, 1), pltpu.roll(e2 * inv, 2 * stride, 1)))
    hi = wt.astype(jnp.bfloat16)
    lo = (wt - hi.astype(jnp.float32)).astype(jnp.bfloat16)
    wexp = (jnp.dot(hi, expand_ref[...], preferred_element_type=jnp.float32)
            + jnp.dot(lo, expand_ref[...], preferred_element_type=jnp.float32))
    o = jnp.concatenate([o0, o1, o2], axis=1)
    yg = (o * wexp * _silu(gate_ref[...].astype(jnp.float32))).astype(jnp.bfloat16)
    y = jnp.dot(yg, w_ref[...], preferred_element_type=jnp.float32)
    z = DEEPNORM_ALPHA * x_ref[...] + mod_ref[:, 2 * D:] * y
    return _layernorm(z, g_ref[...], b_ref[...])


def _rmsnorm(x, g):
    return x * lax.rsqrt(jnp.mean(x * x, axis=-1, keepdims=True) + RMS_EPS) * g


def _project_b(x, mod_ref, w_in_ref, qn_ref, kvn_ref, wqa_ref, wk_ref, wv_ref,
               vone_ref, cq_ref, sq_ref, ck_ref, sk_ref, q_out, k_out, v_out, gate_out):
    u = _modulate(x, mod_ref[...]).astype(jnp.bfloat16)
    z = jnp.dot(u, w_in_ref[...], preferred_element_type=jnp.float32)
    c0 = Q_LORA_RANK
    c1 = c0 + KV_LORA_RANK
    cq = z[:, :c0]
    ckv = z[:, c0:c1]
    kr = z[:, c1:c1 + LANES]
    gate_out[...] = z[:, c1 + LANES:].astype(gate_out.dtype)
    to_rope_lanes = LANES - QK_ROPE_DIM

    cqn = _rmsnorm(cq, qn_ref[...]).astype(jnp.bfloat16)
    qa = jnp.dot(cqn, wqa_ref[...], preferred_element_type=jnp.float32)
    ckvn = _rmsnorm(ckv, kvn_ref[...]).astype(jnp.bfloat16)
    kn = jnp.dot(ckvn, wk_ref[...], preferred_element_type=jnp.float32)
    v_out[...] = (jnp.dot(ckvn, wv_ref[...], preferred_element_type=jnp.float32)
                  + vone_ref[...]).astype(v_out.dtype)

    cos_q, sin_q = cq_ref[...], sq_ref[...]
    k_rope = kr * ck_ref[...] + pltpu.roll(kr, to_rope_lanes, 1) * sk_ref[...]
    for h in range(N_HEADS_B):
        blk = slice(h * LANES, (h + 1) * LANES)
        qh = qa[:, blk]
        q_out[:, blk] = (qh * cos_q + pltpu.roll(qh, to_rope_lanes, 1) * sin_q).astype(q_out.dtype)
        k_out[:, blk] = (kn[:, blk] + k_rope).astype(k_out.dtype)


def _mid_kernel(*refs):
    merge_refs, rest = refs[:N_MERGE_A_REFS], refs[N_MERGE_A_REFS:]
    x1_out, project_refs = rest[-5], rest[:-5] + rest[-4:]
    x1 = _merge_a(*merge_refs)
    x1_out[...] = x1
    _project_b(x1, *project_refs)


def _mid(x, mod0, os_, lses, gate_a, w_out_a, ln_g, ln_b,
         mod1, w_in, q_norm, kv_norm, wqa, wk, wv, tables):
    B, S, D = x.shape
    tl = TL_PROJ
    gw = GROUP_WIDTH_A
    HP = N_HEADS_B * LANES
    col = np.arange(WIDTH_A)
    src_lane = (col // gw) * LSE_COPY_STRIDE + (col % gw) // HEAD_DIM_A
    expand = jnp.asarray((np.arange(LANES)[:, None] == src_lane[None, :]).astype(np.float32), dtype=jnp.bfloat16)
    unperms = [_class_major_perm(PERM_ROWS, dil, inverse=True) for _, dil in DIL_GROUPS[1:]]
    row = lambda b, i: (b, i, 0)
    cls = lambda b, i: (b, 0, i, 0)
    const2 = lambda b, i: (0, 0)
    mod_spec = pl.BlockSpec((None, 1, 3 * D), lambda b, i: (b, 0, 0))
    tab = pl.BlockSpec((tl, LANES), lambda b, i: (i, 0))
    o_specs = [pl.BlockSpec((None, dil, tl // dil, gw), cls) for _, dil in DIL_GROUPS]
    l_specs = [pl.BlockSpec((None, dil, tl // dil, LANES), cls) for _, dil in DIL_GROUPS]
    merge_args = [x, mod0, *os_, *lses, gate_a, *unperms, expand, w_out_a, ln_g.reshape(1, D), ln_b.reshape(1, D)]
    merge_specs = [pl.BlockSpec((None, tl, D), row), mod_spec, *o_specs, *l_specs,
                   pl.BlockSpec((None, tl, WIDTH_A), row),
                   *[pl.BlockSpec(p.shape, const2) for p in unperms],
                   pl.BlockSpec(expand.shape, const2), pl.BlockSpec(w_out_a.shape, const2),
                   pl.BlockSpec((1, D), const2), pl.BlockSpec((1, D), const2)]
    assert len(merge_args) == N_MERGE_A_REFS
    project_args = [mod1, w_in, q_norm.reshape(1, -1), kv_norm.reshape(1, -1), wqa, wk, wv, _v_ones_row(), *tables]
    project_specs = [mod_spec, pl.BlockSpec(w_in.shape, const2),
                     pl.BlockSpec((1, Q_LORA_RANK), const2), pl.BlockSpec((1, KV_LORA_RANK), const2),
                     pl.BlockSpec(wqa.shape, const2), pl.BlockSpec(wk.shape, const2),
                     pl.BlockSpec(wv.shape, const2), pl.BlockSpec((1, HP), const2), tab, tab, tab, tab]
    wide = jax.ShapeDtypeStruct((B, S, HP), jnp.bfloat16)
    return pl.pallas_call(
        _mid_kernel,
        grid=(B, S // tl),
        in_specs=merge_specs + project_specs,
        out_specs=[pl.BlockSpec((None, tl, D), row),
                   pl.BlockSpec((None, tl, HP), row), pl.BlockSpec((None, tl, HP), row),
                   pl.BlockSpec((None, tl, HP), row), pl.BlockSpec((None, tl, WIDTH_B), row)],
        out_shape=[jax.ShapeDtypeStruct((B, S, D), jnp.float32), wide, wide, wide,
                   jax.ShapeDtypeStruct((B, S, WIDTH_B), jnp.bfloat16)],
        compiler_params=_params("arbitrary", "arbitrary"),
        name="mid",
    )(*merge_args, *project_args)


def _attn_b_kernel(q_ref, k_ref, v_ref, o_ref, s_even, s_odd, mx_scr):
    S = k_ref.shape[0]
    tq = s_even.shape[0]
    n_tiles = S // tq
    n_chunks = S // TK_B
    n_heads = q_ref.shape[1] // LANES
    assert n_heads % 2 == 0
    s_bufs = (s_even, s_odd)

    def head_lanes(e):
        return slice(e * LANES, (e + 1) * LANES)

    def tile_rows(t):
        return pl.ds(t * tq, tq) if isinstance(t, int) else pl.ds(pl.multiple_of(t * tq, tq), tq)

    def scores(t, e, c, mx):
        ks = slice(c * TK_B, (c + 1) * TK_B)
        s = lax.dot_general(q_ref[tile_rows(t), head_lanes(e)], k_ref[ks, head_lanes(e)], _NT,
                            preferred_element_type=jnp.float32)
        s_bufs[e % 2][:, ks] = s
        for j in range(TK_B // LANES):
            mx = jnp.maximum(mx, s[:, j * LANES:(j + 1) * LANES])
        return mx

    def weighted_values(e, c, m, acc):
        ks = slice(c * TK_B, (c + 1) * TK_B)
        p = jnp.exp2(s_bufs[e % 2][:, ks] - m).astype(jnp.bfloat16)
        return acc + jnp.dot(p, v_ref[ks, head_lanes(e)], preferred_element_type=jnp.float32)

    neg = jnp.full((tq, LANES), -jnp.inf, jnp.float32)
    zero = jnp.zeros((tq, LANES), jnp.float32)
    lane = lax.broadcasted_iota(jnp.int32, (tq, LANES), 1)

    mx = neg
    for c in range(n_chunks):
        mx = scores(0, 0, c, mx)
    mx_scr[...] = mx

    def tile(t, carry):
        t_next = jnp.minimum(t + 1, n_tiles - 1)
        mx = mx_scr[...]
        acc = []
        for e in range(n_heads):
            m = jnp.max(mx, axis=-1, keepdims=True)
            mx, a = neg, zero
            for c in range(n_chunks):
                mx = scores(t, e + 1, c, mx) if e + 1 < n_heads else scores(t_next, 0, c, mx)
                a = weighted_values(e, c, m, a)
            acc.append(a)
        mx_scr[...] = mx
        for pair in range(n_heads // 2):
            acc_e, acc_o = acc[2 * pair], acc[2 * pair + 1]
            inv_e = 1.0 / acc_e[:, V_HEAD_DIM:V_HEAD_DIM + 1]
            inv_o = 1.0 / acc_o[:, 0:1]
            o_ref[tile_rows(t), pair * LANES:(pair + 1) * LANES] = jnp.where(
                lane < V_HEAD_DIM, acc_e * inv_e, acc_o * inv_o).astype(o_ref.dtype)
        return carry

    lax.fori_loop(0, n_tiles, tile, 0)


def _attn_b(q2, k2, v2):
    B, S, HP = q2.shape
    hw = HEADS_PER_STEP_B * LANES
    ow = HEADS_PER_STEP_B * V_HEAD_DIM
    heads = lambda b, p: (b, 0, p)
    return pl.pallas_call(
        _attn_b_kernel,
        grid=(B, N_HEADS_B // HEADS_PER_STEP_B),
        in_specs=[pl.BlockSpec((None, S, hw), heads), pl.BlockSpec((None, S, hw), heads),
                  pl.BlockSpec((None, S, hw), heads)],
        out_specs=pl.BlockSpec((None, S, ow), heads),
        out_shape=jax.ShapeDtypeStruct((B, S, WIDTH_B), jnp.bfloat16),
        scratch_shapes=[pltpu.VMEM((TQ_B, S), jnp.float32), pltpu.VMEM((TQ_B, S), jnp.float32),
                        pltpu.VMEM((TQ_B, LANES), jnp.float32)],
        compiler_params=_params("arbitrary", "arbitrary"),
        name="attn_b",
    )(q2, k2, v2)


def _out_b_kernel(x_ref, mod_ref, o_ref, gate_ref, w_ref, g_ref, b_ref, out_ref):
    D = x_ref.shape[-1]
    yg = (o_ref[...].astype(jnp.float32) * _silu(gate_ref[...].astype(jnp.float32))).astype(jnp.bfloat16)
    y = jnp.dot(yg, w_ref[...], preferred_element_type=jnp.float32)
    z = DEEPNORM_ALPHA * x_ref[...] + mod_ref[:, 2 * D:] * y
    out_ref[...] = _layernorm(z, g_ref[...], b_ref[...])


def _out_b(x, mod, o, gate, w_out, ln_g, ln_b):
    B, S, D = x.shape
    tl = TL_PROJ
    row = lambda b, i: (b, i, 0)
    const2 = lambda b, i: (0, 0)
    return pl.pallas_call(
        _out_b_kernel,
        grid=(B, S // tl),
        in_specs=[pl.BlockSpec((None, tl, D), row),
                  pl.BlockSpec((None, 1, 3 * D), lambda b, i: (b, 0, 0)),
                  pl.BlockSpec((None, tl, WIDTH_B), row), pl.BlockSpec((None, tl, WIDTH_B), row),
                  pl.BlockSpec(w_out.shape, const2),
                  pl.BlockSpec((1, D), const2), pl.BlockSpec((1, D), const2)],
        out_specs=pl.BlockSpec((None, tl, D), row),
        out_shape=jax.ShapeDtypeStruct((B, S, D), jnp.float32),
        compiler_params=_params("arbitrary", "arbitrary"),
        name="out_b",
    )(x, mod, o, gate, w_out, ln_g.reshape(1, D), ln_b.reshape(1, D))


def _weights_a(a_w_in):
    W = WIDTH_A
    gw = GROUP_WIDTH_A
    q, k, v, gate = (a_w_in[:, i * W:(i + 1) * W] for i in range(4))
    ws = []
    for g in range(len(DIL_GROUPS)):
        cs = slice(g * gw, (g + 1) * gw)
        parts = [q[:, cs] * (HEAD_DIM_A ** -0.5 * LOG2E), k[:, cs], v[:, cs]]
        if g == 0:
            parts.append(gate)
        ws.append(jnp.concatenate(parts, axis=1).astype(jnp.bfloat16))
    return ws


def _swap_halves(w):
    half = w.shape[-1] // 2
    return jnp.concatenate([w[..., half:], w[..., :half]], axis=-1)


def _pad_lanes(w, lo):
    n = w.shape[-1]
    return jnp.pad(w, [(0, 0)] * (w.ndim - 1) + [(lo, LANES - lo - n)])


def _weights_b(b_w_in, b_w_uq, b_w_ukv):
    c0 = Q_LORA_RANK
    c1 = c0 + KV_LORA_RANK
    c2 = c1 + QK_ROPE_DIM
    kr = b_w_in[:, c1:c2]
    w_in = jnp.concatenate([b_w_in[:, :c1], _pad_lanes(jnp.concatenate([kr, _swap_halves(kr)], axis=1), QK_NOPE_DIM),
                            b_w_in[:, c2:]], axis=1)
    uq = b_w_uq.reshape(Q_LORA_RANK, N_HEADS_B, QK_NOPE_DIM + QK_ROPE_DIM)
    wqa = jnp.concatenate([uq, _swap_halves(uq[..., QK_NOPE_DIM:])], axis=-1).reshape(Q_LORA_RANK, -1)
    ukv = b_w_ukv.reshape(KV_LORA_RANK, N_HEADS_B, QK_NOPE_DIM + V_HEAD_DIM)
    wk = _pad_lanes(ukv[..., :QK_NOPE_DIM], 0).reshape(KV_LORA_RANK, -1)
    vv = ukv[..., QK_NOPE_DIM:]
    odd = (jnp.arange(N_HEADS_B) % 2 == 1)[None, :, None]
    wv = jnp.where(odd, _pad_lanes(vv, LANES - V_HEAD_DIM), _pad_lanes(vv, 0)).reshape(KV_LORA_RANK, -1)
    bf = lambda w: w.astype(jnp.bfloat16)
    return bf(w_in), bf(wqa), bf(wk), bf(wv)


def _rope_tables(S):
    pos = jnp.arange(S, dtype=jnp.float32)
    inv_freq = ROPE_THETA ** (-jnp.arange(0, QK_ROPE_DIM, 2, dtype=jnp.float32) / QK_ROPE_DIM)
    ang = pos[:, None] * inv_freq[None, :]
    cos, sin = jnp.cos(ang), jnp.sin(ang)
    cosf = _pad_lanes(jnp.concatenate([cos, cos], axis=-1), QK_NOPE_DIM)
    sinf = _pad_lanes(jnp.concatenate([-sin, sin], axis=-1), QK_NOPE_DIM)
    nope = _pad_lanes(jnp.ones((S, QK_NOPE_DIM), jnp.float32), 0)
    scale = (QK_NOPE_DIM + QK_ROPE_DIM) ** -0.5 * LOG2E
    return (scale * (nope + cosf), scale * sinf, cosf, sinf)


def _v_ones_row():
    row = np.zeros((N_HEADS_B, LANES), np.float32)
    row[0::2, V_HEAD_DIM] = 1.0
    row[1::2, 0] = 1.0
    return jnp.asarray(row.reshape(1, -1))


def kernel(x, c, rel_bias, ada_w, ada_b, ln_g, ln_b, a_w_in, a_w_out,
           b_w_in, b_q_norm, b_w_uq, b_kv_norm, b_w_ukv, b_w_out):
    B, S, D = x.shape
    mods = _modulation(c, ada_w, ada_b)
    mod0 = mods[0].reshape(B, 1, 3 * D)
    mod1 = mods[1].reshape(B, 1, 3 * D)

    for window, dil in DIL_GROUPS:
        assert window // (2 * dil) == HALF_WINDOW
    *qkvs, gate_a = _proj_a(x, mod0, _weights_a(a_w_in[0]))
    os_, lses = [], []
    for g, ((_, dil), qkv) in enumerate(zip(DIL_GROUPS, qkvs)):
        hs = slice(g * HEADS_PER_GROUP_A, (g + 1) * HEADS_PER_GROUP_A)
        o, lse = _attn_a(qkv, _bias_variants(rel_bias[:, hs], dil))
        os_.append(o)
        lses.append(lse)

    w_in, wqa, wk, wv = _weights_b(b_w_in[0], b_w_uq[0], b_w_ukv[0])
    x1, q2, k2, v2, gate_b = _mid(x, mod0, os_, lses, gate_a, a_w_out[0].astype(jnp.bfloat16), ln_g[0], ln_b[0],
                                  mod1, w_in, b_q_norm[0], b_kv_norm[0], wqa, wk, wv, _rope_tables(S))
    o = _attn_b(q2, k2, v2)
    return _out_b(x1, mod1, o, gate_b, b_w_out[0].astype(jnp.bfloat16), ln_g[1], ln_b[1])
```

```python
import math

import numpy as np
import jax
import jax.numpy as jnp
from jax import lax
from jax.experimental import pallas as pl
from jax.experimental.pallas import tpu as pltpu

D_MODEL = 1024
DEPTH = 2
HEAD_DIM_A = 64
DIL_GROUPS = ((128, 1), (512, 4), (2048, 16))
HEADS_PER_GROUP_A = 6
GROUP_WIDTH_A = HEADS_PER_GROUP_A * HEAD_DIM_A
WIDTH_A = GROUP_WIDTH_A * len(DIL_GROUPS)
N_BUCKETS = 32
T5_MAX_DISTANCE = 1024
N_HEADS_B = 16
QK_NOPE_DIM = 64
QK_ROPE_DIM = 32
V_HEAD_DIM = 64
Q_LORA_RANK = 256
KV_LORA_RANK = 128
WIDTH_B = N_HEADS_B * V_HEAD_DIM
ROPE_THETA = 10000.0
DEEPNORM_ALPHA = (2.0 * DEPTH) ** 0.25
LN_EPS = 1e-5
RMS_EPS = 1e-6
NEG_INF = -1e30
LOG2E = math.log2(math.e)
LN2 = math.log(2.0)

LANES = 128
VMEM_LIMIT_BYTES = 56 * 1024 * 1024

HALF_WINDOW = 64
TQ_A = 128
TK_A = TQ_A + 2 * HALF_WINDOW
LSE_COPY_STRIDE = 8
LSE_COPIES = 3
MAX_TILES_PER_STEP_A = 4
TL_PROJ = 512
PERM_ROWS = 256
TQ_B = 512
TK_B = 512
HEADS_PER_STEP_B = 4

_NT = (((1,), (1,)), ((), ()))


def _silu(x):
    return x * (1.0 / (1.0 + jnp.exp(-x)))


def _params(*sem):
    return pltpu.CompilerParams(dimension_semantics=sem, vmem_limit_bytes=VMEM_LIMIT_BYTES)


def _mod_kernel(c_ref, w_ref, b_ref, o_ref):
    sc = _silu(c_ref[...]).astype(jnp.bfloat16)
    o_ref[...] = jnp.dot(sc, w_ref[...].astype(jnp.bfloat16),
                         preferred_element_type=jnp.float32) + b_ref[...]


def _modulation(c, ada_w, ada_b):
    B, D = c.shape
    nj = 3
    return pl.pallas_call(
        _mod_kernel,
        grid=(DEPTH, nj),
        in_specs=[pl.BlockSpec((B, D), lambda i, j: (0, 0)),
                  pl.BlockSpec((None, D, D), lambda i, j: (i, 0, j)),
                  pl.BlockSpec((None, 1, D), lambda i, j: (i, 0, j))],
        out_specs=pl.BlockSpec((None, B, D), lambda i, j: (i, 0, j)),
        out_shape=jax.ShapeDtypeStruct((DEPTH, B, 3 * D), jnp.float32),
        compiler_params=_params("arbitrary", "arbitrary"),
        name="modulation",
    )(c, ada_w, ada_b.reshape(DEPTH, 1, 3 * D))


def _modulate(x, mod):
    D = x.shape[-1]
    return x * (1.0 + mod[:, D:2 * D]) + mod[:, :D]


def _layernorm(z, g, b):
    mu = jnp.mean(z, axis=-1, keepdims=True)
    zc = z - mu
    var = jnp.mean(zc * zc, axis=-1, keepdims=True)
    return zc * lax.rsqrt(var + LN_EPS) * g + b


def _class_major_perm(tl, dil, inverse=False):
    p = np.arange(tl)
    src = (p % (tl // dil)) * dil + p // (tl // dil)
    mat = (src[:, None] == np.arange(tl)[None, :]).astype(np.float32)
    return jnp.asarray(mat.T if inverse else mat, dtype=jnp.bfloat16)


def _proj_a_kernel(x_ref, mod_ref, w0_ref, w1_ref, w2_ref, perm1_ref, perm2_ref,
                   qkv0_ref, qkv1_ref, qkv2_ref, gate_ref):
    tl = x_ref.shape[0]
    n_sub = tl // PERM_ROWS
    u = _modulate(x_ref[...], mod_ref[...]).astype(jnp.bfloat16)
    n_qkv = qkv0_ref.shape[-1]
    z = jnp.dot(u, w0_ref[...], preferred_element_type=jnp.float32)
    qkv0_ref[0] = z[:, :n_qkv].astype(qkv0_ref.dtype)
    gate_ref[...] = z[:, n_qkv:].astype(gate_ref.dtype)
    for perm_ref, w_ref, qkv_ref in ((perm1_ref, w1_ref, qkv1_ref), (perm2_ref, w2_ref, qkv2_ref)):
        pm = perm_ref[...]
        up = jnp.concatenate(
            [jnp.dot(pm, u[sub * PERM_ROWS:(sub + 1) * PERM_ROWS], preferred_element_type=jnp.float32)
             for sub in range(n_sub)], axis=0).astype(jnp.bfloat16)
        z = jnp.dot(up, w_ref[...], preferred_element_type=jnp.float32)
        dil, rows = qkv_ref.shape[:2]
        rs = rows // n_sub
        for sub in range(n_sub):
            for r in range(dil):
                r0 = sub * PERM_ROWS + r * rs
                qkv_ref[r, sub * rs:(sub + 1) * rs] = z[r0:r0 + rs].astype(qkv_ref.dtype)


def _proj_a(x, mod, ws):
    B, S, D = x.shape
    tl = TL_PROJ
    n_qkv = 3 * GROUP_WIDTH_A
    dils = [dil for _, dil in DIL_GROUPS]
    assert dils[0] == 1
    row = lambda b, i: (b, i, 0)
    cls = lambda b, i: (b, 0, i, 0)
    const2 = lambda b, i: (0, 0)
    perms = [_class_major_perm(PERM_ROWS, dil) for dil in dils[1:]]
    return pl.pallas_call(
        _proj_a_kernel,
        grid=(B, S // tl),
        in_specs=[pl.BlockSpec((None, tl, D), row),
                  pl.BlockSpec((None, 1, 3 * D), lambda b, i: (b, 0, 0)),
                  *[pl.BlockSpec(w.shape, const2) for w in ws],
                  *[pl.BlockSpec(p.shape, const2) for p in perms]],
        out_specs=[*[pl.BlockSpec((None, dil, tl // dil, n_qkv), cls) for dil in dils],
                   pl.BlockSpec((None, tl, WIDTH_A), row)],
        out_shape=[*[jax.ShapeDtypeStruct((B, dil, S // dil, n_qkv), jnp.bfloat16) for dil in dils],
                   jax.ShapeDtypeStruct((B, S, WIDTH_A), jnp.bfloat16)],
        compiler_params=_params("arbitrary", "arbitrary"),
        name="proj_a",
    )(x, mod, *ws, *perms)


def _attn_a_kernel(qkv_ref, bias_ref, o_ref, lse_ref, s_scr, p_scr):
    n_cls, L = qkv_ref.shape[:2]
    n_tiles = L // TQ_A
    gw = GROUP_WIDTH_A
    lane = lax.broadcasted_iota(jnp.int32, (TQ_A, LANES), 1)
    low_q = lane < HEAD_DIM_A
    lse_lane = jnp.where(lane < LSE_COPIES * LSE_COPY_STRIDE, lane % LSE_COPY_STRIDE, -1)
    low_k = lax.broadcasted_iota(jnp.int32, (TK_A, LANES), 1) < HEAD_DIM_A

    heads = range(HEADS_PER_GROUP_A)

    def pair_block(cls, rows, section, h):
        c0 = section * gw + (h // 2) * LANES
        return qkv_ref[cls, rows, c0:c0 + LANES]

    def other_lanes(block, low_mask, h, fill):
        mine = low_mask if h % 2 == 0 else jnp.logical_not(low_mask)
        return jnp.where(mine, block, jnp.full_like(block, fill))

    tiles_per_step = s_scr.shape[0]

    def tile_group(tg, carry):
        geo = []
        for u in range(tiles_per_step):
            item = tg * tiles_per_step + u
            cls, t = item // n_tiles, item % n_tiles
            j0 = pl.multiple_of(t * TQ_A, TQ_A)
            ws = pl.multiple_of(jnp.clip(j0 - HALF_WINDOW, 0, L - TK_A), HALF_WINDOW)
            variant = jnp.where(t == 0, 0, jnp.where(t == n_tiles - 1, 2, 1))
            geo.append((cls, pl.ds(j0, TQ_A), pl.ds(ws, TK_A), variant))
        for u, (cls, rq, rk, variant) in enumerate(geo):
            for h in heads:
                q = other_lanes(pair_block(cls, rq, 0, h), low_q, h, 0)
                s = lax.dot_general(q, pair_block(cls, rk, 1, h), _NT, preferred_element_type=jnp.float32)
                s_scr[u, h] = s + bias_ref[variant, h]
        m = [[jnp.max(s_scr[u, h], axis=-1, keepdims=True) for h in heads] for u in range(len(geo))]
        for u in range(len(geo)):
            for h in heads:
                p_scr[u, h] = jnp.exp2(s_scr[u, h] - m[u][h]).astype(jnp.bfloat16)
        for u, (cls, rq, rk, variant) in enumerate(geo):
            lse_tile = jnp.zeros((TQ_A, LANES), jnp.float32)
            for pair in range(HEADS_PER_GROUP_A // 2):
                he, ho = 2 * pair, 2 * pair + 1
                acc_e = jnp.dot(p_scr[u, he], other_lanes(pair_block(cls, rk, 2, he), low_k, he, 1),
                                preferred_element_type=jnp.float32)
                acc_o = jnp.dot(p_scr[u, ho], other_lanes(pair_block(cls, rk, 2, ho), low_k, ho, 1),
                                preferred_element_type=jnp.float32)
                numer = jnp.where(low_q, acc_e, acc_o)
                den_other = jnp.where(low_q, acc_o, acc_e)
                den = pltpu.roll(den_other, HEAD_DIM_A, 1)
                o_ref[cls, rq, pair * LANES:(pair + 1) * LANES] = (numer / den).astype(o_ref.dtype)
                is_odd = lse_lane == ho
                lse = (jnp.where(is_odd, m[u][ho], m[u][he])
                       + jnp.log2(jnp.where(is_odd, den_other, den))) * LN2
                lse_tile = jnp.where(jnp.logical_or(is_odd, lse_lane == he), lse, lse_tile)
            lse_ref[cls, rq, :] = lse_tile
        return carry

    lax.fori_loop(0, n_cls * n_tiles // tiles_per_step, tile_group, 0)


def _attn_a(qkv, biasm):
    B, dil, L, n_qkv = qkv.shape
    gw = GROUP_WIDTH_A
    n_tiles = L // TQ_A
    n_cls = min(dil, max(1, MAX_TILES_PER_STEP_A // n_tiles))
    tiles_per_step = math.gcd(MAX_TILES_PER_STEP_A, n_cls * n_tiles)
    cls = lambda b, r: (b, r, 0, 0)
    return pl.pallas_call(
        _attn_a_kernel,
        grid=(B, dil // n_cls),
        in_specs=[pl.BlockSpec((None, n_cls, L, n_qkv), cls),
                  pl.BlockSpec(biasm.shape, lambda b, r: (0, 0, 0, 0))],
        out_specs=[pl.BlockSpec((None, n_cls, L, gw), cls),
                   pl.BlockSpec((None, n_cls, L, LANES), cls)],
        out_shape=[jax.ShapeDtypeStruct((B, dil, L, gw), jnp.bfloat16),
                   jax.ShapeDtypeStruct((B, dil, L, LANES), jnp.float32)],
        scratch_shapes=[
            pltpu.VMEM((tiles_per_step, HEADS_PER_GROUP_A, TQ_A, TK_A), jnp.float32),
            pltpu.VMEM((tiles_per_step, HEADS_PER_GROUP_A, TQ_A, TK_A), jnp.bfloat16)],
        compiler_params=_params("arbitrary", "arbitrary"),
        name=f"attn_a_d{dil}",
    )(qkv, biasm)


def _t5_bucket_np(rel):
    half = N_BUCKETS // 2
    max_exact = half // 2
    base = np.where(rel > 0, half, 0)
    n = np.abs(rel)
    nf = np.maximum(n, 1).astype(np.float32)
    large = max_exact + (np.log(nf / np.float32(max_exact)) / np.float32(math.log(T5_MAX_DISTANCE / max_exact))
                         * np.float32(half - max_exact)).astype(np.int32)
    large = np.minimum(large, half - 1)
    return base + np.where(n < max_exact, n, large)


def _bias_variants(rel_bias_g, dil):
    H = rel_bias_g.shape[1]
    M = 2 * TK_A
    off = np.arange(M) - (TK_A - 1)
    idx = np.where(np.abs(off) <= HALF_WINDOW, _t5_bucket_np(off * dil), N_BUCKETS).astype(np.int32)
    table = jnp.concatenate([rel_bias_g.astype(jnp.float32) * LOG2E, jnp.full((1, H), NEG_INF, jnp.float32)], axis=0)
    w = table[jnp.asarray(idx)].T
    rot = jnp.tile(w, (1, TQ_A + 1))[:, :TQ_A * (M + 1)].reshape(H, TQ_A, M + 1)[:, ::-1, :]
    variants = []
    for shift in (0, HALF_WINDOW, 2 * HALF_WINDOW):
        j0 = TK_A - TQ_A - shift
        variants.append(rot[:, :, j0:j0 + TK_A])
    return jnp.stack(variants)


def _split3_bf16(v):
    hi = v.astype(jnp.bfloat16)
    r1 = v - hi.astype(jnp.float32)
    mid = r1.astype(jnp.bfloat16)
    lo = (r1 - mid.astype(jnp.float32)).astype(jnp.bfloat16)
    return hi, mid, lo


N_MERGE_A_REFS = 15


def _merge_a(x_ref, mod_ref, o0_ref, o1_ref, o2_ref, l0_ref, l1_ref, l2_ref, gate_ref,
             unperm1_ref, unperm2_ref, expand_ref, w_ref, g_ref, b_ref):
    tl, D = x_ref.shape
    stride = LSE_COPY_STRIDE
    lane_p = lax.broadcasted_iota(jnp.int32, (PERM_ROWS, LANES), 1)
    lane = lax.broadcasted_iota(jnp.int32, (tl, LANES), 1)

    def class_major_rows(ref, sub):
        dil, rows = ref.shape[:2]
        rs = rows * PERM_ROWS // tl
        return jnp.concatenate([ref[r, sub * rs:(sub + 1) * rs] for r in range(dil)], axis=0)

    def natural_order(o_ref, l_ref, unperm_ref):
        if unperm_ref is None:
            return o_ref[0].astype(jnp.float32), l_ref[0]
        pt = unperm_ref[...]
        o_parts, l_parts = [], []
        for sub in range(tl // PERM_ROWS):
            o_parts.append(jnp.dot(pt, class_major_rows(o_ref, sub), preferred_element_type=jnp.float32))
            hi, mid, lo = _split3_bf16(class_major_rows(l_ref, sub))
            packed = jnp.where(lane_p < stride, hi, jnp.where(lane_p < 2 * stride, mid, lo))
            res = jnp.dot(pt, packed, preferred_element_type=jnp.float32)
            l_parts.append(res + pltpu.roll(res, LANES - stride, 1) + pltpu.roll(res, LANES - 2 * stride, 1))
        return jnp.concatenate(o_parts, axis=0), jnp.concatenate(l_parts, axis=0)

    o0, l0 = natural_order(o0_ref, l0_ref, None)
    o1, l1 = natural_order(o1_ref, l1_ref, unperm1_ref)
    o2, l2 = natural_order(o2_ref, l2_ref, unperm2_ref)
    mx = jnp.maximum(jnp.maximum(l0, l1), l2)
    e0, e1, e2 = jnp.exp(l0 - mx), jnp.exp(l1 - mx), jnp.exp(l2 - mx)
    inv = 1.0 / (e0 + e1 + e2)
    wt = jnp.where(lane < stride, e0 * inv,
                   jnp.where(lane < 2 * stride, pltpu.roll(e1 * inv, stride, 1), pltpu.roll(e2 * inv, 2 * stride, 1)))
    hi = wt.astype(jnp.bfloat16)
    lo = (wt - hi.astype(jnp.float32)).astype(jnp.bfloat16)
    wexp = (jnp.dot(hi, expand_ref[...], preferred_element_type=jnp.float32)
            + jnp.dot(lo, expand_ref[...], preferred_element_type=jnp.float32))
    o = jnp.concatenate([o0, o1, o2], axis=1)
    yg = (o * wexp * _silu(gate_ref[...].astype(jnp.float32))).astype(jnp.bfloat16)
    y = jnp.dot(yg, w_ref[...], preferred_element_type=jnp.float32)
    z = DEEPNORM_ALPHA * x_ref[...] + mod_ref[:, 2 * D:] * y
    return _layernorm(z, g_ref[...], b_ref[...])


def _rmsnorm(x, g):
    return x * lax.rsqrt(jnp.mean(x * x, axis=-1, keepdims=True) + RMS_EPS) * g


def _project_b(x, mod_ref, w_in_ref, qn_ref, kvn_ref, wqa_ref, wk_ref, wv_ref,
               vone_ref, cq_ref, sq_ref, ck_ref, sk_ref, q_out, k_out, v_out, gate_out):
    u = _modulate(x, mod_ref[...]).astype(jnp.bfloat16)
    z = jnp.dot(u, w_in_ref[...], preferred_element_type=jnp.float32)
    c0 = Q_LORA_RANK
    c1 = c0 + KV_LORA_RANK
    cq = z[:, :c0]
    ckv = z[:, c0:c1]
    kr = z[:, c1:c1 + LANES]
    gate_out[...] = z[:, c1 + LANES:].astype(gate_out.dtype)
    to_rope_lanes = LANES - QK_ROPE_DIM

    cqn = _rmsnorm(cq, qn_ref[...]).astype(jnp.bfloat16)
    qa = jnp.dot(cqn, wqa_ref[...], preferred_element_type=jnp.float32)
    ckvn = _rmsnorm(ckv, kvn_ref[...]).astype(jnp.bfloat16)
    kn = jnp.dot(ckvn, wk_ref[...], preferred_element_type=jnp.float32)
    v_out[...] = (jnp.dot(ckvn, wv_ref[...], preferred_element_type=jnp.float32)
                  + vone_ref[...]).astype(v_out.dtype)

    cos_q, sin_q = cq_ref[...], sq_ref[...]
    k_rope = kr * ck_ref[...] + pltpu.roll(kr, to_rope_lanes, 1) * sk_ref[...]
    for h in range(N_HEADS_B):
        blk = slice(h * LANES, (h + 1) * LANES)
        qh = qa[:, blk]
        q_out[:, blk] = (qh * cos_q + pltpu.roll(qh, to_rope_lanes, 1) * sin_q).astype(q_out.dtype)
        k_out[:, blk] = (kn[:, blk] + k_rope).astype(k_out.dtype)


def _mid_kernel(*refs):
    merge_refs, rest = refs[:N_MERGE_A_REFS], refs[N_MERGE_A_REFS:]
    x1_out, project_refs = rest[-5], rest[:-5] + rest[-4:]
    x1 = _merge_a(*merge_refs)
    x1_out[...] = x1
    _project_b(x1, *project_refs)


def _mid(x, mod0, os_, lses, gate_a, w_out_a, ln_g, ln_b,
         mod1, w_in, q_norm, kv_norm, wqa, wk, wv, tables):
    B, S, D = x.shape
    tl = TL_PROJ
    gw = GROUP_WIDTH_A
    HP = N_HEADS_B * LANES
    col = np.arange(WIDTH_A)
    src_lane = (col // gw) * LSE_COPY_STRIDE + (col % gw) // HEAD_DIM_A
    expand = jnp.asarray((np.arange(LANES)[:, None] == src_lane[None, :]).astype(np.float32), dtype=jnp.bfloat16)
    unperms = [_class_major_perm(PERM_ROWS, dil, inverse=True) for _, dil in DIL_GROUPS[1:]]
    row = lambda b, i: (b, i, 0)
    cls = lambda b, i: (b, 0, i, 0)
    const2 = lambda b, i: (0, 0)
    mod_spec = pl.BlockSpec((None, 1, 3 * D), lambda b, i: (b, 0, 0))
    tab = pl.BlockSpec((tl, LANES), lambda b, i: (i, 0))
    o_specs = [pl.BlockSpec((None, dil, tl // dil, gw), cls) for _, dil in DIL_GROUPS]
    l_specs = [pl.BlockSpec((None, dil, tl // dil, LANES), cls) for _, dil in DIL_GROUPS]
    merge_args = [x, mod0, *os_, *lses, gate_a, *unperms, expand, w_out_a, ln_g.reshape(1, D), ln_b.reshape(1, D)]
    merge_specs = [pl.BlockSpec((None, tl, D), row), mod_spec, *o_specs, *l_specs,
                   pl.BlockSpec((None, tl, WIDTH_A), row),
                   *[pl.BlockSpec(p.shape, const2) for p in unperms],
                   pl.BlockSpec(expand.shape, const2), pl.BlockSpec(w_out_a.shape, const2),
                   pl.BlockSpec((1, D), const2), pl.BlockSpec((1, D), const2)]
    assert len(merge_args) == N_MERGE_A_REFS
    project_args = [mod1, w_in, q_norm.reshape(1, -1), kv_norm.reshape(1, -1), wqa, wk, wv, _v_ones_row(), *tables]
    project_specs = [mod_spec, pl.BlockSpec(w_in.shape, const2),
                     pl.BlockSpec((1, Q_LORA_RANK), const2), pl.BlockSpec((1, KV_LORA_RANK), const2),
                     pl.BlockSpec(wqa.shape, const2), pl.BlockSpec(wk.shape, const2),
                     pl.BlockSpec(wv.shape, const2), pl.BlockSpec((1, HP), const2), tab, tab, tab, tab]
    wide = jax.ShapeDtypeStruct((B, S, HP), jnp.bfloat16)
    return pl.pallas_call(
        _mid_kernel,
        grid=(B, S // tl),
        in_specs=merge_specs + project_specs,
        out_specs=[pl.BlockSpec((None, tl, D), row),
                   pl.BlockSpec((None, tl, HP), row), pl.BlockSpec((None, tl, HP), row),
                   pl.BlockSpec((None, tl, HP), row), pl.BlockSpec((None, tl, WIDTH_B), row)],
        out_shape=[jax.ShapeDtypeStruct((B, S, D), jnp.float32), wide, wide, wide,
                   jax.ShapeDtypeStruct((B, S, WIDTH_B), jnp.bfloat16)],
        compiler_params=_params("arbitrary", "arbitrary"),
        name="mid",
    )(*merge_args, *project_args)


def _attn_b_kernel(q_ref, k_ref, v_ref, o_ref, s_even, s_odd, mx_scr):
    S = k_ref.shape[0]
    tq = s_even.shape[0]
    n_tiles = S // tq
    n_chunks = S // TK_B
    n_heads = q_ref.shape[1] // LANES
    assert n_heads % 2 == 0
    s_bufs = (s_even, s_odd)

    def head_lanes(e):
        return slice(e * LANES, (e + 1) * LANES)

    def tile_rows(t):
        return pl.ds(t * tq, tq) if isinstance(t, int) else pl.ds(pl.multiple_of(t * tq, tq), tq)

    def scores(t, e, c, mx):
        ks = slice(c * TK_B, (c + 1) * TK_B)
        s = lax.dot_general(q_ref[tile_rows(t), head_lanes(e)], k_ref[ks, head_lanes(e)], _NT,
                            preferred_element_type=jnp.float32)
        s_bufs[e % 2][:, ks] = s
        for j in range(TK_B // LANES):
            mx = jnp.maximum(mx, s[:, j * LANES:(j + 1) * LANES])
        return mx

    def weighted_values(e, c, m, acc):
        ks = slice(c * TK_B, (c + 1) * TK_B)
        p = jnp.exp2(s_bufs[e % 2][:, ks] - m).astype(jnp.bfloat16)
        return acc + jnp.dot(p, v_ref[ks, head_lanes(e)], preferred_element_type=jnp.float32)

    neg = jnp.full((tq, LANES), -jnp.inf, jnp.float32)
    zero = jnp.zeros((tq, LANES), jnp.float32)
    lane = lax.broadcasted_iota(jnp.int32, (tq, LANES), 1)

    mx = neg
    for c in range(n_chunks):
        mx = scores(0, 0, c, mx)
    mx_scr[...] = mx

    def tile(t, carry):
        t_next = jnp.minimum(t + 1, n_tiles - 1)
        mx = mx_scr[...]
        acc = []
        for e in range(n_heads):
            m = jnp.max(mx, axis=-1, keepdims=True)
            mx, a = neg, zero
            for c in range(n_chunks):
                mx = scores(t, e + 1, c, mx) if e + 1 < n_heads else scores(t_next, 0, c, mx)
                a = weighted_values(e, c, m, a)
            acc.append(a)
        mx_scr[...] = mx
        for pair in range(n_heads // 2):
            acc_e, acc_o = acc[2 * pair], acc[2 * pair + 1]
            inv_e = 1.0 / acc_e[:, V_HEAD_DIM:V_HEAD_DIM + 1]
            inv_o = 1.0 / acc_o[:, 0:1]
            o_ref[tile_rows(t), pair * LANES:(pair + 1) * LANES] = jnp.where(
                lane < V_HEAD_DIM, acc_e * inv_e, acc_o * inv_o).astype(o_ref.dtype)
        return carry

    lax.fori_loop(0, n_tiles, tile, 0)


def _attn_b(q2, k2, v2):
    B, S, HP = q2.shape
    hw = HEADS_PER_STEP_B * LANES
    ow = HEADS_PER_STEP_B * V_HEAD_DIM
    heads = lambda b, p: (b, 0, p)
    return pl.pallas_call(
        _attn_b_kernel,
        grid=(B, N_HEADS_B // HEADS_PER_STEP_B),
        in_specs=[pl.BlockSpec((None, S, hw), heads), pl.BlockSpec((None, S, hw), heads),
                  pl.BlockSpec((None, S, hw), heads)],
        out_specs=pl.BlockSpec((None, S, ow), heads),
        out_shape=jax.ShapeDtypeStruct((B, S, WIDTH_B), jnp.bfloat16),
        scratch_shapes=[pltpu.VMEM((TQ_B, S), jnp.float32), pltpu.VMEM((TQ_B, S), jnp.float32),
                        pltpu.VMEM((TQ_B, LANES), jnp.float32)],
        compiler_params=_params("arbitrary", "arbitrary"),
        name="attn_b",
    )(q2, k2, v2)


def _out_b_kernel(x_ref, mod_ref, o_ref, gate_ref, w_ref, g_ref, b_ref, out_ref):
    D = x_ref.shape[-1]
    yg = (o_ref[...].astype(jnp.float32) * _silu(gate_ref[...].astype(jnp.float32))).astype(jnp.bfloat16)
    y = jnp.dot(yg, w_ref[...], preferred_element_type=jnp.float32)
    z = DEEPNORM_ALPHA * x_ref[...] + mod_ref[:, 2 * D:] * y
    out_ref[...] = _layernorm(z, g_ref[...], b_ref[...])


def _out_b(x, mod, o, gate, w_out, ln_g, ln_b):
    B, S, D = x.shape
    tl = TL_PROJ
    row = lambda b, i: (b, i, 0)
    const2 = lambda b, i: (0, 0)
    return pl.pallas_call(
        _out_b_kernel,
        grid=(B, S // tl),
        in_specs=[pl.BlockSpec((None, tl, D), row),
                  pl.BlockSpec((None, 1, 3 * D), lambda b, i: (b, 0, 0)),
                  pl.BlockSpec((None, tl, WIDTH_B), row), pl.BlockSpec((None, tl, WIDTH_B), row),
                  pl.BlockSpec(w_out.shape, const2),
                  pl.BlockSpec((1, D), const2), pl.BlockSpec((1, D), const2)],
        out_specs=pl.BlockSpec((None, tl, D), row),
        out_shape=jax.ShapeDtypeStruct((B, S, D), jnp.float32),
        compiler_params=_params("arbitrary", "arbitrary"),
        name="out_b",
    )(x, mod, o, gate, w_out, ln_g.reshape(1, D), ln_b.reshape(1, D))


def _weights_a(a_w_in):
    W = WIDTH_A
    gw = GROUP_WIDTH_A
    q, k, v, gate = (a_w_in[:, i * W:(i + 1) * W] for i in range(4))
    ws = []
    for g in range(len(DIL_GROUPS)):
        cs = slice(g * gw, (g + 1) * gw)
        parts = [q[:, cs] * (HEAD_DIM_A ** -0.5 * LOG2E), k[:, cs], v[:, cs]]
        if g == 0:
            parts.append(gate)
        ws.append(jnp.concatenate(parts, axis=1).astype(jnp.bfloat16))
    return ws


def _swap_halves(w):
    half = w.shape[-1] // 2
    return jnp.concatenate([w[..., half:], w[..., :half]], axis=-1)


def _pad_lanes(w, lo):
    n = w.shape[-1]
    return jnp.pad(w, [(0, 0)] * (w.ndim - 1) + [(lo, LANES - lo - n)])


def _weights_b(b_w_in, b_w_uq, b_w_ukv):
    c0 = Q_LORA_RANK
    c1 = c0 + KV_LORA_RANK
    c2 = c1 + QK_ROPE_DIM
    kr = b_w_in[:, c1:c2]
    w_in = jnp.concatenate([b_w_in[:, :c1], _pad_lanes(jnp.concatenate([kr, _swap_halves(kr)], axis=1), QK_NOPE_DIM),
                            b_w_in[:, c2:]], axis=1)
    uq = b_w_uq.reshape(Q_LORA_RANK, N_HEADS_B, QK_NOPE_DIM + QK_ROPE_DIM)
    wqa = jnp.concatenate([uq, _swap_halves(uq[..., QK_NOPE_DIM:])], axis=-1).reshape(Q_LORA_RANK, -1)
    ukv = b_w_ukv.reshape(KV_LORA_RANK, N_HEADS_B, QK_NOPE_DIM + V_HEAD_DIM)
    wk = _pad_lanes(ukv[..., :QK_NOPE_DIM], 0).reshape(KV_LORA_RANK, -1)
    vv = ukv[..., QK_NOPE_DIM:]
    odd = (jnp.arange(N_HEADS_B) % 2 == 1)[None, :, None]
    wv = jnp.where(odd, _pad_lanes(vv, LANES - V_HEAD_DIM), _pad_lanes(vv, 0)).reshape(KV_LORA_RANK, -1)
    bf = lambda w: w.astype(jnp.bfloat16)
    return bf(w_in), bf(wqa), bf(wk), bf(wv)


def _rope_tables(S):
    pos = jnp.arange(S, dtype=jnp.float32)
    inv_freq = ROPE_THETA ** (-jnp.arange(0, QK_ROPE_DIM, 2, dtype=jnp.float32) / QK_ROPE_DIM)
    ang = pos[:, None] * inv_freq[None, :]
    cos, sin = jnp.cos(ang), jnp.sin(ang)
    cosf = _pad_lanes(jnp.concatenate([cos, cos], axis=-1), QK_NOPE_DIM)
    sinf = _pad_lanes(jnp.concatenate([-sin, sin], axis=-1), QK_NOPE_DIM)
    nope = _pad_lanes(jnp.ones((S, QK_NOPE_DIM), jnp.float32), 0)
    scale = (QK_NOPE_DIM + QK_ROPE_DIM) ** -0.5 * LOG2E
    return (scale * (nope + cosf), scale * sinf, cosf, sinf)


def _v_ones_row():
    row = np.zeros((N_HEADS_B, LANES), np.float32)
    row[0::2, V_HEAD_DIM] = 1.0
    row[1::2, 0] = 1.0
    return jnp.asarray(row.reshape(1, -1))


def kernel(x, c, rel_bias, ada_w, ada_b, ln_g, ln_b, a_w_in, a_w_out,
           b_w_in, b_q_norm, b_w_uq, b_kv_norm, b_w_ukv, b_w_out):
    B, S, D = x.shape
    mods = _modulation(c, ada_w, ada_b)
    mod0 = mods[0].reshape(B, 1, 3 * D)
    mod1 = mods[1].reshape(B, 1, 3 * D)

    for window, dil in DIL_GROUPS:
        assert window // (2 * dil) == HALF_WINDOW
    *qkvs, gate_a = _proj_a(x, mod0, _weights_a(a_w_in[0]))
    os_, lses = [], []
    for g, ((_, dil), qkv) in enumerate(zip(DIL_GROUPS, qkvs)):
        hs = slice(g * HEADS_PER_GROUP_A, (g + 1) * HEADS_PER_GROUP_A)
        o, lse = _attn_a(qkv, _bias_variants(rel_bias[:, hs], dil))
        os_.append(o)
        lses.append(lse)

    w_in, wqa, wk, wv = _weights_b(b_w_in[0], b_w_uq[0], b_w_ukv[0])
    x1, q2, k2, v2, gate_b = _mid(x, mod0, os_, lses, gate_a, a_w_out[0].astype(jnp.bfloat16), ln_g[0], ln_b[0],
                                  mod1, w_in, b_q_norm[0], b_kv_norm[0], wqa, wk, wv, _rope_tables(S))
    o = _attn_b(q2, k2, v2)
    return _out_b(x1, mod1, o, gate_b, b_w_out[0].astype(jnp.bfloat16), ln_g[1], ln_b[1])
```

```python
import math

import numpy as np
import jax
import jax.numpy as jnp
from jax import lax
from jax.experimental import pallas as pl
from jax.experimental.pallas import tpu as pltpu

D_MODEL = 1024
DEPTH = 2
HEAD_DIM_A = 64
DIL_GROUPS = ((128, 1), (512, 4), (2048, 16))
HEADS_PER_GROUP_A = 6
GROUP_WIDTH_A = HEADS_PER_GROUP_A * HEAD_DIM_A
WIDTH_A = GROUP_WIDTH_A * len(DIL_GROUPS)
N_BUCKETS = 32
T5_MAX_DISTANCE = 1024
N_HEADS_B = 16
QK_NOPE_DIM = 64
QK_ROPE_DIM = 32
V_HEAD_DIM = 64
Q_LORA_RANK = 256
KV_LORA_RANK = 128
WIDTH_B = N_HEADS_B * V_HEAD_DIM
ROPE_THETA = 10000.0
DEEPNORM_ALPHA = (2.0 * DEPTH) ** 0.25
LN_EPS = 1e-5
RMS_EPS = 1e-6
NEG_INF = -1e30
LOG2E = math.log2(math.e)
LN2 = math.log(2.0)

LANES = 128
VMEM_LIMIT_BYTES = 56 * 1024 * 1024

HALF_WINDOW = 64
TQ_A = 128
TK_A = TQ_A + 2 * HALF_WINDOW
LSE_COPY_STRIDE = 8
LSE_COPIES = 3
MAX_TILES_PER_STEP_A = 4
TL_PROJ = 512
PERM_ROWS = 256
TQ_B = 512
TK_B = 512
HEADS_PER_STEP_B = 4

_NT = (((1,), (1,)), ((), ()))


def _silu(x):
    return x * (1.0 / (1.0 + jnp.exp(-x)))


def _params(*sem):
    return pltpu.CompilerParams(dimension_semantics=sem, vmem_limit_bytes=VMEM_LIMIT_BYTES)


def _mod_kernel(c_ref, w_ref, b_ref, o_ref):
    sc = _silu(c_ref[...]).astype(jnp.bfloat16)
    o_ref[...] = jnp.dot(sc, w_ref[...].astype(jnp.bfloat16),
                         preferred_element_type=jnp.float32) + b_ref[...]


def _modulation(c, ada_w, ada_b):
    B, D = c.shape
    nj = 3
    return pl.pallas_call(
        _mod_kernel,
        grid=(DEPTH, nj),
        in_specs=[pl.BlockSpec((B, D), lambda i, j: (0, 0)),
                  pl.BlockSpec((None, D, D), lambda i, j: (i, 0, j)),
                  pl.BlockSpec((None, 1, D), lambda i, j: (i, 0, j))],
        out_specs=pl.BlockSpec((None, B, D), lambda i, j: (i, 0, j)),
        out_shape=jax.ShapeDtypeStruct((DEPTH, B, 3 * D), jnp.float32),
        compiler_params=_params("arbitrary", "arbitrary"),
        name="modulation",
    )(c, ada_w, ada_b.reshape(DEPTH, 1, 3 * D))


def _modulate(x, mod):
    D = x.shape[-1]
    return x * (1.0 + mod[:, D:2 * D]) + mod[:, :D]


def _layernorm(z, g, b):
    mu = jnp.mean(z, axis=-1, keepdims=True)
    zc = z - mu
    var = jnp.mean(zc * zc, axis=-1, keepdims=True)
    return zc * lax.rsqrt(var + LN_EPS) * g + b


def _class_major_perm(tl, dil, inverse=False):
    p = np.arange(tl)
    src = (p % (tl // dil)) * dil + p // (tl // dil)
    mat = (src[:, None] == np.arange(tl)[None, :]).astype(np.float32)
    return jnp.asarray(mat.T if inverse else mat, dtype=jnp.bfloat16)


def _proj_a_kernel(x_ref, mod_ref, w0_ref, w1_ref, w2_ref, perm1_ref, perm2_ref,
                   qkv0_ref, qkv1_ref, qkv2_ref, gate_ref):
    tl = x_ref.shape[0]
    n_sub = tl // PERM_ROWS
    u = _modulate(x_ref[...], mod_ref[...]).astype(jnp.bfloat16)
    n_qkv = qkv0_ref.shape[-1]
    z = jnp.dot(u, w0_ref[...], preferred_element_type=jnp.float32)
    qkv0_ref[0] = z[:, :n_qkv].astype(qkv0_ref.dtype)
    gate_ref[...] = z[:, n_qkv:].astype(gate_ref.dtype)
    for perm_ref, w_ref, qkv_ref in ((perm1_ref, w1_ref, qkv1_ref), (perm2_ref, w2_ref, qkv2_ref)):
        pm = perm_ref[...]
        up = jnp.concatenate(
            [jnp.dot(pm, u[sub * PERM_ROWS:(sub + 1) * PERM_ROWS], preferred_element_type=jnp.float32)
             for sub in range(n_sub)], axis=0).astype(jnp.bfloat16)
        z = jnp.dot(up, w_ref[...], preferred_element_type=jnp.float32)
        dil, rows = qkv_ref.shape[:2]
        rs = rows // n_sub
        for sub in range(n_sub):
            for r in range(dil):
                r0 = sub * PERM_ROWS + r * rs
                qkv_ref[r, sub * rs:(sub + 1) * rs] = z[r0:r0 + rs].astype(qkv_ref.dtype)


def _proj_a(x, mod, ws):
    B, S, D = x.shape
    tl = TL_PROJ
    n_qkv = 3 * GROUP_WIDTH_A
    dils = [dil for _, dil in DIL_GROUPS]
    assert dils[0] == 1
    row = lambda b, i: (b, i, 0)
    cls = lambda b, i: (b, 0, i, 0)
    const2 = lambda b, i: (0, 0)
    perms = [_class_major_perm(PERM_ROWS, dil) for dil in dils[1:]]
    return pl.pallas_call(
        _proj_a_kernel,
        grid=(B, S // tl),
        in_specs=[pl.BlockSpec((None, tl, D), row),
                  pl.BlockSpec((None, 1, 3 * D), lambda b, i: (b, 0, 0)),
                  *[pl.BlockSpec(w.shape, const2) for w in ws],
                  *[pl.BlockSpec(p.shape, const2) for p in perms]],
        out_specs=[*[pl.BlockSpec((None, dil, tl // dil, n_qkv), cls) for dil in dils],
                   pl.BlockSpec((None, tl, WIDTH_A), row)],
        out_shape=[*[jax.ShapeDtypeStruct((B, dil, S // dil, n_qkv), jnp.bfloat16) for dil in dils],
                   jax.ShapeDtypeStruct((B, S, WIDTH_A), jnp.bfloat16)],
        compiler_params=_params("arbitrary", "arbitrary"),
        name="proj_a",
    )(x, mod, *ws, *perms)


def _attn_a_kernel(qkv_ref, bias_ref, o_ref, lse_ref, s_scr, p_scr):
    n_cls, L = qkv_ref.shape[:2]
    n_tiles = L // TQ_A
    gw = GROUP_WIDTH_A
    lane = lax.broadcasted_iota(jnp.int32, (TQ_A, LANES), 1)
    low_q = lane < HEAD_DIM_A
    lse_lane = jnp.where(lane < LSE_COPIES * LSE_COPY_STRIDE, lane % LSE_COPY_STRIDE, -1)
    low_k = lax.broadcasted_iota(jnp.int32, (TK_A, LANES), 1) < HEAD_DIM_A

    heads = range(HEADS_PER_GROUP_A)

    def pair_block(cls, rows, section, h):
        c0 = section * gw + (h // 2) * LANES
        return qkv_ref[cls, rows, c0:c0 + LANES]

    def other_lanes(block, low_mask, h, fill):
        mine = low_mask if h % 2 == 0 else jnp.logical_not(low_mask)
        return jnp.where(mine, block, jnp.full_like(block, fill))

    tiles_per_step = s_scr.shape[0]

    def tile_group(tg, carry):
        geo = []
        for u in range(tiles_per_step):
            item = tg * tiles_per_step + u
            cls, t = (0, item) if n_cls == 1 else (item // n_tiles, item % n_tiles)
            j0 = pl.multiple_of(t * TQ_A, TQ_A)
            ws = pl.multiple_of(jnp.clip(j0 - HALF_WINDOW, 0, L - TK_A), HALF_WINDOW)
            variant = jnp.where(t == 0, 0, jnp.where(t == n_tiles - 1, 2, 1))
            geo.append((cls, pl.ds(j0, TQ_A), pl.ds(ws, TK_A), variant))
        for u, (cls, rq, rk, variant) in enumerate(geo):
            for h in heads:
                q = other_lanes(pair_block(cls, rq, 0, h), low_q, h, 0)
                s = lax.dot_general(q, pair_block(cls, rk, 1, h), _NT, preferred_element_type=jnp.float32)
                s_scr[u, h] = s + bias_ref[variant, h]
        m = [[jnp.max(s_scr[u, h], axis=-1, keepdims=True) for h in heads] for u in range(len(geo))]
        for u in range(len(geo)):
            for h in heads:
                p_scr[u, h] = jnp.exp2(s_scr[u, h] - m[u][h]).astype(jnp.bfloat16)
        for u, (cls, rq, rk, variant) in enumerate(geo):
            lse_tile = jnp.zeros((TQ_A, LANES), jnp.float32)
            for pair in range(HEADS_PER_GROUP_A // 2):
                he, ho = 2 * pair, 2 * pair + 1
                acc_e = jnp.dot(p_scr[u, he], other_lanes(pair_block(cls, rk, 2, he), low_k, he, 1),
                                preferred_element_type=jnp.float32)
                acc_o = jnp.dot(p_scr[u, ho], other_lanes(pair_block(cls, rk, 2, ho), low_k, ho, 1),
                                preferred_element_type=jnp.float32)
                numer = jnp.where(low_q, acc_e, acc_o)
                den_other = jnp.where(low_q, acc_o, acc_e)
                den = pltpu.roll(den_other, HEAD_DIM_A, 1)
                o_ref[cls, rq, pair * LANES:(pair + 1) * LANES] = (numer / den).astype(o_ref.dtype)
                is_odd = lse_lane == ho
                lse = (jnp.where(is_odd, m[u][ho], m[u][he])
                       + jnp.log2(jnp.where(is_odd, den_other, den))) * LN2
                lse_tile = jnp.where(jnp.logical_or(is_odd, lse_lane == he), lse, lse_tile)
            lse_ref[cls, rq, :] = lse_tile
        return carry

    lax.fori_loop(0, n_cls * n_tiles // tiles_per_step, tile_group, 0)


def _attn_a(qkv, biasm):
    B, dil, L, n_qkv = qkv.shape
    gw = GROUP_WIDTH_A
    n_tiles = L // TQ_A
    n_cls = min(dil, max(1, MAX_TILES_PER_STEP_A // n_tiles))
    tiles_per_step = math.gcd(MAX_TILES_PER_STEP_A, n_cls * n_tiles)
    cls = lambda b, r: (b, r, 0, 0)
    return pl.pallas_call(
        _attn_a_kernel,
        grid=(B, dil // n_cls),
        in_specs=[pl.BlockSpec((None, n_cls, L, n_qkv), cls),
                  pl.BlockSpec(biasm.shape, lambda b, r: (0, 0, 0, 0))],
        out_specs=[pl.BlockSpec((None, n_cls, L, gw), cls),
                   pl.BlockSpec((None, n_cls, L, LANES), cls)],
        out_shape=[jax.ShapeDtypeStruct((B, dil, L, gw), jnp.bfloat16),
                   jax.ShapeDtypeStruct((B, dil, L, LANES), jnp.float32)],
        scratch_shapes=[
            pltpu.VMEM((tiles_per_step, HEADS_PER_GROUP_A, TQ_A, TK_A), jnp.float32),
            pltpu.VMEM((tiles_per_step, HEADS_PER_GROUP_A, TQ_A, TK_A), jnp.bfloat16)],
        compiler_params=_params("arbitrary", "arbitrary"),
        name=f"attn_a_d{dil}",
    )(qkv, biasm)


def _t5_bucket_np(rel):
    half = N_BUCKETS // 2
    max_exact = half // 2
    base = np.where(rel > 0, half, 0)
    n = np.abs(rel)
    nf = np.maximum(n, 1).astype(np.float32)
    large = max_exact + (np.log(nf / np.float32(max_exact)) / np.float32(math.log(T5_MAX_DISTANCE / max_exact))
                         * np.float32(half - max_exact)).astype(np.int32)
    large = np.minimum(large, half - 1)
    return base + np.where(n < max_exact, n, large)


def _bias_variants(rel_bias_g, dil):
    H = rel_bias_g.shape[1]
    M = 2 * TK_A
    off = np.arange(M) - (TK_A - 1)
    idx = np.where(np.abs(off) <= HALF_WINDOW, _t5_bucket_np(off * dil), N_BUCKETS).astype(np.int32)
    table = jnp.concatenate([rel_bias_g.astype(jnp.float32) * LOG2E, jnp.full((1, H), NEG_INF, jnp.float32)], axis=0)
    w = table[jnp.asarray(idx)].T
    rot = jnp.tile(w, (1, TQ_A + 1))[:, :TQ_A * (M + 1)].reshape(H, TQ_A, M + 1)[:, ::-1, :]
    variants = []
    for shift in (0, HALF_WINDOW, 2 * HALF_WINDOW):
        j0 = TK_A - TQ_A - shift
        variants.append(rot[:, :, j0:j0 + TK_A])
    return jnp.stack(variants)


def _split3_bf16(v):
    hi = v.astype(jnp.bfloat16)
    r1 = v - hi.astype(jnp.float32)
    mid = r1.astype(jnp.bfloat16)
    lo = (r1 - mid.astype(jnp.float32)).astype(jnp.bfloat16)
    return hi, mid, lo


N_MERGE_A_REFS = 15


def _merge_a(x_ref, mod_ref, o0_ref, o1_ref, o2_ref, l0_ref, l1_ref, l2_ref, gate_ref,
             unperm1_ref, unperm2_ref, expand_ref, w_ref, g_ref, b_ref):
    tl, D = x_ref.shape
    stride = LSE_COPY_STRIDE
    lane_p = lax.broadcasted_iota(jnp.int32, (PERM_ROWS, LANES), 1)
    lane = lax.broadcasted_iota(jnp.int32, (tl, LANES), 1)

    def class_major_rows(ref, sub):
        dil, rows = ref.shape[:2]
        rs = rows * PERM_ROWS // tl
        return jnp.concatenate([ref[r, sub * rs:(sub + 1) * rs] for r in range(dil)], axis=0)

    def natural_order(o_ref, l_ref, unperm_ref):
        if unperm_ref is None:
            return o_ref[0].astype(jnp.float32), l_ref[0]
        pt = unperm_ref[...]
        o_parts, l_parts = [], []
        for sub in range(tl // PERM_ROWS):
            o_parts.append(jnp.dot(pt, class_major_rows(o_ref, sub), preferred_element_type=jnp.float32))
            hi, mid, lo = _split3_bf16(class_major_rows(l_ref, sub))
            packed = jnp.where(lane_p < stride, hi, jnp.where(lane_p < 2 * stride, mid, lo))
            res = jnp.dot(pt, packed, preferred_element_type=jnp.float32)
            l_parts.append(res + pltpu.roll(res, LANES - stride, 1) + pltpu.roll(res, LANES - 2 * stride, 1))
        return jnp.concatenate(o_parts, axis=0), jnp.concatenate(l_parts, axis=0)

    o0, l0 = natural_order(o0_ref, l0_ref, None)
    o1, l1 = natural_order(o1_ref, l1_ref, unperm1_ref)
    o2, l2 = natural_order(o2_ref, l2_ref, unperm2_ref)
    mx = jnp.maximum(jnp.maximum(l0, l1), l2)
    e0, e1, e2 = jnp.exp(l0 - mx), jnp.exp(l1 - mx), jnp.exp(l2 - mx)
    inv = 1.0 / (e0 + e1 + e2)
    wt = jnp.where(lane < stride, e0 * inv,
                   jnp.where(lane < 2 * stride, pltpu.roll(e1 * inv, stride, 1), pltpu.roll(e2 * inv, 2 * stride, 1)))
    hi = wt.astype(jnp.bfloat16)
    lo = (wt - hi.astype(jnp.float32)).astype(jnp.bfloat16)
    wexp = (jnp.dot(hi, expand_ref[...], preferred_element_type=jnp.float32)
            + jnp.dot(lo, expand_ref[...], preferred_element_type=jnp.float32))
    o = jnp.concatenate([o0, o1, o2], axis=1)
    yg = (o * wexp * _silu(gate_ref[...].astype(jnp.float32))).astype(jnp.bfloat16)
    y = jnp.dot(yg, w_ref[...], preferred_element_type=jnp.float32)
    z = DEEPNORM_ALPHA * x_ref[...] + mod_ref[:, 2 * D:] * y
    return _layernorm(z, g_ref[...], b_ref[...])


def _rmsnorm(x, g):
    return x * lax.rsqrt(jnp.mean(x * x, axis=-1, keepdims=True) + RMS_EPS) * g


def _project_b(x, mod_ref, w_in_ref, qn_ref, kvn_ref, wqa_ref, wk_ref, wv_ref,
               vone_ref, cq_ref, sq_ref, ck_ref, sk_ref, q_out, k_out, v_out, gate_out):
    u = _modulate(x, mod_ref[...]).astype(jnp.bfloat16)
    z = jnp.dot(u, w_in_ref[...], preferred_element_type=jnp.float32)
    c0 = Q_LORA_RANK
    c1 = c0 + KV_LORA_RANK
    cq = z[:, :c0]
    ckv = z[:, c0:c1]
    kr = z[:, c1:c1 + LANES]
    gate_out[...] = z[:, c1 + LANES:].astype(gate_out.dtype)
    to_rope_lanes = LANES - QK_ROPE_DIM

    cqn = _rmsnorm(cq, qn_ref[...]).astype(jnp.bfloat16)
    qa = jnp.dot(cqn, wqa_ref[...], preferred_element_type=jnp.float32)
    ckvn = _rmsnorm(ckv, kvn_ref[...]).astype(jnp.bfloat16)
    kn = jnp.dot(ckvn, wk_ref[...], preferred_element_type=jnp.float32)
    v_out[...] = (jnp.dot(ckvn, wv_ref[...], preferred_element_type=jnp.float32)
                  + vone_ref[...]).astype(v_out.dtype)

    cos_q, sin_q = cq_ref[...], sq_ref[...]
    k_rope = kr * ck_ref[...] + pltpu.roll(kr, to_rope_lanes, 1) * sk_ref[...]
    for h in range(N_HEADS_B):
        blk = slice(h * LANES, (h + 1) * LANES)
        qh = qa[:, blk]
        q_out[:, blk] = (qh * cos_q + pltpu.roll(qh, to_rope_lanes, 1) * sin_q).astype(q_out.dtype)
        k_out[:, blk] = (kn[:, blk] + k_rope).astype(k_out.dtype)


def _mid_kernel(*refs):
    merge_refs, rest = refs[:N_MERGE_A_REFS], refs[N_MERGE_A_REFS:]
    x1_out, project_refs = rest[-5], rest[:-5] + rest[-4:]
    x1 = _merge_a(*merge_refs)
    x1_out[...] = x1
    _project_b(x1, *project_refs)


def _mid(x, mod0, os_, lses, gate_a, w_out_a, ln_g, ln_b,
         mod1, w_in, q_norm, kv_norm, wqa, wk, wv, tables):
    B, S, D = x.shape
    tl = TL_PROJ
    gw = GROUP_WIDTH_A
    HP = N_HEADS_B * LANES
    col = np.arange(WIDTH_A)
    src_lane = (col // gw) * LSE_COPY_STRIDE + (col % gw) // HEAD_DIM_A
    expand = jnp.asarray((np.arange(LANES)[:, None] == src_lane[None, :]).astype(np.float32), dtype=jnp.bfloat16)
    unperms = [_class_major_perm(PERM_ROWS, dil, inverse=True) for _, dil in DIL_GROUPS[1:]]
    row = lambda b, i: (b, i, 0)
    cls = lambda b, i: (b, 0, i, 0)
    const2 = lambda b, i: (0, 0)
    mod_spec = pl.BlockSpec((None, 1, 3 * D), lambda b, i: (b, 0, 0))
    tab = pl.BlockSpec((tl, LANES), lambda b, i: (i, 0))
    o_specs = [pl.BlockSpec((None, dil, tl // dil, gw), cls) for _, dil in DIL_GROUPS]
    l_specs = [pl.BlockSpec((None, dil, tl // dil, LANES), cls) for _, dil in DIL_GROUPS]
    merge_args = [x, mod0, *os_, *lses, gate_a, *unperms, expand, w_out_a, ln_g.reshape(1, D), ln_b.reshape(1, D)]
    merge_specs = [pl.BlockSpec((None, tl, D), row), mod_spec, *o_specs, *l_specs,
                   pl.BlockSpec((None, tl, WIDTH_A), row),
                   *[pl.BlockSpec(p.shape, const2) for p in unperms],
                   pl.BlockSpec(expand.shape, const2), pl.BlockSpec(w_out_a.shape, const2),
                   pl.BlockSpec((1, D), const2), pl.BlockSpec((1, D), const2)]
    assert len(merge_args) == N_MERGE_A_REFS
    project_args = [mod1, w_in, q_norm.reshape(1, -1), kv_norm.reshape(1, -1), wqa, wk, wv, _v_ones_row(), *tables]
    project_specs = [mod_spec, pl.BlockSpec(w_in.shape, const2),
                     pl.BlockSpec((1, Q_LORA_RANK), const2), pl.BlockSpec((1, KV_LORA_RANK), const2),
                     pl.BlockSpec(wqa.shape, const2), pl.BlockSpec(wk.shape, const2),
                     pl.BlockSpec(wv.shape, const2), pl.BlockSpec((1, HP), const2), tab, tab, tab, tab]
    wide = jax.ShapeDtypeStruct((B, S, HP), jnp.bfloat16)
    return pl.pallas_call(
        _mid_kernel,
        grid=(B, S // tl),
        in_specs=merge_specs + project_specs,
        out_specs=[pl.BlockSpec((None, tl, D), row),
                   pl.BlockSpec((None, tl, HP), row), pl.BlockSpec((None, tl, HP), row),
                   pl.BlockSpec((None, tl, HP), row), pl.BlockSpec((None, tl, WIDTH_B), row)],
        out_shape=[jax.ShapeDtypeStruct((B, S, D), jnp.float32), wide, wide, wide,
                   jax.ShapeDtypeStruct((B, S, WIDTH_B), jnp.bfloat16)],
        compiler_params=_params("arbitrary", "arbitrary"),
        name="mid",
    )(*merge_args, *project_args)


def _attn_b_kernel(q_ref, k_ref, v_ref, gate_ref, o_ref, s_even, s_odd, mx_scr):
    S = k_ref.shape[0]
    tq = s_even.shape[0]
    n_tiles = S // tq
    n_chunks = S // TK_B
    n_heads = q_ref.shape[1] // LANES
    assert n_heads % 2 == 0
    s_bufs = (s_even, s_odd)

    def head_lanes(e):
        return slice(e * LANES, (e + 1) * LANES)

    def tile_rows(t):
        return pl.ds(t * tq, tq) if isinstance(t, int) else pl.ds(pl.multiple_of(t * tq, tq), tq)

    def scores(t, e, c, mx):
        ks = slice(c * TK_B, (c + 1) * TK_B)
        s = lax.dot_general(q_ref[tile_rows(t), head_lanes(e)], k_ref[ks, head_lanes(e)], _NT,
                            preferred_element_type=jnp.float32)
        s_bufs[e % 2][:, ks] = s
        for j in range(TK_B // LANES):
            mx = jnp.maximum(mx, s[:, j * LANES:(j + 1) * LANES])
        return mx

    def weighted_values(e, c, m, acc):
        ks = slice(c * TK_B, (c + 1) * TK_B)
        p = jnp.exp2(s_bufs[e % 2][:, ks] - m).astype(jnp.bfloat16)
        return acc + jnp.dot(p, v_ref[ks, head_lanes(e)], preferred_element_type=jnp.float32)

    neg = jnp.full((tq, LANES), -jnp.inf, jnp.float32)
    zero = jnp.zeros((tq, LANES), jnp.float32)
    lane = lax.broadcasted_iota(jnp.int32, (tq, LANES), 1)

    mx = neg
    for c in range(n_chunks):
        mx = scores(0, 0, c, mx)
    mx_scr[...] = mx

    def tile(t, carry):
        t_next = jnp.minimum(t + 1, n_tiles - 1)
        mx = mx_scr[...]
        acc = []
        for e in range(n_heads):
            m = jnp.max(mx, axis=-1, keepdims=True)
            mx, a = neg, zero
            for c in range(n_chunks):
                mx = scores(t, e + 1, c, mx) if e + 1 < n_heads else scores(t_next, 0, c, mx)
                a = weighted_values(e, c, m, a)
            acc.append(a)
        mx_scr[...] = mx
        for pair in range(n_heads // 2):
            acc_e, acc_o = acc[2 * pair], acc[2 * pair + 1]
            inv_e = 1.0 / acc_e[:, V_HEAD_DIM:V_HEAD_DIM + 1]
            inv_o = 1.0 / acc_o[:, 0:1]
            cols = slice(pair * LANES, (pair + 1) * LANES)
            o = jnp.where(lane < V_HEAD_DIM, acc_e * inv_e, acc_o * inv_o)
            gate = gate_ref[tile_rows(t), cols].astype(jnp.float32)
            o_ref[tile_rows(t), cols] = (o * _silu(gate)).astype(o_ref.dtype)
        return carry

    lax.fori_loop(0, n_tiles, tile, 0)


def _attn_b(q2, k2, v2, gate):
    B, S, HP = q2.shape
    hw = HEADS_PER_STEP_B * LANES
    ow = HEADS_PER_STEP_B * V_HEAD_DIM
    heads = lambda b, p: (b, 0, p)
    return pl.pallas_call(
        _attn_b_kernel,
        grid=(B, N_HEADS_B // HEADS_PER_STEP_B),
        in_specs=[pl.BlockSpec((None, S, hw), heads), pl.BlockSpec((None, S, hw), heads),
                  pl.BlockSpec((None, S, hw), heads), pl.BlockSpec((None, S, ow), heads)],
        out_specs=pl.BlockSpec((None, S, ow), heads),
        out_shape=jax.ShapeDtypeStruct((B, S, WIDTH_B), jnp.bfloat16),
        scratch_shapes=[pltpu.VMEM((TQ_B, S), jnp.float32), pltpu.VMEM((TQ_B, S), jnp.float32),
                        pltpu.VMEM((TQ_B, LANES), jnp.float32)],
        compiler_params=_params("arbitrary", "arbitrary"),
        name="attn_b",
    )(q2, k2, v2, gate)


def _out_b_kernel(x_ref, mod_ref, yg_ref, w_ref, g_ref, b_ref, out_ref):
    D = x_ref.shape[-1]
    y = jnp.dot(yg_ref[...], w_ref[...], preferred_element_type=jnp.float32)
    z = DEEPNORM_ALPHA * x_ref[...] + mod_ref[:, 2 * D:] * y
    out_ref[...] = _layernorm(z, g_ref[...], b_ref[...])


def _out_b(x, mod, yg, w_out, ln_g, ln_b):
    B, S, D = x.shape
    tl = TL_PROJ
    row = lambda b, i: (b, i, 0)
    const2 = lambda b, i: (0, 0)
    return pl.pallas_call(
        _out_b_kernel,
        grid=(B, S // tl),
        in_specs=[pl.BlockSpec((None, tl, D), row),
                  pl.BlockSpec((None, 1, 3 * D), lambda b, i: (b, 0, 0)),
                  pl.BlockSpec((None, tl, WIDTH_B), row),
                  pl.BlockSpec(w_out.shape, const2),
                  pl.BlockSpec((1, D), const2), pl.BlockSpec((1, D), const2)],
        out_specs=pl.BlockSpec((None, tl, D), row),
        out_shape=jax.ShapeDtypeStruct((B, S, D), jnp.float32),
        compiler_params=_params("arbitrary", "arbitrary"),
        name="out_b",
    )(x, mod, yg, w_out, ln_g.reshape(1, D), ln_b.reshape(1, D))


def _weights_a(a_w_in):
    W = WIDTH_A
    gw = GROUP_WIDTH_A
    q, k, v, gate = (a_w_in[:, i * W:(i + 1) * W] for i in range(4))
    ws = []
    for g in range(len(DIL_GROUPS)):
        cs = slice(g * gw, (g + 1) * gw)
        parts = [q[:, cs] * (HEAD_DIM_A ** -0.5 * LOG2E), k[:, cs], v[:, cs]]
        if g == 0:
            parts.append(gate)
        ws.append(jnp.concatenate(parts, axis=1).astype(jnp.bfloat16))
    return ws


def _swap_halves(w):
    half = w.shape[-1] // 2
    return jnp.concatenate([w[..., half:], w[..., :half]], axis=-1)


def _pad_lanes(w, lo):
    n = w.shape[-1]
    return jnp.pad(w, [(0, 0)] * (w.ndim - 1) + [(lo, LANES - lo - n)])


def _weights_b(b_w_in, b_w_uq, b_w_ukv):
    c0 = Q_LORA_RANK
    c1 = c0 + KV_LORA_RANK
    c2 = c1 + QK_ROPE_DIM
    kr = b_w_in[:, c1:c2]
    w_in = jnp.concatenate([b_w_in[:, :c1], _pad_lanes(jnp.concatenate([kr, _swap_halves(kr)], axis=1), QK_NOPE_DIM),
                            b_w_in[:, c2:]], axis=1)
    uq = b_w_uq.reshape(Q_LORA_RANK, N_HEADS_B, QK_NOPE_DIM + QK_ROPE_DIM)
    wqa = jnp.concatenate([uq, _swap_halves(uq[..., QK_NOPE_DIM:])], axis=-1).reshape(Q_LORA_RANK, -1)
    ukv = b_w_ukv.reshape(KV_LORA_RANK, N_HEADS_B, QK_NOPE_DIM + V_HEAD_DIM)
    wk = _pad_lanes(ukv[..., :QK_NOPE_DIM], 0).reshape(KV_LORA_RANK, -1)
    vv = ukv[..., QK_NOPE_DIM:]
    odd = (jnp.arange(N_HEADS_B) % 2 == 1)[None, :, None]
    wv = jnp.where(odd, _pad_lanes(vv, LANES - V_HEAD_DIM), _pad_lanes(vv, 0)).reshape(KV_LORA_RANK, -1)
    bf = lambda w: w.astype(jnp.bfloat16)
    return bf(w_in), bf(wqa), bf(wk), bf(wv)


def _rope_tables(S):
    pos = jnp.arange(S, dtype=jnp.float32)
    inv_freq = ROPE_THETA ** (-jnp.arange(0, QK_ROPE_DIM, 2, dtype=jnp.float32) / QK_ROPE_DIM)
    ang = pos[:, None] * inv_freq[None, :]
    cos, sin = jnp.cos(ang), jnp.sin(ang)
    cosf = _pad_lanes(jnp.concatenate([cos, cos], axis=-1), QK_NOPE_DIM)
    sinf = _pad_lanes(jnp.concatenate([-sin, sin], axis=-1), QK_NOPE_DIM)
    nope = _pad_lanes(jnp.ones((S, QK_NOPE_DIM), jnp.float32), 0)
    scale = (QK_NOPE_DIM + QK_ROPE_DIM) ** -0.5 * LOG2E
    return (scale * (nope + cosf), scale * sinf, cosf, sinf)


def _v_ones_row():
    row = np.zeros((N_HEADS_B, LANES), np.float32)
    row[0::2, V_HEAD_DIM] = 1.0
    row[1::2, 0] = 1.0
    return jnp.asarray(row.reshape(1, -1))


def kernel(x, c, rel_bias, ada_w, ada_b, ln_g, ln_b, a_w_in, a_w_out,
           b_w_in, b_q_norm, b_w_uq, b_kv_norm, b_w_ukv, b_w_out):
    B, S, D = x.shape
    mods = _modulation(c, ada_w, ada_b)
    mod0 = mods[0].reshape(B, 1, 3 * D)
    mod1 = mods[1].reshape(B, 1, 3 * D)

    for window, dil in DIL_GROUPS:
        assert window // (2 * dil) == HALF_WINDOW
    *qkvs, gate_a = _proj_a(x, mod0, _weights_a(a_w_in[0]))
    os_, lses = [], []
    for g, ((_, dil), qkv) in enumerate(zip(DIL_GROUPS, qkvs)):
        hs = slice(g * HEADS_PER_GROUP_A, (g + 1) * HEADS_PER_GROUP_A)
        o, lse = _attn_a(qkv, _bias_variants(rel_bias[:, hs], dil))
        os_.append(o)
        lses.append(lse)

    w_in, wqa, wk, wv = _weights_b(b_w_in[0], b_w_uq[0], b_w_ukv[0])
    x1, q2, k2, v2, gate_b = _mid(x, mod0, os_, lses, gate_a, a_w_out[0].astype(jnp.bfloat16), ln_g[0], ln_b[0],
                                  mod1, w_in, b_q_norm[0], b_kv_norm[0], wqa, wk, wv, _rope_tables(S))
    yg = _attn_b(q2, k2, v2, gate_b)
    return _out_b(x1, mod1, yg, b_w_out[0].astype(jnp.bfloat16), ln_g[1], ln_b[1])
```

```python
import math

import numpy as np
import jax
import jax.numpy as jnp
from jax import lax
from jax.experimental import pallas as pl
from jax.experimental.pallas import tpu as pltpu

D_MODEL = 1024
DEPTH = 2
HEAD_DIM_A = 64
DIL_GROUPS = ((128, 1), (512, 4), (2048, 16))
HEADS_PER_GROUP_A = 6
GROUP_WIDTH_A = HEADS_PER_GROUP_A * HEAD_DIM_A
WIDTH_A = GROUP_WIDTH_A * len(DIL_GROUPS)
N_BUCKETS = 32
T5_MAX_DISTANCE = 1024
N_HEADS_B = 16
QK_NOPE_DIM = 64
QK_ROPE_DIM = 32
V_HEAD_DIM = 64
Q_LORA_RANK = 256
KV_LORA_RANK = 128
WIDTH_B = N_HEADS_B * V_HEAD_DIM
ROPE_THETA = 10000.0
DEEPNORM_ALPHA = (2.0 * DEPTH) ** 0.25
LN_EPS = 1e-5
RMS_EPS = 1e-6
NEG_INF = -1e30
LOG2E = math.log2(math.e)
LN2 = math.log(2.0)

LANES = 128
VMEM_LIMIT_BYTES = 56 * 1024 * 1024

HALF_WINDOW = 64
TQ_A = 128
TK_A = TQ_A + 2 * HALF_WINDOW
LSE_COPY_STRIDE = 8
LSE_COPIES = 3
MAX_TILES_PER_STEP_A = 4
TL_PROJ = 512
PERM_ROWS = 256
TQ_B = 512
TK_B = 512
HEADS_PER_STEP_B = 4

_NT = (((1,), (1,)), ((), ()))


def _silu(x):
    return x * (1.0 / (1.0 + jnp.exp(-x)))


def _params(*sem):
    return pltpu.CompilerParams(dimension_semantics=sem, vmem_limit_bytes=VMEM_LIMIT_BYTES)


def _mod_kernel(c_ref, w_ref, b_ref, o_ref):
    sc = _silu(c_ref[...]).astype(jnp.bfloat16)
    o_ref[...] = jnp.dot(sc, w_ref[...].astype(jnp.bfloat16),
                         preferred_element_type=jnp.float32) + b_ref[...]


def _modulation(c, ada_w, ada_b):
    B, D = c.shape
    nj = 3
    return pl.pallas_call(
        _mod_kernel,
        grid=(DEPTH, nj),
        in_specs=[pl.BlockSpec((B, D), lambda i, j: (0, 0)),
                  pl.BlockSpec((None, D, D), lambda i, j: (i, 0, j)),
                  pl.BlockSpec((None, 1, D), lambda i, j: (i, 0, j))],
        out_specs=pl.BlockSpec((None, B, D), lambda i, j: (i, 0, j)),
        out_shape=jax.ShapeDtypeStruct((DEPTH, B, 3 * D), jnp.float32),
        compiler_params=_params("arbitrary", "arbitrary"),
        name="modulation",
    )(c, ada_w, ada_b.reshape(DEPTH, 1, 3 * D))


def _modulate(x, mod):
    D = x.shape[-1]
    return x * (1.0 + mod[:, D:2 * D]) + mod[:, :D]


def _layernorm(z, g, b):
    mu = jnp.mean(z, axis=-1, keepdims=True)
    zc = z - mu
    var = jnp.mean(zc * zc, axis=-1, keepdims=True)
    return zc * lax.rsqrt(var + LN_EPS) * g + b


def _class_major_perm(tl, dil, inverse=False):
    p = np.arange(tl)
    src = (p % (tl // dil)) * dil + p // (tl // dil)
    mat = (src[:, None] == np.arange(tl)[None, :]).astype(np.float32)
    return jnp.asarray(mat.T if inverse else mat, dtype=jnp.bfloat16)


def _proj_a_kernel(x_ref, mod_ref, w0_ref, w1_ref, w2_ref, perm1_ref, perm2_ref,
                   qkv0_ref, qkv1_ref, qkv2_ref, gate_ref):
    tl = x_ref.shape[0]
    n_sub = tl // PERM_ROWS
    u = _modulate(x_ref[...], mod_ref[...]).astype(jnp.bfloat16)
    n_qkv = qkv0_ref.shape[-1]
    z = jnp.dot(u, w0_ref[...], preferred_element_type=jnp.float32)
    qkv0_ref[0] = z[:, :n_qkv].astype(qkv0_ref.dtype)
    gate_ref[...] = z[:, n_qkv:].astype(gate_ref.dtype)
    for perm_ref, w_ref, qkv_ref in ((perm1_ref, w1_ref, qkv1_ref), (perm2_ref, w2_ref, qkv2_ref)):
        pm = perm_ref[...]
        up = jnp.concatenate(
            [jnp.dot(pm, u[sub * PERM_ROWS:(sub + 1) * PERM_ROWS], preferred_element_type=jnp.float32)
             for sub in range(n_sub)], axis=0).astype(jnp.bfloat16)
        z = jnp.dot(up, w_ref[...], preferred_element_type=jnp.float32)
        dil, rows = qkv_ref.shape[:2]
        rs = rows // n_sub
        for sub in range(n_sub):
            for r in range(dil):
                r0 = sub * PERM_ROWS + r * rs
                qkv_ref[r, sub * rs:(sub + 1) * rs] = z[r0:r0 + rs].astype(qkv_ref.dtype)


def _proj_a(x, mod, ws):
    B, S, D = x.shape
    tl = 2 * TL_PROJ
    n_qkv = 3 * GROUP_WIDTH_A
    dils = [dil for _, dil in DIL_GROUPS]
    assert dils[0] == 1
    row = lambda b, i: (b, i, 0)
    cls = lambda b, i: (b, 0, i, 0)
    const2 = lambda b, i: (0, 0)
    perms = [_class_major_perm(PERM_ROWS, dil) for dil in dils[1:]]
    return pl.pallas_call(
        _proj_a_kernel,
        grid=(B, S // tl),
        in_specs=[pl.BlockSpec((None, tl, D), row),
                  pl.BlockSpec((None, 1, 3 * D), lambda b, i: (b, 0, 0)),
                  *[pl.BlockSpec(w.shape, const2) for w in ws],
                  *[pl.BlockSpec(p.shape, const2) for p in perms]],
        out_specs=[*[pl.BlockSpec((None, dil, tl // dil, n_qkv), cls) for dil in dils],
                   pl.BlockSpec((None, tl, WIDTH_A), row)],
        out_shape=[*[jax.ShapeDtypeStruct((B, dil, S // dil, n_qkv), jnp.bfloat16) for dil in dils],
                   jax.ShapeDtypeStruct((B, S, WIDTH_A), jnp.bfloat16)],
        compiler_params=_params("arbitrary", "arbitrary"),
        name="proj_a",
    )(x, mod, *ws, *perms)


def _attn_a_kernel(qkv_ref, bias_ref, o_ref, lse_ref, s_scr, p_scr):
    n_cls, L = (1,) + qkv_ref.shape[:1] if len(qkv_ref.shape) == 2 else qkv_ref.shape[:2]

    def of_class(ref, cls):
        return ref if len(ref.shape) == 2 else ref.at[cls]
    n_tiles = L // TQ_A
    gw = GROUP_WIDTH_A
    lane = lax.broadcasted_iota(jnp.int32, (TQ_A, LANES), 1)
    low_q = lane < HEAD_DIM_A
    lse_lane = jnp.where(lane < LSE_COPIES * LSE_COPY_STRIDE, lane % LSE_COPY_STRIDE, -1)
    low_k = lax.broadcasted_iota(jnp.int32, (TK_A, LANES), 1) < HEAD_DIM_A

    heads = range(HEADS_PER_GROUP_A)

    def pair_block(cls, rows, section, h):
        c0 = section * gw + (h // 2) * LANES
        return of_class(qkv_ref, cls)[rows, c0:c0 + LANES]

    def other_lanes(block, low_mask, h, fill):
        mine = low_mask if h % 2 == 0 else jnp.logical_not(low_mask)
        return jnp.where(mine, block, jnp.full_like(block, fill))

    tiles_per_step = s_scr.shape[0]

    def tile_group(tg, carry):
        geo = []
        for u in range(tiles_per_step):
            item = tg * tiles_per_step + u
            cls, t = (0, item) if n_cls == 1 else (item // n_tiles, item % n_tiles)
            j0 = pl.multiple_of(t * TQ_A, TQ_A)
            ws = pl.multiple_of(jnp.clip(j0 - HALF_WINDOW, 0, L - TK_A), HALF_WINDOW)
            variant = jnp.where(t == 0, 0, jnp.where(t == n_tiles - 1, 2, 1))
            geo.append((cls, pl.ds(j0, TQ_A), pl.ds(ws, TK_A), variant))
        for u, (cls, rq, rk, variant) in enumerate(geo):
            for h in heads:
                q = other_lanes(pair_block(cls, rq, 0, h), low_q, h, 0)
                s = lax.dot_general(q, pair_block(cls, rk, 1, h), _NT, preferred_element_type=jnp.float32)
                s_scr[u, h] = s + bias_ref[variant, h]
        m = [[jnp.max(s_scr[u, h], axis=-1, keepdims=True) for h in heads] for u in range(len(geo))]
        for u in range(len(geo)):
            for h in heads:
                p_scr[u, h] = jnp.exp2(s_scr[u, h] - m[u][h]).astype(jnp.bfloat16)
        for u, (cls, rq, rk, variant) in enumerate(geo):
            lse_tile = jnp.zeros((TQ_A, LANES), jnp.float32)
            for pair in range(HEADS_PER_GROUP_A // 2):
                he, ho = 2 * pair, 2 * pair + 1
                acc_e = jnp.dot(p_scr[u, he], other_lanes(pair_block(cls, rk, 2, he), low_k, he, 1),
                                preferred_element_type=jnp.float32)
                acc_o = jnp.dot(p_scr[u, ho], other_lanes(pair_block(cls, rk, 2, ho), low_k, ho, 1),
                                preferred_element_type=jnp.float32)
                numer = jnp.where(low_q, acc_e, acc_o)
                den_other = jnp.where(low_q, acc_o, acc_e)
                den = pltpu.roll(den_other, HEAD_DIM_A, 1)
                of_class(o_ref, cls)[rq, pair * LANES:(pair + 1) * LANES] = (numer / den).astype(o_ref.dtype)
                is_odd = lse_lane == ho
                lse = (jnp.where(is_odd, m[u][ho], m[u][he])
                       + jnp.log2(jnp.where(is_odd, den_other, den))) * LN2
                lse_tile = jnp.where(jnp.logical_or(is_odd, lse_lane == he), lse, lse_tile)
            of_class(lse_ref, cls)[rq, :] = lse_tile
        return carry

    lax.fori_loop(0, n_cls * n_tiles // tiles_per_step, tile_group, 0)


def _attn_a(qkv, biasm):
    B, dil, L, n_qkv = qkv.shape
    gw = GROUP_WIDTH_A
    n_tiles = L // TQ_A
    n_cls = min(dil, max(1, MAX_TILES_PER_STEP_A // n_tiles))
    tiles_per_step = math.gcd(MAX_TILES_PER_STEP_A, n_cls * n_tiles)
    blk_cls = None if n_cls == 1 else n_cls
    cls = lambda b, r: (b, r, 0, 0)
    return pl.pallas_call(
        _attn_a_kernel,
        grid=(B, dil // n_cls),
        in_specs=[pl.BlockSpec((None, blk_cls, L, n_qkv), cls),
                  pl.BlockSpec(biasm.shape, lambda b, r: (0, 0, 0, 0))],
        out_specs=[pl.BlockSpec((None, blk_cls, L, gw), cls),
                   pl.BlockSpec((None, blk_cls, L, LANES), cls)],
        out_shape=[jax.ShapeDtypeStruct((B, dil, L, gw), jnp.bfloat16),
                   jax.ShapeDtypeStruct((B, dil, L, LANES), jnp.float32)],
        scratch_shapes=[
            pltpu.VMEM((tiles_per_step, HEADS_PER_GROUP_A, TQ_A, TK_A), jnp.float32),
            pltpu.VMEM((tiles_per_step, HEADS_PER_GROUP_A, TQ_A, TK_A), jnp.bfloat16)],
        compiler_params=_params("arbitrary", "arbitrary"),
        name=f"attn_a_d{dil}",
    )(qkv, biasm)


def _t5_bucket_np(rel):
    half = N_BUCKETS // 2
    max_exact = half // 2
    base = np.where(rel > 0, half, 0)
    n = np.abs(rel)
    nf = np.maximum(n, 1).astype(np.float32)
    large = max_exact + (np.log(nf / np.float32(max_exact)) / np.float32(math.log(T5_MAX_DISTANCE / max_exact))
                         * np.float32(half - max_exact)).astype(np.int32)
    large = np.minimum(large, half - 1)
    return base + np.where(n < max_exact, n, large)


def _bias_variants(rel_bias_g, dil):
    H = rel_bias_g.shape[1]
    M = 2 * TK_A
    off = np.arange(M) - (TK_A - 1)
    idx = np.where(np.abs(off) <= HALF_WINDOW, _t5_bucket_np(off * dil), N_BUCKETS).astype(np.int32)
    table = jnp.concatenate([rel_bias_g.astype(jnp.float32) * LOG2E, jnp.full((1, H), NEG_INF, jnp.float32)], axis=0)
    w = table[jnp.asarray(idx)].T
    rot = jnp.tile(w, (1, TQ_A + 1))[:, :TQ_A * (M + 1)].reshape(H, TQ_A, M + 1)[:, ::-1, :]
    variants = []
    for shift in (0, HALF_WINDOW, 2 * HALF_WINDOW):
        j0 = TK_A - TQ_A - shift
        variants.append(rot[:, :, j0:j0 + TK_A])
    return jnp.stack(variants)


def _split3_bf16(v):
    hi = v.astype(jnp.bfloat16)
    r1 = v - hi.astype(jnp.float32)
    mid = r1.astype(jnp.bfloat16)
    lo = (r1 - mid.astype(jnp.float32)).astype(jnp.bfloat16)
    return hi, mid, lo


N_MERGE_A_REFS = 15


def _merge_a(x_ref, mod_ref, o0_ref, o1_ref, o2_ref, l0_ref, l1_ref, l2_ref, gate_ref,
             unperm1_ref, unperm2_ref, expand_ref, w_ref, g_ref, b_ref):
    tl, D = x_ref.shape
    stride = LSE_COPY_STRIDE
    lane_p = lax.broadcasted_iota(jnp.int32, (PERM_ROWS, LANES), 1)
    lane = lax.broadcasted_iota(jnp.int32, (tl, LANES), 1)

    def class_major_rows(ref, sub):
        dil, rows = ref.shape[:2]
        rs = rows * PERM_ROWS // tl
        return jnp.concatenate([ref[r, sub * rs:(sub + 1) * rs] for r in range(dil)], axis=0)

    def natural_order(o_ref, l_ref, unperm_ref):
        if unperm_ref is None:
            return o_ref[0].astype(jnp.float32), l_ref[0]
        pt = unperm_ref[...]
        o_parts, l_parts = [], []
        for sub in range(tl // PERM_ROWS):
            o_parts.append(jnp.dot(pt, class_major_rows(o_ref, sub), preferred_element_type=jnp.float32))
            hi, mid, lo = _split3_bf16(class_major_rows(l_ref, sub))
            packed = jnp.where(lane_p < stride, hi, jnp.where(lane_p < 2 * stride, mid, lo))
            res = jnp.dot(pt, packed, preferred_element_type=jnp.float32)
            l_parts.append(res + pltpu.roll(res, LANES - stride, 1) + pltpu.roll(res, LANES - 2 * stride, 1))
        return jnp.concatenate(o_parts, axis=0), jnp.concatenate(l_parts, axis=0)

    o0, l0 = natural_order(o0_ref, l0_ref, None)
    o1, l1 = natural_order(o1_ref, l1_ref, unperm1_ref)
    o2, l2 = natural_order(o2_ref, l2_ref, unperm2_ref)
    mx = jnp.maximum(jnp.maximum(l0, l1), l2)
    e0, e1, e2 = jnp.exp(l0 - mx), jnp.exp(l1 - mx), jnp.exp(l2 - mx)
    inv = 1.0 / (e0 + e1 + e2)
    wt = jnp.where(lane < stride, e0 * inv,
                   jnp.where(lane < 2 * stride, pltpu.roll(e1 * inv, stride, 1), pltpu.roll(e2 * inv, 2 * stride, 1)))
    hi = wt.astype(jnp.bfloat16)
    lo = (wt - hi.astype(jnp.float32)).astype(jnp.bfloat16)
    wexp = (jnp.dot(hi, expand_ref[...], preferred_element_type=jnp.float32)
            + jnp.dot(lo, expand_ref[...], preferred_element_type=jnp.float32))
    o = jnp.concatenate([o0, o1, o2], axis=1)
    yg = (o * wexp * _silu(gate_ref[...].astype(jnp.float32))).astype(jnp.bfloat16)
    y = jnp.dot(yg, w_ref[...], preferred_element_type=jnp.float32)
    z = DEEPNORM_ALPHA * x_ref[...] + mod_ref[:, 2 * D:] * y
    return _layernorm(z, g_ref[...], b_ref[...])


def _rmsnorm(x, g):
    return x * lax.rsqrt(jnp.mean(x * x, axis=-1, keepdims=True) + RMS_EPS) * g


def _project_b(x, mod_ref, w_in_ref, qn_ref, kvn_ref, wqa_ref, wk_ref, wv_ref,
               vone_ref, cq_ref, sq_ref, ck_ref, sk_ref, q_out, k_out, v_out, gate_out):
    u = _modulate(x, mod_ref[...]).astype(jnp.bfloat16)
    z = jnp.dot(u, w_in_ref[...], preferred_element_type=jnp.float32)
    c0 = Q_LORA_RANK
    c1 = c0 + KV_LORA_RANK
    cq = z[:, :c0]
    ckv = z[:, c0:c1]
    kr = z[:, c1:c1 + LANES]
    gate_out[...] = z[:, c1 + LANES:].astype(gate_out.dtype)
    to_rope_lanes = LANES - QK_ROPE_DIM

    cqn = _rmsnorm(cq, qn_ref[...]).astype(jnp.bfloat16)
    qa = jnp.dot(cqn, wqa_ref[...], preferred_element_type=jnp.float32)
    ckvn = _rmsnorm(ckv, kvn_ref[...]).astype(jnp.bfloat16)
    kn = jnp.dot(ckvn, wk_ref[...], preferred_element_type=jnp.float32)
    v_out[...] = (jnp.dot(ckvn, wv_ref[...], preferred_element_type=jnp.float32)
                  + vone_ref[...]).astype(v_out.dtype)

    cos_q, sin_q = cq_ref[...], sq_ref[...]
    k_rope = kr * ck_ref[...] + pltpu.roll(kr, to_rope_lanes, 1) * sk_ref[...]
    for h in range(N_HEADS_B):
        blk = slice(h * LANES, (h + 1) * LANES)
        qh = qa[:, blk]
        q_out[:, blk] = (qh * cos_q + pltpu.roll(qh, to_rope_lanes, 1) * sin_q).astype(q_out.dtype)
        k_out[:, blk] = (kn[:, blk] + k_rope).astype(k_out.dtype)


def _mid_kernel(*refs):
    merge_refs, rest = refs[:N_MERGE_A_REFS], refs[N_MERGE_A_REFS:]
    x1_out, project_refs = rest[-5], rest[:-5] + rest[-4:]
    x1 = _merge_a(*merge_refs)
    x1_out[...] = x1
    _project_b(x1, *project_refs)


def _mid(x, mod0, os_, lses, gate_a, w_out_a, ln_g, ln_b,
         mod1, w_in, q_norm, kv_norm, wqa, wk, wv, tables):
    B, S, D = x.shape
    tl = TL_PROJ
    gw = GROUP_WIDTH_A
    HP = N_HEADS_B * LANES
    col = np.arange(WIDTH_A)
    src_lane = (col // gw) * LSE_COPY_STRIDE + (col % gw) // HEAD_DIM_A
    expand = jnp.asarray((np.arange(LANES)[:, None] == src_lane[None, :]).astype(np.float32), dtype=jnp.bfloat16)
    unperms = [_class_major_perm(PERM_ROWS, dil, inverse=True) for _, dil in DIL_GROUPS[1:]]
    row = lambda b, i: (b, i, 0)
    cls = lambda b, i: (b, 0, i, 0)
    const2 = lambda b, i: (0, 0)
    mod_spec = pl.BlockSpec((None, 1, 3 * D), lambda b, i: (b, 0, 0))
    tab = pl.BlockSpec((tl, LANES), lambda b, i: (i, 0))
    o_specs = [pl.BlockSpec((None, dil, tl // dil, gw), cls) for _, dil in DIL_GROUPS]
    l_specs = [pl.BlockSpec((None, dil, tl // dil, LANES), cls) for _, dil in DIL_GROUPS]
    merge_args = [x, mod0, *os_, *lses, gate_a, *unperms, expand, w_out_a, ln_g.reshape(1, D), ln_b.reshape(1, D)]
    merge_specs = [pl.BlockSpec((None, tl, D), row), mod_spec, *o_specs, *l_specs,
                   pl.BlockSpec((None, tl, WIDTH_A), row),
                   *[pl.BlockSpec(p.shape, const2) for p in unperms],
                   pl.BlockSpec(expand.shape, const2), pl.BlockSpec(w_out_a.shape, const2),
                   pl.BlockSpec((1, D), const2), pl.BlockSpec((1, D), const2)]
    assert len(merge_args) == N_MERGE_A_REFS
    project_args = [mod1, w_in, q_norm.reshape(1, -1), kv_norm.reshape(1, -1), wqa, wk, wv, _v_ones_row(), *tables]
    project_specs = [mod_spec, pl.BlockSpec(w_in.shape, const2),
                     pl.BlockSpec((1, Q_LORA_RANK), const2), pl.BlockSpec((1, KV_LORA_RANK), const2),
                     pl.BlockSpec(wqa.shape, const2), pl.BlockSpec(wk.shape, const2),
                     pl.BlockSpec(wv.shape, const2), pl.BlockSpec((1, HP), const2), tab, tab, tab, tab]
    wide = jax.ShapeDtypeStruct((B, S, HP), jnp.bfloat16)
    return pl.pallas_call(
        _mid_kernel,
        grid=(B, S // tl),
        in_specs=merge_specs + project_specs,
        out_specs=[pl.BlockSpec((None, tl, D), row),
                   pl.BlockSpec((None, tl, HP), row), pl.BlockSpec((None, tl, HP), row),
                   pl.BlockSpec((None, tl, HP), row), pl.BlockSpec((None, tl, WIDTH_B), row)],
        out_shape=[jax.ShapeDtypeStruct((B, S, D), jnp.float32), wide, wide, wide,
                   jax.ShapeDtypeStruct((B, S, WIDTH_B), jnp.bfloat16)],
        compiler_params=_params("arbitrary", "arbitrary"),
        name="mid",
    )(*merge_args, *project_args)


def _attn_b_kernel(q_ref, k_ref, v_ref, gate_ref, o_ref, s_even, s_odd, mx_scr):
    S = k_ref.shape[0]
    tq = s_even.shape[0]
    n_tiles = S // tq
    n_chunks = S // TK_B
    n_heads = q_ref.shape[1] // LANES
    assert n_heads % 2 == 0
    s_bufs = (s_even, s_odd)

    def head_lanes(e):
        return slice(e * LANES, (e + 1) * LANES)

    def tile_rows(t):
        return pl.ds(t * tq, tq) if isinstance(t, int) else pl.ds(pl.multiple_of(t * tq, tq), tq)

    def scores(t, e, c, mx):
        ks = slice(c * TK_B, (c + 1) * TK_B)
        s = lax.dot_general(q_ref[tile_rows(t), head_lanes(e)], k_ref[ks, head_lanes(e)], _NT,
                            preferred_element_type=jnp.float32)
        s_bufs[e % 2][:, ks] = s
        for j in range(TK_B // LANES):
            mx = jnp.maximum(mx, s[:, j * LANES:(j + 1) * LANES])
        return mx

    def weighted_values(e, c, m, acc):
        ks = slice(c * TK_B, (c + 1) * TK_B)
        p = jnp.exp2(s_bufs[e % 2][:, ks] - m).astype(jnp.bfloat16)
        return acc + jnp.dot(p, v_ref[ks, head_lanes(e)], preferred_element_type=jnp.float32)

    neg = jnp.full((tq, LANES), -jnp.inf, jnp.float32)
    zero = jnp.zeros((tq, LANES), jnp.float32)
    lane = lax.broadcasted_iota(jnp.int32, (tq, LANES), 1)

    mx = neg
    for c in range(n_chunks):
        mx = scores(0, 0, c, mx)
    mx_scr[...] = mx

    def tile(t, carry):
        t_next = jnp.minimum(t + 1, n_tiles - 1)
        mx = mx_scr[...]
        acc = []
        for e in range(n_heads):
            m = jnp.max(mx, axis=-1, keepdims=True)
            mx, a = neg, zero
            for c in range(n_chunks):
                mx = scores(t, e + 1, c, mx) if e + 1 < n_heads else scores(t_next, 0, c, mx)
                a = weighted_values(e, c, m, a)
            acc.append(a)
        mx_scr[...] = mx
        for pair in range(n_heads // 2):
            acc_e, acc_o = acc[2 * pair], acc[2 * pair + 1]
            inv_e = 1.0 / acc_e[:, V_HEAD_DIM:V_HEAD_DIM + 1]
            inv_o = 1.0 / acc_o[:, 0:1]
            cols = slice(pair * LANES, (pair + 1) * LANES)
            o = jnp.where(lane < V_HEAD_DIM, acc_e * inv_e, acc_o * inv_o)
            gate = gate_ref[tile_rows(t), cols].astype(jnp.float32)
            o_ref[tile_rows(t), cols] = (o * _silu(gate)).astype(o_ref.dtype)
        return carry

    lax.fori_loop(0, n_tiles, tile, 0)


def _attn_b(q2, k2, v2, gate):
    B, S, HP = q2.shape
    hw = HEADS_PER_STEP_B * LANES
    ow = HEADS_PER_STEP_B * V_HEAD_DIM
    heads = lambda b, p: (b, 0, p)
    return pl.pallas_call(
        _attn_b_kernel,
        grid=(B, N_HEADS_B // HEADS_PER_STEP_B),
        in_specs=[pl.BlockSpec((None, S, hw), heads), pl.BlockSpec((None, S, hw), heads),
                  pl.BlockSpec((None, S, hw), heads), pl.BlockSpec((None, S, ow), heads)],
        out_specs=pl.BlockSpec((None, S, ow), heads),
        out_shape=jax.ShapeDtypeStruct((B, S, WIDTH_B), jnp.bfloat16),
        scratch_shapes=[pltpu.VMEM((TQ_B, S), jnp.float32), pltpu.VMEM((TQ_B, S), jnp.float32),
                        pltpu.VMEM((TQ_B, LANES), jnp.float32)],
        compiler_params=_params("arbitrary", "arbitrary"),
        name="attn_b",
    )(q2, k2, v2, gate)


def _out_b_kernel(x_ref, mod_ref, yg_ref, w_ref, g_ref, b_ref, out_ref):
    D = x_ref.shape[-1]
    y = jnp.dot(yg_ref[...], w_ref[...], preferred_element_type=jnp.float32)
    z = DEEPNORM_ALPHA * x_ref[...] + mod_ref[:, 2 * D:] * y
    out_ref[...] = _layernorm(z, g_ref[...], b_ref[...])


def _out_b(x, mod, yg, w_out, ln_g, ln_b):
    B, S, D = x.shape
    tl = 2 * TL_PROJ
    row = lambda b, i: (b, i, 0)
    const2 = lambda b, i: (0, 0)
    return pl.pallas_call(
        _out_b_kernel,
        grid=(B, S // tl),
        in_specs=[pl.BlockSpec((None, tl, D), row),
                  pl.BlockSpec((None, 1, 3 * D), lambda b, i: (b, 0, 0)),
                  pl.BlockSpec((None, tl, WIDTH_B), row),
                  pl.BlockSpec(w_out.shape, const2),
                  pl.BlockSpec((1, D), const2), pl.BlockSpec((1, D), const2)],
        out_specs=pl.BlockSpec((None, tl, D), row),
        out_shape=jax.ShapeDtypeStruct((B, S, D), jnp.float32),
        compiler_params=_params("arbitrary", "arbitrary"),
        name="out_b",
    )(x, mod, yg, w_out, ln_g.reshape(1, D), ln_b.reshape(1, D))


def _weights_a(a_w_in):
    W = WIDTH_A
    gw = GROUP_WIDTH_A
    q, k, v, gate = (a_w_in[:, i * W:(i + 1) * W] for i in range(4))
    ws = []
    for g in range(len(DIL_GROUPS)):
        cs = slice(g * gw, (g + 1) * gw)
        parts = [q[:, cs] * (HEAD_DIM_A ** -0.5 * LOG2E), k[:, cs], v[:, cs]]
        if g == 0:
            parts.append(gate)
        ws.append(jnp.concatenate(parts, axis=1).astype(jnp.bfloat16))
    return ws


def _swap_halves(w):
    half = w.shape[-1] // 2
    return jnp.concatenate([w[..., half:], w[..., :half]], axis=-1)


def _pad_lanes(w, lo):
    n = w.shape[-1]
    return jnp.pad(w, [(0, 0)] * (w.ndim - 1) + [(lo, LANES - lo - n)])


def _weights_b(b_w_in, b_w_uq, b_w_ukv):
    c0 = Q_LORA_RANK
    c1 = c0 + KV_LORA_RANK
    c2 = c1 + QK_ROPE_DIM
    kr = b_w_in[:, c1:c2]
    w_in = jnp.concatenate([b_w_in[:, :c1], _pad_lanes(jnp.concatenate([kr, _swap_halves(kr)], axis=1), QK_NOPE_DIM),
                            b_w_in[:, c2:]], axis=1)
    uq = b_w_uq.reshape(Q_LORA_RANK, N_HEADS_B, QK_NOPE_DIM + QK_ROPE_DIM)
    wqa = jnp.concatenate([uq, _swap_halves(uq[..., QK_NOPE_DIM:])], axis=-1).reshape(Q_LORA_RANK, -1)
    ukv = b_w_ukv.reshape(KV_LORA_RANK, N_HEADS_B, QK_NOPE_DIM + V_HEAD_DIM)
    wk = _pad_lanes(ukv[..., :QK_NOPE_DIM], 0).reshape(KV_LORA_RANK, -1)
    vv = ukv[..., QK_NOPE_DIM:]
    odd = (jnp.arange(N_HEADS_B) % 2 == 1)[None, :, None]
    wv = jnp.where(odd, _pad_lanes(vv, LANES - V_HEAD_DIM), _pad_lanes(vv, 0)).reshape(KV_LORA_RANK, -1)
    bf = lambda w: w.astype(jnp.bfloat16)
    return bf(w_in), bf(wqa), bf(wk), bf(wv)


def _rope_tables(S):
    pos = jnp.arange(S, dtype=jnp.float32)
    inv_freq = ROPE_THETA ** (-jnp.arange(0, QK_ROPE_DIM, 2, dtype=jnp.float32) / QK_ROPE_DIM)
    ang = pos[:, None] * inv_freq[None, :]
    cos, sin = jnp.cos(ang), jnp.sin(ang)
    cosf = _pad_lanes(jnp.concatenate([cos, cos], axis=-1), QK_NOPE_DIM)
    sinf = _pad_lanes(jnp.concatenate([-sin, sin], axis=-1), QK_NOPE_DIM)
    nope = _pad_lanes(jnp.ones((S, QK_NOPE_DIM), jnp.float32), 0)
    scale = (QK_NOPE_DIM + QK_ROPE_DIM) ** -0.5 * LOG2E
    return (scale * (nope + cosf), scale * sinf, cosf, sinf)


def _v_ones_row():
    row = np.zeros((N_HEADS_B, LANES), np.float32)
    row[0::2, V_HEAD_DIM] = 1.0
    row[1::2, 0] = 1.0
    return jnp.asarray(row.reshape(1, -1))


def kernel(x, c, rel_bias, ada_w, ada_b, ln_g, ln_b, a_w_in, a_w_out,
           b_w_in, b_q_norm, b_w_uq, b_kv_norm, b_w_ukv, b_w_out):
    B, S, D = x.shape
    mods = _modulation(c, ada_w, ada_b)
    mod0 = mods[0].reshape(B, 1, 3 * D)
    mod1 = mods[1].reshape(B, 1, 3 * D)

    for window, dil in DIL_GROUPS:
        assert window // (2 * dil) == HALF_WINDOW
    *qkvs, gate_a = _proj_a(x, mod0, _weights_a(a_w_in[0]))
    os_, lses = [], []
    for g, ((_, dil), qkv) in enumerate(zip(DIL_GROUPS, qkvs)):
        hs = slice(g * HEADS_PER_GROUP_A, (g + 1) * HEADS_PER_GROUP_A)
        o, lse = _attn_a(qkv, _bias_variants(rel_bias[:, hs], dil))
        os_.append(o)
        lses.append(lse)

    w_in, wqa, wk, wv = _weights_b(b_w_in[0], b_w_uq[0], b_w_ukv[0])
    x1, q2, k2, v2, gate_b = _mid(x, mod0, os_, lses, gate_a, a_w_out[0].astype(jnp.bfloat16), ln_g[0], ln_b[0],
                                  mod1, w_in, b_q_norm[0], b_kv_norm[0], wqa, wk, wv, _rope_tables(S))
    yg = _attn_b(q2, k2, v2, gate_b)
    return _out_b(x1, mod1, yg, b_w_out[0].astype(jnp.bfloat16), ln_g[1], ln_b[1])
```

```python
import math

import numpy as np
import jax
import jax.numpy as jnp
from jax import lax
from jax.experimental import pallas as pl
from jax.experimental.pallas import tpu as pltpu

DEPTH = 2
HEAD_DIM_A = 64
DIL_GROUPS = ((128, 1), (512, 4), (2048, 16))
HEADS_PER_GROUP_A = 6
GROUP_WIDTH_A = HEADS_PER_GROUP_A * HEAD_DIM_A
WIDTH_A = GROUP_WIDTH_A * len(DIL_GROUPS)
N_BUCKETS = 32
T5_MAX_DISTANCE = 1024
N_HEADS_B = 16
QK_NOPE_DIM = 64
QK_ROPE_DIM = 32
V_HEAD_DIM = 64
Q_LORA_RANK = 256
KV_LORA_RANK = 128
WIDTH_B = N_HEADS_B * V_HEAD_DIM
ROPE_THETA = 10000.0
DEEPNORM_ALPHA = (2.0 * DEPTH) ** 0.25
LN_EPS = 1e-5
RMS_EPS = 1e-6
NEG_INF = -1e30
LOG2E = math.log2(math.e)
LN2 = math.log(2.0)

LANES = 128
VMEM_LIMIT_BYTES = 56 * 1024 * 1024

HALF_WINDOW = 64
TQ_A = 128
TK_A = TQ_A + 2 * HALF_WINDOW
LSE_COPY_STRIDE = 8
LSE_COPIES = 3
MAX_TILES_PER_STEP_A = 4
TL_PROJ = 512
TL_WIDE = 1024
PERM_ROWS = 256
TQ_B = 512
TK_B = 512
HEADS_PER_STEP_B = 4

_NT = (((1,), (1,)), ((), ()))


def _silu(x):
    return x * (1.0 / (1.0 + jnp.exp(-x)))


def _params(*sem):
    return pltpu.CompilerParams(dimension_semantics=sem, vmem_limit_bytes=VMEM_LIMIT_BYTES)


def _mod_kernel(c_ref, w_ref, b_ref, o_ref):
    sc = _silu(c_ref[...]).astype(jnp.bfloat16)
    o_ref[...] = jnp.dot(sc, w_ref[...].astype(jnp.bfloat16),
                         preferred_element_type=jnp.float32) + b_ref[...]


def _modulation(c, ada_w, ada_b):
    B, D = c.shape
    nj = 3
    return pl.pallas_call(
        _mod_kernel,
        grid=(DEPTH, nj),
        in_specs=[pl.BlockSpec((B, D), lambda i, j: (0, 0)),
                  pl.BlockSpec((None, D, D), lambda i, j: (i, 0, j)),
                  pl.BlockSpec((None, 1, D), lambda i, j: (i, 0, j))],
        out_specs=pl.BlockSpec((None, B, D), lambda i, j: (i, 0, j)),
        out_shape=jax.ShapeDtypeStruct((DEPTH, B, 3 * D), jnp.float32),
        compiler_params=_params("arbitrary", "arbitrary"),
        name="modulation",
    )(c, ada_w, ada_b.reshape(DEPTH, 1, 3 * D))


def _modulate(x, mod):
    D = x.shape[-1]
    return x * (1.0 + mod[:, D:2 * D]) + mod[:, :D]


def _layernorm(z, g, b):
    mu = jnp.mean(z, axis=-1, keepdims=True)
    zc = z - mu
    var = jnp.mean(zc * zc, axis=-1, keepdims=True)
    return zc * lax.rsqrt(var + LN_EPS) * g + b


def _class_major_perm(tl, dil, inverse=False):
    p = np.arange(tl)
    src = (p % (tl // dil)) * dil + p // (tl // dil)
    mat = (src[:, None] == np.arange(tl)[None, :]).astype(np.float32)
    return jnp.asarray(mat.T if inverse else mat, dtype=jnp.bfloat16)


def _proj_a_kernel(x_ref, mod_ref, w0_ref, w1_ref, w2_ref, perm1_ref, perm2_ref,
                   qkv0_ref, qkv1_ref, qkv2_ref, gate_ref):
    tl = x_ref.shape[0]
    n_sub = tl // PERM_ROWS
    u = _modulate(x_ref[...], mod_ref[...]).astype(jnp.bfloat16)
    n_qkv = qkv0_ref.shape[-1]
    z = jnp.dot(u, w0_ref[...], preferred_element_type=jnp.float32)
    qkv0_ref[0] = z[:, :n_qkv].astype(qkv0_ref.dtype)
    gate_ref[...] = z[:, n_qkv:].astype(gate_ref.dtype)
    for perm_ref, w_ref, qkv_ref in ((perm1_ref, w1_ref, qkv1_ref), (perm2_ref, w2_ref, qkv2_ref)):
        pm = perm_ref[...]
        up = jnp.concatenate(
            [jnp.dot(pm, u[sub * PERM_ROWS:(sub + 1) * PERM_ROWS], preferred_element_type=jnp.float32)
             for sub in range(n_sub)], axis=0).astype(jnp.bfloat16)
        z = jnp.dot(up, w_ref[...], preferred_element_type=jnp.float32)
        dil, rows = qkv_ref.shape[:2]
        rs = rows // n_sub
        for sub in range(n_sub):
            for r in range(dil):
                r0 = sub * PERM_ROWS + r * rs
                qkv_ref[r, sub * rs:(sub + 1) * rs] = z[r0:r0 + rs].astype(qkv_ref.dtype)


def _proj_a(x, mod, ws):
    B, S, D = x.shape
    tl = TL_WIDE
    n_qkv = 3 * GROUP_WIDTH_A
    dils = [dil for _, dil in DIL_GROUPS]
    assert dils[0] == 1
    row = lambda b, i: (b, i, 0)
    cls = lambda b, i: (b, 0, i, 0)
    const2 = lambda b, i: (0, 0)
    perms = [_class_major_perm(PERM_ROWS, dil) for dil in dils[1:]]
    return pl.pallas_call(
        _proj_a_kernel,
        grid=(B, S // tl),
        in_specs=[pl.BlockSpec((None, tl, D), row),
                  pl.BlockSpec((None, 1, 3 * D), lambda b, i: (b, 0, 0)),
                  *[pl.BlockSpec(w.shape, const2) for w in ws],
                  *[pl.BlockSpec(p.shape, const2) for p in perms]],
        out_specs=[*[pl.BlockSpec((None, dil, tl // dil, n_qkv), cls) for dil in dils],
                   pl.BlockSpec((None, tl, WIDTH_A), row)],
        out_shape=[*[jax.ShapeDtypeStruct((B, dil, S // dil, n_qkv), jnp.bfloat16) for dil in dils],
                   jax.ShapeDtypeStruct((B, S, WIDTH_A), jnp.bfloat16)],
        compiler_params=_params("arbitrary", "arbitrary"),
        name="proj_a",
    )(x, mod, *ws, *perms)


def _attn_a_kernel(qkv_ref, bias_ref, o_ref, lse_ref, s_scr, p_scr):
    n_cls, L = (1,) + qkv_ref.shape[:1] if len(qkv_ref.shape) == 2 else qkv_ref.shape[:2]

    def of_class(ref, cls):
        return ref if len(ref.shape) == 2 else ref.at[cls]
    n_tiles = L // TQ_A
    gw = GROUP_WIDTH_A
    lane = lax.broadcasted_iota(jnp.int32, (TQ_A, LANES), 1)
    low_q = lane < HEAD_DIM_A
    lse_lane = jnp.where(lane < LSE_COPIES * LSE_COPY_STRIDE, lane % LSE_COPY_STRIDE, -1)
    low_k = lax.broadcasted_iota(jnp.int32, (TK_A, LANES), 1) < HEAD_DIM_A

    heads = range(HEADS_PER_GROUP_A)

    def pair_block(cls, rows, section, h):
        c0 = section * gw + (h // 2) * LANES
        return of_class(qkv_ref, cls)[rows, c0:c0 + LANES]

    def other_lanes(block, low_mask, h, fill):
        mine = low_mask if h % 2 == 0 else jnp.logical_not(low_mask)
        return jnp.where(mine, block, jnp.full_like(block, fill))

    tiles_per_step = s_scr.shape[0]

    def tile_group(tg, carry):
        geo = []
        for u in range(tiles_per_step):
            item = tg * tiles_per_step + u
            cls, t = (0, item) if n_cls == 1 else (item // n_tiles, item % n_tiles)
            j0 = pl.multiple_of(t * TQ_A, TQ_A)
            ws = pl.multiple_of(jnp.clip(j0 - HALF_WINDOW, 0, L - TK_A), HALF_WINDOW)
            variant = jnp.where(t == 0, 0, jnp.where(t == n_tiles - 1, 2, 1))
            geo.append((cls, pl.ds(j0, TQ_A), pl.ds(ws, TK_A), variant))
        for u, (cls, rq, rk, variant) in enumerate(geo):
            for h in heads:
                q = other_lanes(pair_block(cls, rq, 0, h), low_q, h, 0)
                s = lax.dot_general(q, pair_block(cls, rk, 1, h), _NT, preferred_element_type=jnp.float32)
                s_scr[u, h] = s + bias_ref[variant, h]
        m = [[jnp.max(s_scr[u, h], axis=-1, keepdims=True) for h in heads] for u in range(len(geo))]
        for u in range(len(geo)):
            for h in heads:
                p_scr[u, h] = jnp.exp2(s_scr[u, h] - m[u][h]).astype(jnp.bfloat16)
        for u, (cls, rq, rk, variant) in enumerate(geo):
            lse_tile = jnp.zeros((TQ_A, LANES), jnp.float32)
            for pair in range(HEADS_PER_GROUP_A // 2):
                he, ho = 2 * pair, 2 * pair + 1
                acc_e = jnp.dot(p_scr[u, he], other_lanes(pair_block(cls, rk, 2, he), low_k, he, 1),
                                preferred_element_type=jnp.float32)
                acc_o = jnp.dot(p_scr[u, ho], other_lanes(pair_block(cls, rk, 2, ho), low_k, ho, 1),
                                preferred_element_type=jnp.float32)
                numer = jnp.where(low_q, acc_e, acc_o)
                den_other = jnp.where(low_q, acc_o, acc_e)
                den = pltpu.roll(den_other, HEAD_DIM_A, 1)
                of_class(o_ref, cls)[rq, pair * LANES:(pair + 1) * LANES] = (numer / den).astype(o_ref.dtype)
                is_odd = lse_lane == ho
                lse = (jnp.where(is_odd, m[u][ho], m[u][he])
                       + jnp.log2(jnp.where(is_odd, den_other, den))) * LN2
                lse_tile = jnp.where(jnp.logical_or(is_odd, lse_lane == he), lse, lse_tile)
            of_class(lse_ref, cls)[rq, :] = lse_tile
        return carry

    lax.fori_loop(0, n_cls * n_tiles // tiles_per_step, tile_group, 0)


def _attn_a(qkv, biasm):
    B, dil, L, n_qkv = qkv.shape
    gw = GROUP_WIDTH_A
    n_tiles = L // TQ_A
    n_cls = min(dil, max(1, MAX_TILES_PER_STEP_A // n_tiles))
    tiles_per_step = math.gcd(MAX_TILES_PER_STEP_A, n_cls * n_tiles)
    blk_cls = None if n_cls == 1 else n_cls
    cls = lambda b, r: (b, r, 0, 0)
    return pl.pallas_call(
        _attn_a_kernel,
        grid=(B, dil // n_cls),
        in_specs=[pl.BlockSpec((None, blk_cls, L, n_qkv), cls),
                  pl.BlockSpec(biasm.shape, lambda b, r: (0, 0, 0, 0))],
        out_specs=[pl.BlockSpec((None, blk_cls, L, gw), cls),
                   pl.BlockSpec((None, blk_cls, L, LANES), cls)],
        out_shape=[jax.ShapeDtypeStruct((B, dil, L, gw), jnp.bfloat16),
                   jax.ShapeDtypeStruct((B, dil, L, LANES), jnp.float32)],
        scratch_shapes=[
            pltpu.VMEM((tiles_per_step, HEADS_PER_GROUP_A, TQ_A, TK_A), jnp.float32),
            pltpu.VMEM((tiles_per_step, HEADS_PER_GROUP_A, TQ_A, TK_A), jnp.bfloat16)],
        compiler_params=_params("arbitrary", "arbitrary"),
        name=f"attn_a_d{dil}",
    )(qkv, biasm)


def _t5_bucket_np(rel):
    half = N_BUCKETS // 2
    max_exact = half // 2
    base = np.where(rel > 0, half, 0)
    n = np.abs(rel)
    nf = np.maximum(n, 1).astype(np.float32)
    large = max_exact + (np.log(nf / np.float32(max_exact)) / np.float32(math.log(T5_MAX_DISTANCE / max_exact))
                         * np.float32(half - max_exact)).astype(np.int32)
    large = np.minimum(large, half - 1)
    return base + np.where(n < max_exact, n, large)


def _bias_variants(rel_bias_g, dil):
    H = rel_bias_g.shape[1]
    M = 2 * TK_A
    off = np.arange(M) - (TK_A - 1)
    idx = np.where(np.abs(off) <= HALF_WINDOW, _t5_bucket_np(off * dil), N_BUCKETS).astype(np.int32)
    table = jnp.concatenate([rel_bias_g.astype(jnp.float32) * LOG2E, jnp.full((1, H), NEG_INF, jnp.float32)], axis=0)
    w = table[jnp.asarray(idx)].T
    rot = jnp.tile(w, (1, TQ_A + 1))[:, :TQ_A * (M + 1)].reshape(H, TQ_A, M + 1)[:, ::-1, :]
    variants = []
    for shift in (0, HALF_WINDOW, 2 * HALF_WINDOW):
        j0 = TK_A - TQ_A - shift
        variants.append(rot[:, :, j0:j0 + TK_A])
    return jnp.stack(variants)


def _split3_bf16(v):
    hi = v.astype(jnp.bfloat16)
    r1 = v - hi.astype(jnp.float32)
    mid = r1.astype(jnp.bfloat16)
    lo = (r1 - mid.astype(jnp.float32)).astype(jnp.bfloat16)
    return hi, mid, lo


N_MERGE_A_REFS = 15


def _merge_a(x_ref, mod_ref, o0_ref, o1_ref, o2_ref, l0_ref, l1_ref, l2_ref, gate_ref,
             unperm1_ref, unperm2_ref, expand_ref, w_ref, g_ref, b_ref):
    tl, D = x_ref.shape
    stride = LSE_COPY_STRIDE
    lane_p = lax.broadcasted_iota(jnp.int32, (PERM_ROWS, LANES), 1)
    lane = lax.broadcasted_iota(jnp.int32, (tl, LANES), 1)

    def class_major_rows(ref, sub):
        dil, rows = ref.shape[:2]
        rs = rows * PERM_ROWS // tl
        return jnp.concatenate([ref[r, sub * rs:(sub + 1) * rs] for r in range(dil)], axis=0)

    def natural_order(o_ref, l_ref, unperm_ref):
        if unperm_ref is None:
            return o_ref[0].astype(jnp.float32), l_ref[0]
        pt = unperm_ref[...]
        o_parts, l_parts = [], []
        for sub in range(tl // PERM_ROWS):
            o_parts.append(jnp.dot(pt, class_major_rows(o_ref, sub), preferred_element_type=jnp.float32))
            hi, mid, lo = _split3_bf16(class_major_rows(l_ref, sub))
            packed = jnp.where(lane_p < stride, hi, jnp.where(lane_p < 2 * stride, mid, lo))
            res = jnp.dot(pt, packed, preferred_element_type=jnp.float32)
            l_parts.append(res + pltpu.roll(res, LANES - stride, 1) + pltpu.roll(res, LANES - 2 * stride, 1))
        return jnp.concatenate(o_parts, axis=0), jnp.concatenate(l_parts, axis=0)

    o0, l0 = natural_order(o0_ref, l0_ref, None)
    o1, l1 = natural_order(o1_ref, l1_ref, unperm1_ref)
    o2, l2 = natural_order(o2_ref, l2_ref, unperm2_ref)
    mx = jnp.maximum(jnp.maximum(l0, l1), l2)
    e0, e1, e2 = jnp.exp(l0 - mx), jnp.exp(l1 - mx), jnp.exp(l2 - mx)
    inv = 1.0 / (e0 + e1 + e2)
    wt = jnp.where(lane < stride, e0 * inv,
                   jnp.where(lane < 2 * stride, pltpu.roll(e1 * inv, stride, 1), pltpu.roll(e2 * inv, 2 * stride, 1)))
    hi = wt.astype(jnp.bfloat16)
    lo = (wt - hi.astype(jnp.float32)).astype(jnp.bfloat16)
    wexp = (jnp.dot(hi, expand_ref[...], preferred_element_type=jnp.float32)
            + jnp.dot(lo, expand_ref[...], preferred_element_type=jnp.float32))
    o = jnp.concatenate([o0, o1, o2], axis=1)
    yg = (o * wexp * _silu(gate_ref[...].astype(jnp.float32))).astype(jnp.bfloat16)
    y = jnp.dot(yg, w_ref[...], preferred_element_type=jnp.float32)
    z = DEEPNORM_ALPHA * x_ref[...] + mod_ref[:, 2 * D:] * y
    return _layernorm(z, g_ref[...], b_ref[...])


def _rmsnorm(x, g):
    return x * lax.rsqrt(jnp.mean(x * x, axis=-1, keepdims=True) + RMS_EPS) * g


def _project_b(x, mod_ref, w_in_ref, qn_ref, kvn_ref, wqa_ref, wk_ref, wv_ref,
               vone_ref, cq_ref, sq_ref, ck_ref, sk_ref, q_out, k_out, v_out, gate_out):
    u = _modulate(x, mod_ref[...]).astype(jnp.bfloat16)
    z = jnp.dot(u, w_in_ref[...], preferred_element_type=jnp.float32)
    c0 = Q_LORA_RANK
    c1 = c0 + KV_LORA_RANK
    cq = z[:, :c0]
    ckv = z[:, c0:c1]
    kr = z[:, c1:c1 + LANES]
    gate_out[...] = z[:, c1 + LANES:].astype(gate_out.dtype)
    to_rope_lanes = LANES - QK_ROPE_DIM

    cqn = _rmsnorm(cq, qn_ref[...]).astype(jnp.bfloat16)
    qa = jnp.dot(cqn, wqa_ref[...], preferred_element_type=jnp.float32)
    ckvn = _rmsnorm(ckv, kvn_ref[...]).astype(jnp.bfloat16)
    kn = jnp.dot(ckvn, wk_ref[...], preferred_element_type=jnp.float32)
    v_out[...] = (jnp.dot(ckvn, wv_ref[...], preferred_element_type=jnp.float32)
                  + vone_ref[...]).astype(v_out.dtype)

    cos_q, sin_q = cq_ref[...], sq_ref[...]
    k_rope = kr * ck_ref[...] + pltpu.roll(kr, to_rope_lanes, 1) * sk_ref[...]
    for h in range(N_HEADS_B):
        blk = slice(h * LANES, (h + 1) * LANES)
        qh = qa[:, blk]
        q_out[:, blk] = (qh * cos_q + pltpu.roll(qh, to_rope_lanes, 1) * sin_q).astype(q_out.dtype)
        k_out[:, blk] = (kn[:, blk] + k_rope).astype(k_out.dtype)


def _mid_kernel(*refs):
    merge_refs, rest = refs[:N_MERGE_A_REFS], refs[N_MERGE_A_REFS:]
    x1_out, project_refs = rest[-5], rest[:-5] + rest[-4:]
    x1 = _merge_a(*merge_refs)
    x1_out[...] = x1
    _project_b(x1, *project_refs)


def _mid(x, mod0, os_, lses, gate_a, w_out_a, ln_g, ln_b,
         mod1, w_in, q_norm, kv_norm, wqa, wk, wv, tables):
    B, S, D = x.shape
    tl = TL_PROJ
    gw = GROUP_WIDTH_A
    HP = N_HEADS_B * LANES
    col = np.arange(WIDTH_A)
    src_lane = (col // gw) * LSE_COPY_STRIDE + (col % gw) // HEAD_DIM_A
    expand = jnp.asarray((np.arange(LANES)[:, None] == src_lane[None, :]).astype(np.float32), dtype=jnp.bfloat16)
    unperms = [_class_major_perm(PERM_ROWS, dil, inverse=True) for _, dil in DIL_GROUPS[1:]]
    row = lambda b, i: (b, i, 0)
    cls = lambda b, i: (b, 0, i, 0)
    const2 = lambda b, i: (0, 0)
    mod_spec = pl.BlockSpec((None, 1, 3 * D), lambda b, i: (b, 0, 0))
    tab = pl.BlockSpec((tl, LANES), lambda b, i: (i, 0))
    o_specs = [pl.BlockSpec((None, dil, tl // dil, gw), cls) for _, dil in DIL_GROUPS]
    l_specs = [pl.BlockSpec((None, dil, tl // dil, LANES), cls) for _, dil in DIL_GROUPS]
    merge_args = [x, mod0, *os_, *lses, gate_a, *unperms, expand, w_out_a, ln_g.reshape(1, D), ln_b.reshape(1, D)]
    merge_specs = [pl.BlockSpec((None, tl, D), row), mod_spec, *o_specs, *l_specs,
                   pl.BlockSpec((None, tl, WIDTH_A), row),
                   *[pl.BlockSpec(p.shape, const2) for p in unperms],
                   pl.BlockSpec(expand.shape, const2), pl.BlockSpec(w_out_a.shape, const2),
                   pl.BlockSpec((1, D), const2), pl.BlockSpec((1, D), const2)]
    assert len(merge_args) == N_MERGE_A_REFS
    project_args = [mod1, w_in, q_norm.reshape(1, -1), kv_norm.reshape(1, -1), wqa, wk, wv, _v_ones_row(), *tables]
    project_specs = [mod_spec, pl.BlockSpec(w_in.shape, const2),
                     pl.BlockSpec((1, Q_LORA_RANK), const2), pl.BlockSpec((1, KV_LORA_RANK), const2),
                     pl.BlockSpec(wqa.shape, const2), pl.BlockSpec(wk.shape, const2),
                     pl.BlockSpec(wv.shape, const2), pl.BlockSpec((1, HP), const2), tab, tab, tab, tab]
    wide = jax.ShapeDtypeStruct((B, S, HP), jnp.bfloat16)
    return pl.pallas_call(
        _mid_kernel,
        grid=(B, S // tl),
        in_specs=merge_specs + project_specs,
        out_specs=[pl.BlockSpec((None, tl, D), row),
                   pl.BlockSpec((None, tl, HP), row), pl.BlockSpec((None, tl, HP), row),
                   pl.BlockSpec((None, tl, HP), row), pl.BlockSpec((None, tl, WIDTH_B), row)],
        out_shape=[jax.ShapeDtypeStruct((B, S, D), jnp.float32), wide, wide, wide,
                   jax.ShapeDtypeStruct((B, S, WIDTH_B), jnp.bfloat16)],
        compiler_params=_params("arbitrary", "arbitrary"),
        name="mid",
    )(*merge_args, *project_args)


def _attn_b_kernel(q_ref, k_ref, v_ref, gate_ref, o_ref, s_even, s_odd, mx_scr):
    S = k_ref.shape[0]
    tq = s_even.shape[0]
    n_tiles = S // tq
    n_chunks = S // TK_B
    n_heads = q_ref.shape[1] // LANES
    assert n_heads % 2 == 0
    s_bufs = (s_even, s_odd)

    def head_lanes(e):
        return slice(e * LANES, (e + 1) * LANES)

    def tile_rows(t):
        return pl.ds(t * tq, tq) if isinstance(t, int) else pl.ds(pl.multiple_of(t * tq, tq), tq)

    def scores(t, e, c, mx):
        ks = slice(c * TK_B, (c + 1) * TK_B)
        s = lax.dot_general(q_ref[tile_rows(t), head_lanes(e)], k_ref[ks, head_lanes(e)], _NT,
                            preferred_element_type=jnp.float32)
        s_bufs[e % 2][:, ks] = s
        for j in range(TK_B // LANES):
            mx = jnp.maximum(mx, s[:, j * LANES:(j + 1) * LANES])
        return mx

    def weighted_values(e, c, m, acc):
        ks = slice(c * TK_B, (c + 1) * TK_B)
        p = jnp.exp2(s_bufs[e % 2][:, ks] - m).astype(jnp.bfloat16)
        return acc + jnp.dot(p, v_ref[ks, head_lanes(e)], preferred_element_type=jnp.float32)

    neg = jnp.full((tq, LANES), -jnp.inf, jnp.float32)
    zero = jnp.zeros((tq, LANES), jnp.float32)
    lane = lax.broadcasted_iota(jnp.int32, (tq, LANES), 1)

    mx = neg
    for c in range(n_chunks):
        mx = scores(0, 0, c, mx)
    mx_scr[...] = mx

    def tile(t, carry):
        t_next = jnp.minimum(t + 1, n_tiles - 1)
        mx = mx_scr[...]
        acc = []
        for e in range(n_heads):
            m = jnp.max(mx, axis=-1, keepdims=True)
            mx, a = neg, zero
            for c in range(n_chunks):
                mx = scores(t, e + 1, c, mx) if e + 1 < n_heads else scores(t_next, 0, c, mx)
                a = weighted_values(e, c, m, a)
            acc.append(a)
        mx_scr[...] = mx
        for pair in range(n_heads // 2):
            acc_e, acc_o = acc[2 * pair], acc[2 * pair + 1]
            inv_e = 1.0 / acc_e[:, V_HEAD_DIM:V_HEAD_DIM + 1]
            inv_o = 1.0 / acc_o[:, 0:1]
            cols = slice(pair * LANES, (pair + 1) * LANES)
            o = jnp.where(lane < V_HEAD_DIM, acc_e * inv_e, acc_o * inv_o)
            gate = gate_ref[tile_rows(t), cols].astype(jnp.float32)
            o_ref[tile_rows(t), cols] = (o * _silu(gate)).astype(o_ref.dtype)
        return carry

    lax.fori_loop(0, n_tiles, tile, 0)


def _attn_b(q2, k2, v2, gate):
    B, S, HP = q2.shape
    hw = HEADS_PER_STEP_B * LANES
    ow = HEADS_PER_STEP_B * V_HEAD_DIM
    heads = lambda b, p: (b, 0, p)
    return pl.pallas_call(
        _attn_b_kernel,
        grid=(B, N_HEADS_B // HEADS_PER_STEP_B),
        in_specs=[pl.BlockSpec((None, S, hw), heads), pl.BlockSpec((None, S, hw), heads),
                  pl.BlockSpec((None, S, hw), heads), pl.BlockSpec((None, S, ow), heads)],
        out_specs=pl.BlockSpec((None, S, ow), heads),
        out_shape=jax.ShapeDtypeStruct((B, S, WIDTH_B), jnp.bfloat16),
        scratch_shapes=[pltpu.VMEM((TQ_B, S), jnp.float32), pltpu.VMEM((TQ_B, S), jnp.float32),
                        pltpu.VMEM((TQ_B, LANES), jnp.float32)],
        compiler_params=_params("arbitrary", "arbitrary"),
        name="attn_b",
    )(q2, k2, v2, gate)


def _out_b_kernel(x_ref, mod_ref, yg_ref, w_ref, g_ref, b_ref, out_ref):
    D = x_ref.shape[-1]
    y = jnp.dot(yg_ref[...], w_ref[...], preferred_element_type=jnp.float32)
    z = DEEPNORM_ALPHA * x_ref[...] + mod_ref[:, 2 * D:] * y
    out_ref[...] = _layernorm(z, g_ref[...], b_ref[...])


def _out_b(x, mod, yg, w_out, ln_g, ln_b):
    B, S, D = x.shape
    tl = TL_WIDE
    row = lambda b, i: (b, i, 0)
    const2 = lambda b, i: (0, 0)
    return pl.pallas_call(
        _out_b_kernel,
        grid=(B, S // tl),
        in_specs=[pl.BlockSpec((None, tl, D), row),
                  pl.BlockSpec((None, 1, 3 * D), lambda b, i: (b, 0, 0)),
                  pl.BlockSpec((None, tl, WIDTH_B), row),
                  pl.BlockSpec(w_out.shape, const2),
                  pl.BlockSpec((1, D), const2), pl.BlockSpec((1, D), const2)],
        out_specs=pl.BlockSpec((None, tl, D), row),
        out_shape=jax.ShapeDtypeStruct((B, S, D), jnp.float32),
        compiler_params=_params("arbitrary", "arbitrary"),
        name="out_b",
    )(x, mod, yg, w_out, ln_g.reshape(1, D), ln_b.reshape(1, D))


def _weights_a(a_w_in):
    W = WIDTH_A
    gw = GROUP_WIDTH_A
    q, k, v, gate = (a_w_in[:, i * W:(i + 1) * W] for i in range(4))
    ws = []
    for g in range(len(DIL_GROUPS)):
        cs = slice(g * gw, (g + 1) * gw)
        parts = [q[:, cs] * (HEAD_DIM_A ** -0.5 * LOG2E), k[:, cs], v[:, cs]]
        if g == 0:
            parts.append(gate)
        ws.append(jnp.concatenate(parts, axis=1).astype(jnp.bfloat16))
    return ws


def _swap_halves(w):
    half = w.shape[-1] // 2
    return jnp.concatenate([w[..., half:], w[..., :half]], axis=-1)


def _pad_lanes(w, lo):
    n = w.shape[-1]
    return jnp.pad(w, [(0, 0)] * (w.ndim - 1) + [(lo, LANES - lo - n)])


def _weights_b(b_w_in, b_w_uq, b_w_ukv):
    c0 = Q_LORA_RANK
    c1 = c0 + KV_LORA_RANK
    c2 = c1 + QK_ROPE_DIM
    kr = b_w_in[:, c1:c2]
    w_in = jnp.concatenate([b_w_in[:, :c1], _pad_lanes(jnp.concatenate([kr, _swap_halves(kr)], axis=1), QK_NOPE_DIM),
                            b_w_in[:, c2:]], axis=1)
    uq = b_w_uq.reshape(Q_LORA_RANK, N_HEADS_B, QK_NOPE_DIM + QK_ROPE_DIM)
    wqa = jnp.concatenate([uq, _swap_halves(uq[..., QK_NOPE_DIM:])], axis=-1).reshape(Q_LORA_RANK, -1)
    ukv = b_w_ukv.reshape(KV_LORA_RANK, N_HEADS_B, QK_NOPE_DIM + V_HEAD_DIM)
    wk = _pad_lanes(ukv[..., :QK_NOPE_DIM], 0).reshape(KV_LORA_RANK, -1)
    vv = ukv[..., QK_NOPE_DIM:]
    odd = (jnp.arange(N_HEADS_B) % 2 == 1)[None, :, None]
    wv = jnp.where(odd, _pad_lanes(vv, LANES - V_HEAD_DIM), _pad_lanes(vv, 0)).reshape(KV_LORA_RANK, -1)
    bf = lambda w: w.astype(jnp.bfloat16)
    return bf(w_in), bf(wqa), bf(wk), bf(wv)


def _rope_tables(S):
    pos = jnp.arange(S, dtype=jnp.float32)
    inv_freq = ROPE_THETA ** (-jnp.arange(0, QK_ROPE_DIM, 2, dtype=jnp.float32) / QK_ROPE_DIM)
    ang = pos[:, None] * inv_freq[None, :]
    cos, sin = jnp.cos(ang), jnp.sin(ang)
    cosf = _pad_lanes(jnp.concatenate([cos, cos], axis=-1), QK_NOPE_DIM)
    sinf = _pad_lanes(jnp.concatenate([-sin, sin], axis=-1), QK_NOPE_DIM)
    nope = _pad_lanes(jnp.ones((S, QK_NOPE_DIM), jnp.float32), 0)
    scale = (QK_NOPE_DIM + QK_ROPE_DIM) ** -0.5 * LOG2E
    return (scale * (nope + cosf), scale * sinf, cosf, sinf)


def _v_ones_row():
    row = np.zeros((N_HEADS_B, LANES), np.float32)
    row[0::2, V_HEAD_DIM] = 1.0
    row[1::2, 0] = 1.0
    return jnp.asarray(row.reshape(1, -1))


def kernel(x, c, rel_bias, ada_w, ada_b, ln_g, ln_b, a_w_in, a_w_out,
           b_w_in, b_q_norm, b_w_uq, b_kv_norm, b_w_ukv, b_w_out):
    B, S, D = x.shape
    mods = _modulation(c, ada_w, ada_b)
    mod0 = mods[0].reshape(B, 1, 3 * D)
    mod1 = mods[1].reshape(B, 1, 3 * D)

    for window, dil in DIL_GROUPS:
        assert window // (2 * dil) == HALF_WINDOW
    *qkvs, gate_a = _proj_a(x, mod0, _weights_a(a_w_in[0]))
    os_, lses = [], []
    for g, ((_, dil), qkv) in enumerate(zip(DIL_GROUPS, qkvs)):
        hs = slice(g * HEADS_PER_GROUP_A, (g + 1) * HEADS_PER_GROUP_A)
        o, lse = _attn_a(qkv, _bias_variants(rel_bias[:, hs], dil))
        os_.append(o)
        lses.append(lse)

    w_in, wqa, wk, wv = _weights_b(b_w_in[0], b_w_uq[0], b_w_ukv[0])
    x1, q2, k2, v2, gate_b = _mid(x, mod0, os_, lses, gate_a, a_w_out[0].astype(jnp.bfloat16), ln_g[0], ln_b[0],
                                  mod1, w_in, b_q_norm[0], b_kv_norm[0], wqa, wk, wv, _rope_tables(S))
    yg = _attn_b(q2, k2, v2, gate_b)
    return _out_b(x1, mod1, yg, b_w_out[0].astype(jnp.bfloat16), ln_g[1], ln_b[1])
```

```python
import math

import numpy as np
import jax
import jax.numpy as jnp
from jax import lax
from jax.experimental import pallas as pl
from jax.experimental.pallas import tpu as pltpu

DEPTH = 2
HEAD_DIM_A = 64
DIL_GROUPS = ((128, 1), (512, 4), (2048, 16))
HEADS_PER_GROUP_A = 6
GROUP_WIDTH_A = HEADS_PER_GROUP_A * HEAD_DIM_A
WIDTH_A = GROUP_WIDTH_A * len(DIL_GROUPS)
N_BUCKETS = 32
T5_MAX_DISTANCE = 1024
N_HEADS_B = 16
QK_NOPE_DIM = 64
QK_ROPE_DIM = 32
V_HEAD_DIM = 64
Q_LORA_RANK = 256
KV_LORA_RANK = 128
WIDTH_B = N_HEADS_B * V_HEAD_DIM
ROPE_THETA = 10000.0
DEEPNORM_ALPHA = (2.0 * DEPTH) ** 0.25
LN_EPS = 1e-5
RMS_EPS = 1e-6
NEG_INF = -1e30
LOG2E = math.log2(math.e)
LN2 = math.log(2.0)

LANES = 128
VMEM_LIMIT_BYTES = 56 * 1024 * 1024

HALF_WINDOW = 64
TQ_A = 128
TK_A = TQ_A + 2 * HALF_WINDOW
LSE_COPY_STRIDE = 8
LSE_COPIES = 3
MAX_TILES_PER_STEP_A = 4
TL_PROJ = 512
TL_WIDE = 1024
PERM_ROWS = 256
TQ_B = 512
TK_B = 512
HEADS_PER_STEP_B = 4

_NT = (((1,), (1,)), ((), ()))


def _silu(x):
    return x * (1.0 / (1.0 + jnp.exp(-x)))


def _params(*sem):
    return pltpu.CompilerParams(dimension_semantics=sem, vmem_limit_bytes=VMEM_LIMIT_BYTES)


def _mod_kernel(c_ref, w_ref, b_ref, o_ref):
    sc = _silu(c_ref[...]).astype(jnp.bfloat16)
    o_ref[...] = jnp.dot(sc, w_ref[...].astype(jnp.bfloat16),
                         preferred_element_type=jnp.float32) + b_ref[...]


def _modulation(c, ada_w, ada_b):
    B, D = c.shape
    nj = 3
    return pl.pallas_call(
        _mod_kernel,
        grid=(DEPTH, nj),
        in_specs=[pl.BlockSpec((B, D), lambda i, j: (0, 0)),
                  pl.BlockSpec((None, D, D), lambda i, j: (i, 0, j)),
                  pl.BlockSpec((None, 1, D), lambda i, j: (i, 0, j))],
        out_specs=pl.BlockSpec((None, B, D), lambda i, j: (i, 0, j)),
        out_shape=jax.ShapeDtypeStruct((DEPTH, B, 3 * D), jnp.float32),
        compiler_params=_params("arbitrary", "arbitrary"),
        name="modulation",
    )(c, ada_w, ada_b.reshape(DEPTH, 1, 3 * D))


def _modulate(x, mod):
    D = x.shape[-1]
    return x * (1.0 + mod[:, D:2 * D]) + mod[:, :D]


def _layernorm(z, g, b):
    mu = jnp.mean(z, axis=-1, keepdims=True)
    zc = z - mu
    var = jnp.mean(zc * zc, axis=-1, keepdims=True)
    return zc * lax.rsqrt(var + LN_EPS) * g + b


def _class_major_perm(tl, dil, inverse=False):
    p = np.arange(tl)
    src = (p % (tl // dil)) * dil + p // (tl // dil)
    mat = (src[:, None] == np.arange(tl)[None, :]).astype(np.float32)
    return jnp.asarray(mat.T if inverse else mat, dtype=jnp.bfloat16)


def _proj_a_kernel(x_ref, mod_ref, w0_ref, w1_ref, w2_ref, perm1_ref, perm2_ref,
                   qkv0_ref, qkv1_ref, qkv2_ref, gate_ref):
    tl = x_ref.shape[0]
    n_sub = tl // PERM_ROWS
    u = _modulate(x_ref[...], mod_ref[...]).astype(jnp.bfloat16)
    n_qkv = qkv0_ref.shape[-1]
    z = jnp.dot(u, w0_ref[...], preferred_element_type=jnp.float32)
    qkv0_ref[0] = z[:, :n_qkv].astype(qkv0_ref.dtype)
    gate_ref[...] = z[:, n_qkv:].astype(gate_ref.dtype)
    for perm_ref, w_ref, qkv_ref in ((perm1_ref, w1_ref, qkv1_ref), (perm2_ref, w2_ref, qkv2_ref)):
        pm = perm_ref[...]
        up = jnp.concatenate(
            [jnp.dot(pm, u[sub * PERM_ROWS:(sub + 1) * PERM_ROWS], preferred_element_type=jnp.float32)
             for sub in range(n_sub)], axis=0).astype(jnp.bfloat16)
        z = jnp.dot(up, w_ref[...], preferred_element_type=jnp.float32)
        dil, rows = qkv_ref.shape[:2]
        rs = rows // n_sub
        for sub in range(n_sub):
            for r in range(dil):
                r0 = sub * PERM_ROWS + r * rs
                qkv_ref[r, sub * rs:(sub + 1) * rs] = z[r0:r0 + rs].astype(qkv_ref.dtype)


def _proj_a(x, mod, ws):
    B, S, D = x.shape
    tl = TL_WIDE
    n_qkv = 3 * GROUP_WIDTH_A
    dils = [dil for _, dil in DIL_GROUPS]
    assert dils[0] == 1
    row = lambda b, i: (b, i, 0)
    cls = lambda b, i: (b, 0, i, 0)
    const2 = lambda b, i: (0, 0)
    perms = [_class_major_perm(PERM_ROWS, dil) for dil in dils[1:]]
    return pl.pallas_call(
        _proj_a_kernel,
        grid=(B, S // tl),
        in_specs=[pl.BlockSpec((None, tl, D), row),
                  pl.BlockSpec((None, 1, 3 * D), lambda b, i: (b, 0, 0)),
                  *[pl.BlockSpec(w.shape, const2) for w in ws],
                  *[pl.BlockSpec(p.shape, const2) for p in perms]],
        out_specs=[*[pl.BlockSpec((None, dil, tl // dil, n_qkv), cls) for dil in dils],
                   pl.BlockSpec((None, tl, WIDTH_A), row)],
        out_shape=[*[jax.ShapeDtypeStruct((B, dil, S // dil, n_qkv), jnp.bfloat16) for dil in dils],
                   jax.ShapeDtypeStruct((B, S, WIDTH_A), jnp.bfloat16)],
        compiler_params=_params("arbitrary", "arbitrary"),
        name="proj_a",
    )(x, mod, *ws, *perms)


def _attn_a_kernel(qkv_ref, bias_ref, o_ref, lse_ref, s_scr, p_scr):
    n_cls, L = (1,) + qkv_ref.shape[:1] if len(qkv_ref.shape) == 2 else qkv_ref.shape[:2]

    def of_class(ref, cls):
        return ref if len(ref.shape) == 2 else ref.at[cls]
    n_tiles = L // TQ_A
    gw = GROUP_WIDTH_A
    lane = lax.broadcasted_iota(jnp.int32, (TQ_A, LANES), 1)
    low_q = lane < HEAD_DIM_A
    lse_lane = jnp.where(lane < LSE_COPIES * LSE_COPY_STRIDE, lane % LSE_COPY_STRIDE, -1)
    low_k = lax.broadcasted_iota(jnp.int32, (TK_A, LANES), 1) < HEAD_DIM_A

    heads = range(HEADS_PER_GROUP_A)

    def pair_block(cls, rows, section, h):
        c0 = section * gw + (h // 2) * LANES
        return of_class(qkv_ref, cls)[rows, c0:c0 + LANES]

    def other_lanes(block, low_mask, h, fill):
        mine = low_mask if h % 2 == 0 else jnp.logical_not(low_mask)
        return jnp.where(mine, block, jnp.full_like(block, fill))

    tiles_per_step = s_scr.shape[0]

    def tile_group(tg, carry):
        geo = []
        for u in range(tiles_per_step):
            item = tg * tiles_per_step + u
            cls, t = (0, item) if n_cls == 1 else (item // n_tiles, item % n_tiles)
            j0 = pl.multiple_of(t * TQ_A, TQ_A)
            ws = pl.multiple_of(jnp.clip(j0 - HALF_WINDOW, 0, L - TK_A), HALF_WINDOW)
            variant = jnp.where(t == 0, 0, jnp.where(t == n_tiles - 1, 2, 1))
            geo.append((cls, pl.ds(j0, TQ_A), pl.ds(ws, TK_A), variant))
        for u, (cls, rq, rk, variant) in enumerate(geo):
            for h in heads:
                q = other_lanes(pair_block(cls, rq, 0, h), low_q, h, 0)
                s = lax.dot_general(q, pair_block(cls, rk, 1, h), _NT, preferred_element_type=jnp.float32)
                s_scr[u, h] = s + bias_ref[variant, h]
        m = [[jnp.max(s_scr[u, h], axis=-1, keepdims=True) for h in heads] for u in range(len(geo))]
        for u in range(len(geo)):
            for h in heads:
                p_scr[u, h] = jnp.exp2(s_scr[u, h] - m[u][h]).astype(jnp.bfloat16)
        for u, (cls, rq, rk, variant) in enumerate(geo):
            lse_tile = jnp.zeros((TQ_A, LANES), jnp.float32)
            for pair in range(HEADS_PER_GROUP_A // 2):
                he, ho = 2 * pair, 2 * pair + 1
                acc_e = jnp.dot(p_scr[u, he], other_lanes(pair_block(cls, rk, 2, he), low_k, he, 1),
                                preferred_element_type=jnp.float32)
                acc_o = jnp.dot(p_scr[u, ho], other_lanes(pair_block(cls, rk, 2, ho), low_k, ho, 1),
                                preferred_element_type=jnp.float32)
                numer = jnp.where(low_q, acc_e, acc_o)
                den_other = jnp.where(low_q, acc_o, acc_e)
                den = pltpu.roll(den_other, HEAD_DIM_A, 1)
                of_class(o_ref, cls)[rq, pair * LANES:(pair + 1) * LANES] = (numer / den).astype(o_ref.dtype)
                is_odd = lse_lane == ho
                lse = (jnp.where(is_odd, m[u][ho], m[u][he])
                       + jnp.log2(jnp.where(is_odd, den_other, den))) * LN2
                lse_tile = jnp.where(jnp.logical_or(is_odd, lse_lane == he), lse, lse_tile)
            of_class(lse_ref, cls)[rq, :] = lse_tile
        return carry

    lax.fori_loop(0, n_cls * n_tiles // tiles_per_step, tile_group, 0)


def _attn_a(qkv, biasm, group):
    B, dil, L, n_qkv = qkv.shape
    gw = GROUP_WIDTH_A
    n_tiles = L // TQ_A
    n_cls = min(dil, max(1, MAX_TILES_PER_STEP_A // n_tiles))
    tiles_per_step = math.gcd(MAX_TILES_PER_STEP_A, n_cls * n_tiles)
    blk_cls = None if n_cls == 1 else n_cls
    cls = lambda b, r: (b, r, 0, 0)
    return pl.pallas_call(
        _attn_a_kernel,
        grid=(B, dil // n_cls),
        in_specs=[pl.BlockSpec((None, blk_cls, L, n_qkv), cls),
                  pl.BlockSpec((biasm.shape[0], HEADS_PER_GROUP_A) + biasm.shape[2:],
                               lambda b, r: (0, group, 0, 0))],
        out_specs=[pl.BlockSpec((None, blk_cls, L, gw), cls),
                   pl.BlockSpec((None, blk_cls, L, LANES), cls)],
        out_shape=[jax.ShapeDtypeStruct((B, dil, L, gw), jnp.bfloat16),
                   jax.ShapeDtypeStruct((B, dil, L, LANES), jnp.float32)],
        scratch_shapes=[
            pltpu.VMEM((tiles_per_step, HEADS_PER_GROUP_A, TQ_A, TK_A), jnp.float32),
            pltpu.VMEM((tiles_per_step, HEADS_PER_GROUP_A, TQ_A, TK_A), jnp.bfloat16)],
        compiler_params=_params("arbitrary", "arbitrary"),
        name=f"attn_a_d{dil}",
    )(qkv, biasm)


def _t5_bucket_np(rel):
    half = N_BUCKETS // 2
    max_exact = half // 2
    base = np.where(rel > 0, half, 0)
    n = np.abs(rel)
    nf = np.maximum(n, 1).astype(np.float32)
    large = max_exact + (np.log(nf / np.float32(max_exact)) / np.float32(math.log(T5_MAX_DISTANCE / max_exact))
                         * np.float32(half - max_exact)).astype(np.int32)
    large = np.minimum(large, half - 1)
    return base + np.where(n < max_exact, n, large)


def _bias_variants(rel_bias):
    H = rel_bias.shape[1]
    M = 2 * TK_A
    off = np.arange(M) - (TK_A - 1)
    idx = np.stack([np.where(np.abs(off) <= HALF_WINDOW, _t5_bucket_np(off * dil), N_BUCKETS)
                    for _, dil in DIL_GROUPS]).astype(np.int32)
    idx_per_head = jnp.asarray(np.repeat(idx, HEADS_PER_GROUP_A, axis=0))
    table = jnp.concatenate([rel_bias.astype(jnp.float32) * LOG2E, jnp.full((1, H), NEG_INF, jnp.float32)], axis=0)
    w = jnp.take_along_axis(table.T, idx_per_head, axis=1)
    rot = jnp.tile(w, (1, TQ_A + 1))[:, :TQ_A * (M + 1)].reshape(H, TQ_A, M + 1)[:, ::-1, :]
    variants = []
    for shift in (0, HALF_WINDOW, 2 * HALF_WINDOW):
        j0 = TK_A - TQ_A - shift
        variants.append(rot[:, :, j0:j0 + TK_A])
    return jnp.stack(variants)


def _split3_bf16(v):
    hi = v.astype(jnp.bfloat16)
    r1 = v - hi.astype(jnp.float32)
    mid = r1.astype(jnp.bfloat16)
    lo = (r1 - mid.astype(jnp.float32)).astype(jnp.bfloat16)
    return hi, mid, lo


N_MERGE_A_REFS = 15


def _merge_a(x_ref, mod_ref, o0_ref, o1_ref, o2_ref, l0_ref, l1_ref, l2_ref, gate_ref,
             unperm1_ref, unperm2_ref, expand_ref, w_ref, g_ref, b_ref):
    tl, D = x_ref.shape
    stride = LSE_COPY_STRIDE
    lane_p = lax.broadcasted_iota(jnp.int32, (PERM_ROWS, LANES), 1)
    lane = lax.broadcasted_iota(jnp.int32, (tl, LANES), 1)

    def class_major_rows(ref, sub):
        dil, rows = ref.shape[:2]
        rs = rows * PERM_ROWS // tl
        return jnp.concatenate([ref[r, sub * rs:(sub + 1) * rs] for r in range(dil)], axis=0)

    def natural_order(o_ref, l_ref, unperm_ref):
        if unperm_ref is None:
            return o_ref[0].astype(jnp.float32), l_ref[0]
        pt = unperm_ref[...]
        o_parts, l_parts = [], []
        for sub in range(tl // PERM_ROWS):
            o_parts.append(jnp.dot(pt, class_major_rows(o_ref, sub), preferred_element_type=jnp.float32))
            hi, mid, lo = _split3_bf16(class_major_rows(l_ref, sub))
            packed = jnp.where(lane_p < stride, hi, jnp.where(lane_p < 2 * stride, mid, lo))
            res = jnp.dot(pt, packed, preferred_element_type=jnp.float32)
            l_parts.append(res + pltpu.roll(res, LANES - stride, 1) + pltpu.roll(res, LANES - 2 * stride, 1))
        return jnp.concatenate(o_parts, axis=0), jnp.concatenate(l_parts, axis=0)

    o0, l0 = natural_order(o0_ref, l0_ref, None)
    o1, l1 = natural_order(o1_ref, l1_ref, unperm1_ref)
    o2, l2 = natural_order(o2_ref, l2_ref, unperm2_ref)
    mx = jnp.maximum(jnp.maximum(l0, l1), l2)
    e0, e1, e2 = jnp.exp(l0 - mx), jnp.exp(l1 - mx), jnp.exp(l2 - mx)
    inv = 1.0 / (e0 + e1 + e2)
    wt = jnp.where(lane < stride, e0 * inv,
                   jnp.where(lane < 2 * stride, pltpu.roll(e1 * inv, stride, 1), pltpu.roll(e2 * inv, 2 * stride, 1)))
    hi = wt.astype(jnp.bfloat16)
    lo = (wt - hi.astype(jnp.float32)).astype(jnp.bfloat16)
    wexp = (jnp.dot(hi, expand_ref[...], preferred_element_type=jnp.float32)
            + jnp.dot(lo, expand_ref[...], preferred_element_type=jnp.float32))
    o = jnp.concatenate([o0, o1, o2], axis=1)
    yg = (o * wexp * _silu(gate_ref[...].astype(jnp.float32))).astype(jnp.bfloat16)
    y = jnp.dot(yg, w_ref[...], preferred_element_type=jnp.float32)
    z = DEEPNORM_ALPHA * x_ref[...] + mod_ref[:, 2 * D:] * y
    return _layernorm(z, g_ref[...], b_ref[...])


def _rmsnorm(x, g):
    return x * lax.rsqrt(jnp.mean(x * x, axis=-1, keepdims=True) + RMS_EPS) * g


def _project_b(x, mod_ref, w_in_ref, qn_ref, kvn_ref, wqa_ref, wk_ref, wv_ref,
               vone_ref, cq_ref, sq_ref, ck_ref, sk_ref, q_out, k_out, v_out, gate_out):
    u = _modulate(x, mod_ref[...]).astype(jnp.bfloat16)
    z = jnp.dot(u, w_in_ref[...], preferred_element_type=jnp.float32)
    c0 = Q_LORA_RANK
    c1 = c0 + KV_LORA_RANK
    cq = z[:, :c0]
    ckv = z[:, c0:c1]
    kr = z[:, c1:c1 + LANES]
    gate_out[...] = z[:, c1 + LANES:].astype(gate_out.dtype)
    to_rope_lanes = LANES - QK_ROPE_DIM

    cqn = _rmsnorm(cq, qn_ref[...]).astype(jnp.bfloat16)
    qa = jnp.dot(cqn, wqa_ref[...], preferred_element_type=jnp.float32)
    ckvn = _rmsnorm(ckv, kvn_ref[...]).astype(jnp.bfloat16)
    kn = jnp.dot(ckvn, wk_ref[...], preferred_element_type=jnp.float32)
    v_out[...] = (jnp.dot(ckvn, wv_ref[...], preferred_element_type=jnp.float32)
                  + vone_ref[...]).astype(v_out.dtype)

    cos_q, sin_q = cq_ref[...], sq_ref[...]
    k_rope = kr * ck_ref[...] + pltpu.roll(kr, to_rope_lanes, 1) * sk_ref[...]
    for h in range(N_HEADS_B):
        blk = slice(h * LANES, (h + 1) * LANES)
        qh = qa[:, blk]
        q_out[:, blk] = (qh * cos_q + pltpu.roll(qh, to_rope_lanes, 1) * sin_q).astype(q_out.dtype)
        k_out[:, blk] = (kn[:, blk] + k_rope).astype(k_out.dtype)


def _mid_kernel(*refs):
    merge_refs, rest = refs[:N_MERGE_A_REFS], refs[N_MERGE_A_REFS:]
    x1_out, project_refs = rest[-5], rest[:-5] + rest[-4:]
    x1 = _merge_a(*merge_refs)
    x1_out[...] = x1
    _project_b(x1, *project_refs)


def _mid(x, mod0, os_, lses, gate_a, w_out_a, ln_g, ln_b,
         mod1, w_in, q_norm, kv_norm, wqa, wk, wv, tables):
    B, S, D = x.shape
    tl = TL_PROJ
    gw = GROUP_WIDTH_A
    HP = N_HEADS_B * LANES
    col = np.arange(WIDTH_A)
    src_lane = (col // gw) * LSE_COPY_STRIDE + (col % gw) // HEAD_DIM_A
    expand = jnp.asarray((np.arange(LANES)[:, None] == src_lane[None, :]).astype(np.float32), dtype=jnp.bfloat16)
    unperms = [_class_major_perm(PERM_ROWS, dil, inverse=True) for _, dil in DIL_GROUPS[1:]]
    row = lambda b, i: (b, i, 0)
    cls = lambda b, i: (b, 0, i, 0)
    const2 = lambda b, i: (0, 0)
    mod_spec = pl.BlockSpec((None, 1, 3 * D), lambda b, i: (b, 0, 0))
    tab = pl.BlockSpec((tl, LANES), lambda b, i: (i, 0))
    o_specs = [pl.BlockSpec((None, dil, tl // dil, gw), cls) for _, dil in DIL_GROUPS]
    l_specs = [pl.BlockSpec((None, dil, tl // dil, LANES), cls) for _, dil in DIL_GROUPS]
    merge_args = [x, mod0, *os_, *lses, gate_a, *unperms, expand, w_out_a, ln_g.reshape(1, D), ln_b.reshape(1, D)]
    merge_specs = [pl.BlockSpec((None, tl, D), row), mod_spec, *o_specs, *l_specs,
                   pl.BlockSpec((None, tl, WIDTH_A), row),
                   *[pl.BlockSpec(p.shape, const2) for p in unperms],
                   pl.BlockSpec(expand.shape, const2), pl.BlockSpec(w_out_a.shape, const2),
                   pl.BlockSpec((1, D), const2), pl.BlockSpec((1, D), const2)]
    assert len(merge_args) == N_MERGE_A_REFS
    project_args = [mod1, w_in, q_norm.reshape(1, -1), kv_norm.reshape(1, -1), wqa, wk, wv, _v_ones_row(), *tables]
    project_specs = [mod_spec, pl.BlockSpec(w_in.shape, const2),
                     pl.BlockSpec((1, Q_LORA_RANK), const2), pl.BlockSpec((1, KV_LORA_RANK), const2),
                     pl.BlockSpec(wqa.shape, const2), pl.BlockSpec(wk.shape, const2),
                     pl.BlockSpec(wv.shape, const2), pl.BlockSpec((1, HP), const2), tab, tab, tab, tab]
    wide = jax.ShapeDtypeStruct((B, S, HP), jnp.bfloat16)
    return pl.pallas_call(
        _mid_kernel,
        grid=(B, S // tl),
        in_specs=merge_specs + project_specs,
        out_specs=[pl.BlockSpec((None, tl, D), row),
                   pl.BlockSpec((None, tl, HP), row), pl.BlockSpec((None, tl, HP), row),
                   pl.BlockSpec((None, tl, HP), row), pl.BlockSpec((None, tl, WIDTH_B), row)],
        out_shape=[jax.ShapeDtypeStruct((B, S, D), jnp.float32), wide, wide, wide,
                   jax.ShapeDtypeStruct((B, S, WIDTH_B), jnp.bfloat16)],
        compiler_params=_params("arbitrary", "arbitrary"),
        name="mid",
    )(*merge_args, *project_args)


def _attn_b_kernel(q_ref, k_ref, v_ref, gate_ref, o_ref, s_even, s_odd, mx_scr):
    S = k_ref.shape[0]
    tq = s_even.shape[0]
    n_tiles = S // tq
    n_chunks = S // TK_B
    n_heads = q_ref.shape[1] // LANES
    assert n_heads % 2 == 0
    s_bufs = (s_even, s_odd)

    def head_lanes(e):
        return slice(e * LANES, (e + 1) * LANES)

    def tile_rows(t):
        return pl.ds(t * tq, tq) if isinstance(t, int) else pl.ds(pl.multiple_of(t * tq, tq), tq)

    def scores(t, e, c, mx):
        ks = slice(c * TK_B, (c + 1) * TK_B)
        s = lax.dot_general(q_ref[tile_rows(t), head_lanes(e)], k_ref[ks, head_lanes(e)], _NT,
                            preferred_element_type=jnp.float32)
        s_bufs[e % 2][:, ks] = s
        for j in range(TK_B // LANES):
            mx = jnp.maximum(mx, s[:, j * LANES:(j + 1) * LANES])
        return mx

    def weighted_values(e, c, m, acc):
        ks = slice(c * TK_B, (c + 1) * TK_B)
        p = jnp.exp2(s_bufs[e % 2][:, ks] - m).astype(jnp.bfloat16)
        return acc + jnp.dot(p, v_ref[ks, head_lanes(e)], preferred_element_type=jnp.float32)

    neg = jnp.full((tq, LANES), -jnp.inf, jnp.float32)
    zero = jnp.zeros((tq, LANES), jnp.float32)
    lane = lax.broadcasted_iota(jnp.int32, (tq, LANES), 1)

    mx = neg
    for c in range(n_chunks):
        mx = scores(0, 0, c, mx)
    mx_scr[...] = mx

    def tile(t, carry):
        t_next = jnp.minimum(t + 1, n_tiles - 1)
        mx = mx_scr[...]
        acc = []
        for e in range(n_heads):
            m = jnp.max(mx, axis=-1, keepdims=True)
            mx, a = neg, zero
            for c in range(n_chunks):
                mx = scores(t, e + 1, c, mx) if e + 1 < n_heads else scores(t_next, 0, c, mx)
                a = weighted_values(e, c, m, a)
            acc.append(a)
        mx_scr[...] = mx
        for pair in range(n_heads // 2):
            acc_e, acc_o = acc[2 * pair], acc[2 * pair + 1]
            inv_e = 1.0 / acc_e[:, V_HEAD_DIM:V_HEAD_DIM + 1]
            inv_o = 1.0 / acc_o[:, 0:1]
            cols = slice(pair * LANES, (pair + 1) * LANES)
            o = jnp.where(lane < V_HEAD_DIM, acc_e * inv_e, acc_o * inv_o)
            gate = gate_ref[tile_rows(t), cols].astype(jnp.float32)
            o_ref[tile_rows(t), cols] = (o * _silu(gate)).astype(o_ref.dtype)
        return carry

    lax.fori_loop(0, n_tiles, tile, 0)


def _attn_b(q2, k2, v2, gate):
    B, S, HP = q2.shape
    hw = HEADS_PER_STEP_B * LANES
    ow = HEADS_PER_STEP_B * V_HEAD_DIM
    heads = lambda b, p: (b, 0, p)
    return pl.pallas_call(
        _attn_b_kernel,
        grid=(B, N_HEADS_B // HEADS_PER_STEP_B),
        in_specs=[pl.BlockSpec((None, S, hw), heads), pl.BlockSpec((None, S, hw), heads),
                  pl.BlockSpec((None, S, hw), heads), pl.BlockSpec((None, S, ow), heads)],
        out_specs=pl.BlockSpec((None, S, ow), heads),
        out_shape=jax.ShapeDtypeStruct((B, S, WIDTH_B), jnp.bfloat16),
        scratch_shapes=[pltpu.VMEM((TQ_B, S), jnp.float32), pltpu.VMEM((TQ_B, S), jnp.float32),
                        pltpu.VMEM((TQ_B, LANES), jnp.float32)],
        compiler_params=_params("arbitrary", "arbitrary"),
        name="attn_b",
    )(q2, k2, v2, gate)


def _out_b_kernel(x_ref, mod_ref, yg_ref, w_ref, g_ref, b_ref, out_ref):
    D = x_ref.shape[-1]
    y = jnp.dot(yg_ref[...], w_ref[...], preferred_element_type=jnp.float32)
    z = DEEPNORM_ALPHA * x_ref[...] + mod_ref[:, 2 * D:] * y
    out_ref[...] = _layernorm(z, g_ref[...], b_ref[...])


def _out_b(x, mod, yg, w_out, ln_g, ln_b):
    B, S, D = x.shape
    tl = TL_WIDE
    row = lambda b, i: (b, i, 0)
    const2 = lambda b, i: (0, 0)
    return pl.pallas_call(
        _out_b_kernel,
        grid=(B, S // tl),
        in_specs=[pl.BlockSpec((None, tl, D), row),
                  pl.BlockSpec((None, 1, 3 * D), lambda b, i: (b, 0, 0)),
                  pl.BlockSpec((None, tl, WIDTH_B), row),
                  pl.BlockSpec(w_out.shape, const2),
                  pl.BlockSpec((1, D), const2), pl.BlockSpec((1, D), const2)],
        out_specs=pl.BlockSpec((None, tl, D), row),
        out_shape=jax.ShapeDtypeStruct((B, S, D), jnp.float32),
        compiler_params=_params("arbitrary", "arbitrary"),
        name="out_b",
    )(x, mod, yg, w_out, ln_g.reshape(1, D), ln_b.reshape(1, D))


def _weights_a(a_w_in):
    W = WIDTH_A
    gw = GROUP_WIDTH_A
    q, k, v, gate = (a_w_in[:, i * W:(i + 1) * W] for i in range(4))
    ws = []
    for g in range(len(DIL_GROUPS)):
        cs = slice(g * gw, (g + 1) * gw)
        parts = [q[:, cs] * (HEAD_DIM_A ** -0.5 * LOG2E), k[:, cs], v[:, cs]]
        if g == 0:
            parts.append(gate)
        ws.append(jnp.concatenate(parts, axis=1).astype(jnp.bfloat16))
    return ws


def _swap_halves(w):
    half = w.shape[-1] // 2
    return jnp.concatenate([w[..., half:], w[..., :half]], axis=-1)


def _pad_lanes(w, lo):
    n = w.shape[-1]
    return jnp.pad(w, [(0, 0)] * (w.ndim - 1) + [(lo, LANES - lo - n)])


def _weights_b(b_w_in, b_w_uq, b_w_ukv):
    c0 = Q_LORA_RANK
    c1 = c0 + KV_LORA_RANK
    c2 = c1 + QK_ROPE_DIM
    kr = b_w_in[:, c1:c2]
    w_in = jnp.concatenate([b_w_in[:, :c1], _pad_lanes(jnp.concatenate([kr, _swap_halves(kr)], axis=1), QK_NOPE_DIM),
                            b_w_in[:, c2:]], axis=1)
    uq = b_w_uq.reshape(Q_LORA_RANK, N_HEADS_B, QK_NOPE_DIM + QK_ROPE_DIM)
    wqa = jnp.concatenate([uq, _swap_halves(uq[..., QK_NOPE_DIM:])], axis=-1).reshape(Q_LORA_RANK, -1)
    ukv = b_w_ukv.reshape(KV_LORA_RANK, N_HEADS_B, QK_NOPE_DIM + V_HEAD_DIM)
    wk = _pad_lanes(ukv[..., :QK_NOPE_DIM], 0).reshape(KV_LORA_RANK, -1)
    vv = ukv[..., QK_NOPE_DIM:]
    odd = (jnp.arange(N_HEADS_B) % 2 == 1)[None, :, None]
    wv = jnp.where(odd, _pad_lanes(vv, LANES - V_HEAD_DIM), _pad_lanes(vv, 0)).reshape(KV_LORA_RANK, -1)
    bf = lambda w: w.astype(jnp.bfloat16)
    return bf(w_in), bf(wqa), bf(wk), bf(wv)


def _rope_tables(S):
    pos = np.arange(S, dtype=np.float32)
    inv_freq = np.float32(ROPE_THETA) ** (-np.arange(0, QK_ROPE_DIM, 2, dtype=np.float32) / np.float32(QK_ROPE_DIM))
    ang = (pos[:, None] * inv_freq[None, :]).astype(np.float32)
    cos, sin = np.cos(ang), np.sin(ang)
    rope_lanes = slice(QK_NOPE_DIM, QK_NOPE_DIM + QK_ROPE_DIM)
    cosf = np.zeros((S, LANES), np.float32)
    sinf = np.zeros((S, LANES), np.float32)
    cosf[:, rope_lanes] = np.concatenate([cos, cos], axis=-1)
    sinf[:, rope_lanes] = np.concatenate([-sin, sin], axis=-1)
    nope = np.zeros((S, LANES), np.float32)
    nope[:, :QK_NOPE_DIM] = 1.0
    scale = np.float32((QK_NOPE_DIM + QK_ROPE_DIM) ** -0.5 * LOG2E)
    return tuple(jnp.asarray(t) for t in (scale * (nope + cosf), scale * sinf, cosf, sinf))


def _v_ones_row():
    row = np.zeros((N_HEADS_B, LANES), np.float32)
    row[0::2, V_HEAD_DIM] = 1.0
    row[1::2, 0] = 1.0
    return jnp.asarray(row.reshape(1, -1))


def kernel(x, c, rel_bias, ada_w, ada_b, ln_g, ln_b, a_w_in, a_w_out,
           b_w_in, b_q_norm, b_w_uq, b_kv_norm, b_w_ukv, b_w_out):
    B, S, D = x.shape
    mods = _modulation(c, ada_w, ada_b)
    mod0 = mods[0].reshape(B, 1, 3 * D)
    mod1 = mods[1].reshape(B, 1, 3 * D)

    for window, dil in DIL_GROUPS:
        assert window // (2 * dil) == HALF_WINDOW
    *qkvs, gate_a = _proj_a(x, mod0, _weights_a(a_w_in[0]))
    biasm = _bias_variants(rel_bias)
    os_, lses = [], []
    for g, qkv in enumerate(qkvs):
        o, lse = _attn_a(qkv, biasm, g)
        os_.append(o)
        lses.append(lse)

    w_in, wqa, wk, wv = _weights_b(b_w_in[0], b_w_uq[0], b_w_ukv[0])
    x1, q2, k2, v2, gate_b = _mid(x, mod0, os_, lses, gate_a, a_w_out[0].astype(jnp.bfloat16), ln_g[0], ln_b[0],
                                  mod1, w_in, b_q_norm[0], b_kv_norm[0], wqa, wk, wv, _rope_tables(S))
    yg = _attn_b(q2, k2, v2, gate_b)
    return _out_b(x1, mod1, yg, b_w_out[0].astype(jnp.bfloat16), ln_g[1], ln_b[1])
```

```python
import math

import numpy as np
import jax
import jax.numpy as jnp
from jax import lax
from jax.experimental import pallas as pl
from jax.experimental.pallas import tpu as pltpu

DEPTH = 2
HEAD_DIM_A = 64
DIL_GROUPS = ((128, 1), (512, 4), (2048, 16))
HEADS_PER_GROUP_A = 6
GROUP_WIDTH_A = HEADS_PER_GROUP_A * HEAD_DIM_A
WIDTH_A = GROUP_WIDTH_A * len(DIL_GROUPS)
N_BUCKETS = 32
T5_MAX_DISTANCE = 1024
N_HEADS_B = 16
QK_NOPE_DIM = 64
QK_ROPE_DIM = 32
V_HEAD_DIM = 64
Q_LORA_RANK = 256
KV_LORA_RANK = 128
WIDTH_B = N_HEADS_B * V_HEAD_DIM
ROPE_THETA = 10000.0
DEEPNORM_ALPHA = (2.0 * DEPTH) ** 0.25
LN_EPS = 1e-5
RMS_EPS = 1e-6
NEG_INF = -1e30
LOG2E = math.log2(math.e)
LN2 = math.log(2.0)

LANES = 128
VMEM_LIMIT_BYTES = 56 * 1024 * 1024

HALF_WINDOW = 64
TQ_A = 128
TK_A = TQ_A + 2 * HALF_WINDOW
GROUP_LANE_STRIDE = 8
MAX_TILES_PER_STEP_A = 4
TL_PROJ = 512
TL_WIDE = 1024
TQ_B = 512
TK_B = 512
HEADS_PER_STEP_B = 4

_NT = (((1,), (1,)), ((), ()))


def _silu(x):
    return x * (1.0 / (1.0 + jnp.exp(-x)))


def _params(*sem):
    return pltpu.CompilerParams(dimension_semantics=sem, vmem_limit_bytes=VMEM_LIMIT_BYTES)


def _mod_kernel(c_ref, w_ref, b_ref, o_ref):
    sc = _silu(c_ref[...]).astype(jnp.bfloat16)
    o_ref[...] = jnp.dot(sc, w_ref[...].astype(jnp.bfloat16),
                         preferred_element_type=jnp.float32) + b_ref[...]


def _modulation(c, ada_w, ada_b):
    B, D = c.shape
    nj = 3
    return pl.pallas_call(
        _mod_kernel,
        grid=(DEPTH, nj),
        in_specs=[pl.BlockSpec((B, D), lambda i, j: (0, 0)),
                  pl.BlockSpec((None, D, D), lambda i, j: (i, 0, j)),
                  pl.BlockSpec((None, 1, D), lambda i, j: (i, 0, j))],
        out_specs=pl.BlockSpec((None, B, D), lambda i, j: (i, 0, j)),
        out_shape=jax.ShapeDtypeStruct((DEPTH, B, 3 * D), jnp.float32),
        compiler_params=_params("arbitrary", "arbitrary"),
        name="modulation",
    )(c, ada_w, ada_b.reshape(DEPTH, 1, 3 * D))


def _modulate(x, mod):
    D = x.shape[-1]
    return x * (1.0 + mod[:, D:2 * D]) + mod[:, :D]


def _layernorm(z, g, b):
    mu = jnp.mean(z, axis=-1, keepdims=True)
    zc = z - mu
    var = jnp.mean(zc * zc, axis=-1, keepdims=True)
    return zc * lax.rsqrt(var + LN_EPS) * g + b


def _proj_a_kernel(x_ref, mod_ref, w0_ref, w1_ref, w2_ref, qkv0_ref, qkv1_ref, qkv2_ref, gate_ref):
    uf = _modulate(x_ref[...], mod_ref[...])
    u = uf.astype(jnp.bfloat16)
    n_qkv = qkv0_ref.shape[-1]
    z = jnp.dot(u, w0_ref[...], preferred_element_type=jnp.float32)
    qkv0_ref[0] = z[:, :n_qkv].astype(qkv0_ref.dtype)
    gate_ref[...] = z[:, n_qkv:].astype(gate_ref.dtype)
    for w_ref, qkv_ref in ((w1_ref, qkv1_ref), (w2_ref, qkv2_ref)):
        dil, rows = qkv_ref.shape[:2]
        up = pltpu.einshape("(md)n->(dm)n", uf, d=dil).astype(jnp.bfloat16)
        z = jnp.dot(up, w_ref[...], preferred_element_type=jnp.float32)
        for r in range(dil):
            qkv_ref[r] = z[r * rows:(r + 1) * rows].astype(qkv_ref.dtype)


def _proj_a(x, mod, ws):
    B, S, D = x.shape
    tl = TL_WIDE
    n_qkv = 3 * GROUP_WIDTH_A
    dils = [dil for _, dil in DIL_GROUPS]
    assert dils[0] == 1
    row = lambda b, i: (b, i, 0)
    cls = lambda b, i: (b, 0, i, 0)
    const2 = lambda b, i: (0, 0)
    return pl.pallas_call(
        _proj_a_kernel,
        grid=(B, S // tl),
        in_specs=[pl.BlockSpec((None, tl, D), row),
                  pl.BlockSpec((None, 1, 3 * D), lambda b, i: (b, 0, 0)),
                  *[pl.BlockSpec(w.shape, const2) for w in ws]],
        out_specs=[*[pl.BlockSpec((None, dil, tl // dil, n_qkv), cls) for dil in dils],
                   pl.BlockSpec((None, tl, WIDTH_A), row)],
        out_shape=[*[jax.ShapeDtypeStruct((B, dil, S // dil, n_qkv), jnp.bfloat16) for dil in dils],
                   jax.ShapeDtypeStruct((B, S, WIDTH_A), jnp.bfloat16)],
        compiler_params=_params("arbitrary", "arbitrary"),
        name="proj_a",
    )(x, mod, *ws)


def _attn_a_kernel(qkv_ref, bias_ref, o_ref, lse_ref, s_scr, p_scr):
    n_cls, L = (1,) + qkv_ref.shape[:1] if len(qkv_ref.shape) == 2 else qkv_ref.shape[:2]

    def of_class(ref, cls):
        return ref if len(ref.shape) == 2 else ref.at[cls]
    n_tiles = L // TQ_A
    gw = GROUP_WIDTH_A
    lane = lax.broadcasted_iota(jnp.int32, (TQ_A, LANES), 1)
    low_q = lane < HEAD_DIM_A
    low_k = lax.broadcasted_iota(jnp.int32, (TK_A, LANES), 1) < HEAD_DIM_A

    heads = range(HEADS_PER_GROUP_A)

    def pair_block(cls, rows, section, h):
        c0 = section * gw + (h // 2) * LANES
        return of_class(qkv_ref, cls)[rows, c0:c0 + LANES]

    def other_lanes(block, low_mask, h, fill):
        mine = low_mask if h % 2 == 0 else jnp.logical_not(low_mask)
        return jnp.where(mine, block, jnp.full_like(block, fill))

    tiles_per_step = s_scr.shape[0]

    def tile_group(tg, carry):
        geo = []
        for u in range(tiles_per_step):
            item = tg * tiles_per_step + u
            cls, t = (0, item) if n_cls == 1 else (item // n_tiles, item % n_tiles)
            j0 = pl.multiple_of(t * TQ_A, TQ_A)
            ws = pl.multiple_of(jnp.clip(j0 - HALF_WINDOW, 0, L - TK_A), HALF_WINDOW)
            variant = jnp.where(t == 0, 0, jnp.where(t == n_tiles - 1, 2, 1))
            geo.append((cls, pl.ds(j0, TQ_A), pl.ds(ws, TK_A), variant))
        for u, (cls, rq, rk, variant) in enumerate(geo):
            for h in heads:
                q = other_lanes(pair_block(cls, rq, 0, h), low_q, h, 0)
                s = lax.dot_general(q, pair_block(cls, rk, 1, h), _NT, preferred_element_type=jnp.float32)
                s_scr[u, h] = s + bias_ref[variant, h]
        m = [[jnp.max(s_scr[u, h], axis=-1, keepdims=True) for h in heads] for u in range(len(geo))]
        for u in range(len(geo)):
            for h in heads:
                p_scr[u, h] = jnp.exp2(s_scr[u, h] - m[u][h]).astype(jnp.bfloat16)
        for u, (cls, rq, rk, variant) in enumerate(geo):
            lse_tile = jnp.zeros((TQ_A, LANES), jnp.float32)
            for pair in range(HEADS_PER_GROUP_A // 2):
                he, ho = 2 * pair, 2 * pair + 1
                acc_e = jnp.dot(p_scr[u, he], other_lanes(pair_block(cls, rk, 2, he), low_k, he, 1),
                                preferred_element_type=jnp.float32)
                acc_o = jnp.dot(p_scr[u, ho], other_lanes(pair_block(cls, rk, 2, ho), low_k, ho, 1),
                                preferred_element_type=jnp.float32)
                numer = jnp.where(low_q, acc_e, acc_o)
                den_other = jnp.where(low_q, acc_o, acc_e)
                den = pltpu.roll(den_other, HEAD_DIM_A, 1)
                of_class(o_ref, cls)[rq, pair * LANES:(pair + 1) * LANES] = (numer / den).astype(o_ref.dtype)
                is_odd = lane == ho
                lse = (jnp.where(is_odd, m[u][ho], m[u][he])
                       + jnp.log2(jnp.where(is_odd, den_other, den))) * LN2
                lse_tile = jnp.where(jnp.logical_or(is_odd, lane == he), lse, lse_tile)
            of_class(lse_ref, cls)[rq, :] = lse_tile
        return carry

    lax.fori_loop(0, n_cls * n_tiles // tiles_per_step, tile_group, 0)


def _attn_a(qkv, biasm):
    B, dil, L, n_qkv = qkv.shape
    gw = GROUP_WIDTH_A
    n_tiles = L // TQ_A
    n_cls = min(dil, max(1, MAX_TILES_PER_STEP_A // n_tiles))
    tiles_per_step = math.gcd(MAX_TILES_PER_STEP_A, n_cls * n_tiles)
    blk_cls = None if n_cls == 1 else n_cls
    cls = lambda b, r: (b, r, 0, 0)
    return pl.pallas_call(
        _attn_a_kernel,
        grid=(B, dil // n_cls),
        in_specs=[pl.BlockSpec((None, blk_cls, L, n_qkv), cls),
                  pl.BlockSpec(biasm.shape, lambda b, r: (0, 0, 0, 0))],
        out_specs=[pl.BlockSpec((None, blk_cls, L, gw), cls),
                   pl.BlockSpec((None, blk_cls, L, LANES), cls)],
        out_shape=[jax.ShapeDtypeStruct((B, dil, L, gw), jnp.bfloat16),
                   jax.ShapeDtypeStruct((B, dil, L, LANES), jnp.float32)],
        scratch_shapes=[
            pltpu.VMEM((tiles_per_step, HEADS_PER_GROUP_A, TQ_A, TK_A), jnp.float32),
            pltpu.VMEM((tiles_per_step, HEADS_PER_GROUP_A, TQ_A, TK_A), jnp.bfloat16)],
        compiler_params=_params("arbitrary", "arbitrary"),
        name=f"attn_a_d{dil}",
    )(qkv, biasm)


def _t5_bucket_np(rel):
    half = N_BUCKETS // 2
    max_exact = half // 2
    base = np.where(rel > 0, half, 0)
    n = np.abs(rel)
    nf = np.maximum(n, 1).astype(np.float32)
    large = max_exact + (np.log(nf / np.float32(max_exact)) / np.float32(math.log(T5_MAX_DISTANCE / max_exact))
                         * np.float32(half - max_exact)).astype(np.int32)
    large = np.minimum(large, half - 1)
    return base + np.where(n < max_exact, n, large)


def _bias_variants(rel_bias_g, dil):
    H = rel_bias_g.shape[1]
    M = 2 * TK_A
    off = np.arange(M) - (TK_A - 1)
    idx = np.where(np.abs(off) <= HALF_WINDOW, _t5_bucket_np(off * dil), N_BUCKETS).astype(np.int32)
    table = jnp.concatenate([rel_bias_g.astype(jnp.float32) * LOG2E, jnp.full((1, H), NEG_INF, jnp.float32)], axis=0)
    w = table[jnp.asarray(idx)].T
    rot = jnp.tile(w, (1, TQ_A + 1))[:, :TQ_A * (M + 1)].reshape(H, TQ_A, M + 1)[:, ::-1, :]
    variants = []
    for shift in (0, HALF_WINDOW, 2 * HALF_WINDOW):
        j0 = TK_A - TQ_A - shift
        variants.append(rot[:, :, j0:j0 + TK_A])
    return jnp.stack(variants)


N_MERGE_A_REFS = 13


def _merge_a(x_ref, mod_ref, o0_ref, o1_ref, o2_ref, l0_ref, l1_ref, l2_ref, gate_ref,
             expand_ref, w_ref, g_ref, b_ref):
    tl, D = x_ref.shape
    stride = GROUP_LANE_STRIDE
    lane = lax.broadcasted_iota(jnp.int32, (tl, LANES), 1)

    def natural_order(o_ref, l_ref):
        if o_ref.shape[0] == 1:
            return o_ref[0].astype(jnp.float32), l_ref[0]
        return (pltpu.einshape("dmn->(md)n", o_ref[...].astype(jnp.float32)),
                pltpu.einshape("dmn->(md)n", l_ref[...]))

    o0, l0 = natural_order(o0_ref, l0_ref)
    o1, l1 = natural_order(o1_ref, l1_ref)
    o2, l2 = natural_order(o2_ref, l2_ref)
    mx = jnp.maximum(jnp.maximum(l0, l1), l2)
    e0, e1, e2 = jnp.exp(l0 - mx), jnp.exp(l1 - mx), jnp.exp(l2 - mx)
    inv = 1.0 / (e0 + e1 + e2)
    wt = jnp.where(lane < stride, e0 * inv,
                   jnp.where(lane < 2 * stride, pltpu.roll(e1 * inv, stride, 1), pltpu.roll(e2 * inv, 2 * stride, 1)))
    hi = wt.astype(jnp.bfloat16)
    lo = (wt - hi.astype(jnp.float32)).astype(jnp.bfloat16)
    wexp = (jnp.dot(hi, expand_ref[...], preferred_element_type=jnp.float32)
            + jnp.dot(lo, expand_ref[...], preferred_element_type=jnp.float32))
    o = jnp.concatenate([o0, o1, o2], axis=1)
    yg = (o * wexp * _silu(gate_ref[...].astype(jnp.float32))).astype(jnp.bfloat16)
    y = jnp.dot(yg, w_ref[...], preferred_element_type=jnp.float32)
    z = DEEPNORM_ALPHA * x_ref[...] + mod_ref[:, 2 * D:] * y
    return _layernorm(z, g_ref[...], b_ref[...])


def _rmsnorm(x, g):
    return x * lax.rsqrt(jnp.mean(x * x, axis=-1, keepdims=True) + RMS_EPS) * g


def _project_b(x, mod_ref, w_in_ref, qn_ref, kvn_ref, wqa_ref, wk_ref, wv_ref,
               vone_ref, cq_ref, sq_ref, ck_ref, sk_ref, q_out, k_out, v_out, gate_out):
    u = _modulate(x, mod_ref[...]).astype(jnp.bfloat16)
    z = jnp.dot(u, w_in_ref[...], preferred_element_type=jnp.float32)
    c0 = Q_LORA_RANK
    c1 = c0 + KV_LORA_RANK
    cq = z[:, :c0]
    ckv = z[:, c0:c1]
    kr = z[:, c1:c1 + LANES]
    gate_out[...] = z[:, c1 + LANES:].astype(gate_out.dtype)
    to_rope_lanes = LANES - QK_ROPE_DIM

    cqn = _rmsnorm(cq, qn_ref[...]).astype(jnp.bfloat16)
    qa = jnp.dot(cqn, wqa_ref[...], preferred_element_type=jnp.float32)
    ckvn = _rmsnorm(ckv, kvn_ref[...]).astype(jnp.bfloat16)
    kn = jnp.dot(ckvn, wk_ref[...], preferred_element_type=jnp.float32)
    v_out[...] = (jnp.dot(ckvn, wv_ref[...], preferred_element_type=jnp.float32)
                  + vone_ref[...]).astype(v_out.dtype)

    cos_q, sin_q = cq_ref[...], sq_ref[...]
    k_rope = kr * ck_ref[...] + pltpu.roll(kr, to_rope_lanes, 1) * sk_ref[...]
    for h in range(N_HEADS_B):
        blk = slice(h * LANES, (h + 1) * LANES)
        qh = qa[:, blk]
        q_out[:, blk] = (qh * cos_q + pltpu.roll(qh, to_rope_lanes, 1) * sin_q).astype(q_out.dtype)
        k_out[:, blk] = (kn[:, blk] + k_rope).astype(k_out.dtype)


def _mid_kernel(*refs):
    merge_refs, rest = refs[:N_MERGE_A_REFS], refs[N_MERGE_A_REFS:]
    x1_out, project_refs = rest[-5], rest[:-5] + rest[-4:]
    x1 = _merge_a(*merge_refs)
    x1_out[...] = x1
    _project_b(x1, *project_refs)


def _mid(x, mod0, os_, lses, gate_a, w_out_a, ln_g, ln_b,
         mod1, w_in, q_norm, kv_norm, wqa, wk, wv, tables):
    B, S, D = x.shape
    tl = TL_PROJ
    gw = GROUP_WIDTH_A
    HP = N_HEADS_B * LANES
    col = np.arange(WIDTH_A)
    src_lane = (col // gw) * GROUP_LANE_STRIDE + (col % gw) // HEAD_DIM_A
    expand = jnp.asarray((np.arange(LANES)[:, None] == src_lane[None, :]).astype(np.float32), dtype=jnp.bfloat16)
    row = lambda b, i: (b, i, 0)
    cls = lambda b, i: (b, 0, i, 0)
    const2 = lambda b, i: (0, 0)
    mod_spec = pl.BlockSpec((None, 1, 3 * D), lambda b, i: (b, 0, 0))
    tab = pl.BlockSpec((tl, LANES), lambda b, i: (i, 0))
    o_specs = [pl.BlockSpec((None, dil, tl // dil, gw), cls) for _, dil in DIL_GROUPS]
    l_specs = [pl.BlockSpec((None, dil, tl // dil, LANES), cls) for _, dil in DIL_GROUPS]
    merge_args = [x, mod0, *os_, *lses, gate_a, expand, w_out_a, ln_g.reshape(1, D), ln_b.reshape(1, D)]
    merge_specs = [pl.BlockSpec((None, tl, D), row), mod_spec, *o_specs, *l_specs,
                   pl.BlockSpec((None, tl, WIDTH_A), row),
                   pl.BlockSpec(expand.shape, const2), pl.BlockSpec(w_out_a.shape, const2),
                   pl.BlockSpec((1, D), const2), pl.BlockSpec((1, D), const2)]
    assert len(merge_args) == N_MERGE_A_REFS
    project_args = [mod1, w_in, q_norm.reshape(1, -1), kv_norm.reshape(1, -1), wqa, wk, wv, _v_ones_row(), *tables]
    project_specs = [mod_spec, pl.BlockSpec(w_in.shape, const2),
                     pl.BlockSpec((1, Q_LORA_RANK), const2), pl.BlockSpec((1, KV_LORA_RANK), const2),
                     pl.BlockSpec(wqa.shape, const2), pl.BlockSpec(wk.shape, const2),
                     pl.BlockSpec(wv.shape, const2), pl.BlockSpec((1, HP), const2), tab, tab, tab, tab]
    wide = jax.ShapeDtypeStruct((B, S, HP), jnp.bfloat16)
    return pl.pallas_call(
        _mid_kernel,
        grid=(B, S // tl),
        in_specs=merge_specs + project_specs,
        out_specs=[pl.BlockSpec((None, tl, D), row),
                   pl.BlockSpec((None, tl, HP), row), pl.BlockSpec((None, tl, HP), row),
                   pl.BlockSpec((None, tl, HP), row), pl.BlockSpec((None, tl, WIDTH_B), row)],
        out_shape=[jax.ShapeDtypeStruct((B, S, D), jnp.float32), wide, wide, wide,
                   jax.ShapeDtypeStruct((B, S, WIDTH_B), jnp.bfloat16)],
        compiler_params=_params("arbitrary", "arbitrary"),
        name="mid",
    )(*merge_args, *project_args)


def _attn_b_kernel(q_ref, k_ref, v_ref, gate_ref, o_ref, s_even, s_odd, mx_scr):
    S = k_ref.shape[0]
    tq = s_even.shape[0]
    n_tiles = S // tq
    n_chunks = S // TK_B
    n_heads = q_ref.shape[1] // LANES
    assert n_heads % 2 == 0
    s_bufs = (s_even, s_odd)

    def head_lanes(e):
        return slice(e * LANES, (e + 1) * LANES)

    def tile_rows(t):
        return pl.ds(t * tq, tq) if isinstance(t, int) else pl.ds(pl.multiple_of(t * tq, tq), tq)

    def scores(t, e, c, mx):
        ks = slice(c * TK_B, (c + 1) * TK_B)
        s = lax.dot_general(q_ref[tile_rows(t), head_lanes(e)], k_ref[ks, head_lanes(e)], _NT,
                            preferred_element_type=jnp.float32)
        s_bufs[e % 2][:, ks] = s
        for j in range(TK_B // LANES):
            mx = jnp.maximum(mx, s[:, j * LANES:(j + 1) * LANES])
        return mx

    def weighted_values(e, c, m, acc):
        ks = slice(c * TK_B, (c + 1) * TK_B)
        p = jnp.exp2(s_bufs[e % 2][:, ks] - m).astype(jnp.bfloat16)
        return acc + jnp.dot(p, v_ref[ks, head_lanes(e)], preferred_element_type=jnp.float32)

    neg = jnp.full((tq, LANES), -jnp.inf, jnp.float32)
    zero = jnp.zeros((tq, LANES), jnp.float32)
    lane = lax.broadcasted_iota(jnp.int32, (tq, LANES), 1)

    mx = neg
    for c in range(n_chunks):
        mx = scores(0, 0, c, mx)
    mx_scr[...] = mx

    def tile(t, carry):
        t_next = jnp.minimum(t + 1, n_tiles - 1)
        mx = mx_scr[...]
        acc = []
        for e in range(n_heads):
            m = jnp.max(mx, axis=-1, keepdims=True)
            mx, a = neg, zero
            for c in range(n_chunks):
                mx = scores(t, e + 1, c, mx) if e + 1 < n_heads else scores(t_next, 0, c, mx)
                a = weighted_values(e, c, m, a)
            acc.append(a)
        mx_scr[...] = mx
        for pair in range(n_heads // 2):
            acc_e, acc_o = acc[2 * pair], acc[2 * pair + 1]
            inv_e = 1.0 / acc_e[:, V_HEAD_DIM:V_HEAD_DIM + 1]
            inv_o = 1.0 / acc_o[:, 0:1]
            cols = slice(pair * LANES, (pair + 1) * LANES)
            o = jnp.where(lane < V_HEAD_DIM, acc_e * inv_e, acc_o * inv_o)
            gate = gate_ref[tile_rows(t), cols].astype(jnp.float32)
            o_ref[tile_rows(t), cols] = (o * _silu(gate)).astype(o_ref.dtype)
        return carry

    lax.fori_loop(0, n_tiles, tile, 0)


def _attn_b(q2, k2, v2, gate):
    B, S, HP = q2.shape
    hw = HEADS_PER_STEP_B * LANES
    ow = HEADS_PER_STEP_B * V_HEAD_DIM
    heads = lambda b, p: (b, 0, p)
    return pl.pallas_call(
        _attn_b_kernel,
        grid=(B, N_HEADS_B // HEADS_PER_STEP_B),
        in_specs=[pl.BlockSpec((None, S, hw), heads), pl.BlockSpec((None, S, hw), heads),
                  pl.BlockSpec((None, S, hw), heads), pl.BlockSpec((None, S, ow), heads)],
        out_specs=pl.BlockSpec((None, S, ow), heads),
        out_shape=jax.ShapeDtypeStruct((B, S, WIDTH_B), jnp.bfloat16),
        scratch_shapes=[pltpu.VMEM((TQ_B, S), jnp.float32), pltpu.VMEM((TQ_B, S), jnp.float32),
                        pltpu.VMEM((TQ_B, LANES), jnp.float32)],
        compiler_params=_params("arbitrary", "arbitrary"),
        name="attn_b",
    )(q2, k2, v2, gate)


def _out_b_kernel(x_ref, mod_ref, yg_ref, w_ref, g_ref, b_ref, out_ref):
    D = x_ref.shape[-1]
    y = jnp.dot(yg_ref[...], w_ref[...], preferred_element_type=jnp.float32)
    z = DEEPNORM_ALPHA * x_ref[...] + mod_ref[:, 2 * D:] * y
    out_ref[...] = _layernorm(z, g_ref[...], b_ref[...])


def _out_b(x, mod, yg, w_out, ln_g, ln_b):
    B, S, D = x.shape
    tl = TL_WIDE
    row = lambda b, i: (b, i, 0)
    const2 = lambda b, i: (0, 0)
    return pl.pallas_call(
        _out_b_kernel,
        grid=(B, S // tl),
        in_specs=[pl.BlockSpec((None, tl, D), row),
                  pl.BlockSpec((None, 1, 3 * D), lambda b, i: (b, 0, 0)),
                  pl.BlockSpec((None, tl, WIDTH_B), row),
                  pl.BlockSpec(w_out.shape, const2),
                  pl.BlockSpec((1, D), const2), pl.BlockSpec((1, D), const2)],
        out_specs=pl.BlockSpec((None, tl, D), row),
        out_shape=jax.ShapeDtypeStruct((B, S, D), jnp.float32),
        compiler_params=_params("arbitrary", "arbitrary"),
        name="out_b",
    )(x, mod, yg, w_out, ln_g.reshape(1, D), ln_b.reshape(1, D))


def _weights_a(a_w_in):
    W = WIDTH_A
    gw = GROUP_WIDTH_A
    q, k, v, gate = (a_w_in[:, i * W:(i + 1) * W] for i in range(4))
    ws = []
    for g in range(len(DIL_GROUPS)):
        cs = slice(g * gw, (g + 1) * gw)
        parts = [q[:, cs] * (HEAD_DIM_A ** -0.5 * LOG2E), k[:, cs], v[:, cs]]
        if g == 0:
            parts.append(gate)
        ws.append(jnp.concatenate(parts, axis=1).astype(jnp.bfloat16))
    return ws


def _swap_halves(w):
    half = w.shape[-1] // 2
    return jnp.concatenate([w[..., half:], w[..., :half]], axis=-1)


def _pad_lanes(w, lo):
    n = w.shape[-1]
    return jnp.pad(w, [(0, 0)] * (w.ndim - 1) + [(lo, LANES - lo - n)])


def _weights_b(b_w_in, b_w_uq, b_w_ukv):
    c0 = Q_LORA_RANK
    c1 = c0 + KV_LORA_RANK
    c2 = c1 + QK_ROPE_DIM
    kr = b_w_in[:, c1:c2]
    w_in = jnp.concatenate([b_w_in[:, :c1], _pad_lanes(jnp.concatenate([kr, _swap_halves(kr)], axis=1), QK_NOPE_DIM),
                            b_w_in[:, c2:]], axis=1)
    uq = b_w_uq.reshape(Q_LORA_RANK, N_HEADS_B, QK_NOPE_DIM + QK_ROPE_DIM)
    wqa = jnp.concatenate([uq, _swap_halves(uq[..., QK_NOPE_DIM:])], axis=-1).reshape(Q_LORA_RANK, -1)
    ukv = b_w_ukv.reshape(KV_LORA_RANK, N_HEADS_B, QK_NOPE_DIM + V_HEAD_DIM)
    wk = _pad_lanes(ukv[..., :QK_NOPE_DIM], 0).reshape(KV_LORA_RANK, -1)
    vv = ukv[..., QK_NOPE_DIM:]
    odd = (jnp.arange(N_HEADS_B) % 2 == 1)[None, :, None]
    wv = jnp.where(odd, _pad_lanes(vv, LANES - V_HEAD_DIM), _pad_lanes(vv, 0)).reshape(KV_LORA_RANK, -1)
    bf = lambda w: w.astype(jnp.bfloat16)
    return bf(w_in), bf(wqa), bf(wk), bf(wv)


def _rope_tables(S):
    pos = jnp.arange(S, dtype=jnp.float32)
    inv_freq = ROPE_THETA ** (-jnp.arange(0, QK_ROPE_DIM, 2, dtype=jnp.float32) / QK_ROPE_DIM)
    ang = pos[:, None] * inv_freq[None, :]
    cos, sin = jnp.cos(ang), jnp.sin(ang)
    cosf = _pad_lanes(jnp.concatenate([cos, cos], axis=-1), QK_NOPE_DIM)
    sinf = _pad_lanes(jnp.concatenate([-sin, sin], axis=-1), QK_NOPE_DIM)
    nope = _pad_lanes(jnp.ones((S, QK_NOPE_DIM), jnp.float32), 0)
    scale = (QK_NOPE_DIM + QK_ROPE_DIM) ** -0.5 * LOG2E
    return (scale * (nope + cosf), scale * sinf, cosf, sinf)


def _v_ones_row():
    row = np.zeros((N_HEADS_B, LANES), np.float32)
    row[0::2, V_HEAD_DIM] = 1.0
    row[1::2, 0] = 1.0
    return jnp.asarray(row.reshape(1, -1))


def kernel(x, c, rel_bias, ada_w, ada_b, ln_g, ln_b, a_w_in, a_w_out,
           b_w_in, b_q_norm, b_w_uq, b_kv_norm, b_w_ukv, b_w_out):
    B, S, D = x.shape
    mods = _modulation(c, ada_w, ada_b)
    mod0 = mods[0].reshape(B, 1, 3 * D)
    mod1 = mods[1].reshape(B, 1, 3 * D)

    for window, dil in DIL_GROUPS:
        assert window // (2 * dil) == HALF_WINDOW
    *qkvs, gate_a = _proj_a(x, mod0, _weights_a(a_w_in[0]))
    os_, lses = [], []
    for g, ((_, dil), qkv) in enumerate(zip(DIL_GROUPS, qkvs)):
        hs = slice(g * HEADS_PER_GROUP_A, (g + 1) * HEADS_PER_GROUP_A)
        o, lse = _attn_a(qkv, _bias_variants(rel_bias[:, hs], dil))
        os_.append(o)
        lses.append(lse)

    w_in, wqa, wk, wv = _weights_b(b_w_in[0], b_w_uq[0], b_w_ukv[0])
    x1, q2, k2, v2, gate_b = _mid(x, mod0, os_, lses, gate_a, a_w_out[0].astype(jnp.bfloat16), ln_g[0], ln_b[0],
                                  mod1, w_in, b_q_norm[0], b_kv_norm[0], wqa, wk, wv, _rope_tables(S))
    yg = _attn_b(q2, k2, v2, gate_b)
    return _out_b(x1, mod1, yg, b_w_out[0].astype(jnp.bfloat16), ln_g[1], ln_b[1])
```

```python
import math

import numpy as np
import jax
import jax.numpy as jnp
from jax import lax
from jax.experimental import pallas as pl
from jax.experimental.pallas import tpu as pltpu

DEPTH = 2
HEAD_DIM_A = 64
DIL_GROUPS = ((128, 1), (512, 4), (2048, 16))
HEADS_PER_GROUP_A = 6
GROUP_WIDTH_A = HEADS_PER_GROUP_A * HEAD_DIM_A
WIDTH_A = GROUP_WIDTH_A * len(DIL_GROUPS)
N_BUCKETS = 32
T5_MAX_DISTANCE = 1024
N_HEADS_B = 16
QK_NOPE_DIM = 64
QK_ROPE_DIM = 32
V_HEAD_DIM = 64
Q_LORA_RANK = 256
KV_LORA_RANK = 128
WIDTH_B = N_HEADS_B * V_HEAD_DIM
ROPE_THETA = 10000.0
DEEPNORM_ALPHA = (2.0 * DEPTH) ** 0.25
LN_EPS = 1e-5
RMS_EPS = 1e-6
NEG_INF = -1e30
LOG2E = math.log2(math.e)
LN2 = math.log(2.0)

LANES = 128
VMEM_LIMIT_BYTES = 56 * 1024 * 1024

HALF_WINDOW = 64
TQ_A = 128
TK_A = TQ_A + 2 * HALF_WINDOW
GROUP_LANE_STRIDE = 8
MAX_TILES_PER_STEP_A = 4
TL_PROJ = 512
TL_WIDE = 1024
TQ_B = 512
TK_B = 512
HEADS_PER_STEP_B = 4

_NT = (((1,), (1,)), ((), ()))


def _silu(x):
    return x * (1.0 / (1.0 + jnp.exp(-x)))


def _params(*sem):
    return pltpu.CompilerParams(dimension_semantics=sem, vmem_limit_bytes=VMEM_LIMIT_BYTES)


def _mod_kernel(c_ref, w_ref, b_ref, o_ref):
    sc = _silu(c_ref[...]).astype(jnp.bfloat16)
    o_ref[...] = jnp.dot(sc, w_ref[...].astype(jnp.bfloat16),
                         preferred_element_type=jnp.float32) + b_ref[...]


def _modulation(c, ada_w, ada_b):
    B, D = c.shape
    nj = 3
    return pl.pallas_call(
        _mod_kernel,
        grid=(DEPTH, nj),
        in_specs=[pl.BlockSpec((B, D), lambda i, j: (0, 0)),
                  pl.BlockSpec((None, D, D), lambda i, j: (i, 0, j)),
                  pl.BlockSpec((None, 1, D), lambda i, j: (i, 0, j))],
        out_specs=pl.BlockSpec((None, B, D), lambda i, j: (i, 0, j)),
        out_shape=jax.ShapeDtypeStruct((DEPTH, B, 3 * D), jnp.float32),
        compiler_params=_params("arbitrary", "arbitrary"),
        name="modulation",
    )(c, ada_w, ada_b.reshape(DEPTH, 1, 3 * D))


def _modulate(x, mod):
    D = x.shape[-1]
    return x * (1.0 + mod[:, D:2 * D]) + mod[:, :D]


def _layernorm(z, g, b):
    mu = jnp.mean(z, axis=-1, keepdims=True)
    zc = z - mu
    var = jnp.mean(zc * zc, axis=-1, keepdims=True)
    return zc * lax.rsqrt(var + LN_EPS) * g + b


def _proj_a_kernel(x_ref, mod_ref, w0_ref, w1_ref, w2_ref, qkv0_ref, qkv1_ref, qkv2_ref, gate_ref):
    uf = _modulate(x_ref[...], mod_ref[...])
    u = uf.astype(jnp.bfloat16)
    n_qkv = qkv0_ref.shape[-1]
    z = jnp.dot(u, w0_ref[...], preferred_element_type=jnp.float32)
    qkv0_ref[0] = z[:, :n_qkv].astype(qkv0_ref.dtype)
    gate_ref[...] = _silu(z[:, n_qkv:]).astype(gate_ref.dtype)
    for w_ref, qkv_ref in ((w1_ref, qkv1_ref), (w2_ref, qkv2_ref)):
        dil, rows = qkv_ref.shape[:2]
        up = pltpu.einshape("(md)n->(dm)n", uf, d=dil).astype(jnp.bfloat16)
        z = jnp.dot(up, w_ref[...], preferred_element_type=jnp.float32)
        for r in range(dil):
            qkv_ref[r] = z[r * rows:(r + 1) * rows].astype(qkv_ref.dtype)


def _proj_a(x, mod, ws):
    B, S, D = x.shape
    tl = TL_WIDE
    n_qkv = 3 * GROUP_WIDTH_A
    dils = [dil for _, dil in DIL_GROUPS]
    assert dils[0] == 1
    row = lambda b, i: (b, i, 0)
    cls = lambda b, i: (b, 0, i, 0)
    const2 = lambda b, i: (0, 0)
    return pl.pallas_call(
        _proj_a_kernel,
        grid=(B, S // tl),
        in_specs=[pl.BlockSpec((None, tl, D), row),
                  pl.BlockSpec((None, 1, 3 * D), lambda b, i: (b, 0, 0)),
                  *[pl.BlockSpec(w.shape, const2) for w in ws]],
        out_specs=[*[pl.BlockSpec((None, dil, tl // dil, n_qkv), cls) for dil in dils],
                   pl.BlockSpec((None, tl, WIDTH_A), row)],
        out_shape=[*[jax.ShapeDtypeStruct((B, dil, S // dil, n_qkv), jnp.bfloat16) for dil in dils],
                   jax.ShapeDtypeStruct((B, S, WIDTH_A), jnp.bfloat16)],
        compiler_params=_params("arbitrary", "arbitrary"),
        name="proj_a",
    )(x, mod, *ws)


def _attn_a_kernel(qkv_ref, bias_ref, o_ref, lse_ref, s_scr, p_scr):
    n_cls, L = (1,) + qkv_ref.shape[:1] if len(qkv_ref.shape) == 2 else qkv_ref.shape[:2]

    def of_class(ref, cls):
        return ref if len(ref.shape) == 2 else ref.at[cls]
    n_tiles = L // TQ_A
    gw = GROUP_WIDTH_A
    lane = lax.broadcasted_iota(jnp.int32, (TQ_A, LANES), 1)
    low_q = lane < HEAD_DIM_A
    low_k = lax.broadcasted_iota(jnp.int32, (TK_A, LANES), 1) < HEAD_DIM_A

    heads = range(HEADS_PER_GROUP_A)

    def pair_block(cls, rows, section, h):
        c0 = section * gw + (h // 2) * LANES
        return of_class(qkv_ref, cls)[rows, c0:c0 + LANES]

    def other_lanes(block, low_mask, h, fill):
        mine = low_mask if h % 2 == 0 else jnp.logical_not(low_mask)
        return jnp.where(mine, block, jnp.full_like(block, fill))

    tiles_per_step = s_scr.shape[0]

    def tile_group(tg, carry):
        geo = []
        for u in range(tiles_per_step):
            item = tg * tiles_per_step + u
            cls, t = (0, item) if n_cls == 1 else (item // n_tiles, item % n_tiles)
            j0 = pl.multiple_of(t * TQ_A, TQ_A)
            ws = pl.multiple_of(jnp.clip(j0 - HALF_WINDOW, 0, L - TK_A), HALF_WINDOW)
            variant = jnp.where(t == 0, 0, jnp.where(t == n_tiles - 1, 2, 1))
            geo.append((cls, pl.ds(j0, TQ_A), pl.ds(ws, TK_A), variant))
        for u, (cls, rq, rk, variant) in enumerate(geo):
            for h in heads:
                q = other_lanes(pair_block(cls, rq, 0, h), low_q, h, 0)
                s = lax.dot_general(q, pair_block(cls, rk, 1, h), _NT, preferred_element_type=jnp.float32)
                s_scr[u, h] = s + bias_ref[variant, h]
        m = [[jnp.max(s_scr[u, h], axis=-1, keepdims=True) for h in heads] for u in range(len(geo))]
        for u in range(len(geo)):
            for h in heads:
                p_scr[u, h] = jnp.exp2(s_scr[u, h] - m[u][h]).astype(jnp.bfloat16)
        for u, (cls, rq, rk, variant) in enumerate(geo):
            lse_tile = jnp.zeros((TQ_A, LANES), jnp.float32)
            for pair in range(HEADS_PER_GROUP_A // 2):
                he, ho = 2 * pair, 2 * pair + 1
                acc_e = jnp.dot(p_scr[u, he], other_lanes(pair_block(cls, rk, 2, he), low_k, he, 1),
                                preferred_element_type=jnp.float32)
                acc_o = jnp.dot(p_scr[u, ho], other_lanes(pair_block(cls, rk, 2, ho), low_k, ho, 1),
                                preferred_element_type=jnp.float32)
                numer = jnp.where(low_q, acc_e, acc_o)
                den_other = jnp.where(low_q, acc_o, acc_e)
                den = pltpu.roll(den_other, HEAD_DIM_A, 1)
                of_class(o_ref, cls)[rq, pair * LANES:(pair + 1) * LANES] = (numer / den).astype(o_ref.dtype)
                is_odd = lane == ho
                lse = (jnp.where(is_odd, m[u][ho], m[u][he])
                       + jnp.log2(jnp.where(is_odd, den_other, den))) * LN2
                lse_tile = jnp.where(jnp.logical_or(is_odd, lane == he), lse, lse_tile)
            of_class(lse_ref, cls)[rq, :] = lse_tile
        return carry

    lax.fori_loop(0, n_cls * n_tiles // tiles_per_step, tile_group, 0)


def _attn_a(qkv, biasm):
    B, dil, L, n_qkv = qkv.shape
    gw = GROUP_WIDTH_A
    n_tiles = L // TQ_A
    n_cls = min(dil, max(1, MAX_TILES_PER_STEP_A // n_tiles))
    tiles_per_step = math.gcd(MAX_TILES_PER_STEP_A, n_cls * n_tiles)
    blk_cls = None if n_cls == 1 else n_cls
    cls = lambda b, r: (b, r, 0, 0)
    return pl.pallas_call(
        _attn_a_kernel,
        grid=(B, dil // n_cls),
        in_specs=[pl.BlockSpec((None, blk_cls, L, n_qkv), cls),
                  pl.BlockSpec(biasm.shape, lambda b, r: (0, 0, 0, 0))],
        out_specs=[pl.BlockSpec((None, blk_cls, L, gw), cls),
                   pl.BlockSpec((None, blk_cls, L, LANES), cls)],
        out_shape=[jax.ShapeDtypeStruct((B, dil, L, gw), jnp.bfloat16),
                   jax.ShapeDtypeStruct((B, dil, L, LANES), jnp.float32)],
        scratch_shapes=[
            pltpu.VMEM((tiles_per_step, HEADS_PER_GROUP_A, TQ_A, TK_A), jnp.float32),
            pltpu.VMEM((tiles_per_step, HEADS_PER_GROUP_A, TQ_A, TK_A), jnp.bfloat16)],
        compiler_params=_params("arbitrary", "arbitrary"),
        name=f"attn_a_d{dil}",
    )(qkv, biasm)


def _t5_bucket_np(rel):
    half = N_BUCKETS // 2
    max_exact = half // 2
    base = np.where(rel > 0, half, 0)
    n = np.abs(rel)
    nf = np.maximum(n, 1).astype(np.float32)
    large = max_exact + (np.log(nf / np.float32(max_exact)) / np.float32(math.log(T5_MAX_DISTANCE / max_exact))
                         * np.float32(half - max_exact)).astype(np.int32)
    large = np.minimum(large, half - 1)
    return base + np.where(n < max_exact, n, large)


def _bias_variants(rel_bias_g, dil):
    H = rel_bias_g.shape[1]
    M = 2 * TK_A
    off = np.arange(M) - (TK_A - 1)
    idx = np.where(np.abs(off) <= HALF_WINDOW, _t5_bucket_np(off * dil), N_BUCKETS).astype(np.int32)
    table = jnp.concatenate([rel_bias_g.astype(jnp.float32) * LOG2E, jnp.full((1, H), NEG_INF, jnp.float32)], axis=0)
    w = table[jnp.asarray(idx)].T
    rot = jnp.tile(w, (1, TQ_A + 1))[:, :TQ_A * (M + 1)].reshape(H, TQ_A, M + 1)[:, ::-1, :]
    variants = []
    for shift in (0, HALF_WINDOW, 2 * HALF_WINDOW):
        j0 = TK_A - TQ_A - shift
        variants.append(rot[:, :, j0:j0 + TK_A])
    return jnp.stack(variants)


N_MERGE_A_REFS = 13


def _merge_a(x_ref, mod_ref, o0_ref, o1_ref, o2_ref, l0_ref, l1_ref, l2_ref, gate_ref,
             expand_ref, w_ref, g_ref, b_ref):
    tl, D = x_ref.shape
    stride = GROUP_LANE_STRIDE
    lane = lax.broadcasted_iota(jnp.int32, (tl, LANES), 1)

    def natural_order(o_ref, l_ref):
        if o_ref.shape[0] == 1:
            return o_ref[0].astype(jnp.float32), l_ref[0]
        return (pltpu.einshape("dmn->(md)n", o_ref[...].astype(jnp.float32)),
                pltpu.einshape("dmn->(md)n", l_ref[...]))

    o0, l0 = natural_order(o0_ref, l0_ref)
    o1, l1 = natural_order(o1_ref, l1_ref)
    o2, l2 = natural_order(o2_ref, l2_ref)
    mx = jnp.maximum(jnp.maximum(l0, l1), l2)
    e0, e1, e2 = jnp.exp(l0 - mx), jnp.exp(l1 - mx), jnp.exp(l2 - mx)
    inv = 1.0 / (e0 + e1 + e2)
    wt = jnp.where(lane < stride, e0 * inv,
                   jnp.where(lane < 2 * stride, pltpu.roll(e1 * inv, stride, 1), pltpu.roll(e2 * inv, 2 * stride, 1)))
    hi = wt.astype(jnp.bfloat16)
    lo = (wt - hi.astype(jnp.float32)).astype(jnp.bfloat16)
    wexp = (jnp.dot(hi, expand_ref[...], preferred_element_type=jnp.float32)
            + jnp.dot(lo, expand_ref[...], preferred_element_type=jnp.float32))
    o = jnp.concatenate([o0, o1, o2], axis=1)
    yg = (o * wexp * gate_ref[...].astype(jnp.float32)).astype(jnp.bfloat16)
    y = jnp.dot(yg, w_ref[...], preferred_element_type=jnp.float32)
    z = DEEPNORM_ALPHA * x_ref[...] + mod_ref[:, 2 * D:] * y
    return _layernorm(z, g_ref[...], b_ref[...])


def _rmsnorm(x, g):
    return x * lax.rsqrt(jnp.mean(x * x, axis=-1, keepdims=True) + RMS_EPS) * g


def _project_b(x, mod_ref, w_in_ref, qn_ref, kvn_ref, wqa_ref, wk_ref, wv_ref,
               vone_ref, cq_ref, sq_ref, ck_ref, sk_ref, q_out, k_out, v_out, gate_out):
    u = _modulate(x, mod_ref[...]).astype(jnp.bfloat16)
    z = jnp.dot(u, w_in_ref[...], preferred_element_type=jnp.float32)
    c0 = Q_LORA_RANK
    c1 = c0 + KV_LORA_RANK
    cq = z[:, :c0]
    ckv = z[:, c0:c1]
    kr = z[:, c1:c1 + LANES]
    gate_out[...] = z[:, c1 + LANES:].astype(gate_out.dtype)
    to_rope_lanes = LANES - QK_ROPE_DIM

    cqn = _rmsnorm(cq, qn_ref[...]).astype(jnp.bfloat16)
    qa = jnp.dot(cqn, wqa_ref[...], preferred_element_type=jnp.float32)
    ckvn = _rmsnorm(ckv, kvn_ref[...]).astype(jnp.bfloat16)
    kn = jnp.dot(ckvn, wk_ref[...], preferred_element_type=jnp.float32)
    v_out[...] = (jnp.dot(ckvn, wv_ref[...], preferred_element_type=jnp.float32)
                  + vone_ref[...]).astype(v_out.dtype)

    cos_q, sin_q = cq_ref[...], sq_ref[...]
    k_rope = kr * ck_ref[...] + pltpu.roll(kr, to_rope_lanes, 1) * sk_ref[...]
    for h in range(N_HEADS_B):
        blk = slice(h * LANES, (h + 1) * LANES)
        qh = qa[:, blk]
        q_out[:, blk] = (qh * cos_q + pltpu.roll(qh, to_rope_lanes, 1) * sin_q).astype(q_out.dtype)
        k_out[:, blk] = (kn[:, blk] + k_rope).astype(k_out.dtype)


def _mid_kernel(*refs):
    merge_refs, rest = refs[:N_MERGE_A_REFS], refs[N_MERGE_A_REFS:]
    x1_out, project_refs = rest[-5], rest[:-5] + rest[-4:]
    x1 = _merge_a(*merge_refs)
    x1_out[...] = x1
    _project_b(x1, *project_refs)


def _mid(x, mod0, os_, lses, gate_a, w_out_a, ln_g, ln_b,
         mod1, w_in, q_norm, kv_norm, wqa, wk, wv, tables):
    B, S, D = x.shape
    tl = TL_PROJ
    gw = GROUP_WIDTH_A
    HP = N_HEADS_B * LANES
    col = np.arange(WIDTH_A)
    src_lane = (col // gw) * GROUP_LANE_STRIDE + (col % gw) // HEAD_DIM_A
    expand = jnp.asarray((np.arange(LANES)[:, None] == src_lane[None, :]).astype(np.float32), dtype=jnp.bfloat16)
    row = lambda b, i: (b, i, 0)
    cls = lambda b, i: (b, 0, i, 0)
    const2 = lambda b, i: (0, 0)
    mod_spec = pl.BlockSpec((None, 1, 3 * D), lambda b, i: (b, 0, 0))
    tab = pl.BlockSpec((tl, LANES), lambda b, i: (i, 0))
    o_specs = [pl.BlockSpec((None, dil, tl // dil, gw), cls) for _, dil in DIL_GROUPS]
    l_specs = [pl.BlockSpec((None, dil, tl // dil, LANES), cls) for _, dil in DIL_GROUPS]
    merge_args = [x, mod0, *os_, *lses, gate_a, expand, w_out_a, ln_g.reshape(1, D), ln_b.reshape(1, D)]
    merge_specs = [pl.BlockSpec((None, tl, D), row), mod_spec, *o_specs, *l_specs,
                   pl.BlockSpec((None, tl, WIDTH_A), row),
                   pl.BlockSpec(expand.shape, const2), pl.BlockSpec(w_out_a.shape, const2),
                   pl.BlockSpec((1, D), const2), pl.BlockSpec((1, D), const2)]
    assert len(merge_args) == N_MERGE_A_REFS
    project_args = [mod1, w_in, q_norm.reshape(1, -1), kv_norm.reshape(1, -1), wqa, wk, wv, _v_ones_row(), *tables]
    project_specs = [mod_spec, pl.BlockSpec(w_in.shape, const2),
                     pl.BlockSpec((1, Q_LORA_RANK), const2), pl.BlockSpec((1, KV_LORA_RANK), const2),
                     pl.BlockSpec(wqa.shape, const2), pl.BlockSpec(wk.shape, const2),
                     pl.BlockSpec(wv.shape, const2), pl.BlockSpec((1, HP), const2), tab, tab, tab, tab]
    wide = jax.ShapeDtypeStruct((B, S, HP), jnp.bfloat16)
    return pl.pallas_call(
        _mid_kernel,
        grid=(B, S // tl),
        in_specs=merge_specs + project_specs,
        out_specs=[pl.BlockSpec((None, tl, D), row),
                   pl.BlockSpec((None, tl, HP), row), pl.BlockSpec((None, tl, HP), row),
                   pl.BlockSpec((None, tl, HP), row), pl.BlockSpec((None, tl, WIDTH_B), row)],
        out_shape=[jax.ShapeDtypeStruct((B, S, D), jnp.float32), wide, wide, wide,
                   jax.ShapeDtypeStruct((B, S, WIDTH_B), jnp.bfloat16)],
        compiler_params=_params("arbitrary", "arbitrary"),
        name="mid",
    )(*merge_args, *project_args)


def _attn_b_kernel(q_ref, k_ref, v_ref, gate_ref, o_ref, s_even, s_odd, mx_scr):
    S = k_ref.shape[0]
    tq = s_even.shape[0]
    n_tiles = S // tq
    n_chunks = S // TK_B
    n_heads = q_ref.shape[1] // LANES
    assert n_heads % 2 == 0
    s_bufs = (s_even, s_odd)

    def head_lanes(e):
        return slice(e * LANES, (e + 1) * LANES)

    def tile_rows(t):
        return pl.ds(t * tq, tq) if isinstance(t, int) else pl.ds(pl.multiple_of(t * tq, tq), tq)

    def scores(t, e, c, mx):
        ks = slice(c * TK_B, (c + 1) * TK_B)
        s = lax.dot_general(q_ref[tile_rows(t), head_lanes(e)], k_ref[ks, head_lanes(e)], _NT,
                            preferred_element_type=jnp.float32)
        s_bufs[e % 2][:, ks] = s
        for j in range(TK_B // LANES):
            mx = jnp.maximum(mx, s[:, j * LANES:(j + 1) * LANES])
        return mx

    def weighted_values(e, c, m, acc):
        ks = slice(c * TK_B, (c + 1) * TK_B)
        p = jnp.exp2(s_bufs[e % 2][:, ks] - m).astype(jnp.bfloat16)
        return acc + jnp.dot(p, v_ref[ks, head_lanes(e)], preferred_element_type=jnp.float32)

    neg = jnp.full((tq, LANES), -jnp.inf, jnp.float32)
    zero = jnp.zeros((tq, LANES), jnp.float32)
    lane = lax.broadcasted_iota(jnp.int32, (tq, LANES), 1)

    mx = neg
    for c in range(n_chunks):
        mx = scores(0, 0, c, mx)
    mx_scr[...] = mx

    def tile(t, carry):
        t_next = jnp.minimum(t + 1, n_tiles - 1)
        mx = mx_scr[...]
        acc = []
        for e in range(n_heads):
            m = jnp.max(mx, axis=-1, keepdims=True)
            mx, a = neg, zero
            for c in range(n_chunks):
                mx = scores(t, e + 1, c, mx) if e + 1 < n_heads else scores(t_next, 0, c, mx)
                a = weighted_values(e, c, m, a)
            acc.append(a)
        mx_scr[...] = mx
        for pair in range(n_heads // 2):
            acc_e, acc_o = acc[2 * pair], acc[2 * pair + 1]
            inv_e = 1.0 / acc_e[:, V_HEAD_DIM:V_HEAD_DIM + 1]
            inv_o = 1.0 / acc_o[:, 0:1]
            cols = slice(pair * LANES, (pair + 1) * LANES)
            o = jnp.where(lane < V_HEAD_DIM, acc_e * inv_e, acc_o * inv_o)
            gate = gate_ref[tile_rows(t), cols].astype(jnp.float32)
            o_ref[tile_rows(t), cols] = (o * _silu(gate)).astype(o_ref.dtype)
        return carry

    lax.fori_loop(0, n_tiles, tile, 0)


def _attn_b(q2, k2, v2, gate):
    B, S, HP = q2.shape
    hw = HEADS_PER_STEP_B * LANES
    ow = HEADS_PER_STEP_B * V_HEAD_DIM
    heads = lambda b, p: (b, 0, p)
    return pl.pallas_call(
        _attn_b_kernel,
        grid=(B, N_HEADS_B // HEADS_PER_STEP_B),
        in_specs=[pl.BlockSpec((None, S, hw), heads), pl.BlockSpec((None, S, hw), heads),
                  pl.BlockSpec((None, S, hw), heads), pl.BlockSpec((None, S, ow), heads)],
        out_specs=pl.BlockSpec((None, S, ow), heads),
        out_shape=jax.ShapeDtypeStruct((B, S, WIDTH_B), jnp.bfloat16),
        scratch_shapes=[pltpu.VMEM((TQ_B, S), jnp.float32), pltpu.VMEM((TQ_B, S), jnp.float32),
                        pltpu.VMEM((TQ_B, LANES), jnp.float32)],
        compiler_params=_params("arbitrary", "arbitrary"),
        name="attn_b",
    )(q2, k2, v2, gate)


def _out_b_kernel(x_ref, mod_ref, yg_ref, w_ref, g_ref, b_ref, out_ref):
    D = x_ref.shape[-1]
    y = jnp.dot(yg_ref[...], w_ref[...], preferred_element_type=jnp.float32)
    z = DEEPNORM_ALPHA * x_ref[...] + mod_ref[:, 2 * D:] * y
    out_ref[...] = _layernorm(z, g_ref[...], b_ref[...])


def _out_b(x, mod, yg, w_out, ln_g, ln_b):
    B, S, D = x.shape
    tl = TL_WIDE
    row = lambda b, i: (b, i, 0)
    const2 = lambda b, i: (0, 0)
    return pl.pallas_call(
        _out_b_kernel,
        grid=(B, S // tl),
        in_specs=[pl.BlockSpec((None, tl, D), row),
                  pl.BlockSpec((None, 1, 3 * D), lambda b, i: (b, 0, 0)),
                  pl.BlockSpec((None, tl, WIDTH_B), row),
                  pl.BlockSpec(w_out.shape, const2),
                  pl.BlockSpec((1, D), const2), pl.BlockSpec((1, D), const2)],
        out_specs=pl.BlockSpec((None, tl, D), row),
        out_shape=jax.ShapeDtypeStruct((B, S, D), jnp.float32),
        compiler_params=_params("arbitrary", "arbitrary"),
        name="out_b",
    )(x, mod, yg, w_out, ln_g.reshape(1, D), ln_b.reshape(1, D))


def _weights_a(a_w_in):
    W = WIDTH_A
    gw = GROUP_WIDTH_A
    q, k, v, gate = (a_w_in[:, i * W:(i + 1) * W] for i in range(4))
    ws = []
    for g in range(len(DIL_GROUPS)):
        cs = slice(g * gw, (g + 1) * gw)
        parts = [q[:, cs] * (HEAD_DIM_A ** -0.5 * LOG2E), k[:, cs], v[:, cs]]
        if g == 0:
            parts.append(gate)
        ws.append(jnp.concatenate(parts, axis=1).astype(jnp.bfloat16))
    return ws


def _swap_halves(w):
    half = w.shape[-1] // 2
    return jnp.concatenate([w[..., half:], w[..., :half]], axis=-1)


def _pad_lanes(w, lo):
    n = w.shape[-1]
    return jnp.pad(w, [(0, 0)] * (w.ndim - 1) + [(lo, LANES - lo - n)])


def _weights_b(b_w_in, b_w_uq, b_w_ukv):
    c0 = Q_LORA_RANK
    c1 = c0 + KV_LORA_RANK
    c2 = c1 + QK_ROPE_DIM
    kr = b_w_in[:, c1:c2]
    w_in = jnp.concatenate([b_w_in[:, :c1], _pad_lanes(jnp.concatenate([kr, _swap_halves(kr)], axis=1), QK_NOPE_DIM),
                            b_w_in[:, c2:]], axis=1)
    uq = b_w_uq.reshape(Q_LORA_RANK, N_HEADS_B, QK_NOPE_DIM + QK_ROPE_DIM)
    wqa = jnp.concatenate([uq, _swap_halves(uq[..., QK_NOPE_DIM:])], axis=-1).reshape(Q_LORA_RANK, -1)
    ukv = b_w_ukv.reshape(KV_LORA_RANK, N_HEADS_B, QK_NOPE_DIM + V_HEAD_DIM)
    wk = _pad_lanes(ukv[..., :QK_NOPE_DIM], 0).reshape(KV_LORA_RANK, -1)
    vv = ukv[..., QK_NOPE_DIM:]
    odd = (jnp.arange(N_HEADS_B) % 2 == 1)[None, :, None]
    wv = jnp.where(odd, _pad_lanes(vv, LANES - V_HEAD_DIM), _pad_lanes(vv, 0)).reshape(KV_LORA_RANK, -1)
    bf = lambda w: w.astype(jnp.bfloat16)
    return bf(w_in), bf(wqa), bf(wk), bf(wv)


def _rope_tables(S):
    pos = jnp.arange(S, dtype=jnp.float32)
    inv_freq = ROPE_THETA ** (-jnp.arange(0, QK_ROPE_DIM, 2, dtype=jnp.float32) / QK_ROPE_DIM)
    ang = pos[:, None] * inv_freq[None, :]
    cos, sin = jnp.cos(ang), jnp.sin(ang)
    cosf = _pad_lanes(jnp.concatenate([cos, cos], axis=-1), QK_NOPE_DIM)
    sinf = _pad_lanes(jnp.concatenate([-sin, sin], axis=-1), QK_NOPE_DIM)
    nope = _pad_lanes(jnp.ones((S, QK_NOPE_DIM), jnp.float32), 0)
    scale = (QK_NOPE_DIM + QK_ROPE_DIM) ** -0.5 * LOG2E
    return (scale * (nope + cosf), scale * sinf, cosf, sinf)


def _v_ones_row():
    row = np.zeros((N_HEADS_B, LANES), np.float32)
    row[0::2, V_HEAD_DIM] = 1.0
    row[1::2, 0] = 1.0
    return jnp.asarray(row.reshape(1, -1))


def kernel(x, c, rel_bias, ada_w, ada_b, ln_g, ln_b, a_w_in, a_w_out,
           b_w_in, b_q_norm, b_w_uq, b_kv_norm, b_w_ukv, b_w_out):
    B, S, D = x.shape
    mods = _modulation(c, ada_w, ada_b)
    mod0 = mods[0].reshape(B, 1, 3 * D)
    mod1 = mods[1].reshape(B, 1, 3 * D)

    for window, dil in DIL_GROUPS:
        assert window // (2 * dil) == HALF_WINDOW
    *qkvs, gate_a = _proj_a(x, mod0, _weights_a(a_w_in[0]))
    os_, lses = [], []
    for g, ((_, dil), qkv) in enumerate(zip(DIL_GROUPS, qkvs)):
        hs = slice(g * HEADS_PER_GROUP_A, (g + 1) * HEADS_PER_GROUP_A)
        o, lse = _attn_a(qkv, _bias_variants(rel_bias[:, hs], dil))
        os_.append(o)
        lses.append(lse)

    w_in, wqa, wk, wv = _weights_b(b_w_in[0], b_w_uq[0], b_w_ukv[0])
    x1, q2, k2, v2, gate_b = _mid(x, mod0, os_, lses, gate_a, a_w_out[0].astype(jnp.bfloat16), ln_g[0], ln_b[0],
                                  mod1, w_in, b_q_norm[0], b_kv_norm[0], wqa, wk, wv, _rope_tables(S))
    yg = _attn_b(q2, k2, v2, gate_b)
    return _out_b(x1, mod1, yg, b_w_out[0].astype(jnp.bfloat16), ln_g[1], ln_b[1])
```

```python
import math

import numpy as np
import jax
import jax.numpy as jnp
from jax import lax
from jax.experimental import pallas as pl
from jax.experimental.pallas import tpu as pltpu

DEPTH = 2
HEAD_DIM_A = 64
DIL_GROUPS = ((128, 1), (512, 4), (2048, 16))
HEADS_PER_GROUP_A = 6
GROUP_WIDTH_A = HEADS_PER_GROUP_A * HEAD_DIM_A
WIDTH_A = GROUP_WIDTH_A * len(DIL_GROUPS)
N_BUCKETS = 32
T5_MAX_DISTANCE = 1024
N_HEADS_B = 16
QK_NOPE_DIM = 64
QK_ROPE_DIM = 32
V_HEAD_DIM = 64
Q_LORA_RANK = 256
KV_LORA_RANK = 128
WIDTH_B = N_HEADS_B * V_HEAD_DIM
ROPE_THETA = 10000.0
DEEPNORM_ALPHA = (2.0 * DEPTH) ** 0.25
LN_EPS = 1e-5
RMS_EPS = 1e-6
NEG_INF = -1e30
LOG2E = math.log2(math.e)
LN2 = math.log(2.0)

LANES = 128
VMEM_LIMIT_BYTES = 56 * 1024 * 1024

HALF_WINDOW = 64
TQ_A = 128
TK_A = TQ_A + 2 * HALF_WINDOW
GROUP_LANE_STRIDE = 8
MAX_TILES_PER_STEP_A = 4
TL_PROJ = 512
TL_WIDE = 1024
TQ_B = 512
TK_B = 512
HEADS_PER_STEP_B = 4

_NT = (((1,), (1,)), ((), ()))


def _silu(x):
    return x * (1.0 / (1.0 + jnp.exp(-x)))


def _params(*sem):
    return pltpu.CompilerParams(dimension_semantics=sem, vmem_limit_bytes=VMEM_LIMIT_BYTES)


def _mod_kernel(c_ref, w_ref, b_ref, o_ref):
    sc = _silu(c_ref[...]).astype(jnp.bfloat16)
    o_ref[...] = jnp.dot(sc, w_ref[...].astype(jnp.bfloat16),
                         preferred_element_type=jnp.float32) + b_ref[...]


def _modulation(c, ada_w, ada_b):
    B, D = c.shape
    nj = 3
    return pl.pallas_call(
        _mod_kernel,
        grid=(DEPTH, nj),
        in_specs=[pl.BlockSpec((B, D), lambda i, j: (0, 0)),
                  pl.BlockSpec((None, D, D), lambda i, j: (i, 0, j)),
                  pl.BlockSpec((None, 1, D), lambda i, j: (i, 0, j))],
        out_specs=pl.BlockSpec((None, B, D), lambda i, j: (i, 0, j)),
        out_shape=jax.ShapeDtypeStruct((DEPTH, B, 3 * D), jnp.float32),
        compiler_params=_params("arbitrary", "arbitrary"),
        name="modulation",
    )(c, ada_w, ada_b.reshape(DEPTH, 1, 3 * D))


def _modulate(x, mod):
    D = x.shape[-1]
    return x * (1.0 + mod[:, D:2 * D]) + mod[:, :D]


def _layernorm(z, g, b):
    mu = jnp.mean(z, axis=-1, keepdims=True)
    zc = z - mu
    var = jnp.mean(zc * zc, axis=-1, keepdims=True)
    return zc * lax.rsqrt(var + LN_EPS) * g + b


def _proj_a_kernel(x_ref, mod_ref, w0_ref, w1_ref, w2_ref, qkv0_ref, qkv1_ref, qkv2_ref, gate_ref):
    u = _modulate(x_ref[...], mod_ref[...]).astype(jnp.bfloat16)
    n_qkv = qkv0_ref.shape[-1]
    z = jnp.dot(u, w0_ref[...], preferred_element_type=jnp.float32)
    qkv0_ref[0] = z[:, :n_qkv].astype(qkv0_ref.dtype)
    gate_ref[...] = _silu(z[:, n_qkv:]).astype(gate_ref.dtype)
    for w_ref, qkv_ref in ((w1_ref, qkv1_ref), (w2_ref, qkv2_ref)):
        dil, rows = qkv_ref.shape[:2]
        up = pltpu.einshape("(md)n->(dm)n", u, d=dil)
        z = jnp.dot(up, w_ref[...], preferred_element_type=jnp.float32)
        for r in range(dil):
            qkv_ref[r] = z[r * rows:(r + 1) * rows].astype(qkv_ref.dtype)


def _proj_a(x, mod, ws):
    B, S, D = x.shape
    tl = TL_WIDE
    n_qkv = 3 * GROUP_WIDTH_A
    dils = [dil for _, dil in DIL_GROUPS]
    assert dils[0] == 1
    row = lambda b, i: (b, i, 0)
    cls = lambda b, i: (b, 0, i, 0)
    const2 = lambda b, i: (0, 0)
    return pl.pallas_call(
        _proj_a_kernel,
        grid=(B, S // tl),
        in_specs=[pl.BlockSpec((None, tl, D), row),
                  pl.BlockSpec((None, 1, 3 * D), lambda b, i: (b, 0, 0)),
                  *[pl.BlockSpec(w.shape, const2) for w in ws]],
        out_specs=[*[pl.BlockSpec((None, dil, tl // dil, n_qkv), cls) for dil in dils],
                   pl.BlockSpec((None, tl, WIDTH_A), row)],
        out_shape=[*[jax.ShapeDtypeStruct((B, dil, S // dil, n_qkv), jnp.bfloat16) for dil in dils],
                   jax.ShapeDtypeStruct((B, S, WIDTH_A), jnp.bfloat16)],
        compiler_params=_params("arbitrary", "arbitrary"),
        name="proj_a",
    )(x, mod, *ws)


def _attn_a_kernel(qkv_ref, bias_ref, o_ref, lse_ref, s_scr, p_scr):
    n_cls, L = (1,) + qkv_ref.shape[:1] if len(qkv_ref.shape) == 2 else qkv_ref.shape[:2]

    def of_class(ref, cls):
        return ref if len(ref.shape) == 2 else ref.at[cls]
    n_tiles = L // TQ_A
    gw = GROUP_WIDTH_A
    lane = lax.broadcasted_iota(jnp.int32, (TQ_A, LANES), 1)
    low_q = lane < HEAD_DIM_A
    low_k = lax.broadcasted_iota(jnp.int32, (TK_A, LANES), 1) < HEAD_DIM_A

    heads = range(HEADS_PER_GROUP_A)

    def pair_block(cls, rows, section, h):
        c0 = section * gw + (h // 2) * LANES
        return of_class(qkv_ref, cls)[rows, c0:c0 + LANES]

    def other_lanes(block, low_mask, h, fill):
        mine = low_mask if h % 2 == 0 else jnp.logical_not(low_mask)
        return jnp.where(mine, block, jnp.full_like(block, fill))

    tiles_per_step = s_scr.shape[0]

    def tile_group(tg, carry):
        geo = []
        for u in range(tiles_per_step):
            item = tg * tiles_per_step + u
            cls, t = (0, item) if n_cls == 1 else (item // n_tiles, item % n_tiles)
            j0 = pl.multiple_of(t * TQ_A, TQ_A)
            ws = pl.multiple_of(jnp.clip(j0 - HALF_WINDOW, 0, L - TK_A), HALF_WINDOW)
            variant = jnp.where(t == 0, 0, jnp.where(t == n_tiles - 1, 2, 1))
            geo.append((cls, pl.ds(j0, TQ_A), pl.ds(ws, TK_A), variant))
        for u, (cls, rq, rk, variant) in enumerate(geo):
            for h in heads:
                q = other_lanes(pair_block(cls, rq, 0, h), low_q, h, 0)
                s = lax.dot_general(q, pair_block(cls, rk, 1, h), _NT, preferred_element_type=jnp.float32)
                s_scr[u, h] = s + bias_ref[variant, h]
        m = [[jnp.max(s_scr[u, h], axis=-1, keepdims=True) for h in heads] for u in range(len(geo))]
        for u in range(len(geo)):
            for h in heads:
                p_scr[u, h] = jnp.exp2(s_scr[u, h] - m[u][h]).astype(jnp.bfloat16)
        for u, (cls, rq, rk, variant) in enumerate(geo):
            lse_tile = jnp.zeros((TQ_A, LANES), jnp.float32)
            for pair in range(HEADS_PER_GROUP_A // 2):
                he, ho = 2 * pair, 2 * pair + 1
                acc_e = jnp.dot(p_scr[u, he], other_lanes(pair_block(cls, rk, 2, he), low_k, he, 1),
                                preferred_element_type=jnp.float32)
                acc_o = jnp.dot(p_scr[u, ho], other_lanes(pair_block(cls, rk, 2, ho), low_k, ho, 1),
                                preferred_element_type=jnp.float32)
                numer = jnp.where(low_q, acc_e, acc_o)
                den_other = jnp.where(low_q, acc_o, acc_e)
                den = pltpu.roll(den_other, HEAD_DIM_A, 1)
                of_class(o_ref, cls)[rq, pair * LANES:(pair + 1) * LANES] = (numer / den).astype(o_ref.dtype)
                is_odd = lane == ho
                lse = (jnp.where(is_odd, m[u][ho], m[u][he])
                       + jnp.log2(jnp.where(is_odd, den_other, den))) * LN2
                lse_tile = jnp.where(jnp.logical_or(is_odd, lane == he), lse, lse_tile)
            of_class(lse_ref, cls)[rq, :] = lse_tile
        return carry

    lax.fori_loop(0, n_cls * n_tiles // tiles_per_step, tile_group, 0)


def _attn_a(qkv, biasm):
    B, dil, L, n_qkv = qkv.shape
    gw = GROUP_WIDTH_A
    n_tiles = L // TQ_A
    n_cls = min(dil, max(1, MAX_TILES_PER_STEP_A // n_tiles))
    tiles_per_step = math.gcd(MAX_TILES_PER_STEP_A, n_cls * n_tiles)
    blk_cls = None if n_cls == 1 else n_cls
    cls = lambda b, r: (b, r, 0, 0)
    return pl.pallas_call(
        _attn_a_kernel,
        grid=(B, dil // n_cls),
        in_specs=[pl.BlockSpec((None, blk_cls, L, n_qkv), cls),
                  pl.BlockSpec(biasm.shape, lambda b, r: (0, 0, 0, 0))],
        out_specs=[pl.BlockSpec((None, blk_cls, L, gw), cls),
                   pl.BlockSpec((None, blk_cls, L, LANES), cls)],
        out_shape=[jax.ShapeDtypeStruct((B, dil, L, gw), jnp.bfloat16),
                   jax.ShapeDtypeStruct((B, dil, L, LANES), jnp.float32)],
        scratch_shapes=[
            pltpu.VMEM((tiles_per_step, HEADS_PER_GROUP_A, TQ_A, TK_A), jnp.float32),
            pltpu.VMEM((tiles_per_step, HEADS_PER_GROUP_A, TQ_A, TK_A), jnp.bfloat16)],
        compiler_params=_params("arbitrary", "arbitrary"),
        name=f"attn_a_d{dil}",
    )(qkv, biasm)


def _t5_bucket_np(rel):
    half = N_BUCKETS // 2
    max_exact = half // 2
    base = np.where(rel > 0, half, 0)
    n = np.abs(rel)
    nf = np.maximum(n, 1).astype(np.float32)
    large = max_exact + (np.log(nf / np.float32(max_exact)) / np.float32(math.log(T5_MAX_DISTANCE / max_exact))
                         * np.float32(half - max_exact)).astype(np.int32)
    large = np.minimum(large, half - 1)
    return base + np.where(n < max_exact, n, large)


def _bias_variants(rel_bias_g, dil):
    H = rel_bias_g.shape[1]
    M = 2 * TK_A
    off = np.arange(M) - (TK_A - 1)
    idx = np.where(np.abs(off) <= HALF_WINDOW, _t5_bucket_np(off * dil), N_BUCKETS).astype(np.int32)
    table = jnp.concatenate([rel_bias_g.astype(jnp.float32) * LOG2E, jnp.full((1, H), NEG_INF, jnp.float32)], axis=0)
    w = table[jnp.asarray(idx)].T
    rot = jnp.tile(w, (1, TQ_A + 1))[:, :TQ_A * (M + 1)].reshape(H, TQ_A, M + 1)[:, ::-1, :]
    variants = []
    for shift in (0, HALF_WINDOW, 2 * HALF_WINDOW):
        j0 = TK_A - TQ_A - shift
        variants.append(rot[:, :, j0:j0 + TK_A])
    return jnp.stack(variants)


N_MERGE_A_REFS = 13


def _merge_a(x_ref, mod_ref, o0_ref, o1_ref, o2_ref, l0_ref, l1_ref, l2_ref, gate_ref,
             expand_ref, w_ref, g_ref, b_ref):
    tl, D = x_ref.shape
    stride = GROUP_LANE_STRIDE
    lane = lax.broadcasted_iota(jnp.int32, (tl, LANES), 1)

    def natural_order(o_ref, l_ref):
        if o_ref.shape[0] == 1:
            return o_ref[0].astype(jnp.float32), l_ref[0]
        return (pltpu.einshape("dmn->(md)n", o_ref[...].astype(jnp.float32)),
                pltpu.einshape("dmn->(md)n", l_ref[...]))

    o0, l0 = natural_order(o0_ref, l0_ref)
    o1, l1 = natural_order(o1_ref, l1_ref)
    o2, l2 = natural_order(o2_ref, l2_ref)
    mx = jnp.maximum(jnp.maximum(l0, l1), l2)
    e0, e1, e2 = jnp.exp(l0 - mx), jnp.exp(l1 - mx), jnp.exp(l2 - mx)
    inv = 1.0 / (e0 + e1 + e2)
    wt = jnp.where(lane < stride, e0 * inv,
                   jnp.where(lane < 2 * stride, pltpu.roll(e1 * inv, stride, 1), pltpu.roll(e2 * inv, 2 * stride, 1)))
    hi = wt.astype(jnp.bfloat16)
    lo = (wt - hi.astype(jnp.float32)).astype(jnp.bfloat16)
    wexp = (jnp.dot(hi, expand_ref[...], preferred_element_type=jnp.float32)
            + jnp.dot(lo, expand_ref[...], preferred_element_type=jnp.float32))
    o = jnp.concatenate([o0, o1, o2], axis=1)
    yg = (o * wexp * gate_ref[...].astype(jnp.float32)).astype(jnp.bfloat16)
    y = jnp.dot(yg, w_ref[...], preferred_element_type=jnp.float32)
    z = DEEPNORM_ALPHA * x_ref[...] + mod_ref[:, 2 * D:] * y
    return _layernorm(z, g_ref[...], b_ref[...])


def _rmsnorm(x, g):
    return x * lax.rsqrt(jnp.mean(x * x, axis=-1, keepdims=True) + RMS_EPS) * g


def _project_b(x, mod_ref, w_in_ref, qn_ref, kvn_ref, wqa_ref, wk_ref, wv_ref,
               vone_ref, cq_ref, sq_ref, ck_ref, sk_ref, q_out, k_out, v_out, gate_out):
    u = _modulate(x, mod_ref[...]).astype(jnp.bfloat16)
    z = jnp.dot(u, w_in_ref[...], preferred_element_type=jnp.float32)
    c0 = Q_LORA_RANK
    c1 = c0 + KV_LORA_RANK
    cq = z[:, :c0]
    ckv = z[:, c0:c1]
    kr = z[:, c1:c1 + LANES]
    gate_out[...] = z[:, c1 + LANES:].astype(gate_out.dtype)
    to_rope_lanes = LANES - QK_ROPE_DIM

    cqn = _rmsnorm(cq, qn_ref[...]).astype(jnp.bfloat16)
    qa = jnp.dot(cqn, wqa_ref[...], preferred_element_type=jnp.float32)
    ckvn = _rmsnorm(ckv, kvn_ref[...]).astype(jnp.bfloat16)
    kn = jnp.dot(ckvn, wk_ref[...], preferred_element_type=jnp.float32)
    v_out[...] = (jnp.dot(ckvn, wv_ref[...], preferred_element_type=jnp.float32)
                  + vone_ref[...]).astype(v_out.dtype)

    cos_q, sin_q = cq_ref[...], sq_ref[...]
    k_rope = kr * ck_ref[...] + pltpu.roll(kr, to_rope_lanes, 1) * sk_ref[...]
    for h in range(N_HEADS_B):
        blk = slice(h * LANES, (h + 1) * LANES)
        qh = qa[:, blk]
        q_out[:, blk] = (qh * cos_q + pltpu.roll(qh, to_rope_lanes, 1) * sin_q).astype(q_out.dtype)
        k_out[:, blk] = (kn[:, blk] + k_rope).astype(k_out.dtype)


def _mid_kernel(*refs):
    merge_refs, rest = refs[:N_MERGE_A_REFS], refs[N_MERGE_A_REFS:]
    x1_out, project_refs = rest[-5], rest[:-5] + rest[-4:]
    x1 = _merge_a(*merge_refs)
    x1_out[...] = x1
    _project_b(x1, *project_refs)


def _mid(x, mod0, os_, lses, gate_a, w_out_a, ln_g, ln_b,
         mod1, w_in, q_norm, kv_norm, wqa, wk, wv, tables):
    B, S, D = x.shape
    tl = TL_PROJ
    gw = GROUP_WIDTH_A
    HP = N_HEADS_B * LANES
    col = np.arange(WIDTH_A)
    src_lane = (col // gw) * GROUP_LANE_STRIDE + (col % gw) // HEAD_DIM_A
    expand = jnp.asarray((np.arange(LANES)[:, None] == src_lane[None, :]).astype(np.float32), dtype=jnp.bfloat16)
    row = lambda b, i: (b, i, 0)
    cls = lambda b, i: (b, 0, i, 0)
    const2 = lambda b, i: (0, 0)
    mod_spec = pl.BlockSpec((None, 1, 3 * D), lambda b, i: (b, 0, 0))
    tab = pl.BlockSpec((tl, LANES), lambda b, i: (i, 0))
    o_specs = [pl.BlockSpec((None, dil, tl // dil, gw), cls) for _, dil in DIL_GROUPS]
    l_specs = [pl.BlockSpec((None, dil, tl // dil, LANES), cls) for _, dil in DIL_GROUPS]
    merge_args = [x, mod0, *os_, *lses, gate_a, expand, w_out_a, ln_g.reshape(1, D), ln_b.reshape(1, D)]
    merge_specs = [pl.BlockSpec((None, tl, D), row), mod_spec, *o_specs, *l_specs,
                   pl.BlockSpec((None, tl, WIDTH_A), row),
                   pl.BlockSpec(expand.shape, const2), pl.BlockSpec(w_out_a.shape, const2),
                   pl.BlockSpec((1, D), const2), pl.BlockSpec((1, D), const2)]
    assert len(merge_args) == N_MERGE_A_REFS
    project_args = [mod1, w_in, q_norm.reshape(1, -1), kv_norm.reshape(1, -1), wqa, wk, wv, _v_ones_row(), *tables]
    project_specs = [mod_spec, pl.BlockSpec(w_in.shape, const2),
                     pl.BlockSpec((1, Q_LORA_RANK), const2), pl.BlockSpec((1, KV_LORA_RANK), const2),
                     pl.BlockSpec(wqa.shape, const2), pl.BlockSpec(wk.shape, const2),
                     pl.BlockSpec(wv.shape, const2), pl.BlockSpec((1, HP), const2), tab, tab, tab, tab]
    wide = jax.ShapeDtypeStruct((B, S, HP), jnp.bfloat16)
    return pl.pallas_call(
        _mid_kernel,
        grid=(B, S // tl),
        in_specs=merge_specs + project_specs,
        out_specs=[pl.BlockSpec((None, tl, D), row),
                   pl.BlockSpec((None, tl, HP), row), pl.BlockSpec((None, tl, HP), row),
                   pl.BlockSpec((None, tl, HP), row), pl.BlockSpec((None, tl, WIDTH_B), row)],
        out_shape=[jax.ShapeDtypeStruct((B, S, D), jnp.float32), wide, wide, wide,
                   jax.ShapeDtypeStruct((B, S, WIDTH_B), jnp.bfloat16)],
        compiler_params=_params("arbitrary", "arbitrary"),
        name="mid",
    )(*merge_args, *project_args)


def _attn_b_kernel(q_ref, k_ref, v_ref, gate_ref, o_ref, s_even, s_odd, mx_scr):
    S = k_ref.shape[0]
    tq = s_even.shape[0]
    n_tiles = S // tq
    n_chunks = S // TK_B
    n_heads = q_ref.shape[1] // LANES
    assert n_heads % 2 == 0
    s_bufs = (s_even, s_odd)

    def head_lanes(e):
        return slice(e * LANES, (e + 1) * LANES)

    def tile_rows(t):
        return pl.ds(t * tq, tq) if isinstance(t, int) else pl.ds(pl.multiple_of(t * tq, tq), tq)

    def scores(t, e, c, mx):
        ks = slice(c * TK_B, (c + 1) * TK_B)
        s = lax.dot_general(q_ref[tile_rows(t), head_lanes(e)], k_ref[ks, head_lanes(e)], _NT,
                            preferred_element_type=jnp.float32)
        s_bufs[e % 2][:, ks] = s
        for j in range(TK_B // LANES):
            mx = jnp.maximum(mx, s[:, j * LANES:(j + 1) * LANES])
        return mx

    def weighted_values(e, c, m, acc):
        ks = slice(c * TK_B, (c + 1) * TK_B)
        p = jnp.exp2(s_bufs[e % 2][:, ks] - m).astype(jnp.bfloat16)
        return acc + jnp.dot(p, v_ref[ks, head_lanes(e)], preferred_element_type=jnp.float32)

    neg = jnp.full((tq, LANES), -jnp.inf, jnp.float32)
    zero = jnp.zeros((tq, LANES), jnp.float32)
    lane = lax.broadcasted_iota(jnp.int32, (tq, LANES), 1)

    mx = neg
    for c in range(n_chunks):
        mx = scores(0, 0, c, mx)
    mx_scr[...] = mx

    def tile(t, carry):
        t_next = jnp.minimum(t + 1, n_tiles - 1)
        mx = mx_scr[...]
        acc = []
        for e in range(n_heads):
            m = jnp.max(mx, axis=-1, keepdims=True)
            mx, a = neg, zero
            for c in range(n_chunks):
                mx = scores(t, e + 1, c, mx) if e + 1 < n_heads else scores(t_next, 0, c, mx)
                a = weighted_values(e, c, m, a)
            acc.append(a)
        mx_scr[...] = mx
        for pair in range(n_heads // 2):
            acc_e, acc_o = acc[2 * pair], acc[2 * pair + 1]
            inv_e = 1.0 / acc_e[:, V_HEAD_DIM:V_HEAD_DIM + 1]
            inv_o = 1.0 / acc_o[:, 0:1]
            cols = slice(pair * LANES, (pair + 1) * LANES)
            o = jnp.where(lane < V_HEAD_DIM, acc_e * inv_e, acc_o * inv_o)
            gate = gate_ref[tile_rows(t), cols].astype(jnp.float32)
            o_ref[tile_rows(t), cols] = (o * _silu(gate)).astype(o_ref.dtype)
        return carry

    lax.fori_loop(0, n_tiles, tile, 0)


def _attn_b(q2, k2, v2, gate):
    B, S, HP = q2.shape
    hw = HEADS_PER_STEP_B * LANES
    ow = HEADS_PER_STEP_B * V_HEAD_DIM
    heads = lambda b, p: (b, 0, p)
    return pl.pallas_call(
        _attn_b_kernel,
        grid=(B, N_HEADS_B // HEADS_PER_STEP_B),
        in_specs=[pl.BlockSpec((None, S, hw), heads), pl.BlockSpec((None, S, hw), heads),
                  pl.BlockSpec((None, S, hw), heads), pl.BlockSpec((None, S, ow), heads)],
        out_specs=pl.BlockSpec((None, S, ow), heads),
        out_shape=jax.ShapeDtypeStruct((B, S, WIDTH_B), jnp.bfloat16),
        scratch_shapes=[pltpu.VMEM((TQ_B, S), jnp.float32), pltpu.VMEM((TQ_B, S), jnp.float32),
                        pltpu.VMEM((TQ_B, LANES), jnp.float32)],
        compiler_params=_params("arbitrary", "arbitrary"),
        name="attn_b",
    )(q2, k2, v2, gate)


def _out_b_kernel(x_ref, mod_ref, yg_ref, w_ref, g_ref, b_ref, out_ref):
    D = x_ref.shape[-1]
    y = jnp.dot(yg_ref[...], w_ref[...], preferred_element_type=jnp.float32)
    z = DEEPNORM_ALPHA * x_ref[...] + mod_ref[:, 2 * D:] * y
    out_ref[...] = _layernorm(z, g_ref[...], b_ref[...])


def _out_b(x, mod, yg, w_out, ln_g, ln_b):
    B, S, D = x.shape
    tl = TL_WIDE
    row = lambda b, i: (b, i, 0)
    const2 = lambda b, i: (0, 0)
    return pl.pallas_call(
        _out_b_kernel,
        grid=(B, S // tl),
        in_specs=[pl.BlockSpec((None, tl, D), row),
                  pl.BlockSpec((None, 1, 3 * D), lambda b, i: (b, 0, 0)),
                  pl.BlockSpec((None, tl, WIDTH_B), row),
                  pl.BlockSpec(w_out.shape, const2),
                  pl.BlockSpec((1, D), const2), pl.BlockSpec((1, D), const2)],
        out_specs=pl.BlockSpec((None, tl, D), row),
        out_shape=jax.ShapeDtypeStruct((B, S, D), jnp.float32),
        compiler_params=_params("arbitrary", "arbitrary"),
        name="out_b",
    )(x, mod, yg, w_out, ln_g.reshape(1, D), ln_b.reshape(1, D))


def _weights_a(a_w_in):
    W = WIDTH_A
    gw = GROUP_WIDTH_A
    q, k, v, gate = (a_w_in[:, i * W:(i + 1) * W] for i in range(4))
    ws = []
    for g in range(len(DIL_GROUPS)):
        cs = slice(g * gw, (g + 1) * gw)
        parts = [q[:, cs] * (HEAD_DIM_A ** -0.5 * LOG2E), k[:, cs], v[:, cs]]
        if g == 0:
            parts.append(gate)
        ws.append(jnp.concatenate(parts, axis=1).astype(jnp.bfloat16))
    return ws


def _swap_halves(w):
    half = w.shape[-1] // 2
    return jnp.concatenate([w[..., half:], w[..., :half]], axis=-1)


def _pad_lanes(w, lo):
    n = w.shape[-1]
    return jnp.pad(w, [(0, 0)] * (w.ndim - 1) + [(lo, LANES - lo - n)])


def _weights_b(b_w_in, b_w_uq, b_w_ukv):
    c0 = Q_LORA_RANK
    c1 = c0 + KV_LORA_RANK
    c2 = c1 + QK_ROPE_DIM
    kr = b_w_in[:, c1:c2]
    w_in = jnp.concatenate([b_w_in[:, :c1], _pad_lanes(jnp.concatenate([kr, _swap_halves(kr)], axis=1), QK_NOPE_DIM),
                            b_w_in[:, c2:]], axis=1)
    uq = b_w_uq.reshape(Q_LORA_RANK, N_HEADS_B, QK_NOPE_DIM + QK_ROPE_DIM)
    wqa = jnp.concatenate([uq, _swap_halves(uq[..., QK_NOPE_DIM:])], axis=-1).reshape(Q_LORA_RANK, -1)
    ukv = b_w_ukv.reshape(KV_LORA_RANK, N_HEADS_B, QK_NOPE_DIM + V_HEAD_DIM)
    wk = _pad_lanes(ukv[..., :QK_NOPE_DIM], 0).reshape(KV_LORA_RANK, -1)
    vv = ukv[..., QK_NOPE_DIM:]
    odd = (jnp.arange(N_HEADS_B) % 2 == 1)[None, :, None]
    wv = jnp.where(odd, _pad_lanes(vv, LANES - V_HEAD_DIM), _pad_lanes(vv, 0)).reshape(KV_LORA_RANK, -1)
    bf = lambda w: w.astype(jnp.bfloat16)
    return bf(w_in), bf(wqa), bf(wk), bf(wv)


def _rope_tables(S):
    pos = jnp.arange(S, dtype=jnp.float32)
    inv_freq = ROPE_THETA ** (-jnp.arange(0, QK_ROPE_DIM, 2, dtype=jnp.float32) / QK_ROPE_DIM)
    ang = pos[:, None] * inv_freq[None, :]
    cos, sin = jnp.cos(ang), jnp.sin(ang)
    cosf = _pad_lanes(jnp.concatenate([cos, cos], axis=-1), QK_NOPE_DIM)
    sinf = _pad_lanes(jnp.concatenate([-sin, sin], axis=-1), QK_NOPE_DIM)
    nope = _pad_lanes(jnp.ones((S, QK_NOPE_DIM), jnp.float32), 0)
    scale = (QK_NOPE_DIM + QK_ROPE_DIM) ** -0.5 * LOG2E
    return (scale * (nope + cosf), scale * sinf, cosf, sinf)


def _v_ones_row():
    row = np.zeros((N_HEADS_B, LANES), np.float32)
    row[0::2, V_HEAD_DIM] = 1.0
    row[1::2, 0] = 1.0
    return jnp.asarray(row.reshape(1, -1))


def kernel(x, c, rel_bias, ada_w, ada_b, ln_g, ln_b, a_w_in, a_w_out,
           b_w_in, b_q_norm, b_w_uq, b_kv_norm, b_w_ukv, b_w_out):
    B, S, D = x.shape
    mods = _modulation(c, ada_w, ada_b)
    mod0 = mods[0].reshape(B, 1, 3 * D)
    mod1 = mods[1].reshape(B, 1, 3 * D)

    for window, dil in DIL_GROUPS:
        assert window // (2 * dil) == HALF_WINDOW
    *qkvs, gate_a = _proj_a(x, mod0, _weights_a(a_w_in[0]))
    os_, lses = [], []
    for g, ((_, dil), qkv) in enumerate(zip(DIL_GROUPS, qkvs)):
        hs = slice(g * HEADS_PER_GROUP_A, (g + 1) * HEADS_PER_GROUP_A)
        o, lse = _attn_a(qkv, _bias_variants(rel_bias[:, hs], dil))
        os_.append(o)
        lses.append(lse)

    w_in, wqa, wk, wv = _weights_b(b_w_in[0], b_w_uq[0], b_w_ukv[0])
    x1, q2, k2, v2, gate_b = _mid(x, mod0, os_, lses, gate_a, a_w_out[0].astype(jnp.bfloat16), ln_g[0], ln_b[0],
                                  mod1, w_in, b_q_norm[0], b_kv_norm[0], wqa, wk, wv, _rope_tables(S))
    yg = _attn_b(q2, k2, v2, gate_b)
    return _out_b(x1, mod1, yg, b_w_out[0].astype(jnp.bfloat16), ln_g[1], ln_b[1])
```

```python
import math

import numpy as np
import jax
import jax.numpy as jnp
from jax import lax
from jax.experimental import pallas as pl
from jax.experimental.pallas import tpu as pltpu

DEPTH = 2
HEAD_DIM_A = 64
DIL_GROUPS = ((128, 1), (512, 4), (2048, 16))
HEADS_PER_GROUP_A = 6
GROUP_WIDTH_A = HEADS_PER_GROUP_A * HEAD_DIM_A
WIDTH_A = GROUP_WIDTH_A * len(DIL_GROUPS)
N_BUCKETS = 32
T5_MAX_DISTANCE = 1024
N_HEADS_B = 16
QK_NOPE_DIM = 64
QK_ROPE_DIM = 32
V_HEAD_DIM = 64
Q_LORA_RANK = 256
KV_LORA_RANK = 128
WIDTH_B = N_HEADS_B * V_HEAD_DIM
ROPE_THETA = 10000.0
DEEPNORM_ALPHA = (2.0 * DEPTH) ** 0.25
LN_EPS = 1e-5
RMS_EPS = 1e-6
NEG_INF = -1e30
LOG2E = math.log2(math.e)
LN2 = math.log(2.0)

LANES = 128
VMEM_LIMIT_BYTES = 56 * 1024 * 1024

HALF_WINDOW = 64
TQ_A = 128
TK_A = TQ_A + 2 * HALF_WINDOW
GROUP_LANE_STRIDE = 8
MAX_TILES_PER_STEP_A = 4
TL_PROJ = 512
TL_WIDE = 1024
TQ_B = 512
TK_B = 512
HEADS_PER_STEP_B = 4

_NT = (((1,), (1,)), ((), ()))


def _silu(x):
    return x * (0.5 + 0.5 * jnp.tanh(0.5 * x))


def _params(*sem):
    return pltpu.CompilerParams(dimension_semantics=sem, vmem_limit_bytes=VMEM_LIMIT_BYTES)


def _mod_kernel(c_ref, w_ref, b_ref, o_ref):
    sc = _silu(c_ref[...]).astype(jnp.bfloat16)
    o_ref[...] = jnp.dot(sc, w_ref[...].astype(jnp.bfloat16),
                         preferred_element_type=jnp.float32) + b_ref[...]


def _modulation(c, ada_w, ada_b):
    B, D = c.shape
    nj = 3
    return pl.pallas_call(
        _mod_kernel,
        grid=(DEPTH, nj),
        in_specs=[pl.BlockSpec((B, D), lambda i, j: (0, 0)),
                  pl.BlockSpec((None, D, D), lambda i, j: (i, 0, j)),
                  pl.BlockSpec((None, 1, D), lambda i, j: (i, 0, j))],
        out_specs=pl.BlockSpec((None, B, D), lambda i, j: (i, 0, j)),
        out_shape=jax.ShapeDtypeStruct((DEPTH, B, 3 * D), jnp.float32),
        compiler_params=_params("arbitrary", "arbitrary"),
        name="modulation",
    )(c, ada_w, ada_b.reshape(DEPTH, 1, 3 * D))


def _modulate(x, mod):
    D = x.shape[-1]
    return x * (1.0 + mod[:, D:2 * D]) + mod[:, :D]


def _layernorm(z, g, b):
    mu = jnp.mean(z, axis=-1, keepdims=True)
    zc = z - mu
    var = jnp.mean(zc * zc, axis=-1, keepdims=True)
    return zc * lax.rsqrt(var + LN_EPS) * g + b


def _proj_a_kernel(x_ref, mod_ref, w0_ref, w1_ref, w2_ref, qkv0_ref, qkv1_ref, qkv2_ref, gate_ref):
    u = _modulate(x_ref[...], mod_ref[...]).astype(jnp.bfloat16)
    n_qkv = qkv0_ref.shape[-1]
    z = jnp.dot(u, w0_ref[...], preferred_element_type=jnp.float32)
    qkv0_ref[0] = z[:, :n_qkv].astype(qkv0_ref.dtype)
    gate_ref[...] = _silu(z[:, n_qkv:]).astype(gate_ref.dtype)
    for w_ref, qkv_ref in ((w1_ref, qkv1_ref), (w2_ref, qkv2_ref)):
        dil, rows = qkv_ref.shape[:2]
        up = pltpu.einshape("(md)n->(dm)n", u, d=dil)
        z = jnp.dot(up, w_ref[...], preferred_element_type=jnp.float32)
        for r in range(dil):
            qkv_ref[r] = z[r * rows:(r + 1) * rows].astype(qkv_ref.dtype)


def _proj_a(x, mod, ws):
    B, S, D = x.shape
    tl = TL_WIDE
    n_qkv = 3 * GROUP_WIDTH_A
    dils = [dil for _, dil in DIL_GROUPS]
    assert dils[0] == 1
    row = lambda b, i: (b, i, 0)
    cls = lambda b, i: (b, 0, i, 0)
    const2 = lambda b, i: (0, 0)
    return pl.pallas_call(
        _proj_a_kernel,
        grid=(B, S // tl),
        in_specs=[pl.BlockSpec((None, tl, D), row),
                  pl.BlockSpec((None, 1, 3 * D), lambda b, i: (b, 0, 0)),
                  *[pl.BlockSpec(w.shape, const2) for w in ws]],
        out_specs=[*[pl.BlockSpec((None, dil, tl // dil, n_qkv), cls) for dil in dils],
                   pl.BlockSpec((None, tl, WIDTH_A), row)],
        out_shape=[*[jax.ShapeDtypeStruct((B, dil, S // dil, n_qkv), jnp.bfloat16) for dil in dils],
                   jax.ShapeDtypeStruct((B, S, WIDTH_A), jnp.bfloat16)],
        compiler_params=_params("arbitrary", "arbitrary"),
        name="proj_a",
    )(x, mod, *ws)


def _attn_a_kernel(qkv_ref, bias_ref, o_ref, lse_ref, s_scr, p_scr):
    n_cls, L = (1,) + qkv_ref.shape[:1] if len(qkv_ref.shape) == 2 else qkv_ref.shape[:2]

    def of_class(ref, cls):
        return ref if len(ref.shape) == 2 else ref.at[cls]
    n_tiles = L // TQ_A
    gw = GROUP_WIDTH_A
    lane = lax.broadcasted_iota(jnp.int32, (TQ_A, LANES), 1)
    low_q = lane < HEAD_DIM_A
    low_k = lax.broadcasted_iota(jnp.int32, (TK_A, LANES), 1) < HEAD_DIM_A

    heads = range(HEADS_PER_GROUP_A)

    def pair_block(cls, rows, section, h):
        c0 = section * gw + (h // 2) * LANES
        return of_class(qkv_ref, cls)[rows, c0:c0 + LANES]

    def other_lanes(block, low_mask, h, fill):
        mine = low_mask if h % 2 == 0 else jnp.logical_not(low_mask)
        return jnp.where(mine, block, jnp.full_like(block, fill))

    tiles_per_step = s_scr.shape[0]

    def tile_group(tg, carry):
        geo = []
        for u in range(tiles_per_step):
            item = tg * tiles_per_step + u
            cls, t = (0, item) if n_cls == 1 else (item // n_tiles, item % n_tiles)
            j0 = pl.multiple_of(t * TQ_A, TQ_A)
            ws = pl.multiple_of(jnp.clip(j0 - HALF_WINDOW, 0, L - TK_A), HALF_WINDOW)
            variant = jnp.where(t == 0, 0, jnp.where(t == n_tiles - 1, 2, 1))
            geo.append((cls, pl.ds(j0, TQ_A), pl.ds(ws, TK_A), variant))
        for u, (cls, rq, rk, variant) in enumerate(geo):
            for h in heads:
                q = other_lanes(pair_block(cls, rq, 0, h), low_q, h, 0)
                s = lax.dot_general(q, pair_block(cls, rk, 1, h), _NT, preferred_element_type=jnp.float32)
                s_scr[u, h] = s + bias_ref[variant, h]
        m = [[jnp.max(s_scr[u, h], axis=-1, keepdims=True) for h in heads] for u in range(len(geo))]
        for u in range(len(geo)):
            for h in heads:
                p_scr[u, h] = jnp.exp2(s_scr[u, h] - m[u][h]).astype(jnp.bfloat16)
        for u, (cls, rq, rk, variant) in enumerate(geo):
            lse_tile = jnp.zeros((TQ_A, LANES), jnp.float32)
            for pair in range(HEADS_PER_GROUP_A // 2):
                he, ho = 2 * pair, 2 * pair + 1
                acc_e = jnp.dot(p_scr[u, he], other_lanes(pair_block(cls, rk, 2, he), low_k, he, 1),
                                preferred_element_type=jnp.float32)
                acc_o = jnp.dot(p_scr[u, ho], other_lanes(pair_block(cls, rk, 2, ho), low_k, ho, 1),
                                preferred_element_type=jnp.float32)
                numer = jnp.where(low_q, acc_e, acc_o)
                den_other = jnp.where(low_q, acc_o, acc_e)
                den = pltpu.roll(den_other, HEAD_DIM_A, 1)
                of_class(o_ref, cls)[rq, pair * LANES:(pair + 1) * LANES] = (numer / den).astype(o_ref.dtype)
                is_odd = lane == ho
                lse = (jnp.where(is_odd, m[u][ho], m[u][he])
                       + jnp.log2(jnp.where(is_odd, den_other, den))) * LN2
                lse_tile = jnp.where(jnp.logical_or(is_odd, lane == he), lse, lse_tile)
            of_class(lse_ref, cls)[rq, :] = lse_tile
        return carry

    lax.fori_loop(0, n_cls * n_tiles // tiles_per_step, tile_group, 0)


def _attn_a(qkv, biasm):
    B, dil, L, n_qkv = qkv.shape
    gw = GROUP_WIDTH_A
    n_tiles = L // TQ_A
    n_cls = min(dil, max(1, MAX_TILES_PER_STEP_A // n_tiles))
    tiles_per_step = math.gcd(MAX_TILES_PER_STEP_A, n_cls * n_tiles)
    blk_cls = None if n_cls == 1 else n_cls
    cls = lambda b, r: (b, r, 0, 0)
    return pl.pallas_call(
        _attn_a_kernel,
        grid=(B, dil // n_cls),
        in_specs=[pl.BlockSpec((None, blk_cls, L, n_qkv), cls),
                  pl.BlockSpec(biasm.shape, lambda b, r: (0, 0, 0, 0))],
        out_specs=[pl.BlockSpec((None, blk_cls, L, gw), cls),
                   pl.BlockSpec((None, blk_cls, L, LANES), cls)],
        out_shape=[jax.ShapeDtypeStruct((B, dil, L, gw), jnp.bfloat16),
                   jax.ShapeDtypeStruct((B, dil, L, LANES), jnp.float32)],
        scratch_shapes=[
            pltpu.VMEM((tiles_per_step, HEADS_PER_GROUP_A, TQ_A, TK_A), jnp.float32),
            pltpu.VMEM((tiles_per_step, HEADS_PER_GROUP_A, TQ_A, TK_A), jnp.bfloat16)],
        compiler_params=_params("arbitrary", "arbitrary"),
        name=f"attn_a_d{dil}",
    )(qkv, biasm)


def _t5_bucket_np(rel):
    half = N_BUCKETS // 2
    max_exact = half // 2
    base = np.where(rel > 0, half, 0)
    n = np.abs(rel)
    nf = np.maximum(n, 1).astype(np.float32)
    large = max_exact + (np.log(nf / np.float32(max_exact)) / np.float32(math.log(T5_MAX_DISTANCE / max_exact))
                         * np.float32(half - max_exact)).astype(np.int32)
    large = np.minimum(large, half - 1)
    return base + np.where(n < max_exact, n, large)


def _bias_variants(rel_bias_g, dil):
    H = rel_bias_g.shape[1]
    M = 2 * TK_A
    off = np.arange(M) - (TK_A - 1)
    idx = np.where(np.abs(off) <= HALF_WINDOW, _t5_bucket_np(off * dil), N_BUCKETS).astype(np.int32)
    table = jnp.concatenate([rel_bias_g.astype(jnp.float32) * LOG2E, jnp.full((1, H), NEG_INF, jnp.float32)], axis=0)
    w = table[jnp.asarray(idx)].T
    rot = jnp.tile(w, (1, TQ_A + 1))[:, :TQ_A * (M + 1)].reshape(H, TQ_A, M + 1)[:, ::-1, :]
    variants = []
    for shift in (0, HALF_WINDOW, 2 * HALF_WINDOW):
        j0 = TK_A - TQ_A - shift
        variants.append(rot[:, :, j0:j0 + TK_A])
    return jnp.stack(variants)


N_MERGE_A_REFS = 13


def _merge_a(x_ref, mod_ref, o0_ref, o1_ref, o2_ref, l0_ref, l1_ref, l2_ref, gate_ref,
             expand_ref, w_ref, g_ref, b_ref):
    tl, D = x_ref.shape
    stride = GROUP_LANE_STRIDE
    lane = lax.broadcasted_iota(jnp.int32, (tl, LANES), 1)

    def natural_order(o_ref, l_ref):
        if o_ref.shape[0] == 1:
            return o_ref[0].astype(jnp.float32), l_ref[0]
        return (pltpu.einshape("dmn->(md)n", o_ref[...].astype(jnp.float32)),
                pltpu.einshape("dmn->(md)n", l_ref[...]))

    o0, l0 = natural_order(o0_ref, l0_ref)
    o1, l1 = natural_order(o1_ref, l1_ref)
    o2, l2 = natural_order(o2_ref, l2_ref)
    mx = jnp.maximum(jnp.maximum(l0, l1), l2)
    e0, e1, e2 = jnp.exp(l0 - mx), jnp.exp(l1 - mx), jnp.exp(l2 - mx)
    inv = 1.0 / (e0 + e1 + e2)
    wt = jnp.where(lane < stride, e0 * inv,
                   jnp.where(lane < 2 * stride, pltpu.roll(e1 * inv, stride, 1), pltpu.roll(e2 * inv, 2 * stride, 1)))
    hi = wt.astype(jnp.bfloat16)
    lo = (wt - hi.astype(jnp.float32)).astype(jnp.bfloat16)
    wexp = (jnp.dot(hi, expand_ref[...], preferred_element_type=jnp.float32)
            + jnp.dot(lo, expand_ref[...], preferred_element_type=jnp.float32))
    o = jnp.concatenate([o0, o1, o2], axis=1)
    yg = (o * wexp * gate_ref[...].astype(jnp.float32)).astype(jnp.bfloat16)
    y = jnp.dot(yg, w_ref[...], preferred_element_type=jnp.float32)
    z = DEEPNORM_ALPHA * x_ref[...] + mod_ref[:, 2 * D:] * y
    return _layernorm(z, g_ref[...], b_ref[...])


def _rmsnorm(x, g):
    return x * lax.rsqrt(jnp.mean(x * x, axis=-1, keepdims=True) + RMS_EPS) * g


def _project_b(x, mod_ref, w_in_ref, qn_ref, kvn_ref, wqa_ref, wk_ref, wv_ref,
               vone_ref, cq_ref, sq_ref, ck_ref, sk_ref, q_out, k_out, v_out, gate_out):
    u = _modulate(x, mod_ref[...]).astype(jnp.bfloat16)
    z = jnp.dot(u, w_in_ref[...], preferred_element_type=jnp.float32)
    c0 = Q_LORA_RANK
    c1 = c0 + KV_LORA_RANK
    cq = z[:, :c0]
    ckv = z[:, c0:c1]
    kr = z[:, c1:c1 + LANES]
    gate_out[...] = z[:, c1 + LANES:].astype(gate_out.dtype)
    to_rope_lanes = LANES - QK_ROPE_DIM

    cqn = _rmsnorm(cq, qn_ref[...]).astype(jnp.bfloat16)
    qa = jnp.dot(cqn, wqa_ref[...], preferred_element_type=jnp.float32)
    ckvn = _rmsnorm(ckv, kvn_ref[...]).astype(jnp.bfloat16)
    kn = jnp.dot(ckvn, wk_ref[...], preferred_element_type=jnp.float32)
    v_out[...] = (jnp.dot(ckvn, wv_ref[...], preferred_element_type=jnp.float32)
                  + vone_ref[...]).astype(v_out.dtype)

    cos_q, sin_q = cq_ref[...], sq_ref[...]
    k_rope = kr * ck_ref[...] + pltpu.roll(kr, to_rope_lanes, 1) * sk_ref[...]
    for h in range(N_HEADS_B):
        blk = slice(h * LANES, (h + 1) * LANES)
        qh = qa[:, blk]
        q_out[:, blk] = (qh * cos_q + pltpu.roll(qh, to_rope_lanes, 1) * sin_q).astype(q_out.dtype)
        k_out[:, blk] = (kn[:, blk] + k_rope).astype(k_out.dtype)


def _mid_kernel(*refs):
    merge_refs, rest = refs[:N_MERGE_A_REFS], refs[N_MERGE_A_REFS:]
    x1_out, project_refs = rest[-5], rest[:-5] + rest[-4:]
    x1 = _merge_a(*merge_refs)
    x1_out[...] = x1
    _project_b(x1, *project_refs)


def _mid(x, mod0, os_, lses, gate_a, w_out_a, ln_g, ln_b,
         mod1, w_in, q_norm, kv_norm, wqa, wk, wv, tables):
    B, S, D = x.shape
    tl = TL_PROJ
    gw = GROUP_WIDTH_A
    HP = N_HEADS_B * LANES
    col = np.arange(WIDTH_A)
    src_lane = (col // gw) * GROUP_LANE_STRIDE + (col % gw) // HEAD_DIM_A
    expand = jnp.asarray((np.arange(LANES)[:, None] == src_lane[None, :]).astype(np.float32), dtype=jnp.bfloat16)
    row = lambda b, i: (b, i, 0)
    cls = lambda b, i: (b, 0, i, 0)
    const2 = lambda b, i: (0, 0)
    mod_spec = pl.BlockSpec((None, 1, 3 * D), lambda b, i: (b, 0, 0))
    tab = pl.BlockSpec((tl, LANES), lambda b, i: (i, 0))
    o_specs = [pl.BlockSpec((None, dil, tl // dil, gw), cls) for _, dil in DIL_GROUPS]
    l_specs = [pl.BlockSpec((None, dil, tl // dil, LANES), cls) for _, dil in DIL_GROUPS]
    merge_args = [x, mod0, *os_, *lses, gate_a, expand, w_out_a, ln_g.reshape(1, D), ln_b.reshape(1, D)]
    merge_specs = [pl.BlockSpec((None, tl, D), row), mod_spec, *o_specs, *l_specs,
                   pl.BlockSpec((None, tl, WIDTH_A), row),
                   pl.BlockSpec(expand.shape, const2), pl.BlockSpec(w_out_a.shape, const2),
                   pl.BlockSpec((1, D), const2), pl.BlockSpec((1, D), const2)]
    assert len(merge_args) == N_MERGE_A_REFS
    project_args = [mod1, w_in, q_norm.reshape(1, -1), kv_norm.reshape(1, -1), wqa, wk, wv, _v_ones_row(), *tables]
    project_specs = [mod_spec, pl.BlockSpec(w_in.shape, const2),
                     pl.BlockSpec((1, Q_LORA_RANK), const2), pl.BlockSpec((1, KV_LORA_RANK), const2),
                     pl.BlockSpec(wqa.shape, const2), pl.BlockSpec(wk.shape, const2),
                     pl.BlockSpec(wv.shape, const2), pl.BlockSpec((1, HP), const2), tab, tab, tab, tab]
    wide = jax.ShapeDtypeStruct((B, S, HP), jnp.bfloat16)
    return pl.pallas_call(
        _mid_kernel,
        grid=(B, S // tl),
        in_specs=merge_specs + project_specs,
        out_specs=[pl.BlockSpec((None, tl, D), row),
                   pl.BlockSpec((None, tl, HP), row), pl.BlockSpec((None, tl, HP), row),
                   pl.BlockSpec((None, tl, HP), row), pl.BlockSpec((None, tl, WIDTH_B), row)],
        out_shape=[jax.ShapeDtypeStruct((B, S, D), jnp.float32), wide, wide, wide,
                   jax.ShapeDtypeStruct((B, S, WIDTH_B), jnp.bfloat16)],
        compiler_params=_params("arbitrary", "arbitrary"),
        name="mid",
    )(*merge_args, *project_args)


def _attn_b_kernel(q_ref, k_ref, v_ref, gate_ref, o_ref, s_even, s_odd, mx_scr):
    S = k_ref.shape[0]
    tq = s_even.shape[0]
    n_tiles = S // tq
    n_chunks = S // TK_B
    n_heads = q_ref.shape[1] // LANES
    assert n_heads % 2 == 0
    s_bufs = (s_even, s_odd)

    def head_lanes(e):
        return slice(e * LANES, (e + 1) * LANES)

    def tile_rows(t):
        return pl.ds(t * tq, tq) if isinstance(t, int) else pl.ds(pl.multiple_of(t * tq, tq), tq)

    def scores(t, e, c, mx):
        ks = slice(c * TK_B, (c + 1) * TK_B)
        s = lax.dot_general(q_ref[tile_rows(t), head_lanes(e)], k_ref[ks, head_lanes(e)], _NT,
                            preferred_element_type=jnp.float32)
        s_bufs[e % 2][:, ks] = s
        for j in range(TK_B // LANES):
            mx = jnp.maximum(mx, s[:, j * LANES:(j + 1) * LANES])
        return mx

    def weighted_values(e, c, m, acc):
        ks = slice(c * TK_B, (c + 1) * TK_B)
        p = jnp.exp2(s_bufs[e % 2][:, ks] - m).astype(jnp.bfloat16)
        return acc + jnp.dot(p, v_ref[ks, head_lanes(e)], preferred_element_type=jnp.float32)

    neg = jnp.full((tq, LANES), -jnp.inf, jnp.float32)
    zero = jnp.zeros((tq, LANES), jnp.float32)
    lane = lax.broadcasted_iota(jnp.int32, (tq, LANES), 1)

    mx = neg
    for c in range(n_chunks):
        mx = scores(0, 0, c, mx)
    mx_scr[...] = mx

    def tile(t, carry):
        t_next = jnp.minimum(t + 1, n_tiles - 1)
        mx = mx_scr[...]
        acc = []
        for e in range(n_heads):
            m = jnp.max(mx, axis=-1, keepdims=True)
            mx, a = neg, zero
            for c in range(n_chunks):
                mx = scores(t, e + 1, c, mx) if e + 1 < n_heads else scores(t_next, 0, c, mx)
                a = weighted_values(e, c, m, a)
            acc.append(a)
        mx_scr[...] = mx
        for pair in range(n_heads // 2):
            acc_e, acc_o = acc[2 * pair], acc[2 * pair + 1]
            inv_e = 1.0 / acc_e[:, V_HEAD_DIM:V_HEAD_DIM + 1]
            inv_o = 1.0 / acc_o[:, 0:1]
            cols = slice(pair * LANES, (pair + 1) * LANES)
            o = jnp.where(lane < V_HEAD_DIM, acc_e * inv_e, acc_o * inv_o)
            gate = gate_ref[tile_rows(t), cols].astype(jnp.float32)
            o_ref[tile_rows(t), cols] = (o * _silu(gate)).astype(o_ref.dtype)
        return carry

    lax.fori_loop(0, n_tiles, tile, 0)


def _attn_b(q2, k2, v2, gate):
    B, S, HP = q2.shape
    hw = HEADS_PER_STEP_B * LANES
    ow = HEADS_PER_STEP_B * V_HEAD_DIM
    heads = lambda b, p: (b, 0, p)
    return pl.pallas_call(
        _attn_b_kernel,
        grid=(B, N_HEADS_B // HEADS_PER_STEP_B),
        in_specs=[pl.BlockSpec((None, S, hw), heads), pl.BlockSpec((None, S, hw), heads),
                  pl.BlockSpec((None, S, hw), heads), pl.BlockSpec((None, S, ow), heads)],
        out_specs=pl.BlockSpec((None, S, ow), heads),
        out_shape=jax.ShapeDtypeStruct((B, S, WIDTH_B), jnp.bfloat16),
        scratch_shapes=[pltpu.VMEM((TQ_B, S), jnp.float32), pltpu.VMEM((TQ_B, S), jnp.float32),
                        pltpu.VMEM((TQ_B, LANES), jnp.float32)],
        compiler_params=_params("arbitrary", "arbitrary"),
        name="attn_b",
    )(q2, k2, v2, gate)


def _out_b_kernel(x_ref, mod_ref, yg_ref, w_ref, g_ref, b_ref, out_ref):
    D = x_ref.shape[-1]
    y = jnp.dot(yg_ref[...], w_ref[...], preferred_element_type=jnp.float32)
    z = DEEPNORM_ALPHA * x_ref[...] + mod_ref[:, 2 * D:] * y
    out_ref[...] = _layernorm(z, g_ref[...], b_ref[...])


def _out_b(x, mod, yg, w_out, ln_g, ln_b):
    B, S, D = x.shape
    tl = TL_WIDE
    row = lambda b, i: (b, i, 0)
    const2 = lambda b, i: (0, 0)
    return pl.pallas_call(
        _out_b_kernel,
        grid=(B, S // tl),
        in_specs=[pl.BlockSpec((None, tl, D), row),
                  pl.BlockSpec((None, 1, 3 * D), lambda b, i: (b, 0, 0)),
                  pl.BlockSpec((None, tl, WIDTH_B), row),
                  pl.BlockSpec(w_out.shape, const2),
                  pl.BlockSpec((1, D), const2), pl.BlockSpec((1, D), const2)],
        out_specs=pl.BlockSpec((None, tl, D), row),
        out_shape=jax.ShapeDtypeStruct((B, S, D), jnp.float32),
        compiler_params=_params("arbitrary", "arbitrary"),
        name="out_b",
    )(x, mod, yg, w_out, ln_g.reshape(1, D), ln_b.reshape(1, D))


def _weights_a(a_w_in):
    W = WIDTH_A
    gw = GROUP_WIDTH_A
    q, k, v, gate = (a_w_in[:, i * W:(i + 1) * W] for i in range(4))
    ws = []
    for g in range(len(DIL_GROUPS)):
        cs = slice(g * gw, (g + 1) * gw)
        parts = [q[:, cs] * (HEAD_DIM_A ** -0.5 * LOG2E), k[:, cs], v[:, cs]]
        if g == 0:
            parts.append(gate)
        ws.append(jnp.concatenate(parts, axis=1).astype(jnp.bfloat16))
    return ws


def _swap_halves(w):
    half = w.shape[-1] // 2
    return jnp.concatenate([w[..., half:], w[..., :half]], axis=-1)


def _pad_lanes(w, lo):
    n = w.shape[-1]
    return jnp.pad(w, [(0, 0)] * (w.ndim - 1) + [(lo, LANES - lo - n)])


def _weights_b(b_w_in, b_w_uq, b_w_ukv):
    c0 = Q_LORA_RANK
    c1 = c0 + KV_LORA_RANK
    c2 = c1 + QK_ROPE_DIM
    kr = b_w_in[:, c1:c2]
    w_in = jnp.concatenate([b_w_in[:, :c1], _pad_lanes(jnp.concatenate([kr, _swap_halves(kr)], axis=1), QK_NOPE_DIM),
                            b_w_in[:, c2:]], axis=1)
    uq = b_w_uq.reshape(Q_LORA_RANK, N_HEADS_B, QK_NOPE_DIM + QK_ROPE_DIM)
    wqa = jnp.concatenate([uq, _swap_halves(uq[..., QK_NOPE_DIM:])], axis=-1).reshape(Q_LORA_RANK, -1)
    ukv = b_w_ukv.reshape(KV_LORA_RANK, N_HEADS_B, QK_NOPE_DIM + V_HEAD_DIM)
    wk = _pad_lanes(ukv[..., :QK_NOPE_DIM], 0).reshape(KV_LORA_RANK, -1)
    vv = ukv[..., QK_NOPE_DIM:]
    odd = (jnp.arange(N_HEADS_B) % 2 == 1)[None, :, None]
    wv = jnp.where(odd, _pad_lanes(vv, LANES - V_HEAD_DIM), _pad_lanes(vv, 0)).reshape(KV_LORA_RANK, -1)
    bf = lambda w: w.astype(jnp.bfloat16)
    return bf(w_in), bf(wqa), bf(wk), bf(wv)


def _rope_tables(S):
    pos = jnp.arange(S, dtype=jnp.float32)
    inv_freq = ROPE_THETA ** (-jnp.arange(0, QK_ROPE_DIM, 2, dtype=jnp.float32) / QK_ROPE_DIM)
    ang = pos[:, None] * inv_freq[None, :]
    cos, sin = jnp.cos(ang), jnp.sin(ang)
    cosf = _pad_lanes(jnp.concatenate([cos, cos], axis=-1), QK_NOPE_DIM)
    sinf = _pad_lanes(jnp.concatenate([-sin, sin], axis=-1), QK_NOPE_DIM)
    nope = _pad_lanes(jnp.ones((S, QK_NOPE_DIM), jnp.float32), 0)
    scale = (QK_NOPE_DIM + QK_ROPE_DIM) ** -0.5 * LOG2E
    return (scale * (nope + cosf), scale * sinf, cosf, sinf)


def _v_ones_row():
    row = np.zeros((N_HEADS_B, LANES), np.float32)
    row[0::2, V_HEAD_DIM] = 1.0
    row[1::2, 0] = 1.0
    return jnp.asarray(row.reshape(1, -1))


def kernel(x, c, rel_bias, ada_w, ada_b, ln_g, ln_b, a_w_in, a_w_out,
           b_w_in, b_q_norm, b_w_uq, b_kv_norm, b_w_ukv, b_w_out):
    B, S, D = x.shape
    mods = _modulation(c, ada_w, ada_b)
    mod0 = mods[0].reshape(B, 1, 3 * D)
    mod1 = mods[1].reshape(B, 1, 3 * D)

    for window, dil in DIL_GROUPS:
        assert window // (2 * dil) == HALF_WINDOW
    *qkvs, gate_a = _proj_a(x, mod0, _weights_a(a_w_in[0]))
    os_, lses = [], []
    for g, ((_, dil), qkv) in enumerate(zip(DIL_GROUPS, qkvs)):
        hs = slice(g * HEADS_PER_GROUP_A, (g + 1) * HEADS_PER_GROUP_A)
        o, lse = _attn_a(qkv, _bias_variants(rel_bias[:, hs], dil))
        os_.append(o)
        lses.append(lse)

    w_in, wqa, wk, wv = _weights_b(b_w_in[0], b_w_uq[0], b_w_ukv[0])
    x1, q2, k2, v2, gate_b = _mid(x, mod0, os_, lses, gate_a, a_w_out[0].astype(jnp.bfloat16), ln_g[0], ln_b[0],
                                  mod1, w_in, b_q_norm[0], b_kv_norm[0], wqa, wk, wv, _rope_tables(S))
    yg = _attn_b(q2, k2, v2, gate_b)
    return _out_b(x1, mod1, yg, b_w_out[0].astype(jnp.bfloat16), ln_g[1], ln_b[1])
```

```python
import math

import numpy as np
import jax
import jax.numpy as jnp
from jax import lax
from jax.experimental import pallas as pl
from jax.experimental.pallas import tpu as pltpu

DEPTH = 2
HEAD_DIM_A = 64
DIL_GROUPS = ((128, 1), (512, 4), (2048, 16))
HEADS_PER_GROUP_A = 6
GROUP_WIDTH_A = HEADS_PER_GROUP_A * HEAD_DIM_A
WIDTH_A = GROUP_WIDTH_A * len(DIL_GROUPS)
N_BUCKETS = 32
T5_MAX_DISTANCE = 1024
N_HEADS_B = 16
QK_NOPE_DIM = 64
QK_ROPE_DIM = 32
V_HEAD_DIM = 64
Q_LORA_RANK = 256
KV_LORA_RANK = 128
WIDTH_B = N_HEADS_B * V_HEAD_DIM
ROPE_THETA = 10000.0
DEEPNORM_ALPHA = (2.0 * DEPTH) ** 0.25
LN_EPS = 1e-5
RMS_EPS = 1e-6
NEG_INF = -1e30
LOG2E = math.log2(math.e)
LN2 = math.log(2.0)

LANES = 128
VMEM_LIMIT_BYTES = 56 * 1024 * 1024

HALF_WINDOW = 64
TQ_A = 128
TK_A = TQ_A + 2 * HALF_WINDOW
GROUP_LANE_STRIDE = 8
MAX_TILES_PER_STEP_A = 8
TL_PROJ = 512
TL_WIDE = 1024
TQ_B = 512
TK_B = 512
HEADS_PER_STEP_B = 4

_NT = (((1,), (1,)), ((), ()))


def _silu(x):
    return x * (0.5 + 0.5 * jnp.tanh(0.5 * x))


def _params(*sem):
    return pltpu.CompilerParams(dimension_semantics=sem, vmem_limit_bytes=VMEM_LIMIT_BYTES)


def _mod_kernel(c_ref, w_ref, b_ref, o_ref):
    sc = _silu(c_ref[...]).astype(jnp.bfloat16)
    o_ref[...] = jnp.dot(sc, w_ref[...].astype(jnp.bfloat16),
                         preferred_element_type=jnp.float32) + b_ref[...]


def _modulation(c, ada_w, ada_b):
    B, D = c.shape
    nj = 3
    return pl.pallas_call(
        _mod_kernel,
        grid=(DEPTH, nj),
        in_specs=[pl.BlockSpec((B, D), lambda i, j: (0, 0)),
                  pl.BlockSpec((None, D, D), lambda i, j: (i, 0, j)),
                  pl.BlockSpec((None, 1, D), lambda i, j: (i, 0, j))],
        out_specs=pl.BlockSpec((None, B, D), lambda i, j: (i, 0, j)),
        out_shape=jax.ShapeDtypeStruct((DEPTH, B, 3 * D), jnp.float32),
        compiler_params=_params("arbitrary", "arbitrary"),
        name="modulation",
    )(c, ada_w, ada_b.reshape(DEPTH, 1, 3 * D))


def _modulate(x, mod):
    D = x.shape[-1]
    return x * (1.0 + mod[:, D:2 * D]) + mod[:, :D]


def _layernorm(z, g, b):
    mu = jnp.mean(z, axis=-1, keepdims=True)
    zc = z - mu
    var = jnp.mean(zc * zc, axis=-1, keepdims=True)
    return zc * lax.rsqrt(var + LN_EPS) * g + b


def _proj_a_kernel(x_ref, mod_ref, w0_ref, w1_ref, w2_ref, qkv0_ref, qkv1_ref, qkv2_ref, gate_ref):
    u = _modulate(x_ref[...], mod_ref[...]).astype(jnp.bfloat16)
    n_qkv = qkv0_ref.shape[-1]
    z = jnp.dot(u, w0_ref[...], preferred_element_type=jnp.float32)
    qkv0_ref[0] = z[:, :n_qkv].astype(qkv0_ref.dtype)
    gate_ref[...] = _silu(z[:, n_qkv:]).astype(gate_ref.dtype)
    for w_ref, qkv_ref in ((w1_ref, qkv1_ref), (w2_ref, qkv2_ref)):
        dil, rows = qkv_ref.shape[:2]
        up = pltpu.einshape("(md)n->(dm)n", u, d=dil)
        z = jnp.dot(up, w_ref[...], preferred_element_type=jnp.float32)
        for r in range(dil):
            qkv_ref[r] = z[r * rows:(r + 1) * rows].astype(qkv_ref.dtype)


def _proj_a(x, mod, ws):
    B, S, D = x.shape
    tl = TL_WIDE
    n_qkv = 3 * GROUP_WIDTH_A
    dils = [dil for _, dil in DIL_GROUPS]
    assert dils[0] == 1
    row = lambda b, i: (b, i, 0)
    cls = lambda b, i: (b, 0, i, 0)
    const2 = lambda b, i: (0, 0)
    return pl.pallas_call(
        _proj_a_kernel,
        grid=(B, S // tl),
        in_specs=[pl.BlockSpec((None, tl, D), row),
                  pl.BlockSpec((None, 1, 3 * D), lambda b, i: (b, 0, 0)),
                  *[pl.BlockSpec(w.shape, const2) for w in ws]],
        out_specs=[*[pl.BlockSpec((None, dil, tl // dil, n_qkv), cls) for dil in dils],
                   pl.BlockSpec((None, tl, WIDTH_A), row)],
        out_shape=[*[jax.ShapeDtypeStruct((B, dil, S // dil, n_qkv), jnp.bfloat16) for dil in dils],
                   jax.ShapeDtypeStruct((B, S, WIDTH_A), jnp.bfloat16)],
        compiler_params=_params("arbitrary", "arbitrary"),
        name="proj_a",
    )(x, mod, *ws)


def _attn_a_kernel(qkv_ref, bias_ref, o_ref, lse_ref, s_scr, p_scr):
    n_cls, L = (1,) + qkv_ref.shape[:1] if len(qkv_ref.shape) == 2 else qkv_ref.shape[:2]

    def of_class(ref, cls):
        return ref if len(ref.shape) == 2 else ref.at[cls]
    n_tiles = L // TQ_A
    gw = GROUP_WIDTH_A
    lane = lax.broadcasted_iota(jnp.int32, (TQ_A, LANES), 1)
    low_q = lane < HEAD_DIM_A
    low_k = lax.broadcasted_iota(jnp.int32, (TK_A, LANES), 1) < HEAD_DIM_A

    heads = range(HEADS_PER_GROUP_A)

    def pair_block(cls, rows, section, h):
        c0 = section * gw + (h // 2) * LANES
        return of_class(qkv_ref, cls)[rows, c0:c0 + LANES]

    def other_lanes(block, low_mask, h, fill):
        mine = low_mask if h % 2 == 0 else jnp.logical_not(low_mask)
        return jnp.where(mine, block, jnp.full_like(block, fill))

    tiles_per_step = s_scr.shape[0]

    def tile_group(tg, carry):
        geo = []
        for u in range(tiles_per_step):
            item = tg * tiles_per_step + u
            cls, t = (0, item) if n_cls == 1 else (item // n_tiles, item % n_tiles)
            j0 = pl.multiple_of(t * TQ_A, TQ_A)
            ws = pl.multiple_of(jnp.clip(j0 - HALF_WINDOW, 0, L - TK_A), HALF_WINDOW)
            variant = jnp.where(t == 0, 0, jnp.where(t == n_tiles - 1, 2, 1))
            geo.append((cls, pl.ds(j0, TQ_A), pl.ds(ws, TK_A), variant))
        for u, (cls, rq, rk, variant) in enumerate(geo):
            for h in heads:
                q = other_lanes(pair_block(cls, rq, 0, h), low_q, h, 0)
                s = lax.dot_general(q, pair_block(cls, rk, 1, h), _NT, preferred_element_type=jnp.float32)
                s_scr[u, h] = s + bias_ref[variant, h]
        m = [[jnp.max(s_scr[u, h], axis=-1, keepdims=True) for h in heads] for u in range(len(geo))]
        for u in range(len(geo)):
            for h in heads:
                p_scr[u, h] = jnp.exp2(s_scr[u, h] - m[u][h]).astype(jnp.bfloat16)
        for u, (cls, rq, rk, variant) in enumerate(geo):
            lse_tile = jnp.zeros((TQ_A, LANES), jnp.float32)
            for pair in range(HEADS_PER_GROUP_A // 2):
                he, ho = 2 * pair, 2 * pair + 1
                acc_e = jnp.dot(p_scr[u, he], other_lanes(pair_block(cls, rk, 2, he), low_k, he, 1),
                                preferred_element_type=jnp.float32)
                acc_o = jnp.dot(p_scr[u, ho], other_lanes(pair_block(cls, rk, 2, ho), low_k, ho, 1),
                                preferred_element_type=jnp.float32)
                numer = jnp.where(low_q, acc_e, acc_o)
                den_other = jnp.where(low_q, acc_o, acc_e)
                den = pltpu.roll(den_other, HEAD_DIM_A, 1)
                of_class(o_ref, cls)[rq, pair * LANES:(pair + 1) * LANES] = (numer / den).astype(o_ref.dtype)
                is_odd = lane == ho
                lse = (jnp.where(is_odd, m[u][ho], m[u][he])
                       + jnp.log2(jnp.where(is_odd, den_other, den))) * LN2
                lse_tile = jnp.where(jnp.logical_or(is_odd, lane == he), lse, lse_tile)
            of_class(lse_ref, cls)[rq, :] = lse_tile
        return carry

    lax.fori_loop(0, n_cls * n_tiles // tiles_per_step, tile_group, 0)


def _attn_a(qkv, biasm):
    B, dil, L, n_qkv = qkv.shape
    gw = GROUP_WIDTH_A
    n_tiles = L // TQ_A
    n_cls = min(dil, max(1, MAX_TILES_PER_STEP_A // n_tiles))
    tiles_per_step = math.gcd(MAX_TILES_PER_STEP_A, n_cls * n_tiles)
    blk_cls = None if n_cls == 1 else n_cls
    cls = lambda b, r: (b, r, 0, 0)
    return pl.pallas_call(
        _attn_a_kernel,
        grid=(B, dil // n_cls),
        in_specs=[pl.BlockSpec((None, blk_cls, L, n_qkv), cls),
                  pl.BlockSpec(biasm.shape, lambda b, r: (0, 0, 0, 0))],
        out_specs=[pl.BlockSpec((None, blk_cls, L, gw), cls),
                   pl.BlockSpec((None, blk_cls, L, LANES), cls)],
        out_shape=[jax.ShapeDtypeStruct((B, dil, L, gw), jnp.bfloat16),
                   jax.ShapeDtypeStruct((B, dil, L, LANES), jnp.float32)],
        scratch_shapes=[
            pltpu.VMEM((tiles_per_step, HEADS_PER_GROUP_A, TQ_A, TK_A), jnp.float32),
            pltpu.VMEM((tiles_per_step, HEADS_PER_GROUP_A, TQ_A, TK_A), jnp.bfloat16)],
        compiler_params=_params("arbitrary", "arbitrary"),
        name=f"attn_a_d{dil}",
    )(qkv, biasm)


def _t5_bucket_np(rel):
    half = N_BUCKETS // 2
    max_exact = half // 2
    base = np.where(rel > 0, half, 0)
    n = np.abs(rel)
    nf = np.maximum(n, 1).astype(np.float32)
    large = max_exact + (np.log(nf / np.float32(max_exact)) / np.float32(math.log(T5_MAX_DISTANCE / max_exact))
                         * np.float32(half - max_exact)).astype(np.int32)
    large = np.minimum(large, half - 1)
    return base + np.where(n < max_exact, n, large)


def _bias_variants(rel_bias_g, dil):
    H = rel_bias_g.shape[1]
    M = 2 * TK_A
    off = np.arange(M) - (TK_A - 1)
    idx = np.where(np.abs(off) <= HALF_WINDOW, _t5_bucket_np(off * dil), N_BUCKETS).astype(np.int32)
    table = jnp.concatenate([rel_bias_g.astype(jnp.float32) * LOG2E, jnp.full((1, H), NEG_INF, jnp.float32)], axis=0)
    w = table[jnp.asarray(idx)].T
    rot = jnp.tile(w, (1, TQ_A + 1))[:, :TQ_A * (M + 1)].reshape(H, TQ_A, M + 1)[:, ::-1, :]
    variants = []
    for shift in (0, HALF_WINDOW, 2 * HALF_WINDOW):
        j0 = TK_A - TQ_A - shift
        variants.append(rot[:, :, j0:j0 + TK_A])
    return jnp.stack(variants)


N_MERGE_A_REFS = 13


def _merge_a(x_ref, mod_ref, o0_ref, o1_ref, o2_ref, l0_ref, l1_ref, l2_ref, gate_ref,
             expand_ref, w_ref, g_ref, b_ref):
    tl, D = x_ref.shape
    stride = GROUP_LANE_STRIDE
    lane = lax.broadcasted_iota(jnp.int32, (tl, LANES), 1)

    def natural_order(o_ref, l_ref):
        if o_ref.shape[0] == 1:
            return o_ref[0].astype(jnp.float32), l_ref[0]
        return (pltpu.einshape("dmn->(md)n", o_ref[...].astype(jnp.float32)),
                pltpu.einshape("dmn->(md)n", l_ref[...]))

    o0, l0 = natural_order(o0_ref, l0_ref)
    o1, l1 = natural_order(o1_ref, l1_ref)
    o2, l2 = natural_order(o2_ref, l2_ref)
    mx = jnp.maximum(jnp.maximum(l0, l1), l2)
    e0, e1, e2 = jnp.exp(l0 - mx), jnp.exp(l1 - mx), jnp.exp(l2 - mx)
    inv = 1.0 / (e0 + e1 + e2)
    wt = jnp.where(lane < stride, e0 * inv,
                   jnp.where(lane < 2 * stride, pltpu.roll(e1 * inv, stride, 1), pltpu.roll(e2 * inv, 2 * stride, 1)))
    hi = wt.astype(jnp.bfloat16)
    lo = (wt - hi.astype(jnp.float32)).astype(jnp.bfloat16)
    wexp = (jnp.dot(hi, expand_ref[...], preferred_element_type=jnp.float32)
            + jnp.dot(lo, expand_ref[...], preferred_element_type=jnp.float32))
    o = jnp.concatenate([o0, o1, o2], axis=1)
    yg = (o * wexp * gate_ref[...].astype(jnp.float32)).astype(jnp.bfloat16)
    y = jnp.dot(yg, w_ref[...], preferred_element_type=jnp.float32)
    z = DEEPNORM_ALPHA * x_ref[...] + mod_ref[:, 2 * D:] * y
    return _layernorm(z, g_ref[...], b_ref[...])


def _rmsnorm(x, g):
    return x * lax.rsqrt(jnp.mean(x * x, axis=-1, keepdims=True) + RMS_EPS) * g


def _project_b(x, mod_ref, w_in_ref, qn_ref, kvn_ref, wqa_ref, wk_ref, wv_ref,
               vone_ref, cq_ref, sq_ref, ck_ref, sk_ref, q_out, k_out, v_out, gate_out):
    u = _modulate(x, mod_ref[...]).astype(jnp.bfloat16)
    z = jnp.dot(u, w_in_ref[...], preferred_element_type=jnp.float32)
    c0 = Q_LORA_RANK
    c1 = c0 + KV_LORA_RANK
    cq = z[:, :c0]
    ckv = z[:, c0:c1]
    kr = z[:, c1:c1 + LANES]
    gate_out[...] = z[:, c1 + LANES:].astype(gate_out.dtype)
    to_rope_lanes = LANES - QK_ROPE_DIM

    cqn = _rmsnorm(cq, qn_ref[...]).astype(jnp.bfloat16)
    qa = jnp.dot(cqn, wqa_ref[...], preferred_element_type=jnp.float32)
    ckvn = _rmsnorm(ckv, kvn_ref[...]).astype(jnp.bfloat16)
    kn = jnp.dot(ckvn, wk_ref[...], preferred_element_type=jnp.float32)
    v_out[...] = (jnp.dot(ckvn, wv_ref[...], preferred_element_type=jnp.float32)
                  + vone_ref[...]).astype(v_out.dtype)

    cos_q, sin_q = cq_ref[...], sq_ref[...]
    k_rope = kr * ck_ref[...] + pltpu.roll(kr, to_rope_lanes, 1) * sk_ref[...]
    for h in range(N_HEADS_B):
        blk = slice(h * LANES, (h + 1) * LANES)
        qh = qa[:, blk]
        q_out[:, blk] = (qh * cos_q + pltpu.roll(qh, to_rope_lanes, 1) * sin_q).astype(q_out.dtype)
        k_out[:, blk] = (kn[:, blk] + k_rope).astype(k_out.dtype)


def _mid_kernel(*refs):
    merge_refs, rest = refs[:N_MERGE_A_REFS], refs[N_MERGE_A_REFS:]
    x1_out, project_refs = rest[-5], rest[:-5] + rest[-4:]
    x1 = _merge_a(*merge_refs)
    x1_out[...] = x1
    _project_b(x1, *project_refs)


def _mid(x, mod0, os_, lses, gate_a, w_out_a, ln_g, ln_b,
         mod1, w_in, q_norm, kv_norm, wqa, wk, wv, tables):
    B, S, D = x.shape
    tl = TL_PROJ
    gw = GROUP_WIDTH_A
    HP = N_HEADS_B * LANES
    col = np.arange(WIDTH_A)
    src_lane = (col // gw) * GROUP_LANE_STRIDE + (col % gw) // HEAD_DIM_A
    expand = jnp.asarray((np.arange(LANES)[:, None] == src_lane[None, :]).astype(np.float32), dtype=jnp.bfloat16)
    row = lambda b, i: (b, i, 0)
    cls = lambda b, i: (b, 0, i, 0)
    const2 = lambda b, i: (0, 0)
    mod_spec = pl.BlockSpec((None, 1, 3 * D), lambda b, i: (b, 0, 0))
    tab = pl.BlockSpec((tl, LANES), lambda b, i: (i, 0))
    o_specs = [pl.BlockSpec((None, dil, tl // dil, gw), cls) for _, dil in DIL_GROUPS]
    l_specs = [pl.BlockSpec((None, dil, tl // dil, LANES), cls) for _, dil in DIL_GROUPS]
    merge_args = [x, mod0, *os_, *lses, gate_a, expand, w_out_a, ln_g.reshape(1, D), ln_b.reshape(1, D)]
    merge_specs = [pl.BlockSpec((None, tl, D), row), mod_spec, *o_specs, *l_specs,
                   pl.BlockSpec((None, tl, WIDTH_A), row),
                   pl.BlockSpec(expand.shape, const2), pl.BlockSpec(w_out_a.shape, const2),
                   pl.BlockSpec((1, D), const2), pl.BlockSpec((1, D), const2)]
    assert len(merge_args) == N_MERGE_A_REFS
    project_args = [mod1, w_in, q_norm.reshape(1, -1), kv_norm.reshape(1, -1), wqa, wk, wv, _v_ones_row(), *tables]
    project_specs = [mod_spec, pl.BlockSpec(w_in.shape, const2),
                     pl.BlockSpec((1, Q_LORA_RANK), const2), pl.BlockSpec((1, KV_LORA_RANK), const2),
                     pl.BlockSpec(wqa.shape, const2), pl.BlockSpec(wk.shape, const2),
                     pl.BlockSpec(wv.shape, const2), pl.BlockSpec((1, HP), const2), tab, tab, tab, tab]
    wide = jax.ShapeDtypeStruct((B, S, HP), jnp.bfloat16)
    return pl.pallas_call(
        _mid_kernel,
        grid=(B, S // tl),
        in_specs=merge_specs + project_specs,
        out_specs=[pl.BlockSpec((None, tl, D), row),
                   pl.BlockSpec((None, tl, HP), row), pl.BlockSpec((None, tl, HP), row),
                   pl.BlockSpec((None, tl, HP), row), pl.BlockSpec((None, tl, WIDTH_B), row)],
        out_shape=[jax.ShapeDtypeStruct((B, S, D), jnp.float32), wide, wide, wide,
                   jax.ShapeDtypeStruct((B, S, WIDTH_B), jnp.bfloat16)],
        compiler_params=_params("arbitrary", "arbitrary"),
        name="mid",
    )(*merge_args, *project_args)


def _attn_b_kernel(q_ref, k_ref, v_ref, gate_ref, o_ref, s_even, s_odd, mx_scr):
    S = k_ref.shape[0]
    tq = s_even.shape[0]
    n_tiles = S // tq
    n_chunks = S // TK_B
    n_heads = q_ref.shape[1] // LANES
    assert n_heads % 2 == 0
    s_bufs = (s_even, s_odd)

    def head_lanes(e):
        return slice(e * LANES, (e + 1) * LANES)

    def tile_rows(t):
        return pl.ds(t * tq, tq) if isinstance(t, int) else pl.ds(pl.multiple_of(t * tq, tq), tq)

    def scores(t, e, c, mx):
        ks = slice(c * TK_B, (c + 1) * TK_B)
        s = lax.dot_general(q_ref[tile_rows(t), head_lanes(e)], k_ref[ks, head_lanes(e)], _NT,
                            preferred_element_type=jnp.float32)
        s_bufs[e % 2][:, ks] = s
        for j in range(TK_B // LANES):
            mx = jnp.maximum(mx, s[:, j * LANES:(j + 1) * LANES])
        return mx

    def weighted_values(e, c, m, acc):
        ks = slice(c * TK_B, (c + 1) * TK_B)
        p = jnp.exp2(s_bufs[e % 2][:, ks] - m).astype(jnp.bfloat16)
        return acc + jnp.dot(p, v_ref[ks, head_lanes(e)], preferred_element_type=jnp.float32)

    neg = jnp.full((tq, LANES), -jnp.inf, jnp.float32)
    zero = jnp.zeros((tq, LANES), jnp.float32)
    lane = lax.broadcasted_iota(jnp.int32, (tq, LANES), 1)

    mx = neg
    for c in range(n_chunks):
        mx = scores(0, 0, c, mx)
    mx_scr[...] = mx

    def tile(t, carry):
        t_next = jnp.minimum(t + 1, n_tiles - 1)
        mx = mx_scr[...]
        acc = []
        for e in range(n_heads):
            m = jnp.max(mx, axis=-1, keepdims=True)
            mx, a = neg, zero
            for c in range(n_chunks):
                mx = scores(t, e + 1, c, mx) if e + 1 < n_heads else scores(t_next, 0, c, mx)
                a = weighted_values(e, c, m, a)
            acc.append(a)
        mx_scr[...] = mx
        for pair in range(n_heads // 2):
            acc_e, acc_o = acc[2 * pair], acc[2 * pair + 1]
            inv_e = 1.0 / acc_e[:, V_HEAD_DIM:V_HEAD_DIM + 1]
            inv_o = 1.0 / acc_o[:, 0:1]
            cols = slice(pair * LANES, (pair + 1) * LANES)
            o = jnp.where(lane < V_HEAD_DIM, acc_e * inv_e, acc_o * inv_o)
            gate = gate_ref[tile_rows(t), cols].astype(jnp.float32)
            o_ref[tile_rows(t), cols] = (o * _silu(gate)).astype(o_ref.dtype)
        return carry

    lax.fori_loop(0, n_tiles, tile, 0)


def _attn_b(q2, k2, v2, gate):
    B, S, HP = q2.shape
    hw = HEADS_PER_STEP_B * LANES
    ow = HEADS_PER_STEP_B * V_HEAD_DIM
    heads = lambda b, p: (b, 0, p)
    return pl.pallas_call(
        _attn_b_kernel,
        grid=(B, N_HEADS_B // HEADS_PER_STEP_B),
        in_specs=[pl.BlockSpec((None, S, hw), heads), pl.BlockSpec((None, S, hw), heads),
                  pl.BlockSpec((None, S, hw), heads), pl.BlockSpec((None, S, ow), heads)],
        out_specs=pl.BlockSpec((None, S, ow), heads),
        out_shape=jax.ShapeDtypeStruct((B, S, WIDTH_B), jnp.bfloat16),
        scratch_shapes=[pltpu.VMEM((TQ_B, S), jnp.float32), pltpu.VMEM((TQ_B, S), jnp.float32),
                        pltpu.VMEM((TQ_B, LANES), jnp.float32)],
        compiler_params=_params("arbitrary", "arbitrary"),
        name="attn_b",
    )(q2, k2, v2, gate)


def _out_b_kernel(x_ref, mod_ref, yg_ref, w_ref, g_ref, b_ref, out_ref):
    D = x_ref.shape[-1]
    y = jnp.dot(yg_ref[...], w_ref[...], preferred_element_type=jnp.float32)
    z = DEEPNORM_ALPHA * x_ref[...] + mod_ref[:, 2 * D:] * y
    out_ref[...] = _layernorm(z, g_ref[...], b_ref[...])


def _out_b(x, mod, yg, w_out, ln_g, ln_b):
    B, S, D = x.shape
    tl = TL_WIDE
    row = lambda b, i: (b, i, 0)
    const2 = lambda b, i: (0, 0)
    return pl.pallas_call(
        _out_b_kernel,
        grid=(B, S // tl),
        in_specs=[pl.BlockSpec((None, tl, D), row),
                  pl.BlockSpec((None, 1, 3 * D), lambda b, i: (b, 0, 0)),
                  pl.BlockSpec((None, tl, WIDTH_B), row),
                  pl.BlockSpec(w_out.shape, const2),
                  pl.BlockSpec((1, D), const2), pl.BlockSpec((1, D), const2)],
        out_specs=pl.BlockSpec((None, tl, D), row),
        out_shape=jax.ShapeDtypeStruct((B, S, D), jnp.float32),
        compiler_params=_params("arbitrary", "arbitrary"),
        name="out_b",
    )(x, mod, yg, w_out, ln_g.reshape(1, D), ln_b.reshape(1, D))


def _weights_a(a_w_in):
    W = WIDTH_A
    gw = GROUP_WIDTH_A
    q, k, v, gate = (a_w_in[:, i * W:(i + 1) * W] for i in range(4))
    ws = []
    for g in range(len(DIL_GROUPS)):
        cs = slice(g * gw, (g + 1) * gw)
        parts = [q[:, cs] * (HEAD_DIM_A ** -0.5 * LOG2E), k[:, cs], v[:, cs]]
        if g == 0:
            parts.append(gate)
        ws.append(jnp.concatenate(parts, axis=1).astype(jnp.bfloat16))
    return ws


def _swap_halves(w):
    half = w.shape[-1] // 2
    return jnp.concatenate([w[..., half:], w[..., :half]], axis=-1)


def _pad_lanes(w, lo):
    n = w.shape[-1]
    return jnp.pad(w, [(0, 0)] * (w.ndim - 1) + [(lo, LANES - lo - n)])


def _weights_b(b_w_in, b_w_uq, b_w_ukv):
    c0 = Q_LORA_RANK
    c1 = c0 + KV_LORA_RANK
    c2 = c1 + QK_ROPE_DIM
    kr = b_w_in[:, c1:c2]
    w_in = jnp.concatenate([b_w_in[:, :c1], _pad_lanes(jnp.concatenate([kr, _swap_halves(kr)], axis=1), QK_NOPE_DIM),
                            b_w_in[:, c2:]], axis=1)
    uq = b_w_uq.reshape(Q_LORA_RANK, N_HEADS_B, QK_NOPE_DIM + QK_ROPE_DIM)
    wqa = jnp.concatenate([uq, _swap_halves(uq[..., QK_NOPE_DIM:])], axis=-1).reshape(Q_LORA_RANK, -1)
    ukv = b_w_ukv.reshape(KV_LORA_RANK, N_HEADS_B, QK_NOPE_DIM + V_HEAD_DIM)
    wk = _pad_lanes(ukv[..., :QK_NOPE_DIM], 0).reshape(KV_LORA_RANK, -1)
    vv = ukv[..., QK_NOPE_DIM:]
    odd = (jnp.arange(N_HEADS_B) % 2 == 1)[None, :, None]
    wv = jnp.where(odd, _pad_lanes(vv, LANES - V_HEAD_DIM), _pad_lanes(vv, 0)).reshape(KV_LORA_RANK, -1)
    bf = lambda w: w.astype(jnp.bfloat16)
    return bf(w_in), bf(wqa), bf(wk), bf(wv)


def _rope_tables(S):
    pos = jnp.arange(S, dtype=jnp.float32)
    inv_freq = ROPE_THETA ** (-jnp.arange(0, QK_ROPE_DIM, 2, dtype=jnp.float32) / QK_ROPE_DIM)
    ang = pos[:, None] * inv_freq[None, :]
    cos, sin = jnp.cos(ang), jnp.sin(ang)
    cosf = _pad_lanes(jnp.concatenate([cos, cos], axis=-1), QK_NOPE_DIM)
    sinf = _pad_lanes(jnp.concatenate([-sin, sin], axis=-1), QK_NOPE_DIM)
    nope = _pad_lanes(jnp.ones((S, QK_NOPE_DIM), jnp.float32), 0)
    scale = (QK_NOPE_DIM + QK_ROPE_DIM) ** -0.5 * LOG2E
    return (scale * (nope + cosf), scale * sinf, cosf, sinf)


def _v_ones_row():
    row = np.zeros((N_HEADS_B, LANES), np.float32)
    row[0::2, V_HEAD_DIM] = 1.0
    row[1::2, 0] = 1.0
    return jnp.asarray(row.reshape(1, -1))


def kernel(x, c, rel_bias, ada_w, ada_b, ln_g, ln_b, a_w_in, a_w_out,
           b_w_in, b_q_norm, b_w_uq, b_kv_norm, b_w_ukv, b_w_out):
    B, S, D = x.shape
    mods = _modulation(c, ada_w, ada_b)
    mod0 = mods[0].reshape(B, 1, 3 * D)
    mod1 = mods[1].reshape(B, 1, 3 * D)

    for window, dil in DIL_GROUPS:
        assert window // (2 * dil) == HALF_WINDOW
    *qkvs, gate_a = _proj_a(x, mod0, _weights_a(a_w_in[0]))
    os_, lses = [], []
    for g, ((_, dil), qkv) in enumerate(zip(DIL_GROUPS, qkvs)):
        hs = slice(g * HEADS_PER_GROUP_A, (g + 1) * HEADS_PER_GROUP_A)
        o, lse = _attn_a(qkv, _bias_variants(rel_bias[:, hs], dil))
        os_.append(o)
        lses.append(lse)

    w_in, wqa, wk, wv = _weights_b(b_w_in[0], b_w_uq[0], b_w_ukv[0])
    x1, q2, k2, v2, gate_b = _mid(x, mod0, os_, lses, gate_a, a_w_out[0].astype(jnp.bfloat16), ln_g[0], ln_b[0],
                                  mod1, w_in, b_q_norm[0], b_kv_norm[0], wqa, wk, wv, _rope_tables(S))
    yg = _attn_b(q2, k2, v2, gate_b)
    return _out_b(x1, mod1, yg, b_w_out[0].astype(jnp.bfloat16), ln_g[1], ln_b[1])
```

```python
import math

import numpy as np
import jax
import jax.numpy as jnp
from jax import lax
from jax.experimental import pallas as pl
from jax.experimental.pallas import tpu as pltpu

DEPTH = 2
HEAD_DIM_A = 64
DIL_GROUPS = ((128, 1), (512, 4), (2048, 16))
HEADS_PER_GROUP_A = 6
GROUP_WIDTH_A = HEADS_PER_GROUP_A * HEAD_DIM_A
WIDTH_A = GROUP_WIDTH_A * len(DIL_GROUPS)
N_BUCKETS = 32
T5_MAX_DISTANCE = 1024
N_HEADS_B = 16
QK_NOPE_DIM = 64
QK_ROPE_DIM = 32
V_HEAD_DIM = 64
Q_LORA_RANK = 256
KV_LORA_RANK = 128
WIDTH_B = N_HEADS_B * V_HEAD_DIM
ROPE_THETA = 10000.0
DEEPNORM_ALPHA = (2.0 * DEPTH) ** 0.25
LN_EPS = 1e-5
RMS_EPS = 1e-6
NEG_INF = -1e30
LOG2E = math.log2(math.e)
LN2 = math.log(2.0)

LANES = 128
VMEM_LIMIT_BYTES = 56 * 1024 * 1024

HALF_WINDOW = 64
TQ_A = 128
TK_A = TQ_A + 2 * HALF_WINDOW
GROUP_LANE_STRIDE = 8
MAX_TILES_PER_STEP_A = 16
TL_PROJ = 512
TL_WIDE = 1024
TQ_B = 512
TK_B = 512
HEADS_PER_STEP_B = 4

_NT = (((1,), (1,)), ((), ()))


def _silu(x):
    return x * (0.5 + 0.5 * jnp.tanh(0.5 * x))


def _params(*sem):
    return pltpu.CompilerParams(dimension_semantics=sem, vmem_limit_bytes=VMEM_LIMIT_BYTES)


def _mod_kernel(c_ref, w_ref, b_ref, o_ref):
    sc = _silu(c_ref[...]).astype(jnp.bfloat16)
    o_ref[...] = jnp.dot(sc, w_ref[...].astype(jnp.bfloat16),
                         preferred_element_type=jnp.float32) + b_ref[...]


def _modulation(c, ada_w, ada_b):
    B, D = c.shape
    nj = 3
    return pl.pallas_call(
        _mod_kernel,
        grid=(DEPTH, nj),
        in_specs=[pl.BlockSpec((B, D), lambda i, j: (0, 0)),
                  pl.BlockSpec((None, D, D), lambda i, j: (i, 0, j)),
                  pl.BlockSpec((None, 1, D), lambda i, j: (i, 0, j))],
        out_specs=pl.BlockSpec((None, B, D), lambda i, j: (i, 0, j)),
        out_shape=jax.ShapeDtypeStruct((DEPTH, B, 3 * D), jnp.float32),
        compiler_params=_params("arbitrary", "arbitrary"),
        name="modulation",
    )(c, ada_w, ada_b.reshape(DEPTH, 1, 3 * D))


def _modulate(x, mod):
    D = x.shape[-1]
    return x * (1.0 + mod[:, D:2 * D]) + mod[:, :D]


def _layernorm(z, g, b):
    mu = jnp.mean(z, axis=-1, keepdims=True)
    zc = z - mu
    var = jnp.mean(zc * zc, axis=-1, keepdims=True)
    return zc * lax.rsqrt(var + LN_EPS) * g + b


def _proj_a_kernel(x_ref, mod_ref, w0_ref, w1_ref, w2_ref, qkv0_ref, qkv1_ref, qkv2_ref, gate_ref):
    u = _modulate(x_ref[...], mod_ref[...]).astype(jnp.bfloat16)
    n_qkv = qkv0_ref.shape[-1]
    z = jnp.dot(u, w0_ref[...], preferred_element_type=jnp.float32)
    qkv0_ref[0] = z[:, :n_qkv].astype(qkv0_ref.dtype)
    gate_ref[...] = _silu(z[:, n_qkv:]).astype(gate_ref.dtype)
    for w_ref, qkv_ref in ((w1_ref, qkv1_ref), (w2_ref, qkv2_ref)):
        dil, rows = qkv_ref.shape[:2]
        up = pltpu.einshape("(md)n->(dm)n", u, d=dil)
        z = jnp.dot(up, w_ref[...], preferred_element_type=jnp.float32)
        for r in range(dil):
            qkv_ref[r] = z[r * rows:(r + 1) * rows].astype(qkv_ref.dtype)


def _proj_a(x, mod, ws):
    B, S, D = x.shape
    tl = TL_WIDE
    n_qkv = 3 * GROUP_WIDTH_A
    dils = [dil for _, dil in DIL_GROUPS]
    assert dils[0] == 1
    row = lambda b, i: (b, i, 0)
    cls = lambda b, i: (b, 0, i, 0)
    const2 = lambda b, i: (0, 0)
    return pl.pallas_call(
        _proj_a_kernel,
        grid=(B, S // tl),
        in_specs=[pl.BlockSpec((None, tl, D), row),
                  pl.BlockSpec((None, 1, 3 * D), lambda b, i: (b, 0, 0)),
                  *[pl.BlockSpec(w.shape, const2) for w in ws]],
        out_specs=[*[pl.BlockSpec((None, dil, tl // dil, n_qkv), cls) for dil in dils],
                   pl.BlockSpec((None, tl, WIDTH_A), row)],
        out_shape=[*[jax.ShapeDtypeStruct((B, dil, S // dil, n_qkv), jnp.bfloat16) for dil in dils],
                   jax.ShapeDtypeStruct((B, S, WIDTH_A), jnp.bfloat16)],
        compiler_params=_params("arbitrary", "arbitrary"),
        name="proj_a",
    )(x, mod, *ws)


def _attn_a_kernel(qkv_ref, bias_ref, o_ref, lse_ref, s_scr, p_scr):
    n_cls, L = (1,) + qkv_ref.shape[:1] if len(qkv_ref.shape) == 2 else qkv_ref.shape[:2]

    def of_class(ref, cls):
        return ref if len(ref.shape) == 2 else ref.at[cls]
    n_tiles = L // TQ_A
    gw = GROUP_WIDTH_A
    lane = lax.broadcasted_iota(jnp.int32, (TQ_A, LANES), 1)
    low_q = lane < HEAD_DIM_A
    low_k = lax.broadcasted_iota(jnp.int32, (TK_A, LANES), 1) < HEAD_DIM_A

    heads = range(HEADS_PER_GROUP_A)

    def pair_block(cls, rows, section, h):
        c0 = section * gw + (h // 2) * LANES
        return of_class(qkv_ref, cls)[rows, c0:c0 + LANES]

    def other_lanes(block, low_mask, h, fill):
        mine = low_mask if h % 2 == 0 else jnp.logical_not(low_mask)
        return jnp.where(mine, block, jnp.full_like(block, fill))

    tiles_per_step = s_scr.shape[0]

    def tile_group(tg, carry):
        geo = []
        for u in range(tiles_per_step):
            item = tg * tiles_per_step + u
            cls, t = (0, item) if n_cls == 1 else (item // n_tiles, item % n_tiles)
            j0 = pl.multiple_of(t * TQ_A, TQ_A)
            ws = pl.multiple_of(jnp.clip(j0 - HALF_WINDOW, 0, L - TK_A), HALF_WINDOW)
            variant = jnp.where(t == 0, 0, jnp.where(t == n_tiles - 1, 2, 1))
            geo.append((cls, pl.ds(j0, TQ_A), pl.ds(ws, TK_A), variant))
        for u, (cls, rq, rk, variant) in enumerate(geo):
            for h in heads:
                q = other_lanes(pair_block(cls, rq, 0, h), low_q, h, 0)
                s = lax.dot_general(q, pair_block(cls, rk, 1, h), _NT, preferred_element_type=jnp.float32)
                s_scr[u, h] = s + bias_ref[variant, h]
        m = [[jnp.max(s_scr[u, h], axis=-1, keepdims=True) for h in heads] for u in range(len(geo))]
        for u in range(len(geo)):
            for h in heads:
                p_scr[u, h] = jnp.exp2(s_scr[u, h] - m[u][h]).astype(jnp.bfloat16)
        for u, (cls, rq, rk, variant) in enumerate(geo):
            lse_tile = jnp.zeros((TQ_A, LANES), jnp.float32)
            for pair in range(HEADS_PER_GROUP_A // 2):
                he, ho = 2 * pair, 2 * pair + 1
                acc_e = jnp.dot(p_scr[u, he], other_lanes(pair_block(cls, rk, 2, he), low_k, he, 1),
                                preferred_element_type=jnp.float32)
                acc_o = jnp.dot(p_scr[u, ho], other_lanes(pair_block(cls, rk, 2, ho), low_k, ho, 1),
                                preferred_element_type=jnp.float32)
                numer = jnp.where(low_q, acc_e, acc_o)
                den_other = jnp.where(low_q, acc_o, acc_e)
                den = pltpu.roll(den_other, HEAD_DIM_A, 1)
                of_class(o_ref, cls)[rq, pair * LANES:(pair + 1) * LANES] = (numer / den).astype(o_ref.dtype)
                is_odd = lane == ho
                lse = (jnp.where(is_odd, m[u][ho], m[u][he])
                       + jnp.log2(jnp.where(is_odd, den_other, den))) * LN2
                lse_tile = jnp.where(jnp.logical_or(is_odd, lane == he), lse, lse_tile)
            of_class(lse_ref, cls)[rq, :] = lse_tile
        return carry

    lax.fori_loop(0, n_cls * n_tiles // tiles_per_step, tile_group, 0)


def _attn_a(qkv, biasm):
    B, dil, L, n_qkv = qkv.shape
    gw = GROUP_WIDTH_A
    n_tiles = L // TQ_A
    n_cls = min(dil, max(1, MAX_TILES_PER_STEP_A // n_tiles))
    tiles_per_step = math.gcd(MAX_TILES_PER_STEP_A, n_cls * n_tiles)
    blk_cls = None if n_cls == 1 else n_cls
    cls = lambda b, r: (b, r, 0, 0)
    return pl.pallas_call(
        _attn_a_kernel,
        grid=(B, dil // n_cls),
        in_specs=[pl.BlockSpec((None, blk_cls, L, n_qkv), cls),
                  pl.BlockSpec(biasm.shape, lambda b, r: (0, 0, 0, 0))],
        out_specs=[pl.BlockSpec((None, blk_cls, L, gw), cls),
                   pl.BlockSpec((None, blk_cls, L, LANES), cls)],
        out_shape=[jax.ShapeDtypeStruct((B, dil, L, gw), jnp.bfloat16),
                   jax.ShapeDtypeStruct((B, dil, L, LANES), jnp.float32)],
        scratch_shapes=[
            pltpu.VMEM((tiles_per_step, HEADS_PER_GROUP_A, TQ_A, TK_A), jnp.float32),
            pltpu.VMEM((tiles_per_step, HEADS_PER_GROUP_A, TQ_A, TK_A), jnp.bfloat16)],
        compiler_params=_params("arbitrary", "arbitrary"),
        name=f"attn_a_d{dil}",
    )(qkv, biasm)


def _t5_bucket_np(rel):
    half = N_BUCKETS // 2
    max_exact = half // 2
    base = np.where(rel > 0, half, 0)
    n = np.abs(rel)
    nf = np.maximum(n, 1).astype(np.float32)
    large = max_exact + (np.log(nf / np.float32(max_exact)) / np.float32(math.log(T5_MAX_DISTANCE / max_exact))
                         * np.float32(half - max_exact)).astype(np.int32)
    large = np.minimum(large, half - 1)
    return base + np.where(n < max_exact, n, large)


def _bias_variants(rel_bias_g, dil):
    H = rel_bias_g.shape[1]
    M = 2 * TK_A
    off = np.arange(M) - (TK_A - 1)
    idx = np.where(np.abs(off) <= HALF_WINDOW, _t5_bucket_np(off * dil), N_BUCKETS).astype(np.int32)
    table = jnp.concatenate([rel_bias_g.astype(jnp.float32) * LOG2E, jnp.full((1, H), NEG_INF, jnp.float32)], axis=0)
    w = table[jnp.asarray(idx)].T
    rot = jnp.tile(w, (1, TQ_A + 1))[:, :TQ_A * (M + 1)].reshape(H, TQ_A, M + 1)[:, ::-1, :]
    variants = []
    for shift in (0, HALF_WINDOW, 2 * HALF_WINDOW):
        j0 = TK_A - TQ_A - shift
        variants.append(rot[:, :, j0:j0 + TK_A])
    return jnp.stack(variants)


N_MERGE_A_REFS = 13


def _merge_a(x_ref, mod_ref, o0_ref, o1_ref, o2_ref, l0_ref, l1_ref, l2_ref, gate_ref,
             expand_ref, w_ref, g_ref, b_ref):
    tl, D = x_ref.shape
    stride = GROUP_LANE_STRIDE
    lane = lax.broadcasted_iota(jnp.int32, (tl, LANES), 1)

    def natural_order(o_ref, l_ref):
        if o_ref.shape[0] == 1:
            return o_ref[0].astype(jnp.float32), l_ref[0]
        return (pltpu.einshape("dmn->(md)n", o_ref[...].astype(jnp.float32)),
                pltpu.einshape("dmn->(md)n", l_ref[...]))

    o0, l0 = natural_order(o0_ref, l0_ref)
    o1, l1 = natural_order(o1_ref, l1_ref)
    o2, l2 = natural_order(o2_ref, l2_ref)
    mx = jnp.maximum(jnp.maximum(l0, l1), l2)
    e0, e1, e2 = jnp.exp(l0 - mx), jnp.exp(l1 - mx), jnp.exp(l2 - mx)
    inv = 1.0 / (e0 + e1 + e2)
    wt = jnp.where(lane < stride, e0 * inv,
                   jnp.where(lane < 2 * stride, pltpu.roll(e1 * inv, stride, 1), pltpu.roll(e2 * inv, 2 * stride, 1)))
    hi = wt.astype(jnp.bfloat16)
    lo = (wt - hi.astype(jnp.float32)).astype(jnp.bfloat16)
    wexp = (jnp.dot(hi, expand_ref[...], preferred_element_type=jnp.float32)
            + jnp.dot(lo, expand_ref[...], preferred_element_type=jnp.float32))
    o = jnp.concatenate([o0, o1, o2], axis=1)
    yg = (o * wexp * gate_ref[...].astype(jnp.float32)).astype(jnp.bfloat16)
    y = jnp.dot(yg, w_ref[...], preferred_element_type=jnp.float32)
    z = DEEPNORM_ALPHA * x_ref[...] + mod_ref[:, 2 * D:] * y
    return _layernorm(z, g_ref[...], b_ref[...])


def _rmsnorm(x, g):
    return x * lax.rsqrt(jnp.mean(x * x, axis=-1, keepdims=True) + RMS_EPS) * g


def _project_b(x, mod_ref, w_in_ref, qn_ref, kvn_ref, wqa_ref, wk_ref, wv_ref,
               vone_ref, cq_ref, sq_ref, ck_ref, sk_ref, q_out, k_out, v_out, gate_out):
    u = _modulate(x, mod_ref[...]).astype(jnp.bfloat16)
    z = jnp.dot(u, w_in_ref[...], preferred_element_type=jnp.float32)
    c0 = Q_LORA_RANK
    c1 = c0 + KV_LORA_RANK
    cq = z[:, :c0]
    ckv = z[:, c0:c1]
    kr = z[:, c1:c1 + LANES]
    gate_out[...] = z[:, c1 + LANES:].astype(gate_out.dtype)
    to_rope_lanes = LANES - QK_ROPE_DIM

    cqn = _rmsnorm(cq, qn_ref[...]).astype(jnp.bfloat16)
    qa = jnp.dot(cqn, wqa_ref[...], preferred_element_type=jnp.float32)
    ckvn = _rmsnorm(ckv, kvn_ref[...]).astype(jnp.bfloat16)
    kn = jnp.dot(ckvn, wk_ref[...], preferred_element_type=jnp.float32)
    v_out[...] = (jnp.dot(ckvn, wv_ref[...], preferred_element_type=jnp.float32)
                  + vone_ref[...]).astype(v_out.dtype)

    cos_q, sin_q = cq_ref[...], sq_ref[...]
    k_rope = kr * ck_ref[...] + pltpu.roll(kr, to_rope_lanes, 1) * sk_ref[...]
    for h in range(N_HEADS_B):
        blk = slice(h * LANES, (h + 1) * LANES)
        qh = qa[:, blk]
        q_out[:, blk] = (qh * cos_q + pltpu.roll(qh, to_rope_lanes, 1) * sin_q).astype(q_out.dtype)
        k_out[:, blk] = (kn[:, blk] + k_rope).astype(k_out.dtype)


def _mid_kernel(*refs):
    merge_refs, rest = refs[:N_MERGE_A_REFS], refs[N_MERGE_A_REFS:]
    x1_out, project_refs = rest[-5], rest[:-5] + rest[-4:]
    x1 = _merge_a(*merge_refs)
    x1_out[...] = x1
    _project_b(x1, *project_refs)


def _mid(x, mod0, os_, lses, gate_a, w_out_a, ln_g, ln_b,
         mod1, w_in, q_norm, kv_norm, wqa, wk, wv, tables):
    B, S, D = x.shape
    tl = TL_PROJ
    gw = GROUP_WIDTH_A
    HP = N_HEADS_B * LANES
    col = np.arange(WIDTH_A)
    src_lane = (col // gw) * GROUP_LANE_STRIDE + (col % gw) // HEAD_DIM_A
    expand = jnp.asarray((np.arange(LANES)[:, None] == src_lane[None, :]).astype(np.float32), dtype=jnp.bfloat16)
    row = lambda b, i: (b, i, 0)
    cls = lambda b, i: (b, 0, i, 0)
    const2 = lambda b, i: (0, 0)
    mod_spec = pl.BlockSpec((None, 1, 3 * D), lambda b, i: (b, 0, 0))
    tab = pl.BlockSpec((tl, LANES), lambda b, i: (i, 0))
    o_specs = [pl.BlockSpec((None, dil, tl // dil, gw), cls) for _, dil in DIL_GROUPS]
    l_specs = [pl.BlockSpec((None, dil, tl // dil, LANES), cls) for _, dil in DIL_GROUPS]
    merge_args = [x, mod0, *os_, *lses, gate_a, expand, w_out_a, ln_g.reshape(1, D), ln_b.reshape(1, D)]
    merge_specs = [pl.BlockSpec((None, tl, D), row), mod_spec, *o_specs, *l_specs,
                   pl.BlockSpec((None, tl, WIDTH_A), row),
                   pl.BlockSpec(expand.shape, const2), pl.BlockSpec(w_out_a.shape, const2),
                   pl.BlockSpec((1, D), const2), pl.BlockSpec((1, D), const2)]
    assert len(merge_args) == N_MERGE_A_REFS
    project_args = [mod1, w_in, q_norm.reshape(1, -1), kv_norm.reshape(1, -1), wqa, wk, wv, _v_ones_row(), *tables]
    project_specs = [mod_spec, pl.BlockSpec(w_in.shape, const2),
                     pl.BlockSpec((1, Q_LORA_RANK), const2), pl.BlockSpec((1, KV_LORA_RANK), const2),
                     pl.BlockSpec(wqa.shape, const2), pl.BlockSpec(wk.shape, const2),
                     pl.BlockSpec(wv.shape, const2), pl.BlockSpec((1, HP), const2), tab, tab, tab, tab]
    wide = jax.ShapeDtypeStruct((B, S, HP), jnp.bfloat16)
    return pl.pallas_call(
        _mid_kernel,
        grid=(B, S // tl),
        in_specs=merge_specs + project_specs,
        out_specs=[pl.BlockSpec((None, tl, D), row),
                   pl.BlockSpec((None, tl, HP), row), pl.BlockSpec((None, tl, HP), row),
                   pl.BlockSpec((None, tl, HP), row), pl.BlockSpec((None, tl, WIDTH_B), row)],
        out_shape=[jax.ShapeDtypeStruct((B, S, D), jnp.float32), wide, wide, wide,
                   jax.ShapeDtypeStruct((B, S, WIDTH_B), jnp.bfloat16)],
        compiler_params=_params("arbitrary", "arbitrary"),
        name="mid",
    )(*merge_args, *project_args)


def _attn_b_kernel(q_ref, k_ref, v_ref, gate_ref, o_ref, s_even, s_odd, mx_scr):
    S = k_ref.shape[0]
    tq = s_even.shape[0]
    n_tiles = S // tq
    n_chunks = S // TK_B
    n_heads = q_ref.shape[1] // LANES
    assert n_heads % 2 == 0
    s_bufs = (s_even, s_odd)

    def head_lanes(e):
        return slice(e * LANES, (e + 1) * LANES)

    def tile_rows(t):
        return pl.ds(t * tq, tq) if isinstance(t, int) else pl.ds(pl.multiple_of(t * tq, tq), tq)

    def scores(t, e, c, mx):
        ks = slice(c * TK_B, (c + 1) * TK_B)
        s = lax.dot_general(q_ref[tile_rows(t), head_lanes(e)], k_ref[ks, head_lanes(e)], _NT,
                            preferred_element_type=jnp.float32)
        s_bufs[e % 2][:, ks] = s
        for j in range(TK_B // LANES):
            mx = jnp.maximum(mx, s[:, j * LANES:(j + 1) * LANES])
        return mx

    def weighted_values(e, c, m, acc):
        ks = slice(c * TK_B, (c + 1) * TK_B)
        p = jnp.exp2(s_bufs[e % 2][:, ks] - m).astype(jnp.bfloat16)
        return acc + jnp.dot(p, v_ref[ks, head_lanes(e)], preferred_element_type=jnp.float32)

    neg = jnp.full((tq, LANES), -jnp.inf, jnp.float32)
    zero = jnp.zeros((tq, LANES), jnp.float32)
    lane = lax.broadcasted_iota(jnp.int32, (tq, LANES), 1)

    mx = neg
    for c in range(n_chunks):
        mx = scores(0, 0, c, mx)
    mx_scr[...] = mx

    def tile(t, carry):
        t_next = jnp.minimum(t + 1, n_tiles - 1)
        mx = mx_scr[...]
        acc = []
        for e in range(n_heads):
            m = jnp.max(mx, axis=-1, keepdims=True)
            mx, a = neg, zero
            for c in range(n_chunks):
                mx = scores(t, e + 1, c, mx) if e + 1 < n_heads else scores(t_next, 0, c, mx)
                a = weighted_values(e, c, m, a)
            acc.append(a)
        mx_scr[...] = mx
        for pair in range(n_heads // 2):
            acc_e, acc_o = acc[2 * pair], acc[2 * pair + 1]
            inv_e = 1.0 / acc_e[:, V_HEAD_DIM:V_HEAD_DIM + 1]
            inv_o = 1.0 / acc_o[:, 0:1]
            cols = slice(pair * LANES, (pair + 1) * LANES)
            o = jnp.where(lane < V_HEAD_DIM, acc_e * inv_e, acc_o * inv_o)
            gate = gate_ref[tile_rows(t), cols].astype(jnp.float32)
            o_ref[tile_rows(t), cols] = (o * _silu(gate)).astype(o_ref.dtype)
        return carry

    lax.fori_loop(0, n_tiles, tile, 0)


def _attn_b(q2, k2, v2, gate):
    B, S, HP = q2.shape
    hw = HEADS_PER_STEP_B * LANES
    ow = HEADS_PER_STEP_B * V_HEAD_DIM
    heads = lambda b, p: (b, 0, p)
    return pl.pallas_call(
        _attn_b_kernel,
        grid=(B, N_HEADS_B // HEADS_PER_STEP_B),
        in_specs=[pl.BlockSpec((None, S, hw), heads), pl.BlockSpec((None, S, hw), heads),
                  pl.BlockSpec((None, S, hw), heads), pl.BlockSpec((None, S, ow), heads)],
        out_specs=pl.BlockSpec((None, S, ow), heads),
        out_shape=jax.ShapeDtypeStruct((B, S, WIDTH_B), jnp.bfloat16),
        scratch_shapes=[pltpu.VMEM((TQ_B, S), jnp.float32), pltpu.VMEM((TQ_B, S), jnp.float32),
                        pltpu.VMEM((TQ_B, LANES), jnp.float32)],
        compiler_params=_params("arbitrary", "arbitrary"),
        name="attn_b",
    )(q2, k2, v2, gate)


def _out_b_kernel(x_ref, mod_ref, yg_ref, w_ref, g_ref, b_ref, out_ref):
    D = x_ref.shape[-1]
    y = jnp.dot(yg_ref[...], w_ref[...], preferred_element_type=jnp.float32)
    z = DEEPNORM_ALPHA * x_ref[...] + mod_ref[:, 2 * D:] * y
    out_ref[...] = _layernorm(z, g_ref[...], b_ref[...])


def _out_b(x, mod, yg, w_out, ln_g, ln_b):
    B, S, D = x.shape
    tl = TL_WIDE
    row = lambda b, i: (b, i, 0)
    const2 = lambda b, i: (0, 0)
    return pl.pallas_call(
        _out_b_kernel,
        grid=(B, S // tl),
        in_specs=[pl.BlockSpec((None, tl, D), row),
                  pl.BlockSpec((None, 1, 3 * D), lambda b, i: (b, 0, 0)),
                  pl.BlockSpec((None, tl, WIDTH_B), row),
                  pl.BlockSpec(w_out.shape, const2),
                  pl.BlockSpec((1, D), const2), pl.BlockSpec((1, D), const2)],
        out_specs=pl.BlockSpec((None, tl, D), row),
        out_shape=jax.ShapeDtypeStruct((B, S, D), jnp.float32),
        compiler_params=_params("arbitrary", "arbitrary"),
        name="out_b",
    )(x, mod, yg, w_out, ln_g.reshape(1, D), ln_b.reshape(1, D))


def _weights_a(a_w_in):
    W = WIDTH_A
    gw = GROUP_WIDTH_A
    q, k, v, gate = (a_w_in[:, i * W:(i + 1) * W] for i in range(4))
    ws = []
    for g in range(len(DIL_GROUPS)):
        cs = slice(g * gw, (g + 1) * gw)
        parts = [q[:, cs] * (HEAD_DIM_A ** -0.5 * LOG2E), k[:, cs], v[:, cs]]
        if g == 0:
            parts.append(gate)
        ws.append(jnp.concatenate(parts, axis=1).astype(jnp.bfloat16))
    return ws


def _swap_halves(w):
    half = w.shape[-1] // 2
    return jnp.concatenate([w[..., half:], w[..., :half]], axis=-1)


def _pad_lanes(w, lo):
    n = w.shape[-1]
    return jnp.pad(w, [(0, 0)] * (w.ndim - 1) + [(lo, LANES - lo - n)])


def _weights_b(b_w_in, b_w_uq, b_w_ukv):
    c0 = Q_LORA_RANK
    c1 = c0 + KV_LORA_RANK
    c2 = c1 + QK_ROPE_DIM
    kr = b_w_in[:, c1:c2]
    w_in = jnp.concatenate([b_w_in[:, :c1], _pad_lanes(jnp.concatenate([kr, _swap_halves(kr)], axis=1), QK_NOPE_DIM),
                            b_w_in[:, c2:]], axis=1)
    uq = b_w_uq.reshape(Q_LORA_RANK, N_HEADS_B, QK_NOPE_DIM + QK_ROPE_DIM)
    wqa = jnp.concatenate([uq, _swap_halves(uq[..., QK_NOPE_DIM:])], axis=-1).reshape(Q_LORA_RANK, -1)
    ukv = b_w_ukv.reshape(KV_LORA_RANK, N_HEADS_B, QK_NOPE_DIM + V_HEAD_DIM)
    wk = _pad_lanes(ukv[..., :QK_NOPE_DIM], 0).reshape(KV_LORA_RANK, -1)
    vv = ukv[..., QK_NOPE_DIM:]
    odd = (jnp.arange(N_HEADS_B) % 2 == 1)[None, :, None]
    wv = jnp.where(odd, _pad_lanes(vv, LANES - V_HEAD_DIM), _pad_lanes(vv, 0)).reshape(KV_LORA_RANK, -1)
    bf = lambda w: w.astype(jnp.bfloat16)
    return bf(w_in), bf(wqa), bf(wk), bf(wv)


def _rope_tables(S):
    pos = jnp.arange(S, dtype=jnp.float32)
    inv_freq = ROPE_THETA ** (-jnp.arange(0, QK_ROPE_DIM, 2, dtype=jnp.float32) / QK_ROPE_DIM)
    ang = pos[:, None] * inv_freq[None, :]
    cos, sin = jnp.cos(ang), jnp.sin(ang)
    cosf = _pad_lanes(jnp.concatenate([cos, cos], axis=-1), QK_NOPE_DIM)
    sinf = _pad_lanes(jnp.concatenate([-sin, sin], axis=-1), QK_NOPE_DIM)
    nope = _pad_lanes(jnp.ones((S, QK_NOPE_DIM), jnp.float32), 0)
    scale = (QK_NOPE_DIM + QK_ROPE_DIM) ** -0.5 * LOG2E
    return (scale * (nope + cosf), scale * sinf, cosf, sinf)


def _v_ones_row():
    row = np.zeros((N_HEADS_B, LANES), np.float32)
    row[0::2, V_HEAD_DIM] = 1.0
    row[1::2, 0] = 1.0
    return jnp.asarray(row.reshape(1, -1))


def kernel(x, c, rel_bias, ada_w, ada_b, ln_g, ln_b, a_w_in, a_w_out,
           b_w_in, b_q_norm, b_w_uq, b_kv_norm, b_w_ukv, b_w_out):
    B, S, D = x.shape
    mods = _modulation(c, ada_w, ada_b)
    mod0 = mods[0].reshape(B, 1, 3 * D)
    mod1 = mods[1].reshape(B, 1, 3 * D)

    for window, dil in DIL_GROUPS:
        assert window // (2 * dil) == HALF_WINDOW
    *qkvs, gate_a = _proj_a(x, mod0, _weights_a(a_w_in[0]))
    os_, lses = [], []
    for g, ((_, dil), qkv) in enumerate(zip(DIL_GROUPS, qkvs)):
        hs = slice(g * HEADS_PER_GROUP_A, (g + 1) * HEADS_PER_GROUP_A)
        o, lse = _attn_a(qkv, _bias_variants(rel_bias[:, hs], dil))
        os_.append(o)
        lses.append(lse)

    w_in, wqa, wk, wv = _weights_b(b_w_in[0], b_w_uq[0], b_w_ukv[0])
    x1, q2, k2, v2, gate_b = _mid(x, mod0, os_, lses, gate_a, a_w_out[0].astype(jnp.bfloat16), ln_g[0], ln_b[0],
                                  mod1, w_in, b_q_norm[0], b_kv_norm[0], wqa, wk, wv, _rope_tables(S))
    yg = _attn_b(q2, k2, v2, gate_b)
    return _out_b(x1, mod1, yg, b_w_out[0].astype(jnp.bfloat16), ln_g[1], ln_b[1])
```

```python
import math

import numpy as np
import jax
import jax.numpy as jnp
from jax import lax
from jax.experimental import pallas as pl
from jax.experimental.pallas import tpu as pltpu

DEPTH = 2
HEAD_DIM_A = 64
DIL_GROUPS = ((128, 1), (512, 4), (2048, 16))
HEADS_PER_GROUP_A = 6
GROUP_WIDTH_A = HEADS_PER_GROUP_A * HEAD_DIM_A
WIDTH_A = GROUP_WIDTH_A * len(DIL_GROUPS)
N_BUCKETS = 32
T5_MAX_DISTANCE = 1024
N_HEADS_B = 16
QK_NOPE_DIM = 64
QK_ROPE_DIM = 32
V_HEAD_DIM = 64
Q_LORA_RANK = 256
KV_LORA_RANK = 128
WIDTH_B = N_HEADS_B * V_HEAD_DIM
ROPE_THETA = 10000.0
DEEPNORM_ALPHA = (2.0 * DEPTH) ** 0.25
LN_EPS = 1e-5
RMS_EPS = 1e-6
NEG_INF = -1e30
LOG2E = math.log2(math.e)
LN2 = math.log(2.0)

LANES = 128
VMEM_LIMIT_BYTES = 56 * 1024 * 1024

HALF_WINDOW = 64
TQ_A = 128
TK_A = TQ_A + 2 * HALF_WINDOW
GROUP_LANE_STRIDE = 8
MAX_TILES_PER_STEP_A = 16
TL_PROJ = 512
TL_WIDE = 1024
TQ_B = 512
TK_B = 512
HEADS_PER_STEP_B = 4

_NT = (((1,), (1,)), ((), ()))


def _silu(x):
    return x * (0.5 + 0.5 * jnp.tanh(0.5 * x))


def _resident(shape):
    return pl.BlockSpec(shape, lambda *_: (0,) * len(shape), pipeline_mode=pl.Buffered(1))


def _params(*sem):
    return pltpu.CompilerParams(dimension_semantics=sem, vmem_limit_bytes=VMEM_LIMIT_BYTES)


def _mod_kernel(c_ref, w_ref, b_ref, o_ref):
    sc = _silu(c_ref[...]).astype(jnp.bfloat16)
    o_ref[...] = jnp.dot(sc, w_ref[...].astype(jnp.bfloat16),
                         preferred_element_type=jnp.float32) + b_ref[...]


def _modulation(c, ada_w, ada_b):
    B, D = c.shape
    nj = 3
    return pl.pallas_call(
        _mod_kernel,
        grid=(DEPTH, nj),
        in_specs=[pl.BlockSpec((B, D), lambda i, j: (0, 0)),
                  pl.BlockSpec((None, D, D), lambda i, j: (i, 0, j)),
                  pl.BlockSpec((None, 1, D), lambda i, j: (i, 0, j))],
        out_specs=pl.BlockSpec((None, B, D), lambda i, j: (i, 0, j)),
        out_shape=jax.ShapeDtypeStruct((DEPTH, B, 3 * D), jnp.float32),
        compiler_params=_params("arbitrary", "arbitrary"),
        name="modulation",
    )(c, ada_w, ada_b.reshape(DEPTH, 1, 3 * D))


def _modulate(x, mod):
    D = x.shape[-1]
    return x * (1.0 + mod[:, D:2 * D]) + mod[:, :D]


def _layernorm(z, g, b):
    mu = jnp.mean(z, axis=-1, keepdims=True)
    zc = z - mu
    var = jnp.mean(zc * zc, axis=-1, keepdims=True)
    return zc * lax.rsqrt(var + LN_EPS) * g + b


def _proj_a_kernel(x_ref, mod_ref, w0_ref, w1_ref, w2_ref, qkv0_ref, qkv1_ref, qkv2_ref, gate_ref):
    u = _modulate(x_ref[...], mod_ref[...]).astype(jnp.bfloat16)
    n_qkv = qkv0_ref.shape[-1]
    z = jnp.dot(u, w0_ref[...], preferred_element_type=jnp.float32)
    qkv0_ref[0] = z[:, :n_qkv].astype(qkv0_ref.dtype)
    gate_ref[...] = _silu(z[:, n_qkv:]).astype(gate_ref.dtype)
    for w_ref, qkv_ref in ((w1_ref, qkv1_ref), (w2_ref, qkv2_ref)):
        dil, rows = qkv_ref.shape[:2]
        up = pltpu.einshape("(md)n->(dm)n", u, d=dil)
        z = jnp.dot(up, w_ref[...], preferred_element_type=jnp.float32)
        for r in range(dil):
            qkv_ref[r] = z[r * rows:(r + 1) * rows].astype(qkv_ref.dtype)


def _proj_a(x, mod, ws):
    B, S, D = x.shape
    tl = TL_WIDE
    n_qkv = 3 * GROUP_WIDTH_A
    dils = [dil for _, dil in DIL_GROUPS]
    assert dils[0] == 1
    row = lambda b, i: (b, i, 0)
    cls = lambda b, i: (b, 0, i, 0)
    const2 = lambda b, i: (0, 0)
    return pl.pallas_call(
        _proj_a_kernel,
        grid=(B, S // tl),
        in_specs=[pl.BlockSpec((None, tl, D), row),
                  pl.BlockSpec((None, 1, 3 * D), lambda b, i: (b, 0, 0)),
                  *[_resident(w.shape) for w in ws]],
        out_specs=[*[pl.BlockSpec((None, dil, tl // dil, n_qkv), cls) for dil in dils],
                   pl.BlockSpec((None, tl, WIDTH_A), row)],
        out_shape=[*[jax.ShapeDtypeStruct((B, dil, S // dil, n_qkv), jnp.bfloat16) for dil in dils],
                   jax.ShapeDtypeStruct((B, S, WIDTH_A), jnp.bfloat16)],
        compiler_params=_params("arbitrary", "arbitrary"),
        name="proj_a",
    )(x, mod, *ws)


def _attn_a_kernel(qkv_ref, bias_ref, o_ref, lse_ref, s_scr, p_scr):
    n_cls, L = (1,) + qkv_ref.shape[:1] if len(qkv_ref.shape) == 2 else qkv_ref.shape[:2]

    def of_class(ref, cls):
        return ref if len(ref.shape) == 2 else ref.at[cls]
    n_tiles = L // TQ_A
    gw = GROUP_WIDTH_A
    lane = lax.broadcasted_iota(jnp.int32, (TQ_A, LANES), 1)
    low_q = lane < HEAD_DIM_A
    low_k = lax.broadcasted_iota(jnp.int32, (TK_A, LANES), 1) < HEAD_DIM_A

    heads = range(HEADS_PER_GROUP_A)

    def pair_block(cls, rows, section, h):
        c0 = section * gw + (h // 2) * LANES
        return of_class(qkv_ref, cls)[rows, c0:c0 + LANES]

    def other_lanes(block, low_mask, h, fill):
        mine = low_mask if h % 2 == 0 else jnp.logical_not(low_mask)
        return jnp.where(mine, block, jnp.full_like(block, fill))

    tiles_per_step = s_scr.shape[0]

    def tile_group(tg, carry):
        geo = []
        for u in range(tiles_per_step):
            item = tg * tiles_per_step + u
            cls, t = (0, item) if n_cls == 1 else (item // n_tiles, item % n_tiles)
            j0 = pl.multiple_of(t * TQ_A, TQ_A)
            ws = pl.multiple_of(jnp.clip(j0 - HALF_WINDOW, 0, L - TK_A), HALF_WINDOW)
            variant = jnp.where(t == 0, 0, jnp.where(t == n_tiles - 1, 2, 1))
            geo.append((cls, pl.ds(j0, TQ_A), pl.ds(ws, TK_A), variant))
        for u, (cls, rq, rk, variant) in enumerate(geo):
            for h in heads:
                q = other_lanes(pair_block(cls, rq, 0, h), low_q, h, 0)
                s = lax.dot_general(q, pair_block(cls, rk, 1, h), _NT, preferred_element_type=jnp.float32)
                s_scr[u, h] = s + bias_ref[variant, h]
        m = [[jnp.max(s_scr[u, h], axis=-1, keepdims=True) for h in heads] for u in range(len(geo))]
        for u in range(len(geo)):
            for h in heads:
                p_scr[u, h] = jnp.exp2(s_scr[u, h] - m[u][h]).astype(jnp.bfloat16)
        for u, (cls, rq, rk, variant) in enumerate(geo):
            lse_tile = jnp.zeros((TQ_A, LANES), jnp.float32)
            for pair in range(HEADS_PER_GROUP_A // 2):
                he, ho = 2 * pair, 2 * pair + 1
                acc_e = jnp.dot(p_scr[u, he], other_lanes(pair_block(cls, rk, 2, he), low_k, he, 1),
                                preferred_element_type=jnp.float32)
                acc_o = jnp.dot(p_scr[u, ho], other_lanes(pair_block(cls, rk, 2, ho), low_k, ho, 1),
                                preferred_element_type=jnp.float32)
                numer = jnp.where(low_q, acc_e, acc_o)
                den_other = jnp.where(low_q, acc_o, acc_e)
                den = pltpu.roll(den_other, HEAD_DIM_A, 1)
                of_class(o_ref, cls)[rq, pair * LANES:(pair + 1) * LANES] = (numer / den).astype(o_ref.dtype)
                is_odd = lane == ho
                lse = (jnp.where(is_odd, m[u][ho], m[u][he])
                       + jnp.log2(jnp.where(is_odd, den_other, den))) * LN2
                lse_tile = jnp.where(jnp.logical_or(is_odd, lane == he), lse, lse_tile)
            of_class(lse_ref, cls)[rq, :] = lse_tile
        return carry

    lax.fori_loop(0, n_cls * n_tiles // tiles_per_step, tile_group, 0)


def _attn_a(qkv, biasm):
    B, dil, L, n_qkv = qkv.shape
    gw = GROUP_WIDTH_A
    n_tiles = L // TQ_A
    n_cls = min(dil, max(1, MAX_TILES_PER_STEP_A // n_tiles))
    tiles_per_step = math.gcd(MAX_TILES_PER_STEP_A, n_cls * n_tiles)
    blk_cls = None if n_cls == 1 else n_cls
    cls = lambda b, r: (b, r, 0, 0)
    return pl.pallas_call(
        _attn_a_kernel,
        grid=(B, dil // n_cls),
        in_specs=[pl.BlockSpec((None, blk_cls, L, n_qkv), cls),
                  pl.BlockSpec(biasm.shape, lambda b, r: (0, 0, 0, 0))],
        out_specs=[pl.BlockSpec((None, blk_cls, L, gw), cls),
                   pl.BlockSpec((None, blk_cls, L, LANES), cls)],
        out_shape=[jax.ShapeDtypeStruct((B, dil, L, gw), jnp.bfloat16),
                   jax.ShapeDtypeStruct((B, dil, L, LANES), jnp.float32)],
        scratch_shapes=[
            pltpu.VMEM((tiles_per_step, HEADS_PER_GROUP_A, TQ_A, TK_A), jnp.float32),
            pltpu.VMEM((tiles_per_step, HEADS_PER_GROUP_A, TQ_A, TK_A), jnp.bfloat16)],
        compiler_params=_params("arbitrary", "arbitrary"),
        name=f"attn_a_d{dil}",
    )(qkv, biasm)


def _t5_bucket_np(rel):
    half = N_BUCKETS // 2
    max_exact = half // 2
    base = np.where(rel > 0, half, 0)
    n = np.abs(rel)
    nf = np.maximum(n, 1).astype(np.float32)
    large = max_exact + (np.log(nf / np.float32(max_exact)) / np.float32(math.log(T5_MAX_DISTANCE / max_exact))
                         * np.float32(half - max_exact)).astype(np.int32)
    large = np.minimum(large, half - 1)
    return base + np.where(n < max_exact, n, large)


def _bias_variants(rel_bias_g, dil):
    H = rel_bias_g.shape[1]
    M = 2 * TK_A
    off = np.arange(M) - (TK_A - 1)
    idx = np.where(np.abs(off) <= HALF_WINDOW, _t5_bucket_np(off * dil), N_BUCKETS).astype(np.int32)
    table = jnp.concatenate([rel_bias_g.astype(jnp.float32) * LOG2E, jnp.full((1, H), NEG_INF, jnp.float32)], axis=0)
    w = table[jnp.asarray(idx)].T
    rot = jnp.tile(w, (1, TQ_A + 1))[:, :TQ_A * (M + 1)].reshape(H, TQ_A, M + 1)[:, ::-1, :]
    variants = []
    for shift in (0, HALF_WINDOW, 2 * HALF_WINDOW):
        j0 = TK_A - TQ_A - shift
        variants.append(rot[:, :, j0:j0 + TK_A])
    return jnp.stack(variants)


N_MERGE_A_REFS = 13


def _merge_a(x_ref, mod_ref, o0_ref, o1_ref, o2_ref, l0_ref, l1_ref, l2_ref, gate_ref,
             expand_ref, w_ref, g_ref, b_ref):
    tl, D = x_ref.shape
    stride = GROUP_LANE_STRIDE
    lane = lax.broadcasted_iota(jnp.int32, (tl, LANES), 1)

    def natural_order(o_ref, l_ref):
        if o_ref.shape[0] == 1:
            return o_ref[0].astype(jnp.float32), l_ref[0]
        return (pltpu.einshape("dmn->(md)n", o_ref[...].astype(jnp.float32)),
                pltpu.einshape("dmn->(md)n", l_ref[...]))

    o0, l0 = natural_order(o0_ref, l0_ref)
    o1, l1 = natural_order(o1_ref, l1_ref)
    o2, l2 = natural_order(o2_ref, l2_ref)
    mx = jnp.maximum(jnp.maximum(l0, l1), l2)
    e0, e1, e2 = jnp.exp(l0 - mx), jnp.exp(l1 - mx), jnp.exp(l2 - mx)
    inv = 1.0 / (e0 + e1 + e2)
    wt = jnp.where(lane < stride, e0 * inv,
                   jnp.where(lane < 2 * stride, pltpu.roll(e1 * inv, stride, 1), pltpu.roll(e2 * inv, 2 * stride, 1)))
    hi = wt.astype(jnp.bfloat16)
    lo = (wt - hi.astype(jnp.float32)).astype(jnp.bfloat16)
    wexp = (jnp.dot(hi, expand_ref[...], preferred_element_type=jnp.float32)
            + jnp.dot(lo, expand_ref[...], preferred_element_type=jnp.float32))
    o = jnp.concatenate([o0, o1, o2], axis=1)
    yg = (o * wexp * gate_ref[...].astype(jnp.float32)).astype(jnp.bfloat16)
    y = jnp.dot(yg, w_ref[...], preferred_element_type=jnp.float32)
    z = DEEPNORM_ALPHA * x_ref[...] + mod_ref[:, 2 * D:] * y
    return _layernorm(z, g_ref[...], b_ref[...])


def _rmsnorm(x, g):
    return x * lax.rsqrt(jnp.mean(x * x, axis=-1, keepdims=True) + RMS_EPS) * g


def _project_b(x, mod_ref, w_in_ref, qn_ref, kvn_ref, wqa_ref, wk_ref, wv_ref,
               vone_ref, cq_ref, sq_ref, ck_ref, sk_ref, q_out, k_out, v_out, gate_out):
    u = _modulate(x, mod_ref[...]).astype(jnp.bfloat16)
    z = jnp.dot(u, w_in_ref[...], preferred_element_type=jnp.float32)
    c0 = Q_LORA_RANK
    c1 = c0 + KV_LORA_RANK
    cq = z[:, :c0]
    ckv = z[:, c0:c1]
    kr = z[:, c1:c1 + LANES]
    gate_out[...] = z[:, c1 + LANES:].astype(gate_out.dtype)
    to_rope_lanes = LANES - QK_ROPE_DIM

    cqn = _rmsnorm(cq, qn_ref[...]).astype(jnp.bfloat16)
    qa = jnp.dot(cqn, wqa_ref[...], preferred_element_type=jnp.float32)
    ckvn = _rmsnorm(ckv, kvn_ref[...]).astype(jnp.bfloat16)
    kn = jnp.dot(ckvn, wk_ref[...], preferred_element_type=jnp.float32)
    v_out[...] = (jnp.dot(ckvn, wv_ref[...], preferred_element_type=jnp.float32)
                  + vone_ref[...]).astype(v_out.dtype)

    cos_q, sin_q = cq_ref[...], sq_ref[...]
    k_rope = kr * ck_ref[...] + pltpu.roll(kr, to_rope_lanes, 1) * sk_ref[...]
    for h in range(N_HEADS_B):
        blk = slice(h * LANES, (h + 1) * LANES)
        qh = qa[:, blk]
        q_out[:, blk] = (qh * cos_q + pltpu.roll(qh, to_rope_lanes, 1) * sin_q).astype(q_out.dtype)
        k_out[:, blk] = (kn[:, blk] + k_rope).astype(k_out.dtype)


def _mid_kernel(*refs):
    merge_refs, rest = refs[:N_MERGE_A_REFS], refs[N_MERGE_A_REFS:]
    x1_out, project_refs = rest[-5], rest[:-5] + rest[-4:]
    x1 = _merge_a(*merge_refs)
    x1_out[...] = x1
    _project_b(x1, *project_refs)


def _mid(x, mod0, os_, lses, gate_a, w_out_a, ln_g, ln_b,
         mod1, w_in, q_norm, kv_norm, wqa, wk, wv, tables):
    B, S, D = x.shape
    tl = TL_PROJ
    gw = GROUP_WIDTH_A
    HP = N_HEADS_B * LANES
    col = np.arange(WIDTH_A)
    src_lane = (col // gw) * GROUP_LANE_STRIDE + (col % gw) // HEAD_DIM_A
    expand = jnp.asarray((np.arange(LANES)[:, None] == src_lane[None, :]).astype(np.float32), dtype=jnp.bfloat16)
    row = lambda b, i: (b, i, 0)
    cls = lambda b, i: (b, 0, i, 0)
    const2 = lambda b, i: (0, 0)
    mod_spec = pl.BlockSpec((None, 1, 3 * D), lambda b, i: (b, 0, 0))
    tab = pl.BlockSpec((tl, LANES), lambda b, i: (i, 0))
    o_specs = [pl.BlockSpec((None, dil, tl // dil, gw), cls) for _, dil in DIL_GROUPS]
    l_specs = [pl.BlockSpec((None, dil, tl // dil, LANES), cls) for _, dil in DIL_GROUPS]
    merge_args = [x, mod0, *os_, *lses, gate_a, expand, w_out_a, ln_g.reshape(1, D), ln_b.reshape(1, D)]
    merge_specs = [pl.BlockSpec((None, tl, D), row), mod_spec, *o_specs, *l_specs,
                   pl.BlockSpec((None, tl, WIDTH_A), row),
                   _resident(expand.shape), _resident(w_out_a.shape),
                   pl.BlockSpec((1, D), const2), pl.BlockSpec((1, D), const2)]
    assert len(merge_args) == N_MERGE_A_REFS
    project_args = [mod1, w_in, q_norm.reshape(1, -1), kv_norm.reshape(1, -1), wqa, wk, wv, _v_ones_row(), *tables]
    project_specs = [mod_spec, _resident(w_in.shape),
                     pl.BlockSpec((1, Q_LORA_RANK), const2), pl.BlockSpec((1, KV_LORA_RANK), const2),
                     _resident(wqa.shape), _resident(wk.shape),
                     _resident(wv.shape), pl.BlockSpec((1, HP), const2), tab, tab, tab, tab]
    wide = jax.ShapeDtypeStruct((B, S, HP), jnp.bfloat16)
    return pl.pallas_call(
        _mid_kernel,
        grid=(B, S // tl),
        in_specs=merge_specs + project_specs,
        out_specs=[pl.BlockSpec((None, tl, D), row),
                   pl.BlockSpec((None, tl, HP), row), pl.BlockSpec((None, tl, HP), row),
                   pl.BlockSpec((None, tl, HP), row), pl.BlockSpec((None, tl, WIDTH_B), row)],
        out_shape=[jax.ShapeDtypeStruct((B, S, D), jnp.float32), wide, wide, wide,
                   jax.ShapeDtypeStruct((B, S, WIDTH_B), jnp.bfloat16)],
        compiler_params=_params("arbitrary", "arbitrary"),
        name="mid",
    )(*merge_args, *project_args)


def _attn_b_kernel(q_ref, k_ref, v_ref, gate_ref, o_ref, s_even, s_odd, mx_scr):
    S = k_ref.shape[0]
    tq = s_even.shape[0]
    n_tiles = S // tq
    n_chunks = S // TK_B
    n_heads = q_ref.shape[1] // LANES
    assert n_heads % 2 == 0
    s_bufs = (s_even, s_odd)

    def head_lanes(e):
        return slice(e * LANES, (e + 1) * LANES)

    def tile_rows(t):
        return pl.ds(t * tq, tq) if isinstance(t, int) else pl.ds(pl.multiple_of(t * tq, tq), tq)

    def scores(t, e, c, mx):
        ks = slice(c * TK_B, (c + 1) * TK_B)
        s = lax.dot_general(q_ref[tile_rows(t), head_lanes(e)], k_ref[ks, head_lanes(e)], _NT,
                            preferred_element_type=jnp.float32)
        s_bufs[e % 2][:, ks] = s
        for j in range(TK_B // LANES):
            mx = jnp.maximum(mx, s[:, j * LANES:(j + 1) * LANES])
        return mx

    def weighted_values(e, c, m, acc):
        ks = slice(c * TK_B, (c + 1) * TK_B)
        p = jnp.exp2(s_bufs[e % 2][:, ks] - m).astype(jnp.bfloat16)
        return acc + jnp.dot(p, v_ref[ks, head_lanes(e)], preferred_element_type=jnp.float32)

    neg = jnp.full((tq, LANES), -jnp.inf, jnp.float32)
    zero = jnp.zeros((tq, LANES), jnp.float32)
    lane = lax.broadcasted_iota(jnp.int32, (tq, LANES), 1)

    mx = neg
    for c in range(n_chunks):
        mx = scores(0, 0, c, mx)
    mx_scr[...] = mx

    def tile(t, carry):
        t_next = jnp.minimum(t + 1, n_tiles - 1)
        mx = mx_scr[...]
        acc = []
        for e in range(n_heads):
            m = jnp.max(mx, axis=-1, keepdims=True)
            mx, a = neg, zero
            for c in range(n_chunks):
                mx = scores(t, e + 1, c, mx) if e + 1 < n_heads else scores(t_next, 0, c, mx)
                a = weighted_values(e, c, m, a)
            acc.append(a)
        mx_scr[...] = mx
        for pair in range(n_heads // 2):
            acc_e, acc_o = acc[2 * pair], acc[2 * pair + 1]
            inv_e = 1.0 / acc_e[:, V_HEAD_DIM:V_HEAD_DIM + 1]
            inv_o = 1.0 / acc_o[:, 0:1]
            cols = slice(pair * LANES, (pair + 1) * LANES)
            o = jnp.where(lane < V_HEAD_DIM, acc_e * inv_e, acc_o * inv_o)
            gate = gate_ref[tile_rows(t), cols].astype(jnp.float32)
            o_ref[tile_rows(t), cols] = (o * _silu(gate)).astype(o_ref.dtype)
        return carry

    lax.fori_loop(0, n_tiles, tile, 0)


def _attn_b(q2, k2, v2, gate):
    B, S, HP = q2.shape
    hw = HEADS_PER_STEP_B * LANES
    ow = HEADS_PER_STEP_B * V_HEAD_DIM
    heads = lambda b, p: (b, 0, p)
    return pl.pallas_call(
        _attn_b_kernel,
        grid=(B, N_HEADS_B // HEADS_PER_STEP_B),
        in_specs=[pl.BlockSpec((None, S, hw), heads), pl.BlockSpec((None, S, hw), heads),
                  pl.BlockSpec((None, S, hw), heads), pl.BlockSpec((None, S, ow), heads)],
        out_specs=pl.BlockSpec((None, S, ow), heads),
        out_shape=jax.ShapeDtypeStruct((B, S, WIDTH_B), jnp.bfloat16),
        scratch_shapes=[pltpu.VMEM((TQ_B, S), jnp.float32), pltpu.VMEM((TQ_B, S), jnp.float32),
                        pltpu.VMEM((TQ_B, LANES), jnp.float32)],
        compiler_params=_params("arbitrary", "arbitrary"),
        name="attn_b",
    )(q2, k2, v2, gate)


def _out_b_kernel(x_ref, mod_ref, yg_ref, w_ref, g_ref, b_ref, out_ref):
    D = x_ref.shape[-1]
    y = jnp.dot(yg_ref[...], w_ref[...], preferred_element_type=jnp.float32)
    z = DEEPNORM_ALPHA * x_ref[...] + mod_ref[:, 2 * D:] * y
    out_ref[...] = _layernorm(z, g_ref[...], b_ref[...])


def _out_b(x, mod, yg, w_out, ln_g, ln_b):
    B, S, D = x.shape
    tl = TL_WIDE
    row = lambda b, i: (b, i, 0)
    const2 = lambda b, i: (0, 0)
    return pl.pallas_call(
        _out_b_kernel,
        grid=(B, S // tl),
        in_specs=[pl.BlockSpec((None, tl, D), row),
                  pl.BlockSpec((None, 1, 3 * D), lambda b, i: (b, 0, 0)),
                  pl.BlockSpec((None, tl, WIDTH_B), row),
                  _resident(w_out.shape),
                  pl.BlockSpec((1, D), const2), pl.BlockSpec((1, D), const2)],
        out_specs=pl.BlockSpec((None, tl, D), row),
        out_shape=jax.ShapeDtypeStruct((B, S, D), jnp.float32),
        compiler_params=_params("arbitrary", "arbitrary"),
        name="out_b",
    )(x, mod, yg, w_out, ln_g.reshape(1, D), ln_b.reshape(1, D))


def _weights_a(a_w_in):
    W = WIDTH_A
    gw = GROUP_WIDTH_A
    q, k, v, gate = (a_w_in[:, i * W:(i + 1) * W] for i in range(4))
    ws = []
    for g in range(len(DIL_GROUPS)):
        cs = slice(g * gw, (g + 1) * gw)
        parts = [q[:, cs] * (HEAD_DIM_A ** -0.5 * LOG2E), k[:, cs], v[:, cs]]
        if g == 0:
            parts.append(gate)
        ws.append(jnp.concatenate(parts, axis=1).astype(jnp.bfloat16))
    return ws


def _swap_halves(w):
    half = w.shape[-1] // 2
    return jnp.concatenate([w[..., half:], w[..., :half]], axis=-1)


def _pad_lanes(w, lo):
    n = w.shape[-1]
    return jnp.pad(w, [(0, 0)] * (w.ndim - 1) + [(lo, LANES - lo - n)])


def _weights_b(b_w_in, b_w_uq, b_w_ukv):
    c0 = Q_LORA_RANK
    c1 = c0 + KV_LORA_RANK
    c2 = c1 + QK_ROPE_DIM
    kr = b_w_in[:, c1:c2]
    w_in = jnp.concatenate([b_w_in[:, :c1], _pad_lanes(jnp.concatenate([kr, _swap_halves(kr)], axis=1), QK_NOPE_DIM),
                            b_w_in[:, c2:]], axis=1)
    uq = b_w_uq.reshape(Q_LORA_RANK, N_HEADS_B, QK_NOPE_DIM + QK_ROPE_DIM)
    wqa = jnp.concatenate([uq, _swap_halves(uq[..., QK_NOPE_DIM:])], axis=-1).reshape(Q_LORA_RANK, -1)
    ukv = b_w_ukv.reshape(KV_LORA_RANK, N_HEADS_B, QK_NOPE_DIM + V_HEAD_DIM)
    wk = _pad_lanes(ukv[..., :QK_NOPE_DIM], 0).reshape(KV_LORA_RANK, -1)
    vv = ukv[..., QK_NOPE_DIM:]
    odd = (jnp.arange(N_HEADS_B) % 2 == 1)[None, :, None]
    wv = jnp.where(odd, _pad_lanes(vv, LANES - V_HEAD_DIM), _pad_lanes(vv, 0)).reshape(KV_LORA_RANK, -1)
    bf = lambda w: w.astype(jnp.bfloat16)
    return bf(w_in), bf(wqa), bf(wk), bf(wv)


def _rope_tables(S):
    pos = jnp.arange(S, dtype=jnp.float32)
    inv_freq = ROPE_THETA ** (-jnp.arange(0, QK_ROPE_DIM, 2, dtype=jnp.float32) / QK_ROPE_DIM)
    ang = pos[:, None] * inv_freq[None, :]
    cos, sin = jnp.cos(ang), jnp.sin(ang)
    cosf = _pad_lanes(jnp.concatenate([cos, cos], axis=-1), QK_NOPE_DIM)
    sinf = _pad_lanes(jnp.concatenate([-sin, sin], axis=-1), QK_NOPE_DIM)
    nope = _pad_lanes(jnp.ones((S, QK_NOPE_DIM), jnp.float32), 0)
    scale = (QK_NOPE_DIM + QK_ROPE_DIM) ** -0.5 * LOG2E
    return (scale * (nope + cosf), scale * sinf, cosf, sinf)


def _v_ones_row():
    row = np.zeros((N_HEADS_B, LANES), np.float32)
    row[0::2, V_HEAD_DIM] = 1.0
    row[1::2, 0] = 1.0
    return jnp.asarray(row.reshape(1, -1))


def kernel(x, c, rel_bias, ada_w, ada_b, ln_g, ln_b, a_w_in, a_w_out,
           b_w_in, b_q_norm, b_w_uq, b_kv_norm, b_w_ukv, b_w_out):
    B, S, D = x.shape
    mods = _modulation(c, ada_w, ada_b)
    mod0 = mods[0].reshape(B, 1, 3 * D)
    mod1 = mods[1].reshape(B, 1, 3 * D)

    for window, dil in DIL_GROUPS:
        assert window // (2 * dil) == HALF_WINDOW
    *qkvs, gate_a = _proj_a(x, mod0, _weights_a(a_w_in[0]))
    os_, lses = [], []
    for g, ((_, dil), qkv) in enumerate(zip(DIL_GROUPS, qkvs)):
        hs = slice(g * HEADS_PER_GROUP_A, (g + 1) * HEADS_PER_GROUP_A)
        o, lse = _attn_a(qkv, _bias_variants(rel_bias[:, hs], dil))
        os_.append(o)
        lses.append(lse)

    w_in, wqa, wk, wv = _weights_b(b_w_in[0], b_w_uq[0], b_w_ukv[0])
    x1, q2, k2, v2, gate_b = _mid(x, mod0, os_, lses, gate_a, a_w_out[0].astype(jnp.bfloat16), ln_g[0], ln_b[0],
                                  mod1, w_in, b_q_norm[0], b_kv_norm[0], wqa, wk, wv, _rope_tables(S))
    yg = _attn_b(q2, k2, v2, gate_b)
    return _out_b(x1, mod1, yg, b_w_out[0].astype(jnp.bfloat16), ln_g[1], ln_b[1])
```

```python
import math

import numpy as np
import jax
import jax.numpy as jnp
from jax import lax
from jax.experimental import pallas as pl
from jax.experimental.pallas import tpu as pltpu

DEPTH = 2
HEAD_DIM_A = 64
DIL_GROUPS = ((128, 1), (512, 4), (2048, 16))
HEADS_PER_GROUP_A = 6
GROUP_WIDTH_A = HEADS_PER_GROUP_A * HEAD_DIM_A
WIDTH_A = GROUP_WIDTH_A * len(DIL_GROUPS)
N_BUCKETS = 32
T5_MAX_DISTANCE = 1024
N_HEADS_B = 16
QK_NOPE_DIM = 64
QK_ROPE_DIM = 32
V_HEAD_DIM = 64
Q_LORA_RANK = 256
KV_LORA_RANK = 128
WIDTH_B = N_HEADS_B * V_HEAD_DIM
ROPE_THETA = 10000.0
DEEPNORM_ALPHA = (2.0 * DEPTH) ** 0.25
LN_EPS = 1e-5
RMS_EPS = 1e-6
NEG_INF = -1e30
LOG2E = math.log2(math.e)
LN2 = math.log(2.0)

LANES = 128
VMEM_LIMIT_BYTES = 56 * 1024 * 1024

HALF_WINDOW = 64
TQ_A = 128
TK_A = TQ_A + 2 * HALF_WINDOW
GROUP_LANE_STRIDE = 8
MAX_TILES_PER_STEP_A = 16
TL_PROJ = 512
TL_WIDE = 1024
TQ_B = 512
TK_B = 512
HEADS_PER_STEP_B = 4

_NT = (((1,), (1,)), ((), ()))


def _silu(x):
    return x * (0.5 + 0.5 * jnp.tanh(0.5 * x))


def _params(*sem):
    return pltpu.CompilerParams(dimension_semantics=sem, vmem_limit_bytes=VMEM_LIMIT_BYTES)


def _mod_kernel(c_ref, w_ref, b_ref, o_ref):
    sc = _silu(c_ref[...]).astype(jnp.bfloat16)
    o_ref[...] = jnp.dot(sc, w_ref[...].astype(jnp.bfloat16),
                         preferred_element_type=jnp.float32) + b_ref[...]


def _modulation(c, ada_w, ada_b):
    B, D = c.shape
    nj = 3
    return pl.pallas_call(
        _mod_kernel,
        grid=(DEPTH, nj),
        in_specs=[pl.BlockSpec((B, D), lambda i, j: (0, 0)),
                  pl.BlockSpec((None, D, D), lambda i, j: (i, 0, j)),
                  pl.BlockSpec((None, 1, D), lambda i, j: (i, 0, j))],
        out_specs=pl.BlockSpec((None, B, D), lambda i, j: (i, 0, j)),
        out_shape=jax.ShapeDtypeStruct((DEPTH, B, 3 * D), jnp.float32),
        compiler_params=_params("arbitrary", "arbitrary"),
        name="modulation",
    )(c, ada_w, ada_b.reshape(DEPTH, 1, 3 * D))


def _modulate(x, mod):
    D = x.shape[-1]
    return x * (1.0 + mod[:, D:2 * D]) + mod[:, :D]


def _layernorm(z, g, b):
    mu = jnp.mean(z, axis=-1, keepdims=True)
    zc = z - mu
    var = jnp.mean(zc * zc, axis=-1, keepdims=True)
    return zc * lax.rsqrt(var + LN_EPS) * g + b


def _proj_a_kernel(x_ref, mod_ref, w0_ref, w1_ref, w2_ref, qkv0_ref, qkv1_ref, qkv2_ref, gate_ref):
    u = _modulate(x_ref[...], mod_ref[...]).astype(jnp.bfloat16)
    n_qkv = qkv0_ref.shape[-1]
    z = jnp.dot(u, w0_ref[...], preferred_element_type=jnp.float32)
    qkv0_ref[0] = z[:, :n_qkv].astype(qkv0_ref.dtype)
    gate_ref[...] = _silu(z[:, n_qkv:]).astype(gate_ref.dtype)
    for w_ref, qkv_ref in ((w1_ref, qkv1_ref), (w2_ref, qkv2_ref)):
        dil, rows = qkv_ref.shape[:2]
        up = pltpu.einshape("(md)n->(dm)n", u, d=dil)
        z = jnp.dot(up, w_ref[...], preferred_element_type=jnp.float32)
        for r in range(dil):
            qkv_ref[r] = z[r * rows:(r + 1) * rows].astype(qkv_ref.dtype)


def _proj_a(x, mod, ws):
    B, S, D = x.shape
    tl = TL_WIDE
    n_qkv = 3 * GROUP_WIDTH_A
    dils = [dil for _, dil in DIL_GROUPS]
    assert dils[0] == 1
    row = lambda b, i: (b, i, 0)
    cls = lambda b, i: (b, 0, i, 0)
    const2 = lambda b, i: (0, 0)
    return pl.pallas_call(
        _proj_a_kernel,
        grid=(B, S // tl),
        in_specs=[pl.BlockSpec((None, tl, D), row),
                  pl.BlockSpec((None, 1, 3 * D), lambda b, i: (b, 0, 0)),
                  *[pl.BlockSpec(w.shape, const2) for w in ws]],
        out_specs=[*[pl.BlockSpec((None, dil, tl // dil, n_qkv), cls) for dil in dils],
                   pl.BlockSpec((None, tl, WIDTH_A), row)],
        out_shape=[*[jax.ShapeDtypeStruct((B, dil, S // dil, n_qkv), jnp.bfloat16) for dil in dils],
                   jax.ShapeDtypeStruct((B, S, WIDTH_A), jnp.bfloat16)],
        compiler_params=pltpu.CompilerParams(
            dimension_semantics=("arbitrary", "arbitrary"), vmem_limit_bytes=VMEM_LIMIT_BYTES,
            allow_input_fusion=[False, False, True, True, True]),
        name="proj_a",
    )(x, mod, *ws)


def _attn_a_kernel(qkv_ref, bias_ref, o_ref, lse_ref, s_scr, p_scr):
    n_cls, L = (1,) + qkv_ref.shape[:1] if len(qkv_ref.shape) == 2 else qkv_ref.shape[:2]

    def of_class(ref, cls):
        return ref if len(ref.shape) == 2 else ref.at[cls]
    n_tiles = L // TQ_A
    gw = GROUP_WIDTH_A
    lane = lax.broadcasted_iota(jnp.int32, (TQ_A, LANES), 1)
    low_q = lane < HEAD_DIM_A
    low_k = lax.broadcasted_iota(jnp.int32, (TK_A, LANES), 1) < HEAD_DIM_A

    heads = range(HEADS_PER_GROUP_A)

    def pair_block(cls, rows, section, h):
        c0 = section * gw + (h // 2) * LANES
        return of_class(qkv_ref, cls)[rows, c0:c0 + LANES]

    def other_lanes(block, low_mask, h, fill):
        mine = low_mask if h % 2 == 0 else jnp.logical_not(low_mask)
        return jnp.where(mine, block, jnp.full_like(block, fill))

    tiles_per_step = s_scr.shape[0]

    def tile_group(tg, carry):
        geo = []
        for u in range(tiles_per_step):
            item = tg * tiles_per_step + u
            cls, t = (0, item) if n_cls == 1 else (item // n_tiles, item % n_tiles)
            j0 = pl.multiple_of(t * TQ_A, TQ_A)
            ws = pl.multiple_of(jnp.clip(j0 - HALF_WINDOW, 0, L - TK_A), HALF_WINDOW)
            variant = jnp.where(t == 0, 0, jnp.where(t == n_tiles - 1, 2, 1))
            geo.append((cls, pl.ds(j0, TQ_A), pl.ds(ws, TK_A), variant))
        for u, (cls, rq, rk, variant) in enumerate(geo):
            for h in heads:
                q = other_lanes(pair_block(cls, rq, 0, h), low_q, h, 0)
                s = lax.dot_general(q, pair_block(cls, rk, 1, h), _NT, preferred_element_type=jnp.float32)
                s_scr[u, h] = s + bias_ref[variant, h]
        m = [[jnp.max(s_scr[u, h], axis=-1, keepdims=True) for h in heads] for u in range(len(geo))]
        for u in range(len(geo)):
            for h in heads:
                p_scr[u, h] = jnp.exp2(s_scr[u, h] - m[u][h]).astype(jnp.bfloat16)
        for u, (cls, rq, rk, variant) in enumerate(geo):
            lse_tile = jnp.zeros((TQ_A, LANES), jnp.float32)
            for pair in range(HEADS_PER_GROUP_A // 2):
                he, ho = 2 * pair, 2 * pair + 1
                acc_e = jnp.dot(p_scr[u, he], other_lanes(pair_block(cls, rk, 2, he), low_k, he, 1),
                                preferred_element_type=jnp.float32)
                acc_o = jnp.dot(p_scr[u, ho], other_lanes(pair_block(cls, rk, 2, ho), low_k, ho, 1),
                                preferred_element_type=jnp.float32)
                numer = jnp.where(low_q, acc_e, acc_o)
                den_other = jnp.where(low_q, acc_o, acc_e)
                den = pltpu.roll(den_other, HEAD_DIM_A, 1)
                of_class(o_ref, cls)[rq, pair * LANES:(pair + 1) * LANES] = (numer / den).astype(o_ref.dtype)
                is_odd = lane == ho
                lse = (jnp.where(is_odd, m[u][ho], m[u][he])
                       + jnp.log2(jnp.where(is_odd, den_other, den))) * LN2
                lse_tile = jnp.where(jnp.logical_or(is_odd, lane == he), lse, lse_tile)
            of_class(lse_ref, cls)[rq, :] = lse_tile
        return carry

    lax.fori_loop(0, n_cls * n_tiles // tiles_per_step, tile_group, 0)


def _attn_a(qkv, biasm):
    B, dil, L, n_qkv = qkv.shape
    gw = GROUP_WIDTH_A
    n_tiles = L // TQ_A
    n_cls = min(dil, max(1, MAX_TILES_PER_STEP_A // n_tiles))
    tiles_per_step = math.gcd(MAX_TILES_PER_STEP_A, n_cls * n_tiles)
    blk_cls = None if n_cls == 1 else n_cls
    cls = lambda b, r: (b, r, 0, 0)
    return pl.pallas_call(
        _attn_a_kernel,
        grid=(B, dil // n_cls),
        in_specs=[pl.BlockSpec((None, blk_cls, L, n_qkv), cls),
                  pl.BlockSpec(biasm.shape, lambda b, r: (0, 0, 0, 0))],
        out_specs=[pl.BlockSpec((None, blk_cls, L, gw), cls),
                   pl.BlockSpec((None, blk_cls, L, LANES), cls)],
        out_shape=[jax.ShapeDtypeStruct((B, dil, L, gw), jnp.bfloat16),
                   jax.ShapeDtypeStruct((B, dil, L, LANES), jnp.float32)],
        scratch_shapes=[
            pltpu.VMEM((tiles_per_step, HEADS_PER_GROUP_A, TQ_A, TK_A), jnp.float32),
            pltpu.VMEM((tiles_per_step, HEADS_PER_GROUP_A, TQ_A, TK_A), jnp.bfloat16)],
        compiler_params=_params("arbitrary", "arbitrary"),
        name=f"attn_a_d{dil}",
    )(qkv, biasm)


def _t5_bucket_np(rel):
    half = N_BUCKETS // 2
    max_exact = half // 2
    base = np.where(rel > 0, half, 0)
    n = np.abs(rel)
    nf = np.maximum(n, 1).astype(np.float32)
    large = max_exact + (np.log(nf / np.float32(max_exact)) / np.float32(math.log(T5_MAX_DISTANCE / max_exact))
                         * np.float32(half - max_exact)).astype(np.int32)
    large = np.minimum(large, half - 1)
    return base + np.where(n < max_exact, n, large)


def _bias_variants(rel_bias_g, dil):
    H = rel_bias_g.shape[1]
    M = 2 * TK_A
    off = np.arange(M) - (TK_A - 1)
    idx = np.where(np.abs(off) <= HALF_WINDOW, _t5_bucket_np(off * dil), N_BUCKETS).astype(np.int32)
    table = jnp.concatenate([rel_bias_g.astype(jnp.float32) * LOG2E, jnp.full((1, H), NEG_INF, jnp.float32)], axis=0)
    w = table[jnp.asarray(idx)].T
    rot = jnp.tile(w, (1, TQ_A + 1))[:, :TQ_A * (M + 1)].reshape(H, TQ_A, M + 1)[:, ::-1, :]
    variants = []
    for shift in (0, HALF_WINDOW, 2 * HALF_WINDOW):
        j0 = TK_A - TQ_A - shift
        variants.append(rot[:, :, j0:j0 + TK_A])
    return jnp.stack(variants)


N_MERGE_A_REFS = 13


def _merge_a(x_ref, mod_ref, o0_ref, o1_ref, o2_ref, l0_ref, l1_ref, l2_ref, gate_ref,
             expand_ref, w_ref, g_ref, b_ref):
    tl, D = x_ref.shape
    stride = GROUP_LANE_STRIDE
    lane = lax.broadcasted_iota(jnp.int32, (tl, LANES), 1)

    def natural_order(o_ref, l_ref):
        if o_ref.shape[0] == 1:
            return o_ref[0].astype(jnp.float32), l_ref[0]
        return (pltpu.einshape("dmn->(md)n", o_ref[...].astype(jnp.float32)),
                pltpu.einshape("dmn->(md)n", l_ref[...]))

    o0, l0 = natural_order(o0_ref, l0_ref)
    o1, l1 = natural_order(o1_ref, l1_ref)
    o2, l2 = natural_order(o2_ref, l2_ref)
    mx = jnp.maximum(jnp.maximum(l0, l1), l2)
    e0, e1, e2 = jnp.exp(l0 - mx), jnp.exp(l1 - mx), jnp.exp(l2 - mx)
    inv = 1.0 / (e0 + e1 + e2)
    wt = jnp.where(lane < stride, e0 * inv,
                   jnp.where(lane < 2 * stride, pltpu.roll(e1 * inv, stride, 1), pltpu.roll(e2 * inv, 2 * stride, 1)))
    hi = wt.astype(jnp.bfloat16)
    lo = (wt - hi.astype(jnp.float32)).astype(jnp.bfloat16)
    wexp = (jnp.dot(hi, expand_ref[...], preferred_element_type=jnp.float32)
            + jnp.dot(lo, expand_ref[...], preferred_element_type=jnp.float32))
    o = jnp.concatenate([o0, o1, o2], axis=1)
    yg = (o * wexp * gate_ref[...].astype(jnp.float32)).astype(jnp.bfloat16)
    y = jnp.dot(yg, w_ref[...], preferred_element_type=jnp.float32)
    z = DEEPNORM_ALPHA * x_ref[...] + mod_ref[:, 2 * D:] * y
    return _layernorm(z, g_ref[...], b_ref[...])


def _rmsnorm(x, g):
    return x * lax.rsqrt(jnp.mean(x * x, axis=-1, keepdims=True) + RMS_EPS) * g


def _project_b(x, mod_ref, w_in_ref, qn_ref, kvn_ref, wqa_ref, wk_ref, wv_ref,
               vone_ref, cq_ref, sq_ref, ck_ref, sk_ref, q_out, k_out, v_out, gate_out):
    u = _modulate(x, mod_ref[...]).astype(jnp.bfloat16)
    z = jnp.dot(u, w_in_ref[...], preferred_element_type=jnp.float32)
    c0 = Q_LORA_RANK
    c1 = c0 + KV_LORA_RANK
    cq = z[:, :c0]
    ckv = z[:, c0:c1]
    kr = z[:, c1:c1 + LANES]
    gate_out[...] = z[:, c1 + LANES:].astype(gate_out.dtype)
    to_rope_lanes = LANES - QK_ROPE_DIM

    cqn = _rmsnorm(cq, qn_ref[...]).astype(jnp.bfloat16)
    qa = jnp.dot(cqn, wqa_ref[...], preferred_element_type=jnp.float32)
    ckvn = _rmsnorm(ckv, kvn_ref[...]).astype(jnp.bfloat16)
    kn = jnp.dot(ckvn, wk_ref[...], preferred_element_type=jnp.float32)
    v_out[...] = (jnp.dot(ckvn, wv_ref[...], preferred_element_type=jnp.float32)
                  + vone_ref[...]).astype(v_out.dtype)

    cos_q, sin_q = cq_ref[...], sq_ref[...]
    k_rope = kr * ck_ref[...] + pltpu.roll(kr, to_rope_lanes, 1) * sk_ref[...]
    for h in range(N_HEADS_B):
        blk = slice(h * LANES, (h + 1) * LANES)
        qh = qa[:, blk]
        q_out[:, blk] = (qh * cos_q + pltpu.roll(qh, to_rope_lanes, 1) * sin_q).astype(q_out.dtype)
        k_out[:, blk] = (kn[:, blk] + k_rope).astype(k_out.dtype)


def _mid_kernel(*refs):
    merge_refs, rest = refs[:N_MERGE_A_REFS], refs[N_MERGE_A_REFS:]
    x1_out, project_refs = rest[-5], rest[:-5] + rest[-4:]
    x1 = _merge_a(*merge_refs)
    x1_out[...] = x1
    _project_b(x1, *project_refs)


def _mid(x, mod0, os_, lses, gate_a, w_out_a, ln_g, ln_b,
         mod1, w_in, q_norm, kv_norm, wqa, wk, wv, tables):
    B, S, D = x.shape
    tl = TL_PROJ
    gw = GROUP_WIDTH_A
    HP = N_HEADS_B * LANES
    col = np.arange(WIDTH_A)
    src_lane = (col // gw) * GROUP_LANE_STRIDE + (col % gw) // HEAD_DIM_A
    expand = jnp.asarray((np.arange(LANES)[:, None] == src_lane[None, :]).astype(np.float32), dtype=jnp.bfloat16)
    row = lambda b, i: (b, i, 0)
    cls = lambda b, i: (b, 0, i, 0)
    const2 = lambda b, i: (0, 0)
    mod_spec = pl.BlockSpec((None, 1, 3 * D), lambda b, i: (b, 0, 0))
    tab = pl.BlockSpec((tl, LANES), lambda b, i: (i, 0))
    o_specs = [pl.BlockSpec((None, dil, tl // dil, gw), cls) for _, dil in DIL_GROUPS]
    l_specs = [pl.BlockSpec((None, dil, tl // dil, LANES), cls) for _, dil in DIL_GROUPS]
    merge_args = [x, mod0, *os_, *lses, gate_a, expand, w_out_a, ln_g.reshape(1, D), ln_b.reshape(1, D)]
    merge_specs = [pl.BlockSpec((None, tl, D), row), mod_spec, *o_specs, *l_specs,
                   pl.BlockSpec((None, tl, WIDTH_A), row),
                   pl.BlockSpec(expand.shape, const2), pl.BlockSpec(w_out_a.shape, const2),
                   pl.BlockSpec((1, D), const2), pl.BlockSpec((1, D), const2)]
    assert len(merge_args) == N_MERGE_A_REFS
    project_args = [mod1, w_in, q_norm.reshape(1, -1), kv_norm.reshape(1, -1), wqa, wk, wv, _v_ones_row(), *tables]
    project_specs = [mod_spec, pl.BlockSpec(w_in.shape, const2),
                     pl.BlockSpec((1, Q_LORA_RANK), const2), pl.BlockSpec((1, KV_LORA_RANK), const2),
                     pl.BlockSpec(wqa.shape, const2), pl.BlockSpec(wk.shape, const2),
                     pl.BlockSpec(wv.shape, const2), pl.BlockSpec((1, HP), const2), tab, tab, tab, tab]
    wide = jax.ShapeDtypeStruct((B, S, HP), jnp.bfloat16)
    return pl.pallas_call(
        _mid_kernel,
        grid=(B, S // tl),
        in_specs=merge_specs + project_specs,
        out_specs=[pl.BlockSpec((None, tl, D), row),
                   pl.BlockSpec((None, tl, HP), row), pl.BlockSpec((None, tl, HP), row),
                   pl.BlockSpec((None, tl, HP), row), pl.BlockSpec((None, tl, WIDTH_B), row)],
        out_shape=[jax.ShapeDtypeStruct((B, S, D), jnp.float32), wide, wide, wide,
                   jax.ShapeDtypeStruct((B, S, WIDTH_B), jnp.bfloat16)],
        compiler_params=_params("arbitrary", "arbitrary"),
        name="mid",
    )(*merge_args, *project_args)


def _attn_b_kernel(q_ref, k_ref, v_ref, gate_ref, o_ref, s_even, s_odd, mx_scr):
    S = k_ref.shape[0]
    tq = s_even.shape[0]
    n_tiles = S // tq
    n_chunks = S // TK_B
    n_heads = q_ref.shape[1] // LANES
    assert n_heads % 2 == 0
    s_bufs = (s_even, s_odd)

    def head_lanes(e):
        return slice(e * LANES, (e + 1) * LANES)

    def tile_rows(t):
        return pl.ds(t * tq, tq) if isinstance(t, int) else pl.ds(pl.multiple_of(t * tq, tq), tq)

    def scores(t, e, c, mx):
        ks = slice(c * TK_B, (c + 1) * TK_B)
        s = lax.dot_general(q_ref[tile_rows(t), head_lanes(e)], k_ref[ks, head_lanes(e)], _NT,
                            preferred_element_type=jnp.float32)
        s_bufs[e % 2][:, ks] = s
        for j in range(TK_B // LANES):
            mx = jnp.maximum(mx, s[:, j * LANES:(j + 1) * LANES])
        return mx

    def weighted_values(e, c, m, acc):
        ks = slice(c * TK_B, (c + 1) * TK_B)
        p = jnp.exp2(s_bufs[e % 2][:, ks] - m).astype(jnp.bfloat16)
        return acc + jnp.dot(p, v_ref[ks, head_lanes(e)], preferred_element_type=jnp.float32)

    neg = jnp.full((tq, LANES), -jnp.inf, jnp.float32)
    zero = jnp.zeros((tq, LANES), jnp.float32)
    lane = lax.broadcasted_iota(jnp.int32, (tq, LANES), 1)

    mx = neg
    for c in range(n_chunks):
        mx = scores(0, 0, c, mx)
    mx_scr[...] = mx

    def tile(t, carry):
        t_next = jnp.minimum(t + 1, n_tiles - 1)
        mx = mx_scr[...]
        acc = []
        for e in range(n_heads):
            m = jnp.max(mx, axis=-1, keepdims=True)
            mx, a = neg, zero
            for c in range(n_chunks):
                mx = scores(t, e + 1, c, mx) if e + 1 < n_heads else scores(t_next, 0, c, mx)
                a = weighted_values(e, c, m, a)
            acc.append(a)
        mx_scr[...] = mx
        for pair in range(n_heads // 2):
            acc_e, acc_o = acc[2 * pair], acc[2 * pair + 1]
            inv_e = 1.0 / acc_e[:, V_HEAD_DIM:V_HEAD_DIM + 1]
            inv_o = 1.0 / acc_o[:, 0:1]
            cols = slice(pair * LANES, (pair + 1) * LANES)
            o = jnp.where(lane < V_HEAD_DIM, acc_e * inv_e, acc_o * inv_o)
            gate = gate_ref[tile_rows(t), cols].astype(jnp.float32)
            o_ref[tile_rows(t), cols] = (o * _silu(gate)).astype(o_ref.dtype)
        return carry

    lax.fori_loop(0, n_tiles, tile, 0)


def _attn_b(q2, k2, v2, gate):
    B, S, HP = q2.shape
    hw = HEADS_PER_STEP_B * LANES
    ow = HEADS_PER_STEP_B * V_HEAD_DIM
    heads = lambda b, p: (b, 0, p)
    return pl.pallas_call(
        _attn_b_kernel,
        grid=(B, N_HEADS_B // HEADS_PER_STEP_B),
        in_specs=[pl.BlockSpec((None, S, hw), heads), pl.BlockSpec((None, S, hw), heads),
                  pl.BlockSpec((None, S, hw), heads), pl.BlockSpec((None, S, ow), heads)],
        out_specs=pl.BlockSpec((None, S, ow), heads),
        out_shape=jax.ShapeDtypeStruct((B, S, WIDTH_B), jnp.bfloat16),
        scratch_shapes=[pltpu.VMEM((TQ_B, S), jnp.float32), pltpu.VMEM((TQ_B, S), jnp.float32),
                        pltpu.VMEM((TQ_B, LANES), jnp.float32)],
        compiler_params=_params("arbitrary", "arbitrary"),
        name="attn_b",
    )(q2, k2, v2, gate)


def _out_b_kernel(x_ref, mod_ref, yg_ref, w_ref, g_ref, b_ref, out_ref):
    D = x_ref.shape[-1]
    y = jnp.dot(yg_ref[...], w_ref[...], preferred_element_type=jnp.float32)
    z = DEEPNORM_ALPHA * x_ref[...] + mod_ref[:, 2 * D:] * y
    out_ref[...] = _layernorm(z, g_ref[...], b_ref[...])


def _out_b(x, mod, yg, w_out, ln_g, ln_b):
    B, S, D = x.shape
    tl = TL_WIDE
    row = lambda b, i: (b, i, 0)
    const2 = lambda b, i: (0, 0)
    return pl.pallas_call(
        _out_b_kernel,
        grid=(B, S // tl),
        in_specs=[pl.BlockSpec((None, tl, D), row),
                  pl.BlockSpec((None, 1, 3 * D), lambda b, i: (b, 0, 0)),
                  pl.BlockSpec((None, tl, WIDTH_B), row),
                  pl.BlockSpec(w_out.shape, const2),
                  pl.BlockSpec((1, D), const2), pl.BlockSpec((1, D), const2)],
        out_specs=pl.BlockSpec((None, tl, D), row),
        out_shape=jax.ShapeDtypeStruct((B, S, D), jnp.float32),
        compiler_params=_params("arbitrary", "arbitrary"),
        name="out_b",
    )(x, mod, yg, w_out, ln_g.reshape(1, D), ln_b.reshape(1, D))


def _weights_a(a_w_in):
    W = WIDTH_A
    gw = GROUP_WIDTH_A
    q, k, v, gate = (a_w_in[:, i * W:(i + 1) * W] for i in range(4))
    ws = []
    for g in range(len(DIL_GROUPS)):
        cs = slice(g * gw, (g + 1) * gw)
        parts = [q[:, cs] * (HEAD_DIM_A ** -0.5 * LOG2E), k[:, cs], v[:, cs]]
        if g == 0:
            parts.append(gate)
        ws.append(jnp.concatenate(parts, axis=1).astype(jnp.bfloat16))
    return ws


def _swap_halves(w):
    half = w.shape[-1] // 2
    return jnp.concatenate([w[..., half:], w[..., :half]], axis=-1)


def _pad_lanes(w, lo):
    n = w.shape[-1]
    return jnp.pad(w, [(0, 0)] * (w.ndim - 1) + [(lo, LANES - lo - n)])


def _weights_b(b_w_in, b_w_uq, b_w_ukv):
    c0 = Q_LORA_RANK
    c1 = c0 + KV_LORA_RANK
    c2 = c1 + QK_ROPE_DIM
    kr = b_w_in[:, c1:c2]
    w_in = jnp.concatenate([b_w_in[:, :c1], _pad_lanes(jnp.concatenate([kr, _swap_halves(kr)], axis=1), QK_NOPE_DIM),
                            b_w_in[:, c2:]], axis=1)
    uq = b_w_uq.reshape(Q_LORA_RANK, N_HEADS_B, QK_NOPE_DIM + QK_ROPE_DIM)
    wqa = jnp.concatenate([uq, _swap_halves(uq[..., QK_NOPE_DIM:])], axis=-1).reshape(Q_LORA_RANK, -1)
    ukv = b_w_ukv.reshape(KV_LORA_RANK, N_HEADS_B, QK_NOPE_DIM + V_HEAD_DIM)
    wk = _pad_lanes(ukv[..., :QK_NOPE_DIM], 0).reshape(KV_LORA_RANK, -1)
    vv = ukv[..., QK_NOPE_DIM:]
    odd = (jnp.arange(N_HEADS_B) % 2 == 1)[None, :, None]
    wv = jnp.where(odd, _pad_lanes(vv, LANES - V_HEAD_DIM), _pad_lanes(vv, 0)).reshape(KV_LORA_RANK, -1)
    bf = lambda w: w.astype(jnp.bfloat16)
    return bf(w_in), bf(wqa), bf(wk), bf(wv)


def _rope_tables(S):
    pos = jnp.arange(S, dtype=jnp.float32)
    inv_freq = ROPE_THETA ** (-jnp.arange(0, QK_ROPE_DIM, 2, dtype=jnp.float32) / QK_ROPE_DIM)
    ang = pos[:, None] * inv_freq[None, :]
    cos, sin = jnp.cos(ang), jnp.sin(ang)
    cosf = _pad_lanes(jnp.concatenate([cos, cos], axis=-1), QK_NOPE_DIM)
    sinf = _pad_lanes(jnp.concatenate([-sin, sin], axis=-1), QK_NOPE_DIM)
    nope = _pad_lanes(jnp.ones((S, QK_NOPE_DIM), jnp.float32), 0)
    scale = (QK_NOPE_DIM + QK_ROPE_DIM) ** -0.5 * LOG2E
    return (scale * (nope + cosf), scale * sinf, cosf, sinf)


def _v_ones_row():
    row = np.zeros((N_HEADS_B, LANES), np.float32)
    row[0::2, V_HEAD_DIM] = 1.0
    row[1::2, 0] = 1.0
    return jnp.asarray(row.reshape(1, -1))


def kernel(x, c, rel_bias, ada_w, ada_b, ln_g, ln_b, a_w_in, a_w_out,
           b_w_in, b_q_norm, b_w_uq, b_kv_norm, b_w_ukv, b_w_out):
    B, S, D = x.shape
    mods = _modulation(c, ada_w, ada_b)
    mod0 = mods[0].reshape(B, 1, 3 * D)
    mod1 = mods[1].reshape(B, 1, 3 * D)

    for window, dil in DIL_GROUPS:
        assert window // (2 * dil) == HALF_WINDOW
    *qkvs, gate_a = _proj_a(x, mod0, _weights_a(a_w_in[0]))
    os_, lses = [], []
    for g, ((_, dil), qkv) in enumerate(zip(DIL_GROUPS, qkvs)):
        hs = slice(g * HEADS_PER_GROUP_A, (g + 1) * HEADS_PER_GROUP_A)
        o, lse = _attn_a(qkv, _bias_variants(rel_bias[:, hs], dil))
        os_.append(o)
        lses.append(lse)

    w_in, wqa, wk, wv = _weights_b(b_w_in[0], b_w_uq[0], b_w_ukv[0])
    x1, q2, k2, v2, gate_b = _mid(x, mod0, os_, lses, gate_a, a_w_out[0].astype(jnp.bfloat16), ln_g[0], ln_b[0],
                                  mod1, w_in, b_q_norm[0], b_kv_norm[0], wqa, wk, wv, _rope_tables(S))
    yg = _attn_b(q2, k2, v2, gate_b)
    return _out_b(x1, mod1, yg, b_w_out[0].astype(jnp.bfloat16), ln_g[1], ln_b[1])
```
